```python
import math
import jax, jax.numpy as jnp
from jax import lax
import numpy as np

D_MODEL = 1024
BATCH = 2
SEQ = 8192
DEPTH = 2
DEC_BATCH = 128
DEC_SEQ = 4
PAST_LEN = 2048
PAGE_SIZE = 128

N_MIXERS = 2
N_S5 = (DEPTH + 1) // 2
N_FOX = DEPTH // 2
MIX_WIDTH = D_MODEL
HEAD_DIM = 64
CROSS_HEADS = 4
CROSS_WIDTH = CROSS_HEADS * HEAD_DIM
MIXER_WIDTH = MIX_WIDTH - CROSS_WIDTH
S5_GROUP = 16
S5_GROUPS = MIXER_WIDTH // S5_GROUP
S5_STATE = 64
FOX_HEADS = MIXER_WIDTH // HEAD_DIM
N_MEM = 256
D_FF = 2816
Q_BLOCK = 128
RMS_EPS = 1e-6
NEG_INF = -1e30
FOX_IN_WIDTH = 3 * MIXER_WIDTH + FOX_HEADS + CROSS_WIDTH

kernel_name = 'hybrid_s5_fox_memory_decode_step'


def rms_norm(x, g):
    xf = x.astype(jnp.float32)
    xf = xf * lax.rsqrt(jnp.mean(xf * xf, axis=-1, keepdims=True) + RMS_EPS)
    return (xf * g.astype(jnp.float32)).astype(x.dtype)


def half_ffn(x, g, w_gate, w_up, w_down):
    h = rms_norm(x, g)
    return x + 0.5 * ((jax.nn.silu(h @ w_gate) * (h @ w_up)) @ w_down)


def memory_kv(mem, g, w_kv):
    nb, nm, _ = mem.shape
    kv = rms_norm(mem, g) @ w_kv
    k = kv[..., :CROSS_WIDTH].reshape(nb, nm, CROSS_HEADS, HEAD_DIM)
    v = kv[..., CROSS_WIDTH:].reshape(nb, nm, CROSS_HEADS, HEAD_DIM)
    return k, v


def cross_attend(qc, mk, mv):
    nb, L, _ = qc.shape
    q = qc.reshape(nb, L, CROSS_HEADS, HEAD_DIM).astype(jnp.float32)
    s = jnp.einsum('blhd,bmhd->bhlm', q, mk.astype(jnp.float32)) * HEAD_DIM ** -0.5
    p = jax.nn.softmax(s, axis=-1)
    o = jnp.einsum('bhlm,bmhd->blhd', p, mv.astype(jnp.float32))
    return o.reshape(nb, L, CROSS_WIDTH).astype(qc.dtype)


def _s5_combine(e1, e2):
    a1r, a1i, b1r, b1i = e1
    a2r, a2i, b2r, b2i = e2
    return (a1r * a2r - a1i * a2i,
            a1r * a2i + a1i * a2r,
            a2r * b1r - a2i * b1i + b2r,
            a2r * b1i + a2i * b1r + b2i)


def s5_mixer(u, h0_re, h0_im, a_re, a_im, log_dt, b_re, b_im, c_re, c_im, d_skip, w_glu):
    f32 = jnp.float32
    nb, L, _ = u.shape
    uf = u.astype(f32).reshape(nb, L, S5_GROUPS, S5_GROUP)
    a_re = a_re.astype(f32)
    a_im = a_im.astype(f32)
    dt = jnp.exp(log_dt.astype(f32))[:, None]
    mag = jnp.exp(dt * a_re)
    ab_re = mag * jnp.cos(dt * a_im)
    ab_im = mag * jnp.sin(dt * a_im)
    den = a_re * a_re + a_im * a_im
    nr = ab_re - 1.0
    ni = ab_im
    zr = (nr * a_re + ni * a_im) / den
    zi = (ni * a_re - nr * a_im) / den
    b_re = b_re.astype(f32)
    b_im = b_im.astype(f32)
    bb_re = zr[..., None] * b_re - zi[..., None] * b_im
    bb_im = zr[..., None] * b_im + zi[..., None] * b_re
    bu_re = jnp.einsum('blgc,gpc->blgp', uf, bb_re)
    bu_im = jnp.einsum('blgc,gpc->blgp', uf, bb_im)
    h0r = h0_re.astype(f32)
    h0i = h0_im.astype(f32)
    bu_re = bu_re.at[:, 0].add(ab_re * h0r - ab_im * h0i)
    bu_im = bu_im.at[:, 0].add(ab_re * h0i + ab_im * h0r)
    shape = (1, L, S5_GROUPS, S5_STATE)
    _, _, h_re, h_im = lax.associative_scan(
        _s5_combine,
        (jnp.broadcast_to(ab_re, shape), jnp.broadcast_to(ab_im, shape), bu_re, bu_im),
        axis=1)
    y = (jnp.einsum('blgp,gcp->blgc', h_re, c_re.astype(f32))
         - jnp.einsum('blgp,gcp->blgc', h_im, c_im.astype(f32))
         + d_skip.astype(f32).reshape(S5_GROUPS, S5_GROUP) * uf)
    y = jax.nn.gelu(y.reshape(nb, L, MIXER_WIDTH))
    z = y @ w_glu.astype(f32)
    out = z[..., :MIXER_WIDTH] * jax.nn.sigmoid(z[..., MIXER_WIDTH:])
    return out.astype(u.dtype), h_re[:, -1], h_im[:, -1]


def fox_project(proj, b_f):
    nb, L, _ = proj.shape
    W = MIXER_WIDTH
    q = proj[..., :W].reshape(nb, L, FOX_HEADS, HEAD_DIM)
    k = proj[..., W:2 * W].reshape(nb, L, FOX_HEADS, HEAD_DIM)
    v = proj[..., 2 * W:3 * W].reshape(nb, L, FOX_HEADS, HEAD_DIM)
    logf = jax.nn.log_sigmoid(proj[..., 3 * W:3 * W + FOX_HEADS].astype(jnp.float32)
                              + b_f.astype(jnp.float32))
    qc = proj[..., 3 * W + FOX_HEADS:]
    return q, k, v, logf, qc


def fox_attend(q, k, v, c_q, c_k, q_pos, k_pos):
    f32 = jnp.float32
    s = jnp.einsum('bqhd,bkhd->bhqk', q.astype(f32), k.astype(f32)) * HEAD_DIM ** -0.5
    s = s + jnp.swapaxes(c_q, 1, 2)[..., :, None] - jnp.swapaxes(c_k, 1, 2)[..., None, :]
    s = jnp.where(k_pos[None, :] <= q_pos[:, None], s, NEG_INF)
    p = jax.nn.softmax(s, axis=-1)
    return jnp.einsum('bhqk,bkhd->bqhd', p, v.astype(f32))


def fox_prompt(q, k, v, logf):
    nb, L = q.shape[0], q.shape[1]
    c = jnp.cumsum(logf, axis=1)
    nqb = L // Q_BLOCK
    pos = jnp.arange(L, dtype=jnp.int32)
    qb = q.reshape(nb, nqb, Q_BLOCK, FOX_HEADS, HEAD_DIM).swapaxes(0, 1)
    cb = c.reshape(nb, nqb, Q_BLOCK, FOX_HEADS).swapaxes(0, 1)
    pb = pos.reshape(nqb, Q_BLOCK)
    o = lax.map(lambda blk: fox_attend(blk[0], k, v, blk[1], c, blk[2], pos), (qb, cb, pb))
    return o.swapaxes(0, 1).reshape(nb, L, MIXER_WIDTH).astype(q.dtype)


def fox_sample(q, k, v, logf, k_past, v_past, logf_past):
    nb, T = q.shape[0], q.shape[1]
    n_past = k_past.shape[1]
    k_all = jnp.concatenate([k_past.astype(k.dtype), k], axis=1)
    v_all = jnp.concatenate([v_past.astype(v.dtype), v], axis=1)
    c_all = jnp.cumsum(jnp.concatenate([logf_past.astype(jnp.float32), logf], axis=1), axis=1)
    q_pos = n_past + jnp.arange(T, dtype=jnp.int32)
    k_pos = jnp.arange(n_past + T, dtype=jnp.int32)
    o = fox_attend(q, k_all, v_all, c_all[:, n_past:], c_all, q_pos, k_pos)
    return o.reshape(nb, T, MIXER_WIDTH).astype(q.dtype)


def setup_inputs(seed: int = 0) -> dict:
    key = jax.random.key(seed)
    ks = iter(jax.random.split(key, 48))
    f32 = jnp.float32

    def nrm(shape, scale):
        return jax.random.normal(next(ks), shape, f32) * scale

    n_pages = PAST_LEN // PAGE_SIZE
    n_phys = (DEC_BATCH * n_pages * 5) // 4
    perm = jax.random.permutation(next(ks), n_phys)
    page_table = perm[:DEC_BATCH * n_pages].reshape(DEC_BATCH, n_pages).astype(jnp.int32)

    inp = {}
    inp['x_prompt'] = nrm((BATCH, SEQ, D_MODEL), 1.0)
    inp['x_sample'] = nrm((DEC_BATCH, DEC_SEQ, D_MODEL), 1.0)
    inp['mem_prompt'] = nrm((BATCH, N_MEM, D_MODEL), 1.0)
    inp['state_s5_re'] = nrm((N_S5, DEC_BATCH, S5_GROUPS, S5_STATE), 0.1)
    inp['state_s5_im'] = nrm((N_S5, DEC_BATCH, S5_GROUPS, S5_STATE), 0.1)
    inp['cache_fox_k'] = nrm((N_FOX, n_phys, PAGE_SIZE, FOX_HEADS, HEAD_DIM), 1.0)
    inp['cache_fox_v'] = nrm((N_FOX, n_phys, PAGE_SIZE, FOX_HEADS, HEAD_DIM), 1.0)
    inp['cache_fox_logf'] = jax.nn.log_sigmoid(nrm((N_FOX, n_phys, PAGE_SIZE, FOX_HEADS), 1.0) + 3.0)
    inp['cache_mem_k'] = nrm((DEPTH, DEC_BATCH, N_MEM, CROSS_HEADS, HEAD_DIM), 1.0)
    inp['cache_mem_v'] = nrm((DEPTH, DEC_BATCH, N_MEM, CROSS_HEADS, HEAD_DIM), 1.0)
    inp['page_table'] = page_table
    inp['ffn_norm'] = 1.0 + nrm((DEPTH, 2, D_MODEL), 0.01)
    inp['ffn_w_gate'] = nrm((DEPTH, 2, D_MODEL, D_FF), D_MODEL ** -0.5)
    inp['ffn_w_up'] = nrm((DEPTH, 2, D_MODEL, D_FF), D_MODEL ** -0.5)
    inp['ffn_w_down'] = nrm((DEPTH, 2, D_FF, D_MODEL), D_FF ** -0.5)
    inp['norm_mix'] = 1.0 + nrm((DEPTH, D_MODEL), 0.01)
    inp['norm_mem'] = 1.0 + nrm((DEPTH, D_MODEL), 0.01)
    inp['w_mem_kv'] = nrm((DEPTH, D_MODEL, 2 * CROSS_WIDTH), D_MODEL ** -0.5)
    inp['w_in_s5'] = nrm((N_S5, D_MODEL, MIXER_WIDTH + CROSS_WIDTH), D_MODEL ** -0.5)
    inp['s5_a_re'] = -0.5 + nrm((N_S5, S5_GROUPS, S5_STATE), 0.01)
    inp['s5_a_im'] = (math.pi * jnp.arange(S5_STATE, dtype=f32)
                      + nrm((N_S5, S5_GROUPS, S5_STATE), 0.01))
    inp['s5_log_dt'] = jax.random.uniform(next(ks), (N_S5, S5_GROUPS), f32,
                                          minval=math.log(1e-3), maxval=math.log(1e-1))
    inp['s5_b_re'] = nrm((N_S5, S5_GROUPS, S5_STATE, S5_GROUP), (2.0 * S5_GROUP) ** -0.5)
    inp['s5_b_im'] = nrm((N_S5, S5_GROUPS, S5_STATE, S5_GROUP), (2.0 * S5_GROUP) ** -0.5)
    inp['s5_c_re'] = nrm((N_S5, S5_GROUPS, S5_GROUP, S5_STATE), S5_STATE ** -0.5)
    inp['s5_c_im'] = nrm((N_S5, S5_GROUPS, S5_GROUP, S5_STATE), S5_STATE ** -0.5)
    inp['s5_d'] = nrm((N_S5, MIXER_WIDTH), 1.0)
    inp['s5_w_glu'] = nrm((N_S5, MIXER_WIDTH, 2 * MIXER_WIDTH), MIXER_WIDTH ** -0.5)
    inp['w_in_fox'] = nrm((N_FOX, D_MODEL, FOX_IN_WIDTH), D_MODEL ** -0.5)
    inp['fox_b_f'] = (jnp.linspace(1.0, 5.0, FOX_HEADS, dtype=f32)
                      + nrm((N_FOX, FOX_HEADS), 0.01))
    inp['w_out'] = nrm((DEPTH, MIX_WIDTH, D_MODEL), MIX_WIDTH ** -0.5)
    inp['norm_final'] = 1.0 + nrm((D_MODEL,), 0.01)
    return inp


def reference(x_prompt, x_sample, mem_prompt, state_s5_re, state_s5_im,
              cache_fox_k, cache_fox_v, cache_fox_logf, cache_mem_k, cache_mem_v,
              page_table, ffn_norm, ffn_w_gate, ffn_w_up, ffn_w_down, norm_mix,
              norm_mem, w_mem_kv, w_in_s5, s5_a_re, s5_a_im, s5_log_dt, s5_b_re,
              s5_b_im, s5_c_re, s5_c_im, s5_d, s5_w_glu, w_in_fox, fox_b_f, w_out,
              norm_final):
    nb_p = x_prompt.shape[0]
    nb_s = x_sample.shape[0]
    n_past = page_table.shape[1] * PAGE_SIZE
    x_p = x_prompt
    x_s = x_sample
    s5_re_p, s5_im_p, s5_re_s, s5_im_s = [], [], [], []
    fk_p, fv_p, fl_p, fk_s, fv_s, fl_s = [], [], [], [], [], []
    mk_list, mv_list = [], []
    for i in range(DEPTH):
        j = i // N_MIXERS
        x_p = half_ffn(x_p, ffn_norm[i, 0], ffn_w_gate[i, 0], ffn_w_up[i, 0], ffn_w_down[i, 0])
        x_s = half_ffn(x_s, ffn_norm[i, 0], ffn_w_gate[i, 0], ffn_w_up[i, 0], ffn_w_down[i, 0])
        h_p = rms_norm(x_p, norm_mix[i])
        h_s = rms_norm(x_s, norm_mix[i])
        mk_p, mv_p = memory_kv(mem_prompt, norm_mem[i], w_mem_kv[i])
        mk_list.append(mk_p)
        mv_list.append(mv_p)
        mk_s = cache_mem_k[i]
        mv_s = cache_mem_v[i]
        if i % N_MIXERS == 0:
            proj_p = h_p @ w_in_s5[j]
            proj_s = h_s @ w_in_s5[j]
            s5p = (s5_a_re[j], s5_a_im[j], s5_log_dt[j], s5_b_re[j], s5_b_im[j],
                   s5_c_re[j], s5_c_im[j], s5_d[j], s5_w_glu[j])
            h0 = jnp.zeros((nb_p, S5_GROUPS, S5_STATE), jnp.float32)
            mix_p, hr_p, hi_p = s5_mixer(proj_p[..., :MIXER_WIDTH], h0, h0, *s5p)
            mix_s, hr_s, hi_s = s5_mixer(proj_s[..., :MIXER_WIDTH], state_s5_re[j],
                                         state_s5_im[j], *s5p)
            s5_re_p.append(hr_p)
            s5_im_p.append(hi_p)
            s5_re_s.append(hr_s)
            s5_im_s.append(hi_s)
            qc_p = proj_p[..., MIXER_WIDTH:]
            qc_s = proj_s[..., MIXER_WIDTH:]
        else:
            proj_p = h_p @ w_in_fox[j]
            proj_s = h_s @ w_in_fox[j]
            q_p, k_p, v_p, lf_p, qc_p = fox_project(proj_p, fox_b_f[j])
            q_s, k_s, v_s, lf_s, qc_s = fox_project(proj_s, fox_b_f[j])
            mix_p = fox_prompt(q_p, k_p, v_p, lf_p)
            k_past = cache_fox_k[j, page_table].reshape(nb_s, n_past, FOX_HEADS, HEAD_DIM)
            v_past = cache_fox_v[j, page_table].reshape(nb_s, n_past, FOX_HEADS, HEAD_DIM)
            lf_past = cache_fox_logf[j, page_table].reshape(nb_s, n_past, FOX_HEADS)
            mix_s = fox_sample(q_s, k_s, v_s, lf_s, k_past, v_past, lf_past)
            fk_p.append(k_p)
            fv_p.append(v_p)
            fl_p.append(lf_p)
            fk_s.append(k_s)
            fv_s.append(v_s)
            fl_s.append(lf_s)
        cross_p = cross_attend(qc_p, mk_p, mv_p)
        cross_s = cross_attend(qc_s, mk_s, mv_s)
        x_p = x_p + jnp.concatenate([mix_p, cross_p], axis=-1) @ w_out[i]
        x_s = x_s + jnp.concatenate([mix_s, cross_s], axis=-1) @ w_out[i]
        x_p = half_ffn(x_p, ffn_norm[i, 1], ffn_w_gate[i, 1], ffn_w_up[i, 1], ffn_w_down[i, 1])
        x_s = half_ffn(x_s, ffn_norm[i, 1], ffn_w_gate[i, 1], ffn_w_up[i, 1], ffn_w_down[i, 1])
    y_prompt = rms_norm(x_p, norm_final)
    y_sample = rms_norm(x_s, norm_final)
    new_s5_re_prompt = jnp.stack(s5_re_p)
    new_s5_im_prompt = jnp.stack(s5_im_p)
    new_s5_re_sample = jnp.stack(s5_re_s)
    new_s5_im_sample = jnp.stack(s5_im_s)
    new_fox_k_prompt = jnp.stack(fk_p)
    new_fox_v_prompt = jnp.stack(fv_p)
    new_fox_logf_prompt = jnp.stack(fl_p)
    new_fox_k_sample = jnp.stack(fk_s)
    new_fox_v_sample = jnp.stack(fv_s)
    new_fox_logf_sample = jnp.stack(fl_s)
    new_mem_k_prompt = jnp.stack(mk_list)
    new_mem_v_prompt = jnp.stack(mv_list)
    return (y_prompt, y_sample,
            new_s5_re_prompt, new_s5_im_prompt, new_s5_re_sample, new_s5_im_sample,
            new_fox_k_prompt, new_fox_v_prompt, new_fox_logf_prompt,
            new_fox_k_sample, new_fox_v_sample, new_fox_logf_sample,
            new_mem_k_prompt, new_mem_v_prompt)
```

```python
import functools

import jax
import jax.numpy as jnp
from jax import lax
from jax.experimental import pallas as pl
from jax.experimental.pallas import tpu as pltpu

F32 = jnp.float32
BF16 = jnp.bfloat16

RMS_EPS = 1e-6
NEG_INF = -1e30
N_MIXERS = 2

LANES = 128
SUBLANES = 8
MXU_DIM = 256
VMEM_LIMIT_BYTES = 56 * 1024 * 1024

NT_DIMS = (((1,), (1,)), ((), ()))


def _params(*sem):
    return pltpu.CompilerParams(dimension_semantics=sem, vmem_limit_bytes=VMEM_LIMIT_BYTES)


def _resident(shape):
    nd = len(shape)
    return pl.BlockSpec(shape, lambda *_: (0,) * nd, pipeline_mode=pl.Buffered(1))


def _rms(x, g):
    return x * lax.rsqrt(jnp.mean(x * x, axis=-1, keepdims=True) + RMS_EPS) * g


def _dot(a, b):
    return jnp.dot(a, b, preferred_element_type=F32)


def _dot_nt(a, b):
    return lax.dot_general(a, b, NT_DIMS, preferred_element_type=F32)


def _log_sigmoid(x):
    return jnp.minimum(x, 0.0) - jnp.log1p(jnp.exp(-jnp.abs(x)))


def _split3(x):
    hi = x.astype(BF16)
    r1 = x - hi.astype(F32)
    mid = r1.astype(BF16)
    lo = (r1 - mid.astype(F32)).astype(BF16)
    return hi, mid, lo


def _cumsum_lanes(x, tri):
    hi, mid, lo = _split3(x)
    return _dot(hi, tri) + _dot(mid, tri) + _dot(lo, tri)


def _upper_tri():
    r = lax.broadcasted_iota(jnp.int32, (LANES, LANES), 0)
    c = lax.broadcasted_iota(jnp.int32, (LANES, LANES), 1)
    return jnp.where(r <= c, 1.0, 0.0).astype(BF16)


def _ffn_body(*refs, d_ff, ff_chunk, final):
    if final:
        x_ref, g_ref, wg_ref, wu_ref, wd_ref, gf_ref, o_ref, t_ref = refs
    else:
        x_ref, g_ref, wg_ref, wu_ref, wd_ref, o_ref, t_ref = refs
    x = x_ref[...]
    h = _rms(x, g_ref[...]).astype(BF16)
    for c in range(d_ff // ff_chunk):
        lo, hi = c * ff_chunk, (c + 1) * ff_chunk
        a = _dot(h, wg_ref[:, lo:hi])
        b = _dot(h, wu_ref[:, lo:hi])
        t_ref[:, lo:hi] = (jax.nn.silu(a) * b).astype(BF16)
    y = x + 0.5 * _dot(t_ref[...], wd_ref[...])
    if final:
        y = _rms(y, gf_ref[...])
    o_ref[...] = y


def _ffn(x, g, wg, wu, wd, g_final=None, *, tm):
    t, d = x.shape
    d_ff = wg.shape[1]
    final = g_final is not None
    row = pl.BlockSpec((tm, d), lambda i: (i, 0))
    in_specs = [row, _resident((1, d)), _resident((d, d_ff)), _resident((d, d_ff)),
                _resident((d_ff, d))]
    args = [x, g.reshape(1, d), wg.astype(BF16), wu.astype(BF16), wd.astype(BF16)]
    if final:
        in_specs.append(_resident((1, d)))
        args.append(g_final.reshape(1, d))
    return pl.pallas_call(
        functools.partial(_ffn_body, d_ff=d_ff, ff_chunk=MXU_DIM, final=final),
        grid=(t // tm,),
        in_specs=in_specs,
        out_specs=row,
        out_shape=jax.ShapeDtypeStruct((t, d), F32),
        scratch_shapes=[pltpu.VMEM((tm, d_ff), BF16)],
        compiler_params=_params("parallel"),
        name="ffn_final" if final else "ffn",
    )(*args)


def _norm_proj_body(x_ref, g_ref, *refs, n):
    h = _rms(x_ref[...], g_ref[...]).astype(BF16)
    for w_ref, o_ref in zip(refs[:n], refs[n:]):
        o_ref[...] = _dot(h, w_ref[...])


def _norm_proj(x, g, ws, *, tm):
    t, d = x.shape
    n = len(ws)
    in_specs = [pl.BlockSpec((tm, d), lambda i: (i, 0)), _resident((1, d))]
    in_specs += [_resident(w.shape) for w in ws]
    return pl.pallas_call(
        functools.partial(_norm_proj_body, n=n),
        grid=(t // tm,),
        in_specs=in_specs,
        out_specs=[pl.BlockSpec((tm, w.shape[1]), lambda i: (i, 0)) for w in ws],
        out_shape=[jax.ShapeDtypeStruct((t, w.shape[1]), F32) for w in ws],
        compiler_params=_params("parallel"),
        name="norm_proj",
    )(x, g.reshape(1, d), *[w.astype(BF16) for w in ws])


def _fox_proj_body(x_ref, g_ref, wq_ref, wk_ref, wv_ref, wc_ref, wf_ref, wft_ref, bf_ref, bft_ref,
                   q_ref, k_ref, v_ref, kb_ref, vb_ref, qc_ref, lf_ref, lft_ref, *, scale):
    h = _rms(x_ref[...], g_ref[...]).astype(BF16)
    q_ref[...] = _dot(h, wq_ref[...]) * scale
    k = _dot(h, wk_ref[...])
    k_ref[...] = k
    kb_ref[...] = k.astype(BF16)
    v = _dot(h, wv_ref[...])
    v_ref[...] = v
    vb_ref[...] = v.astype(BF16)
    qc_ref[...] = _dot(h, wc_ref[...])
    lf_ref[...] = _log_sigmoid(_dot(h, wf_ref[...]) + bf_ref[...])
    lft_ref[...] = _log_sigmoid(_dot_nt(wft_ref[...], h) + bft_ref[...])


def _fox_proj(x, g, w_in, b_f, *, heads, head_dim, cross_width, tm):
    t, d = x.shape
    w = heads * head_dim
    hp = 2 * SUBLANES
    wq, wk, wv = w_in[:, :w], w_in[:, w:2 * w], w_in[:, 2 * w:3 * w]
    wf = jnp.pad(w_in[:, 3 * w:3 * w + heads], ((0, 0), (0, LANES - heads)))
    wc = w_in[:, 3 * w + heads:]
    bf = jnp.pad(b_f, (0, LANES - heads)).reshape(1, LANES)
    wft = wf[:, :hp].T
    bft = bf[0, :hp].reshape(hp, 1)
    row = lambda n: pl.BlockSpec((tm, n), lambda i: (i, 0))
    outs = [(w, F32), (w, F32), (w, F32), (w, BF16), (w, BF16), (cross_width, F32), (LANES, F32)]
    return pl.pallas_call(
        functools.partial(_fox_proj_body, scale=head_dim ** -0.5),
        grid=(t // tm,),
        in_specs=[row(d), _resident((1, d)), _resident((d, w)), _resident((d, w)),
                  _resident((d, w)), _resident((d, cross_width)), _resident((d, LANES)),
                  _resident((hp, d)), _resident((1, LANES)), _resident((hp, 1))],
        out_specs=[row(n) for n, _ in outs] + [pl.BlockSpec((hp, tm), lambda i: (0, i))],
        out_shape=[jax.ShapeDtypeStruct((t, n), dt) for n, dt in outs]
        + [jax.ShapeDtypeStruct((hp, t), F32)],
        compiler_params=_params("parallel"),
        name="fox_proj",
    )(x, g.reshape(1, d), wq.astype(BF16), wk.astype(BF16), wv.astype(BF16), wc.astype(BF16),
      wf.astype(BF16), wft.astype(BF16), bf, bft)


def _s5_param_body(are_ref, aim_ref, ldt_ref, bre_ref, bim_ref,
                   abr_ref, abi_ref, bbr_ref, bbi_ref, pwr_ref, pwi_ref):
    a_re = are_ref[...]
    a_im = aim_ref[...]
    dt = jnp.exp(ldt_ref[...])
    mag = jnp.exp(dt * a_re)
    ab_re = mag * jnp.cos(dt * a_im)
    ab_im = mag * jnp.sin(dt * a_im)
    den = a_re * a_re + a_im * a_im
    nr = ab_re - 1.0
    ni = ab_im
    zr = (nr * a_re + ni * a_im) / den
    zi = (ni * a_re - nr * a_im) / den
    b_re = bre_ref[...]
    b_im = bim_ref[...]
    abr_ref[...] = ab_re
    abi_ref[...] = ab_im
    bbr_ref[...] = zr * b_re - zi * b_im
    bbi_ref[...] = zr * b_im + zi * b_re
    pr, pi = ab_re, ab_im
    for r in range(SUBLANES):
        pwr_ref[r] = pr
        pwi_ref[r] = pi
        pr, pi = pr * ab_re - pi * ab_im, pr * ab_im + pi * ab_re


def _s5_params(a_re, a_im, log_dt, b_re, b_im, c_re, c_im):
    g, p, c = b_re.shape
    rows = g * c
    rep = lambda a: jnp.repeat(a, c, axis=0)
    bt = lambda b: b.transpose(0, 2, 1).reshape(rows, p)
    full = pl.BlockSpec((rows, p), lambda: (0, 0))
    pw_spec = pl.BlockSpec((SUBLANES, rows, p), lambda: (0, 0, 0))
    abr, abi, bbr, bbi, pwr, pwi = pl.pallas_call(
        _s5_param_body,
        in_specs=[full, full, pl.BlockSpec((rows, 1), lambda: (0, 0)), full, full],
        out_specs=[full, full, full, full, pw_spec, pw_spec],
        out_shape=[jax.ShapeDtypeStruct((rows, p), F32)] * 4
        + [jax.ShapeDtypeStruct((SUBLANES, rows, p), F32)] * 2,
        name="s5_params",
    )(rep(a_re), rep(a_im), rep(log_dt.reshape(g, 1)), bt(b_re), bt(b_im))
    width = g * p
    flat = lambda a: a[::c].reshape(1, width)
    flat_pw = lambda a: a[:, ::c].reshape(SUBLANES, width)
    gpt = MXU_DIM // c
    n_tiles = g // gpt
    eye = jnp.eye(gpt, dtype=F32)

    def in_tiles(bb):
        blocks = bb.reshape(n_tiles, gpt, c, p)
        return jnp.einsum("tgcp,gh->tgchp", blocks, eye).reshape(n_tiles, gpt * c, gpt * p)

    def out_tiles(cc):
        blocks = cc.reshape(n_tiles, gpt, c, p)
        return jnp.einsum("tgcp,gh->tgphc", blocks, eye).reshape(n_tiles, gpt * p, gpt * c)

    w_in = jnp.concatenate([in_tiles(bbr), in_tiles(bbi)], axis=-1).astype(BF16)
    return (flat(abr), flat(abi), flat_pw(pwr), flat_pw(pwi), w_in,
            out_tiles(c_re).astype(BF16), out_tiles(c_im).astype(BF16))


def _s5_in_proj(u, wb_ref, bre_ref, bim_ref):
    ub = u.astype(BF16)
    n_tiles, ch, two_w = wb_ref.shape
    w = two_w // 2
    for t in range(n_tiles):
        bu = _dot(ub[:, t * ch:(t + 1) * ch], wb_ref[t])
        bre_ref[:, t * w:(t + 1) * w] = bu[:, :w]
        bim_ref[:, t * w:(t + 1) * w] = bu[:, w:]


def _s5_out(u, hre_ref, him_ref, wcr_ref, wci_ref, d_ref, wglu_ref):
    n_tiles, w, _ = wcr_ref.shape
    ys = []
    for t in range(n_tiles):
        hr = hre_ref[:, t * w:(t + 1) * w].astype(BF16)
        hi = him_ref[:, t * w:(t + 1) * w].astype(BF16)
        ys.append(_dot(hr, wcr_ref[t]) - _dot(hi, wci_ref[t]))
    y = jax.nn.gelu(jnp.concatenate(ys, axis=-1) + d_ref[...] * u)
    z = _dot(y.astype(BF16), wglu_ref[...])
    half = z.shape[-1] // 2
    return z[:, :half] * jax.nn.sigmoid(z[:, half:])


def _s5_prompt_body(u_ref, wb_ref, wcr_ref, wci_ref, d_ref, wglu_ref, tab_ref,
                    o_ref, hre_ref, him_ref, bre_ref, bim_ref, cre_ref, cim_ref, *, lane_group):
    ci = pl.program_id(1)
    rows, width = bre_ref.shape

    @pl.when(ci == 0)
    def _():
        cre_ref[...] = jnp.zeros_like(cre_ref)
        cim_ref[...] = jnp.zeros_like(cim_ref)

    u = u_ref[0]
    _s5_in_proj(u, wb_ref, bre_ref, bim_ref)

    for lg in range(width // lane_group):
        ls = slice(lg * lane_group, (lg + 1) * lane_group)

        def tile(i, carry, ls=ls):
            cr, cim = carry
            r0 = pl.multiple_of(i * SUBLANES, SUBLANES)
            xr = bre_ref[pl.ds(r0, SUBLANES), ls]
            xi = bim_ref[pl.ds(r0, SUBLANES), ls]
            for j, shift in enumerate((1, 2, 4)):
                rr = pltpu.roll(xr, shift, 0)
                ri = pltpu.roll(xi, shift, 0)
                ar = tab_ref[2 * j, :, ls]
                ai = tab_ref[2 * j + 1, :, ls]
                xr, xi = xr + ar * rr - ai * ri, xi + ar * ri + ai * rr
            pr = tab_ref[6, :, ls]
            pi = tab_ref[7, :, ls]
            xr, xi = xr + pr * cr - pi * cim, xi + pr * cim + pi * cr
            bre_ref[pl.ds(r0, SUBLANES), ls] = xr
            bim_ref[pl.ds(r0, SUBLANES), ls] = xi
            last = SUBLANES - 1
            return (jnp.broadcast_to(xr[last:, :], xr.shape), jnp.broadcast_to(xi[last:, :], xi.shape))

        cr, cim = lax.fori_loop(0, rows // SUBLANES, tile, (cre_ref[:, ls], cim_ref[:, ls]))
        cre_ref[:, ls] = cr
        cim_ref[:, ls] = cim

    o_ref[0] = _s5_out(u, bre_ref, bim_ref, wcr_ref, wci_ref, d_ref, wglu_ref)

    @pl.when(ci == pl.num_programs(1) - 1)
    def _():
        hre_ref[0] = bre_ref[rows - 1:rows, :]
        him_ref[0] = bim_ref[rows - 1:rows, :]


def _s5_prompt(u, params, d_skip, w_glu, *, rows):
    ab_re, ab_im, pw_re, pw_im, w_in, wc_re, wc_im = params
    nb, length, ch = u.shape
    width = ab_re.shape[1]
    row_idx = jnp.arange(SUBLANES)[:, None]
    tabs = []
    for shift in (1, 2, 4):
        keep = row_idx >= shift
        tabs += [jnp.where(keep, pw_re[shift - 1][None, :], 0.0),
                 jnp.where(keep, pw_im[shift - 1][None, :], 0.0)]
    tab = jnp.stack(tabs + [pw_re, pw_im])
    state = jax.ShapeDtypeStruct((nb, 1, width), F32)
    state_spec = pl.BlockSpec((1, 1, width), lambda b, c: (b, 0, 0))
    tok_spec = pl.BlockSpec((1, rows, ch), lambda b, c: (b, c, 0))
    return pl.pallas_call(
        functools.partial(_s5_prompt_body, lane_group=4 * LANES),
        grid=(nb, length // rows),
        in_specs=[tok_spec, _resident(w_in.shape), _resident(wc_re.shape), _resident(wc_im.shape),
                  _resident((1, ch)), _resident(w_glu.shape), _resident(tab.shape)],
        out_specs=[tok_spec, state_spec, state_spec],
        out_shape=[jax.ShapeDtypeStruct((nb, length, ch), F32), state, state],
        scratch_shapes=[pltpu.VMEM((rows, width), F32), pltpu.VMEM((rows, width), F32),
                        pltpu.VMEM((SUBLANES, width), F32), pltpu.VMEM((SUBLANES, width), F32)],
        compiler_params=_params("parallel", "arbitrary"),
        name="s5_prompt",
    )(u, w_in, wc_re, wc_im, d_skip.reshape(1, ch), w_glu.astype(BF16), tab)


def _s5_sample_body(u_ref, h0r_ref, h0i_ref, wb_ref, wcr_ref, wci_ref, d_ref, wglu_ref, ab_ref,
                    o_ref, hre_ref, him_ref, bre_ref, bim_ref):
    hre_ref[...] = h0r_ref[...]
    him_ref[...] = h0i_ref[...]
    ar = ab_ref[0:1, :]
    ai = ab_ref[1:2, :]
    for t in range(u_ref.shape[0]):
        u = u_ref[t]
        _s5_in_proj(u, wb_ref, bre_ref, bim_ref)
        hr = hre_ref[...]
        hi = him_ref[...]
        hre_ref[...] = ar * hr - ai * hi + bre_ref[...]
        him_ref[...] = ar * hi + ai * hr + bim_ref[...]
        o_ref[t] = _s5_out(u, hre_ref, him_ref, wcr_ref, wci_ref, d_ref, wglu_ref)


def _s5_sample(u, h0_re, h0_im, params, d_skip, w_glu):
    ab_re, ab_im, _, _, w_in, wc_re, wc_im = params
    nt, nb, ch = u.shape
    width = ab_re.shape[1]
    ab = jnp.concatenate([ab_re, ab_im], axis=0)
    whole = lambda shape: pl.BlockSpec(shape, lambda: (0,) * len(shape))
    state = jax.ShapeDtypeStruct((nb, width), F32)
    return pl.pallas_call(
        _s5_sample_body,
        in_specs=[whole(u.shape), whole((nb, width)), whole((nb, width)), whole(w_in.shape),
                  whole(wc_re.shape), whole(wc_im.shape), whole((1, ch)), whole(w_glu.shape),
                  whole(ab.shape)],
        out_specs=[whole(u.shape), whole((nb, width)), whole((nb, width))],
        out_shape=[jax.ShapeDtypeStruct(u.shape, F32), state, state],
        scratch_shapes=[pltpu.VMEM((nb, width), F32), pltpu.VMEM((nb, width), F32)],
        compiler_params=pltpu.CompilerParams(vmem_limit_bytes=VMEM_LIMIT_BYTES),
        name="s5_sample",
    )(u, h0_re, h0_im, w_in, wc_re, wc_im, d_skip.reshape(1, ch), w_glu.astype(BF16), ab)


def _cumsum_body(x_ref, o_ref):
    tri = _upper_tri()
    carry = jnp.zeros((x_ref.shape[1], 1), F32)
    for blk in range(x_ref.shape[2] // LANES):
        ls = slice(blk * LANES, (blk + 1) * LANES)
        c = _cumsum_lanes(x_ref[0, :, ls], tri) + carry
        o_ref[0, :, ls] = c
        carry = c[:, LANES - 1:]


def _cumsum_tokens(x):
    nb, rows, length = x.shape
    spec = pl.BlockSpec((1, rows, length), lambda b: (b, 0, 0))
    return pl.pallas_call(
        _cumsum_body, grid=(nb,), in_specs=[spec], out_specs=spec,
        out_shape=jax.ShapeDtypeStruct(x.shape, F32),
        compiler_params=_params("parallel"), name="fox_cumsum",
    )(x)


def _fox_prompt_body(q_ref, k_ref, v_ref, c_ref, o_ref, m_ref, l_ref, acc_ref, *, head_dim):
    qi = pl.program_id(2)
    tq = q_ref.shape[1]
    lane = lax.broadcasted_iota(jnp.int32, (1, LANES), 1)
    first = lane < head_dim
    q2 = q_ref[0]
    zero = jnp.zeros_like(q2)
    qh = (jnp.where(first, q2, zero).astype(BF16), jnp.where(first, zero, q2).astype(BF16))
    m_ref[...] = jnp.full_like(m_ref, NEG_INF)
    l_ref[...] = jnp.zeros_like(l_ref)
    acc_ref[...] = jnp.zeros_like(acc_ref)

    def block(kj, diagonal):
        ks = pl.multiple_of(kj * tq, tq)
        kb = k_ref[0, pl.ds(ks, tq), :]
        vb = v_ref[0, pl.ds(ks, tq), :]
        for h in range(2):
            s = _dot_nt(qh[h], kb) - c_ref[0, 0, h:h + 1, pl.ds(ks, tq)]
            if diagonal:
                r = lax.broadcasted_iota(jnp.int32, s.shape, 0)
                c = lax.broadcasted_iota(jnp.int32, s.shape, 1)
                s = jnp.where(c <= r, s, NEG_INF)
            m_old = m_ref[h]
            m_new = jnp.maximum(m_old, jnp.max(s, axis=1, keepdims=True))
            alpha = jnp.exp(m_old - m_new)
            p = jnp.exp(s - m_new)
            l_ref[h] = alpha * l_ref[h] + jnp.sum(p, axis=1, keepdims=True)
            acc_ref[h] = alpha * acc_ref[h] + _dot(p.astype(BF16), vb)
            m_ref[h] = m_new

    def body(kj, carry):
        block(kj, False)
        return carry

    lax.fori_loop(0, qi, body, 0)
    block(qi, True)
    o_ref[0] = jnp.where(first, acc_ref[0] / l_ref[0], acc_ref[1] / l_ref[1])


def _fox_prompt(q, kb, vb, c, *, head_dim, tq):
    nb, length, w = q.shape
    pairs = w // LANES
    kv_spec = pl.BlockSpec((1, length, LANES), lambda b, hp, i: (b, 0, hp))
    q_spec = pl.BlockSpec((1, tq, LANES), lambda b, hp, i: (b, i, hp))
    return pl.pallas_call(
        functools.partial(_fox_prompt_body, head_dim=head_dim),
        grid=(nb, pairs, length // tq),
        in_specs=[q_spec, kv_spec, kv_spec,
                  pl.BlockSpec((1, 1, 2, length), lambda b, hp, i: (b, hp, 0, 0))],
        out_specs=q_spec,
        out_shape=jax.ShapeDtypeStruct((nb, length, w), F32),
        scratch_shapes=[pltpu.VMEM((2, tq, 1), F32), pltpu.VMEM((2, tq, 1), F32),
                        pltpu.VMEM((2, tq, LANES), F32)],
        compiler_params=_params("parallel", "parallel", "arbitrary"),
        name="fox_prompt",
    )(q, kb, vb, c)


def _fox_sample_body(pt_ref, q_ref, *refs, pages_per_step, heads, head_dim):
    n = pages_per_step
    k_refs, v_refs, lf_refs = refs[:n], refs[n:2 * n], refs[2 * n:3 * n]
    kn_ref, vn_ref, lfn_ref, o_ref, qrow_ref, m_ref, l_ref, acc_ref, carry_ref = refs[3 * n:]
    del pt_ref
    g = pl.program_id(1)
    n_tok, w = q_ref.shape
    hp = lf_refs[0].shape[0]
    page = k_refs[0].shape[0]
    head_of_lane = lax.broadcasted_iota(jnp.int32, (hp, w), 1) // head_dim
    head_mask = head_of_lane == lax.broadcasted_iota(jnp.int32, (hp, w), 0)
    tri = _upper_tri()

    @pl.when(g == 0)
    def _():
        zero = jnp.zeros((hp, w), F32)
        rows = [jnp.where(head_mask, jnp.broadcast_to(q_ref[t:t + 1, :], (hp, w)), zero)
                for t in range(n_tok)]
        qrow_ref[...] = jnp.concatenate(rows, axis=0).astype(BF16)
        m_ref[...] = jnp.full_like(m_ref, NEG_INF)
        l_ref[...] = jnp.zeros_like(l_ref)
        acc_ref[...] = jnp.zeros_like(acc_ref)
        carry_ref[...] = jnp.zeros_like(carry_ref)

    def update(kb, vb, lf, valid=None):
        c = _cumsum_lanes(lf, tri) + carry_ref[...]
        carry_ref[...] = jnp.broadcast_to(c[:, page - 1:], c.shape)
        s = _dot_nt(qrow_ref[...], kb) - jnp.concatenate([c] * n_tok, axis=0)
        if valid is not None:
            s = jnp.where(valid, s, NEG_INF)
        m_old = m_ref[...]
        m_new = jnp.maximum(m_old, jnp.max(s, axis=1, keepdims=True))
        alpha = jnp.exp(m_old - m_new)
        p = jnp.exp(s - m_new)
        l_ref[...] = alpha * l_ref[...] + jnp.sum(p, axis=1, keepdims=True)
        acc_ref[...] = alpha * acc_ref[...] + _dot(p.astype(BF16), vb)
        m_ref[...] = m_new

    for i in range(n):
        update(k_refs[i][...].astype(BF16), v_refs[i][...].astype(BF16), lf_refs[i][...])

    @pl.when(g == pl.num_programs(1) - 1)
    def _():
        pad = jnp.zeros((page - kn_ref.shape[0], w), F32)
        kn = jnp.concatenate([kn_ref[...], pad], axis=0).astype(BF16)
        vn = jnp.concatenate([vn_ref[...], pad], axis=0).astype(BF16)
        key = lax.broadcasted_iota(jnp.int32, (n_tok * hp, page), 1)
        tok = lax.broadcasted_iota(jnp.int32, (n_tok * hp, page), 0) // hp
        update(kn, vn, lfn_ref[...], key <= tok)
        out = acc_ref[...] / l_ref[...]
        for t in range(n_tok):
            picked = jnp.where(head_mask, out[t * hp:(t + 1) * hp, :], 0.0)
            o_ref[t:t + 1, :] = jnp.sum(picked, axis=0, keepdims=True)


def _fox_sample(q, k_new, v_new, lft_new, cache_k, cache_v, cache_lft, page_table,
                *, heads, head_dim, pages_per_step):
    ns, n_tok, w = q.shape
    n_pages = page_table.shape[1]
    page = cache_k.shape[1]
    hp = cache_lft.shape[1]
    n = pages_per_step
    pad_rows = lambda a: jnp.pad(a, ((0, 0), (0, SUBLANES - n_tok), (0, 0)))

    def paged(shape, i):
        return pl.BlockSpec((None,) + shape, lambda s, g, pt: (pt[s, g * n + i], 0, 0))

    per_seq = lambda shape: pl.BlockSpec((None,) + shape, lambda s, g, pt: (s, 0, 0))
    in_specs = [per_seq((n_tok, w))]
    in_specs += [paged((page, w), i) for i in range(n)]
    in_specs += [paged((page, w), i) for i in range(n)]
    in_specs += [paged((hp, page), i) for i in range(n)]
    in_specs += [per_seq((SUBLANES, w)), per_seq((SUBLANES, w)), per_seq((hp, page))]
    rows = n_tok * hp
    return pl.pallas_call(
        functools.partial(_fox_sample_body, pages_per_step=n, heads=heads, head_dim=head_dim),
        grid_spec=pltpu.PrefetchScalarGridSpec(
            num_scalar_prefetch=1,
            grid=(ns, n_pages // n),
            in_specs=in_specs,
            out_specs=per_seq((n_tok, w)),
            scratch_shapes=[pltpu.VMEM((rows, w), BF16), pltpu.VMEM((rows, 1), F32),
                            pltpu.VMEM((rows, 1), F32), pltpu.VMEM((rows, w), F32),
                            pltpu.VMEM((hp, page), F32)],
        ),
        out_shape=jax.ShapeDtypeStruct((ns, n_tok, w), F32),
        compiler_params=_params("parallel", "arbitrary"),
        name="fox_sample",
    )(page_table, q, *([cache_k] * n), *([cache_v] * n), *([cache_lft] * n),
      pad_rows(k_new), pad_rows(v_new), lft_new)


def _softmax_pv(s, vb):
    m = jnp.max(s, axis=-1, keepdims=True)
    p = jnp.exp(s - m)
    return _dot(p.astype(BF16), vb) / jnp.sum(p, axis=-1, keepdims=True)


def _cross_out_prompt_body(x_ref, mix_ref, qc_ref, mk_ref, mv_ref, wom_ref, woc_ref, o_ref,
                           *, heads, head_dim):
    tm, cw = qc_ref.shape
    q = qc_ref[...] * head_dim ** -0.5
    head_of_lane = lax.broadcasted_iota(jnp.int32, (1, cw), 1) // head_dim
    zero = jnp.zeros_like(q)
    q4 = jnp.concatenate([jnp.where(head_of_lane == h, q, zero) for h in range(heads)], axis=0)
    s = _dot_nt(q4.astype(BF16), mk_ref[0].astype(BF16))
    o4 = _softmax_pv(s, mv_ref[0].astype(BF16))
    cross = zero
    for h in range(heads):
        cross = cross + jnp.where(head_of_lane == h, o4[h * tm:(h + 1) * tm, :], zero)
    o_ref[...] = (x_ref[...] + _dot(mix_ref[...].astype(BF16), wom_ref[...])
                  + _dot(cross.astype(BF16), woc_ref[...]))


def _cross_out_prompt(x, mix, qc, mk, mv, wom, woc, *, heads, head_dim, tm):
    t, d = x.shape
    nb, n_mem, cw = mk.shape
    per_b = t // nb // tm
    row = lambda n: pl.BlockSpec((tm, n), lambda i: (i, 0))
    mem = pl.BlockSpec((1, n_mem, cw), lambda i: (i // per_b, 0, 0))
    return pl.pallas_call(
        functools.partial(_cross_out_prompt_body, heads=heads, head_dim=head_dim),
        grid=(t // tm,),
        in_specs=[row(d), row(mix.shape[1]), row(cw), mem, mem, _resident(wom.shape),
                  _resident(woc.shape)],
        out_specs=row(d),
        out_shape=jax.ShapeDtypeStruct((t, d), F32),
        compiler_params=_params("parallel"),
        name="cross_out_prompt",
    )(x, mix, qc, mk, mv, wom, woc)


def _cross_out_sample_body(x_ref, mix_ref, qc_ref, mk_ref, mv_ref, wom_ref, woc_ref, o_ref, cross_ref,
                           *, heads, head_dim, n_tok):
    n_seq, _, cw = mk_ref.shape
    scale = head_dim ** -0.5
    head_of_lane = lax.broadcasted_iota(jnp.int32, (SUBLANES, cw), 1) // head_dim
    head_mask = head_of_lane == lax.broadcasted_iota(jnp.int32, (SUBLANES, cw), 0)
    zero = jnp.zeros((SUBLANES, cw), F32)
    for i in range(n_seq):
        rows = [jnp.where(head_mask, jnp.broadcast_to(qc_ref[pl.ds(i * n_tok + t, 1), :] * scale,
                                                     (SUBLANES, cw)), zero) for t in range(n_tok)]
        q = jnp.concatenate(rows, axis=0).astype(BF16)
        o = _softmax_pv(_dot_nt(q, mk_ref[i].astype(BF16)), mv_ref[i].astype(BF16))
        for t in range(n_tok):
            picked = jnp.where(head_mask, o[t * SUBLANES:(t + 1) * SUBLANES, :], zero)
            cross_ref[pl.ds(i * n_tok + t, 1), :] = jnp.sum(picked, axis=0, keepdims=True)
    o_ref[...] = (x_ref[...] + _dot(mix_ref[...].astype(BF16), wom_ref[...])
                  + _dot(cross_ref[...].astype(BF16), woc_ref[...]))


def _cross_out_sample(x, mix, qc, mk, mv, wom, woc, *, heads, head_dim, n_tok, seqs_per_step):
    t, d = x.shape
    ns, n_mem, cw = mk.shape
    tm = seqs_per_step * n_tok
    row = lambda n: pl.BlockSpec((tm, n), lambda i: (i, 0))
    mem = pl.BlockSpec((seqs_per_step, n_mem, cw), lambda i: (i, 0, 0))
    return pl.pallas_call(
        functools.partial(_cross_out_sample_body, heads=heads, head_dim=head_dim, n_tok=n_tok),
        grid=(ns // seqs_per_step,),
        in_specs=[row(d), row(mix.shape[1]), row(cw), mem, mem, _resident(wom.shape),
                  _resident(woc.shape)],
        out_specs=row(d),
        out_shape=jax.ShapeDtypeStruct((t, d), F32),
        scratch_shapes=[pltpu.VMEM((tm, cw), F32)],
        compiler_params=_params("parallel"),
        name="cross_out_sample",
    )(x, mix, qc, mk, mv, wom, woc)


def kernel(x_prompt, x_sample, mem_prompt, state_s5_re, state_s5_im, cache_fox_k, cache_fox_v, cache_fox_logf, cache_mem_k, cache_mem_v, page_table, ffn_norm, ffn_w_gate, ffn_w_up, ffn_w_down, norm_mix, norm_mem, w_mem_kv, w_in_s5, s5_a_re, s5_a_im, s5_log_dt, s5_b_re, s5_b_im, s5_c_re, s5_c_im, s5_d, s5_w_glu, w_in_fox, fox_b_f, w_out, norm_final):
    nb, seq, d = x_prompt.shape
    ns, n_tok, _ = x_sample.shape
    depth = ffn_norm.shape[0]
    n_mem = mem_prompt.shape[1]
    cross_heads, head_dim = cache_mem_k.shape[3], cache_mem_k.shape[4]
    cross_width = cross_heads * head_dim
    mixer_width = w_out.shape[1] - cross_width
    fox_heads = cache_fox_k.shape[3]
    n_phys, page = cache_fox_k.shape[1], cache_fox_k.shape[2]
    groups, n_state = state_s5_re.shape[2], state_s5_re.shape[3]
    hp = 2 * SUBLANES

    tm_p = 512
    tm_s = ns * n_tok
    xp = x_prompt.reshape(nb * seq, d)
    xs = x_sample.reshape(tm_s, d)
    mem = mem_prompt.reshape(nb * n_mem, d)

    s5_re_p, s5_im_p, s5_re_s, s5_im_s = [], [], [], []
    fk_p, fv_p, fl_p, fk_s, fv_s, fl_s = [], [], [], [], [], []
    mk_list, mv_list = [], []
    for i in range(depth):
        j = i // N_MIXERS
        ffn_w = lambda half: (ffn_norm[i, half], ffn_w_gate[i, half], ffn_w_up[i, half], ffn_w_down[i, half])
        xp = _ffn(xp, *ffn_w(0), tm=tm_p)
        xs = _ffn(xs, *ffn_w(0), tm=tm_s)

        mk_p, mv_p = _norm_proj(mem, norm_mem[i],
                                [w_mem_kv[i][:, :cross_width], w_mem_kv[i][:, cross_width:]], tm=nb * n_mem)
        mk_list.append(mk_p.reshape(nb, n_mem, cross_heads, head_dim))
        mv_list.append(mv_p.reshape(nb, n_mem, cross_heads, head_dim))
        mk_s = cache_mem_k[i].reshape(ns, n_mem, cross_width)
        mv_s = cache_mem_v[i].reshape(ns, n_mem, cross_width)

        if i % N_MIXERS == 0:
            w_in = [w_in_s5[j][:, :mixer_width], w_in_s5[j][:, mixer_width:]]
            u_p, qc_p = _norm_proj(xp, norm_mix[i], w_in, tm=tm_p)
            u_s, qc_s = _norm_proj(xs, norm_mix[i], w_in, tm=tm_s)
            params = _s5_params(s5_a_re[j], s5_a_im[j], s5_log_dt[j], s5_b_re[j], s5_b_im[j],
                                s5_c_re[j], s5_c_im[j])
            mix_p, hr_p, hi_p = _s5_prompt(u_p.reshape(nb, seq, mixer_width), params, s5_d[j],
                                           s5_w_glu[j], rows=tm_p)
            mix_p = mix_p.reshape(nb * seq, mixer_width)
            u_t = u_s.reshape(ns, n_tok, mixer_width).transpose(1, 0, 2)
            mix_t, hr_s, hi_s = _s5_sample(u_t, state_s5_re[j].reshape(ns, groups * n_state),
                                           state_s5_im[j].reshape(ns, groups * n_state),
                                           params, s5_d[j], s5_w_glu[j])
            mix_s = mix_t.transpose(1, 0, 2).reshape(tm_s, mixer_width)
            s5_re_p.append(hr_p.reshape(nb, groups, n_state))
            s5_im_p.append(hi_p.reshape(nb, groups, n_state))
            s5_re_s.append(hr_s.reshape(ns, groups, n_state))
            s5_im_s.append(hi_s.reshape(ns, groups, n_state))
        else:
            fox = functools.partial(_fox_proj, heads=fox_heads, head_dim=head_dim, cross_width=cross_width)
            q_p, k_p, v_p, kb_p, vb_p, qc_p, lf_p, lft_p = fox(xp, norm_mix[i], w_in_fox[j], fox_b_f[j], tm=tm_p)
            q_s, k_s, v_s, _, _, qc_s, lf_s, lft_s = fox(xs, norm_mix[i], w_in_fox[j], fox_b_f[j], tm=tm_s)

            c_p = _cumsum_tokens(lft_p.reshape(hp, nb, seq).transpose(1, 0, 2))
            c_p = c_p[:, :fox_heads].reshape(nb, fox_heads // 2, 2, seq)
            seq3 = lambda a: a.reshape(nb, seq, mixer_width)
            mix_p = _fox_prompt(seq3(q_p), seq3(kb_p), seq3(vb_p), c_p, head_dim=head_dim, tq=tm_p)
            mix_p = mix_p.reshape(nb * seq, mixer_width)

            tok3 = lambda a: a.reshape(ns, n_tok, mixer_width)
            lft_new = jnp.pad(lft_s.reshape(hp, ns, n_tok).transpose(1, 0, 2),
                              ((0, 0), (0, 0), (0, page - n_tok)))
            cache_lft = jnp.pad(cache_fox_logf[j].transpose(0, 2, 1), ((0, 0), (0, hp - fox_heads), (0, 0)))
            mix_s = _fox_sample(tok3(q_s), tok3(k_s), tok3(v_s), lft_new,
                                cache_fox_k[j].reshape(n_phys, page, mixer_width),
                                cache_fox_v[j].reshape(n_phys, page, mixer_width),
                                cache_lft, page_table,
                                heads=fox_heads, head_dim=head_dim, pages_per_step=4)
            mix_s = mix_s.reshape(tm_s, mixer_width)

            fk_p.append(k_p.reshape(nb, seq, fox_heads, head_dim))
            fv_p.append(v_p.reshape(nb, seq, fox_heads, head_dim))
            fl_p.append(lf_p[:, :fox_heads].reshape(nb, seq, fox_heads))
            fk_s.append(k_s.reshape(ns, n_tok, fox_heads, head_dim))
            fv_s.append(v_s.reshape(ns, n_tok, fox_heads, head_dim))
            fl_s.append(lf_s[:, :fox_heads].reshape(ns, n_tok, fox_heads))

        wom = w_out[i][:mixer_width].astype(BF16)
        woc = w_out[i][mixer_width:].astype(BF16)
        xp = _cross_out_prompt(xp, mix_p, qc_p, mk_p.reshape(nb, n_mem, cross_width),
                               mv_p.reshape(nb, n_mem, cross_width), wom, woc,
                               heads=cross_heads, head_dim=head_dim, tm=tm_p)
        xs = _cross_out_sample(xs, mix_s, qc_s, mk_s, mv_s, wom, woc,
                               heads=cross_heads, head_dim=head_dim, n_tok=n_tok, seqs_per_step=8)

        g_final = norm_final if i == depth - 1 else None
        xp = _ffn(xp, *ffn_w(1), g_final, tm=tm_p)
        xs = _ffn(xs, *ffn_w(1), g_final, tm=tm_s)

    return (xp.reshape(nb, seq, d), xs.reshape(ns, n_tok, d),
            jnp.stack(s5_re_p), jnp.stack(s5_im_p), jnp.stack(s5_re_s), jnp.stack(s5_im_s),
            jnp.stack(fk_p), jnp.stack(fv_p), jnp.stack(fl_p),
            jnp.stack(fk_s), jnp.stack(fv_s), jnp.stack(fl_s),
            jnp.stack(mk_list), jnp.stack(mv_list))
```

```python
import functools

import jax
import jax.numpy as jnp
from jax import lax
from jax.experimental import pallas as pl
from jax.experimental.pallas import tpu as pltpu

F32 = jnp.float32
BF16 = jnp.bfloat16

RMS_EPS = 1e-6
NEG_INF = -1e30
N_MIXERS = 2

LANES = 128
SUBLANES = 8
MXU_DIM = 256
VMEM_LIMIT_BYTES = 56 * 1024 * 1024

NT_DIMS = (((1,), (1,)), ((), ()))


def _params(*sem):
    return pltpu.CompilerParams(dimension_semantics=sem, vmem_limit_bytes=VMEM_LIMIT_BYTES)


def _resident(shape):
    nd = len(shape)
    return pl.BlockSpec(shape, lambda *_: (0,) * nd, pipeline_mode=pl.Buffered(1))


def _rms(x, g):
    return x * lax.rsqrt(jnp.mean(x * x, axis=-1, keepdims=True) + RMS_EPS) * g


def _dot(a, b):
    return jnp.dot(a, b, preferred_element_type=F32)


def _dot_nt(a, b):
    return lax.dot_general(a, b, NT_DIMS, preferred_element_type=F32)


def _log_sigmoid(x):
    return jnp.minimum(x, 0.0) - jnp.log1p(jnp.exp(-jnp.abs(x)))


def _split3(x):
    hi = x.astype(BF16)
    r1 = x - hi.astype(F32)
    mid = r1.astype(BF16)
    lo = (r1 - mid.astype(F32)).astype(BF16)
    return hi, mid, lo


def _cumsum_lanes(x, tri):
    hi, mid, lo = _split3(x)
    return _dot(hi, tri) + _dot(mid, tri) + _dot(lo, tri)


def _upper_tri():
    r = lax.broadcasted_iota(jnp.int32, (LANES, LANES), 0)
    c = lax.broadcasted_iota(jnp.int32, (LANES, LANES), 1)
    return jnp.where(r <= c, 1.0, 0.0).astype(BF16)


def _ffn_body(*refs, d_ff, ff_chunk, final):
    if final:
        x_ref, g_ref, wg_ref, wu_ref, wd_ref, gf_ref, o_ref, t_ref = refs
    else:
        x_ref, g_ref, wg_ref, wu_ref, wd_ref, o_ref, t_ref = refs
    x = x_ref[...]
    h = _rms(x, g_ref[...]).astype(BF16)
    for c in range(d_ff // ff_chunk):
        lo, hi = c * ff_chunk, (c + 1) * ff_chunk
        a = _dot(h, wg_ref[:, lo:hi])
        b = _dot(h, wu_ref[:, lo:hi])
        t_ref[:, lo:hi] = (jax.nn.silu(a) * b).astype(BF16)
    y = x + 0.5 * _dot(t_ref[...], wd_ref[...])
    if final:
        y = _rms(y, gf_ref[...])
    o_ref[...] = y


def _ffn(x, g, wg, wu, wd, g_final=None, *, tm):
    t, d = x.shape
    d_ff = wg.shape[1]
    final = g_final is not None
    row = pl.BlockSpec((tm, d), lambda i: (i, 0))
    in_specs = [row, _resident((1, d)), _resident((d, d_ff)), _resident((d, d_ff)),
                _resident((d_ff, d))]
    args = [x, g.reshape(1, d), wg.astype(BF16), wu.astype(BF16), wd.astype(BF16)]
    if final:
        in_specs.append(_resident((1, d)))
        args.append(g_final.reshape(1, d))
    return pl.pallas_call(
        functools.partial(_ffn_body, d_ff=d_ff, ff_chunk=MXU_DIM, final=final),
        grid=(t // tm,),
        in_specs=in_specs,
        out_specs=row,
        out_shape=jax.ShapeDtypeStruct((t, d), F32),
        scratch_shapes=[pltpu.VMEM((tm, d_ff), BF16)],
        compiler_params=_params("parallel"),
        name="ffn_final" if final else "ffn",
    )(*args)


def _norm_proj_body(x_ref, g_ref, *refs, n):
    h = _rms(x_ref[...], g_ref[...]).astype(BF16)
    for w_ref, o_ref in zip(refs[:n], refs[n:]):
        o_ref[...] = _dot(h, w_ref[...])


def _norm_proj(x, g, ws, *, tm):
    t, d = x.shape
    n = len(ws)
    in_specs = [pl.BlockSpec((tm, d), lambda i: (i, 0)), _resident((1, d))]
    in_specs += [_resident(w.shape) for w in ws]
    return pl.pallas_call(
        functools.partial(_norm_proj_body, n=n),
        grid=(t // tm,),
        in_specs=in_specs,
        out_specs=[pl.BlockSpec((tm, w.shape[1]), lambda i: (i, 0)) for w in ws],
        out_shape=[jax.ShapeDtypeStruct((t, w.shape[1]), F32) for w in ws],
        compiler_params=_params("parallel"),
        name="norm_proj",
    )(x, g.reshape(1, d), *[w.astype(BF16) for w in ws])


def _fox_split(w_in, b_f, heads, head_dim):
    w = heads * head_dim
    wq, wk, wv = w_in[:, :w], w_in[:, w:2 * w], w_in[:, 2 * w:3 * w]
    wf = jnp.pad(w_in[:, 3 * w:3 * w + heads], ((0, 0), (0, LANES - heads)))
    wc = w_in[:, 3 * w + heads:]
    bf = jnp.pad(b_f, (0, LANES - heads)).reshape(1, LANES)
    return wq, wk, wv, wf, wc, bf


def _fox_proj_sample_body(x_ref, g_ref, wq_ref, wk_ref, wv_ref, wc_ref, wf_ref, wft_ref, bf_ref, bft_ref,
                          q_ref, k_ref, v_ref, qc_ref, lf_ref, lft_ref, *, scale):
    h = _rms(x_ref[...], g_ref[...]).astype(BF16)
    q_ref[...] = _dot(h, wq_ref[...]) * scale
    k_ref[...] = _dot(h, wk_ref[...])
    v_ref[...] = _dot(h, wv_ref[...])
    qc_ref[...] = _dot(h, wc_ref[...])
    lf_ref[...] = _log_sigmoid(_dot(h, wf_ref[...]) + bf_ref[...])
    lft_ref[...] = _log_sigmoid(_dot_nt(wft_ref[...], h) + bft_ref[...])


def _fox_proj_sample(x, g, w_in, b_f, *, heads, head_dim, cross_width, tm):
    t, d = x.shape
    w = heads * head_dim
    hp = 2 * SUBLANES
    wq, wk, wv, wf, wc, bf = _fox_split(w_in, b_f, heads, head_dim)
    wft = wf[:, :hp].T
    bft = bf[0, :hp].reshape(hp, 1)
    row = lambda n: pl.BlockSpec((tm, n), lambda i: (i, 0))
    outs = [w, w, w, cross_width, LANES]
    return pl.pallas_call(
        functools.partial(_fox_proj_sample_body, scale=head_dim ** -0.5),
        grid=(t // tm,),
        in_specs=[row(d), _resident((1, d)), _resident((d, w)), _resident((d, w)),
                  _resident((d, w)), _resident((d, cross_width)), _resident((d, LANES)),
                  _resident((hp, d)), _resident((1, LANES)), _resident((hp, 1))],
        out_specs=[row(n) for n in outs] + [pl.BlockSpec((hp, tm), lambda i: (0, i))],
        out_shape=[jax.ShapeDtypeStruct((t, n), F32) for n in outs]
        + [jax.ShapeDtypeStruct((hp, t), F32)],
        compiler_params=_params("parallel"),
        name="fox_proj_sample",
    )(x, g.reshape(1, d), wq.astype(BF16), wk.astype(BF16), wv.astype(BF16), wc.astype(BF16),
      wf.astype(BF16), wft.astype(BF16), bf, bft)


N_BIAS = 3


def _fox_proj_prompt_body(x_ref, g_ref, wqa_ref, wk_ref, wka_ref, wv_ref, wvat_ref, wc_ref, wf_ref,
                          bf_ref, place_ref, oneq_ref, onev_ref,
                          k_ref, v_ref, qc_ref, lf_ref, qa_ref, ka_ref, vta_ref, carry_ref,
                          *, scale, tiles_per_seq):
    @pl.when(pl.program_id(0) % tiles_per_seq == 0)
    def _():
        carry_ref[...] = jnp.zeros_like(carry_ref)

    h = _rms(x_ref[...], g_ref[...]).astype(BF16)
    qa_ref[...] = (_dot(h, wqa_ref[...]) * scale + oneq_ref[...]).astype(BF16)
    k_ref[...] = _dot(h, wk_ref[...])
    v_ref[...] = _dot(h, wv_ref[...])
    qc_ref[...] = _dot(h, wc_ref[...])
    lf = _log_sigmoid(_dot(h, wf_ref[...]) + bf_ref[...])
    lf_ref[...] = lf
    tm = lf.shape[0]
    r = lax.broadcasted_iota(jnp.int32, (tm, tm), 0)
    c = lax.broadcasted_iota(jnp.int32, (tm, tm), 1)
    low = jnp.where(c <= r, 1.0, 0.0).astype(BF16)
    cum = carry_ref[0:1, :] + sum(_dot(low, piece) for piece in _split3(lf))
    carry_ref[...] = jnp.broadcast_to(cum[tm - 1:, :], carry_ref.shape)
    bias = sum(_dot(piece, place_ref[j]) for j, piece in enumerate(_split3(-cum)))
    ka_ref[...] = (_dot(h, wka_ref[...]) + bias).astype(BF16)
    vta_ref[...] = (_dot_nt(wvat_ref[...], h) + onev_ref[...]).astype(BF16)


def _fox_proj_prompt(x, g, w_in, b_f, *, heads, head_dim, cross_width, tm, tiles_per_seq):
    t, d = x.shape
    w = heads * head_dim
    wa = heads * LANES
    wq, wk, wv, wf, wc, bf = _fox_split(w_in, b_f, heads, head_dim)
    aug = lambda a: jnp.pad(a.reshape(d, heads, head_dim), ((0, 0), (0, 0), (0, LANES - head_dim))).reshape(d, wa)
    head = jnp.arange(heads)
    place = jnp.zeros((N_BIAS, LANES, wa), F32)
    oneq = jnp.zeros((1, wa), F32)
    for j in range(N_BIAS):
        place = place.at[j, head, head * LANES + head_dim + j].set(1.0)
        oneq = oneq.at[0, head * LANES + head_dim + j].set(1.0)
    onev = jnp.zeros((wa, 1), F32).at[head * LANES + head_dim, 0].set(1.0)
    row = lambda n: pl.BlockSpec((tm, n), lambda i: (i, 0))
    outs = [(w, F32), (w, F32), (cross_width, F32), (LANES, F32), (wa, BF16), (wa, BF16)]
    return pl.pallas_call(
        functools.partial(_fox_proj_prompt_body, scale=head_dim ** -0.5, tiles_per_seq=tiles_per_seq),
        grid=(t // tm,),
        in_specs=[row(d), _resident((1, d)), _resident((d, wa)), _resident((d, w)), _resident((d, wa)),
                  _resident((d, w)), _resident((wa, d)), _resident((d, cross_width)),
                  _resident((d, LANES)), _resident((1, LANES)), _resident(place.shape),
                  _resident((1, wa)), _resident((wa, 1))],
        out_specs=[row(n) for n, _ in outs] + [pl.BlockSpec((wa, tm), lambda i: (0, i))],
        out_shape=[jax.ShapeDtypeStruct((t, n), dt) for n, dt in outs]
        + [jax.ShapeDtypeStruct((wa, t), BF16)],
        scratch_shapes=[pltpu.VMEM((SUBLANES, LANES), F32)],
        compiler_params=_params("arbitrary"),
        name="fox_proj_prompt",
    )(x, g.reshape(1, d), aug(wq).astype(BF16), wk.astype(BF16), aug(wk).astype(BF16),
      wv.astype(BF16), aug(wv).T.astype(BF16), wc.astype(BF16), wf.astype(BF16), bf,
      place.astype(BF16), oneq, onev)


def _s5_param_body(are_ref, aim_ref, ldt_ref, bre_ref, bim_ref,
                   abr_ref, abi_ref, bbr_ref, bbi_ref, pwr_ref, pwi_ref):
    a_re = are_ref[...]
    a_im = aim_ref[...]
    dt = jnp.exp(ldt_ref[...])
    mag = jnp.exp(dt * a_re)
    ab_re = mag * jnp.cos(dt * a_im)
    ab_im = mag * jnp.sin(dt * a_im)
    den = a_re * a_re + a_im * a_im
    nr = ab_re - 1.0
    ni = ab_im
    zr = (nr * a_re + ni * a_im) / den
    zi = (ni * a_re - nr * a_im) / den
    b_re = bre_ref[...]
    b_im = bim_ref[...]
    abr_ref[...] = ab_re
    abi_ref[...] = ab_im
    bbr_ref[...] = zr * b_re - zi * b_im
    bbi_ref[...] = zr * b_im + zi * b_re
    pr, pi = ab_re, ab_im
    for r in range(SUBLANES):
        pwr_ref[r] = pr
        pwi_ref[r] = pi
        pr, pi = pr * ab_re - pi * ab_im, pr * ab_im + pi * ab_re


def _s5_params(a_re, a_im, log_dt, b_re, b_im, c_re, c_im):
    g, p, c = b_re.shape
    rows = g * c
    rep = lambda a: jnp.repeat(a, c, axis=0)
    bt = lambda b: b.transpose(0, 2, 1).reshape(rows, p)
    full = pl.BlockSpec((rows, p), lambda: (0, 0))
    pw_spec = pl.BlockSpec((SUBLANES, rows, p), lambda: (0, 0, 0))
    abr, abi, bbr, bbi, pwr, pwi = pl.pallas_call(
        _s5_param_body,
        in_specs=[full, full, pl.BlockSpec((rows, 1), lambda: (0, 0)), full, full],
        out_specs=[full, full, full, full, pw_spec, pw_spec],
        out_shape=[jax.ShapeDtypeStruct((rows, p), F32)] * 4
        + [jax.ShapeDtypeStruct((SUBLANES, rows, p), F32)] * 2,
        name="s5_params",
    )(rep(a_re), rep(a_im), rep(log_dt.reshape(g, 1)), bt(b_re), bt(b_im))
    width = g * p
    flat = lambda a: a[::c].reshape(1, width)
    flat_pw = lambda a: a[:, ::c].reshape(SUBLANES, width)
    gpt = MXU_DIM // c
    n_tiles = g // gpt
    eye = jnp.eye(gpt, dtype=F32)

    def in_tiles(bb):
        blocks = bb.reshape(n_tiles, gpt, c, p)
        return jnp.einsum("tgcp,gh->tgchp", blocks, eye).reshape(n_tiles, gpt * c, gpt * p)

    def out_tiles(cc):
        blocks = cc.reshape(n_tiles, gpt, c, p)
        return jnp.einsum("tgcp,gh->tgphc", blocks, eye).reshape(n_tiles, gpt * p, gpt * c)

    w_in = jnp.concatenate([in_tiles(bbr), in_tiles(bbi)], axis=-1).astype(BF16)
    return (flat(abr), flat(abi), flat_pw(pwr), flat_pw(pwi), w_in,
            out_tiles(c_re).astype(BF16), out_tiles(c_im).astype(BF16))


def _s5_in_proj(u, wb_ref, bre_ref, bim_ref):
    ub = u.astype(BF16)
    n_tiles, ch, two_w = wb_ref.shape
    w = two_w // 2
    for t in range(n_tiles):
        bu = _dot(ub[:, t * ch:(t + 1) * ch], wb_ref[t])
        bre_ref[:, t * w:(t + 1) * w] = bu[:, :w]
        bim_ref[:, t * w:(t + 1) * w] = bu[:, w:]


def _s5_out(u, hre_ref, him_ref, wcr_ref, wci_ref, d_ref, wglu_ref):
    n_tiles, w, _ = wcr_ref.shape
    ys = []
    for t in range(n_tiles):
        hr = hre_ref[:, t * w:(t + 1) * w].astype(BF16)
        hi = him_ref[:, t * w:(t + 1) * w].astype(BF16)
        ys.append(_dot(hr, wcr_ref[t]) - _dot(hi, wci_ref[t]))
    y = jax.nn.gelu(jnp.concatenate(ys, axis=-1) + d_ref[...] * u)
    z = _dot(y.astype(BF16), wglu_ref[...])
    half = z.shape[-1] // 2
    return z[:, :half] * jax.nn.sigmoid(z[:, half:])


def _s5_prompt_body(u_ref, wb_ref, wcr_ref, wci_ref, d_ref, wglu_ref, tab_ref,
                    o_ref, hre_ref, him_ref, bre_ref, bim_ref, cre_ref, cim_ref, *, lane_group):
    ci = pl.program_id(1)
    rows, width = bre_ref.shape

    @pl.when(ci == 0)
    def _():
        cre_ref[...] = jnp.zeros_like(cre_ref)
        cim_ref[...] = jnp.zeros_like(cim_ref)

    u = u_ref[0]
    _s5_in_proj(u, wb_ref, bre_ref, bim_ref)

    for lg in range(width // lane_group):
        ls = slice(lg * lane_group, (lg + 1) * lane_group)

        def tile(i, carry, ls=ls):
            cr, cim = carry
            r0 = pl.multiple_of(i * SUBLANES, SUBLANES)
            xr = bre_ref[pl.ds(r0, SUBLANES), ls]
            xi = bim_ref[pl.ds(r0, SUBLANES), ls]
            for j, shift in enumerate((1, 2, 4)):
                rr = pltpu.roll(xr, shift, 0)
                ri = pltpu.roll(xi, shift, 0)
                ar = tab_ref[2 * j, :, ls]
                ai = tab_ref[2 * j + 1, :, ls]
                xr, xi = xr + ar * rr - ai * ri, xi + ar * ri + ai * rr
            pr = tab_ref[6, :, ls]
            pi = tab_ref[7, :, ls]
            xr, xi = xr + pr * cr - pi * cim, xi + pr * cim + pi * cr
            bre_ref[pl.ds(r0, SUBLANES), ls] = xr
            bim_ref[pl.ds(r0, SUBLANES), ls] = xi
            last = SUBLANES - 1
            return (jnp.broadcast_to(xr[last:, :], xr.shape), jnp.broadcast_to(xi[last:, :], xi.shape))

        cr, cim = lax.fori_loop(0, rows // SUBLANES, tile, (cre_ref[:, ls], cim_ref[:, ls]))
        cre_ref[:, ls] = cr
        cim_ref[:, ls] = cim

    o_ref[0] = _s5_out(u, bre_ref, bim_ref, wcr_ref, wci_ref, d_ref, wglu_ref)

    @pl.when(ci == pl.num_programs(1) - 1)
    def _():
        hre_ref[0] = bre_ref[rows - 1:rows, :]
        him_ref[0] = bim_ref[rows - 1:rows, :]


def _s5_prompt(u, params, d_skip, w_glu, *, rows):
    ab_re, ab_im, pw_re, pw_im, w_in, wc_re, wc_im = params
    nb, length, ch = u.shape
    width = ab_re.shape[1]
    row_idx = jnp.arange(SUBLANES)[:, None]
    tabs = []
    for shift in (1, 2, 4):
        keep = row_idx >= shift
        tabs += [jnp.where(keep, pw_re[shift - 1][None, :], 0.0),
                 jnp.where(keep, pw_im[shift - 1][None, :], 0.0)]
    tab = jnp.stack(tabs + [pw_re, pw_im])
    state = jax.ShapeDtypeStruct((nb, 1, width), F32)
    state_spec = pl.BlockSpec((1, 1, width), lambda b, c: (b, 0, 0))
    tok_spec = pl.BlockSpec((1, rows, ch), lambda b, c: (b, c, 0))
    return pl.pallas_call(
        functools.partial(_s5_prompt_body, lane_group=4 * LANES),
        grid=(nb, length // rows),
        in_specs=[tok_spec, _resident(w_in.shape), _resident(wc_re.shape), _resident(wc_im.shape),
                  _resident((1, ch)), _resident(w_glu.shape), _resident(tab.shape)],
        out_specs=[tok_spec, state_spec, state_spec],
        out_shape=[jax.ShapeDtypeStruct((nb, length, ch), F32), state, state],
        scratch_shapes=[pltpu.VMEM((rows, width), F32), pltpu.VMEM((rows, width), F32),
                        pltpu.VMEM((SUBLANES, width), F32), pltpu.VMEM((SUBLANES, width), F32)],
        compiler_params=_params("parallel", "arbitrary"),
        name="s5_prompt",
    )(u, w_in, wc_re, wc_im, d_skip.reshape(1, ch), w_glu.astype(BF16), tab)


def _s5_sample_body(u_ref, h0r_ref, h0i_ref, wb_ref, wcr_ref, wci_ref, d_ref, wglu_ref, ab_ref,
                    o_ref, hre_ref, him_ref, bre_ref, bim_ref):
    hre_ref[...] = h0r_ref[...]
    him_ref[...] = h0i_ref[...]
    ar = ab_ref[0:1, :]
    ai = ab_ref[1:2, :]
    for t in range(u_ref.shape[0]):
        u = u_ref[t]
        _s5_in_proj(u, wb_ref, bre_ref, bim_ref)
        hr = hre_ref[...]
        hi = him_ref[...]
        hre_ref[...] = ar * hr - ai * hi + bre_ref[...]
        him_ref[...] = ar * hi + ai * hr + bim_ref[...]
        o_ref[t] = _s5_out(u, hre_ref, him_ref, wcr_ref, wci_ref, d_ref, wglu_ref)


def _s5_sample(u, h0_re, h0_im, params, d_skip, w_glu):
    ab_re, ab_im, _, _, w_in, wc_re, wc_im = params
    nt, nb, ch = u.shape
    width = ab_re.shape[1]
    ab = jnp.concatenate([ab_re, ab_im], axis=0)
    whole = lambda shape: pl.BlockSpec(shape, lambda: (0,) * len(shape))
    state = jax.ShapeDtypeStruct((nb, width), F32)
    return pl.pallas_call(
        _s5_sample_body,
        in_specs=[whole(u.shape), whole((nb, width)), whole((nb, width)), whole(w_in.shape),
                  whole(wc_re.shape), whole(wc_im.shape), whole((1, ch)), whole(w_glu.shape),
                  whole(ab.shape)],
        out_specs=[whole(u.shape), whole((nb, width)), whole((nb, width))],
        out_shape=[jax.ShapeDtypeStruct(u.shape, F32), state, state],
        scratch_shapes=[pltpu.VMEM((nb, width), F32), pltpu.VMEM((nb, width), F32)],
        compiler_params=pltpu.CompilerParams(vmem_limit_bytes=VMEM_LIMIT_BYTES),
        name="s5_sample",
    )(u, h0_re, h0_im, w_in, wc_re, wc_im, d_skip.reshape(1, ch), w_glu.astype(BF16), ab)


HEADS_PER_STEP = 2


def _fox_prompt_body(q_ref, k_ref, vt_ref, o_ref, m_ref, acc_ref, *, head_dim, q_chunk):
    qi = pl.program_id(2)
    tq = q_ref.shape[1]
    m_ref[...] = jnp.full_like(m_ref, NEG_INF)
    acc_ref[...] = jnp.zeros_like(acc_ref)

    def block(kj, diagonal):
        ks = pl.multiple_of(kj * tq, tq)
        chains = [(h, slice(h * LANES, (h + 1) * LANES), slice(c * q_chunk, (c + 1) * q_chunk))
                  for h in range(HEADS_PER_STEP) for c in range(tq // q_chunk)]
        scores = [_dot_nt(k_ref[0, pl.ds(ks, tq), hl], q_ref[0, qs, hl]) for _, hl, qs in chains]
        if diagonal:
            key = lax.broadcasted_iota(jnp.int32, (tq, q_chunk), 0)
            qry = lax.broadcasted_iota(jnp.int32, (tq, q_chunk), 1)
            scores = [jnp.where(key <= qry + qs.start, s, NEG_INF) for s, (_, _, qs) in zip(scores, chains)]
        probs, alphas = [], []
        for s, (h, _, qs) in zip(scores, chains):
            m_old = m_ref[h, :, qs]
            m_new = jnp.maximum(m_old, jnp.max(s, axis=0, keepdims=True))
            alphas.append(jnp.exp(m_old - m_new))
            probs.append(jnp.exp(s - m_new).astype(BF16))
            m_ref[h, :, qs] = m_new
        for p, alpha, (h, hl, qs) in zip(probs, alphas, chains):
            acc_ref[h, :, qs] = alpha * acc_ref[h, :, qs] + _dot(vt_ref[hl, pl.ds(ks, tq)], p)

    def body(kj, carry):
        block(kj, False)
        return carry

    lax.fori_loop(0, qi, body, 0)
    block(qi, True)
    outs = [acc_ref[h, :head_dim, :] / acc_ref[h, head_dim:head_dim + 1, :] for h in range(HEADS_PER_STEP)]
    o_ref[0] = jnp.concatenate(outs, axis=0).T


def _fox_prompt(qa, ka, vta, *, heads, head_dim, tq):
    nb, length, _ = qa.shape
    wide = HEADS_PER_STEP * LANES
    return pl.pallas_call(
        functools.partial(_fox_prompt_body, head_dim=head_dim, q_chunk=MXU_DIM),
        grid=(nb, heads // HEADS_PER_STEP, length // tq),
        in_specs=[pl.BlockSpec((1, tq, wide), lambda b, hp, i: (b, i, hp)),
                  pl.BlockSpec((1, length, wide), lambda b, hp, i: (b, 0, hp)),
                  pl.BlockSpec((wide, length), lambda b, hp, i: (hp, b))],
        out_specs=pl.BlockSpec((1, tq, HEADS_PER_STEP * head_dim), lambda b, hp, i: (b, i, hp)),
        out_shape=jax.ShapeDtypeStruct((nb, length, heads * head_dim), F32),
        scratch_shapes=[pltpu.VMEM((HEADS_PER_STEP, 1, tq), F32),
                        pltpu.VMEM((HEADS_PER_STEP, LANES, tq), F32)],
        compiler_params=_params("parallel", "parallel", "arbitrary"),
        name="fox_prompt",
    )(qa, ka, vta)


def _fox_sample_body(pt_ref, q_ref, *refs, pages_per_step, heads, head_dim):
    n = pages_per_step
    k_refs, v_refs, lf_refs = refs[:n], refs[n:2 * n], refs[2 * n:3 * n]
    kn_ref, vn_ref, lfn_ref, o_ref, qrow_ref, m_ref, l_ref, acc_ref, carry_ref = refs[3 * n:]
    del pt_ref
    g = pl.program_id(1)
    n_tok, w = q_ref.shape
    hp = lf_refs[0].shape[0]
    page = k_refs[0].shape[0]
    head_of_lane = lax.broadcasted_iota(jnp.int32, (hp, w), 1) // head_dim
    head_mask = head_of_lane == lax.broadcasted_iota(jnp.int32, (hp, w), 0)
    tri = _upper_tri()

    @pl.when(g == 0)
    def _():
        zero = jnp.zeros((hp, w), F32)
        rows = [jnp.where(head_mask, jnp.broadcast_to(q_ref[t:t + 1, :], (hp, w)), zero)
                for t in range(n_tok)]
        qrow_ref[...] = jnp.concatenate(rows, axis=0).astype(BF16)
        m_ref[...] = jnp.full_like(m_ref, NEG_INF)
        l_ref[...] = jnp.zeros_like(l_ref)
        acc_ref[...] = jnp.zeros_like(acc_ref)
        carry_ref[...] = jnp.zeros_like(carry_ref)

    def update(kb, vb, lf, valid=None):
        c = _cumsum_lanes(lf, tri) + carry_ref[...]
        carry_ref[...] = jnp.broadcast_to(c[:, page - 1:], c.shape)
        s = _dot_nt(qrow_ref[...], kb) - jnp.concatenate([c] * n_tok, axis=0)
        if valid is not None:
            s = jnp.where(valid, s, NEG_INF)
        m_old = m_ref[...]
        m_new = jnp.maximum(m_old, jnp.max(s, axis=1, keepdims=True))
        alpha = jnp.exp(m_old - m_new)
        p = jnp.exp(s - m_new)
        l_ref[...] = alpha * l_ref[...] + jnp.sum(p, axis=1, keepdims=True)
        acc_ref[...] = alpha * acc_ref[...] + _dot(p.astype(BF16), vb)
        m_ref[...] = m_new

    for i in range(n):
        update(k_refs[i][...].astype(BF16), v_refs[i][...].astype(BF16), lf_refs[i][...])

    @pl.when(g == pl.num_programs(1) - 1)
    def _():
        pad = jnp.zeros((page - kn_ref.shape[0], w), F32)
        kn = jnp.concatenate([kn_ref[...], pad], axis=0).astype(BF16)
        vn = jnp.concatenate([vn_ref[...], pad], axis=0).astype(BF16)
        key = lax.broadcasted_iota(jnp.int32, (n_tok * hp, page), 1)
        tok = lax.broadcasted_iota(jnp.int32, (n_tok * hp, page), 0) // hp
        update(kn, vn, lfn_ref[...], key <= tok)
        out = acc_ref[...] / l_ref[...]
        for t in range(n_tok):
            picked = jnp.where(head_mask, out[t * hp:(t + 1) * hp, :], 0.0)
            o_ref[t:t + 1, :] = jnp.sum(picked, axis=0, keepdims=True)


def _fox_sample(q, k_new, v_new, lft_new, cache_k, cache_v, cache_lft, page_table,
                *, heads, head_dim, pages_per_step):
    ns, n_tok, w = q.shape
    n_pages = page_table.shape[1]
    page = cache_k.shape[1]
    hp = cache_lft.shape[1]
    n = pages_per_step
    pad_rows = lambda a: jnp.pad(a, ((0, 0), (0, SUBLANES - n_tok), (0, 0)))

    def paged(shape, i):
        return pl.BlockSpec((None,) + shape, lambda s, g, pt: (pt[s, g * n + i], 0, 0))

    per_seq = lambda shape: pl.BlockSpec((None,) + shape, lambda s, g, pt: (s, 0, 0))
    in_specs = [per_seq((n_tok, w))]
    in_specs += [paged((page, w), i) for i in range(n)]
    in_specs += [paged((page, w), i) for i in range(n)]
    in_specs += [paged((hp, page), i) for i in range(n)]
    in_specs += [per_seq((SUBLANES, w)), per_seq((SUBLANES, w)), per_seq((hp, page))]
    rows = n_tok * hp
    return pl.pallas_call(
        functools.partial(_fox_sample_body, pages_per_step=n, heads=heads, head_dim=head_dim),
        grid_spec=pltpu.PrefetchScalarGridSpec(
            num_scalar_prefetch=1,
            grid=(ns, n_pages // n),
            in_specs=in_specs,
            out_specs=per_seq((n_tok, w)),
            scratch_shapes=[pltpu.VMEM((rows, w), BF16), pltpu.VMEM((rows, 1), F32),
                            pltpu.VMEM((rows, 1), F32), pltpu.VMEM((rows, w), F32),
                            pltpu.VMEM((hp, page), F32)],
        ),
        out_shape=jax.ShapeDtypeStruct((ns, n_tok, w), F32),
        compiler_params=_params("parallel", "arbitrary"),
        name="fox_sample",
    )(page_table, q, *([cache_k] * n), *([cache_v] * n), *([cache_lft] * n),
      pad_rows(k_new), pad_rows(v_new), lft_new)


def _softmax_pv(s, vb):
    m = jnp.max(s, axis=-1, keepdims=True)
    p = jnp.exp(s - m)
    return _dot(p.astype(BF16), vb) / jnp.sum(p, axis=-1, keepdims=True)


def _cross_out_prompt_body(x_ref, mix_ref, qc_ref, mk_ref, mv_ref, wom_ref, woc_ref, o_ref,
                           *, heads, head_dim):
    tm, cw = qc_ref.shape
    q = qc_ref[...] * head_dim ** -0.5
    head_of_lane = lax.broadcasted_iota(jnp.int32, (1, cw), 1) // head_dim
    zero = jnp.zeros_like(q)
    q4 = jnp.concatenate([jnp.where(head_of_lane == h, q, zero) for h in range(heads)], axis=0)
    s = _dot_nt(q4.astype(BF16), mk_ref[0].astype(BF16))
    o4 = _softmax_pv(s, mv_ref[0].astype(BF16))
    cross = zero
    for h in range(heads):
        cross = cross + jnp.where(head_of_lane == h, o4[h * tm:(h + 1) * tm, :], zero)
    o_ref[...] = (x_ref[...] + _dot(mix_ref[...].astype(BF16), wom_ref[...])
                  + _dot(cross.astype(BF16), woc_ref[...]))


def _cross_out_prompt(x, mix, qc, mk, mv, wom, woc, *, heads, head_dim, tm):
    t, d = x.shape
    nb, n_mem, cw = mk.shape
    per_b = t // nb // tm
    row = lambda n: pl.BlockSpec((tm, n), lambda i: (i, 0))
    mem = pl.BlockSpec((1, n_mem, cw), lambda i: (i // per_b, 0, 0))
    return pl.pallas_call(
        functools.partial(_cross_out_prompt_body, heads=heads, head_dim=head_dim),
        grid=(t // tm,),
        in_specs=[row(d), row(mix.shape[1]), row(cw), mem, mem, _resident(wom.shape),
                  _resident(woc.shape)],
        out_specs=row(d),
        out_shape=jax.ShapeDtypeStruct((t, d), F32),
        compiler_params=_params("parallel"),
        name="cross_out_prompt",
    )(x, mix, qc, mk, mv, wom, woc)


def _cross_out_sample_body(x_ref, mix_ref, qc_ref, mk_ref, mv_ref, wom_ref, woc_ref, o_ref, cross_ref,
                           *, heads, head_dim, n_tok):
    n_seq, _, cw = mk_ref.shape
    scale = head_dim ** -0.5
    head_of_lane = lax.broadcasted_iota(jnp.int32, (SUBLANES, cw), 1) // head_dim
    head_mask = head_of_lane == lax.broadcasted_iota(jnp.int32, (SUBLANES, cw), 0)
    zero = jnp.zeros((SUBLANES, cw), F32)
    for i in range(n_seq):
        rows = [jnp.where(head_mask, jnp.broadcast_to(qc_ref[pl.ds(i * n_tok + t, 1), :] * scale,
                                                     (SUBLANES, cw)), zero) for t in range(n_tok)]
        q = jnp.concatenate(rows, axis=0).astype(BF16)
        o = _softmax_pv(_dot_nt(q, mk_ref[i].astype(BF16)), mv_ref[i].astype(BF16))
        for t in range(n_tok):
            picked = jnp.where(head_mask, o[t * SUBLANES:(t + 1) * SUBLANES, :], zero)
            cross_ref[pl.ds(i * n_tok + t, 1), :] = jnp.sum(picked, axis=0, keepdims=True)
    o_ref[...] = (x_ref[...] + _dot(mix_ref[...].astype(BF16), wom_ref[...])
                  + _dot(cross_ref[...].astype(BF16), woc_ref[...]))


def _cross_out_sample(x, mix, qc, mk, mv, wom, woc, *, heads, head_dim, n_tok, seqs_per_step):
    t, d = x.shape
    ns, n_mem, cw = mk.shape
    tm = seqs_per_step * n_tok
    row = lambda n: pl.BlockSpec((tm, n), lambda i: (i, 0))
    mem = pl.BlockSpec((seqs_per_step, n_mem, cw), lambda i: (i, 0, 0))
    return pl.pallas_call(
        functools.partial(_cross_out_sample_body, heads=heads, head_dim=head_dim, n_tok=n_tok),
        grid=(ns // seqs_per_step,),
        in_specs=[row(d), row(mix.shape[1]), row(cw), mem, mem, _resident(wom.shape),
                  _resident(woc.shape)],
        out_specs=row(d),
        out_shape=jax.ShapeDtypeStruct((t, d), F32),
        scratch_shapes=[pltpu.VMEM((tm, cw), F32)],
        compiler_params=_params("parallel"),
        name="cross_out_sample",
    )(x, mix, qc, mk, mv, wom, woc)


def kernel(x_prompt, x_sample, mem_prompt, state_s5_re, state_s5_im, cache_fox_k, cache_fox_v, cache_fox_logf, cache_mem_k, cache_mem_v, page_table, ffn_norm, ffn_w_gate, ffn_w_up, ffn_w_down, norm_mix, norm_mem, w_mem_kv, w_in_s5, s5_a_re, s5_a_im, s5_log_dt, s5_b_re, s5_b_im, s5_c_re, s5_c_im, s5_d, s5_w_glu, w_in_fox, fox_b_f, w_out, norm_final):
    nb, seq, d = x_prompt.shape
    ns, n_tok, _ = x_sample.shape
    depth = ffn_norm.shape[0]
    n_mem = mem_prompt.shape[1]
    cross_heads, head_dim = cache_mem_k.shape[3], cache_mem_k.shape[4]
    cross_width = cross_heads * head_dim
    mixer_width = w_out.shape[1] - cross_width
    fox_heads = cache_fox_k.shape[3]
    n_phys, page = cache_fox_k.shape[1], cache_fox_k.shape[2]
    groups, n_state = state_s5_re.shape[2], state_s5_re.shape[3]
    hp = 2 * SUBLANES

    tm_p = 512
    tm_s = ns * n_tok
    xp = x_prompt.reshape(nb * seq, d)
    xs = x_sample.reshape(tm_s, d)
    mem = mem_prompt.reshape(nb * n_mem, d)

    s5_re_p, s5_im_p, s5_re_s, s5_im_s = [], [], [], []
    fk_p, fv_p, fl_p, fk_s, fv_s, fl_s = [], [], [], [], [], []
    mk_list, mv_list = [], []
    for i in range(depth):
        j = i // N_MIXERS
        ffn_w = lambda half: (ffn_norm[i, half], ffn_w_gate[i, half], ffn_w_up[i, half], ffn_w_down[i, half])
        xp = _ffn(xp, *ffn_w(0), tm=tm_p)
        xs = _ffn(xs, *ffn_w(0), tm=tm_s)

        mk_p, mv_p = _norm_proj(mem, norm_mem[i],
                                [w_mem_kv[i][:, :cross_width], w_mem_kv[i][:, cross_width:]], tm=nb * n_mem)
        mk_list.append(mk_p.reshape(nb, n_mem, cross_heads, head_dim))
        mv_list.append(mv_p.reshape(nb, n_mem, cross_heads, head_dim))
        mk_s = cache_mem_k[i].reshape(ns, n_mem, cross_width)
        mv_s = cache_mem_v[i].reshape(ns, n_mem, cross_width)

        if i % N_MIXERS == 0:
            w_in = [w_in_s5[j][:, :mixer_width], w_in_s5[j][:, mixer_width:]]
            u_p, qc_p = _norm_proj(xp, norm_mix[i], w_in, tm=tm_p)
            u_s, qc_s = _norm_proj(xs, norm_mix[i], w_in, tm=tm_s)
            params = _s5_params(s5_a_re[j], s5_a_im[j], s5_log_dt[j], s5_b_re[j], s5_b_im[j],
                                s5_c_re[j], s5_c_im[j])
            mix_p, hr_p, hi_p = _s5_prompt(u_p.reshape(nb, seq, mixer_width), params, s5_d[j],
                                           s5_w_glu[j], rows=tm_p)
            mix_p = mix_p.reshape(nb * seq, mixer_width)
            u_t = u_s.reshape(ns, n_tok, mixer_width).transpose(1, 0, 2)
            mix_t, hr_s, hi_s = _s5_sample(u_t, state_s5_re[j].reshape(ns, groups * n_state),
                                           state_s5_im[j].reshape(ns, groups * n_state),
                                           params, s5_d[j], s5_w_glu[j])
            mix_s = mix_t.transpose(1, 0, 2).reshape(tm_s, mixer_width)
            s5_re_p.append(hr_p.reshape(nb, groups, n_state))
            s5_im_p.append(hi_p.reshape(nb, groups, n_state))
            s5_re_s.append(hr_s.reshape(ns, groups, n_state))
            s5_im_s.append(hi_s.reshape(ns, groups, n_state))
        else:
            fox = dict(heads=fox_heads, head_dim=head_dim, cross_width=cross_width)
            k_p, v_p, qc_p, lf_p, qa_p, ka_p, vta_p = _fox_proj_prompt(
                xp, norm_mix[i], w_in_fox[j], fox_b_f[j], tm=tm_p, tiles_per_seq=seq // tm_p, **fox)
            q_s, k_s, v_s, qc_s, lf_s, lft_s = _fox_proj_sample(
                xs, norm_mix[i], w_in_fox[j], fox_b_f[j], tm=tm_s, **fox)

            seq3 = lambda a: a.reshape(nb, seq, a.shape[-1])
            mix_p = _fox_prompt(seq3(qa_p), seq3(ka_p), vta_p, heads=fox_heads, head_dim=head_dim, tq=tm_p)
            mix_p = mix_p.reshape(nb * seq, mixer_width)

            tok3 = lambda a: a.reshape(ns, n_tok, mixer_width)
            lft_new = jnp.pad(lft_s.reshape(hp, ns, n_tok).transpose(1, 0, 2),
                              ((0, 0), (0, 0), (0, page - n_tok)))
            cache_lft = jnp.pad(cache_fox_logf[j].transpose(0, 2, 1), ((0, 0), (0, hp - fox_heads), (0, 0)))
            mix_s = _fox_sample(tok3(q_s), tok3(k_s), tok3(v_s), lft_new,
                                cache_fox_k[j].reshape(n_phys, page, mixer_width),
                                cache_fox_v[j].reshape(n_phys, page, mixer_width),
                                cache_lft, page_table,
                                heads=fox_heads, head_dim=head_dim, pages_per_step=4)
            mix_s = mix_s.reshape(tm_s, mixer_width)

            fk_p.append(k_p.reshape(nb, seq, fox_heads, head_dim))
            fv_p.append(v_p.reshape(nb, seq, fox_heads, head_dim))
            fl_p.append(lf_p[:, :fox_heads].reshape(nb, seq, fox_heads))
            fk_s.append(k_s.reshape(ns, n_tok, fox_heads, head_dim))
            fv_s.append(v_s.reshape(ns, n_tok, fox_heads, head_dim))
            fl_s.append(lf_s[:, :fox_heads].reshape(ns, n_tok, fox_heads))

        wom = w_out[i][:mixer_width].astype(BF16)
        woc = w_out[i][mixer_width:].astype(BF16)
        xp = _cross_out_prompt(xp, mix_p, qc_p, mk_p.reshape(nb, n_mem, cross_width),
                               mv_p.reshape(nb, n_mem, cross_width), wom, woc,
                               heads=cross_heads, head_dim=head_dim, tm=tm_p)
        xs = _cross_out_sample(xs, mix_s, qc_s, mk_s, mv_s, wom, woc,
                               heads=cross_heads, head_dim=head_dim, n_tok=n_tok, seqs_per_step=8)

        g_final = norm_final if i == depth - 1 else None
        xp = _ffn(xp, *ffn_w(1), g_final, tm=tm_p)
        xs = _ffn(xs, *ffn_w(1), g_final, tm=tm_s)

    return (xp.reshape(nb, seq, d), xs.reshape(ns, n_tok, d),
            jnp.stack(s5_re_p), jnp.stack(s5_im_p), jnp.stack(s5_re_s), jnp.stack(s5_im_s),
            jnp.stack(fk_p), jnp.stack(fv_p), jnp.stack(fl_p),
            jnp.stack(fk_s), jnp.stack(fv_s), jnp.stack(fl_s),
            jnp.stack(mk_list), jnp.stack(mv_list))
```

```python
import functools

import jax
import jax.numpy as jnp
from jax import lax
from jax.experimental import pallas as pl
from jax.experimental.pallas import tpu as pltpu

F32 = jnp.float32
BF16 = jnp.bfloat16

RMS_EPS = 1e-6
NEG_INF = -1e30
N_MIXERS = 2

LANES = 128
SUBLANES = 8
MXU_DIM = 256
VMEM_LIMIT_BYTES = 56 * 1024 * 1024

NT_DIMS = (((1,), (1,)), ((), ()))


def _params(*sem):
    return pltpu.CompilerParams(dimension_semantics=sem, vmem_limit_bytes=VMEM_LIMIT_BYTES)


def _resident(shape):
    nd = len(shape)
    return pl.BlockSpec(shape, lambda *_: (0,) * nd, pipeline_mode=pl.Buffered(1))


def _rms(x, g):
    return x * lax.rsqrt(jnp.mean(x * x, axis=-1, keepdims=True) + RMS_EPS) * g


def _dot(a, b):
    return jnp.dot(a, b, preferred_element_type=F32)


def _dot_nt(a, b):
    return lax.dot_general(a, b, NT_DIMS, preferred_element_type=F32)


def _log_sigmoid(x):
    return jnp.minimum(x, 0.0) - jnp.log1p(jnp.exp(-jnp.abs(x)))


def _split3(x):
    hi = x.astype(BF16)
    r1 = x - hi.astype(F32)
    mid = r1.astype(BF16)
    lo = (r1 - mid.astype(F32)).astype(BF16)
    return hi, mid, lo


def _cumsum_lanes(x, tri):
    hi, mid, lo = _split3(x)
    return _dot(hi, tri) + _dot(mid, tri) + _dot(lo, tri)


def _upper_tri():
    r = lax.broadcasted_iota(jnp.int32, (LANES, LANES), 0)
    c = lax.broadcasted_iota(jnp.int32, (LANES, LANES), 1)
    return jnp.where(r <= c, 1.0, 0.0).astype(BF16)


def _ffn_body(*refs, d_ff, ff_chunk, final):
    if final:
        x_ref, g_ref, wg_ref, wu_ref, wd_ref, gf_ref, o_ref, t_ref = refs
    else:
        x_ref, g_ref, wg_ref, wu_ref, wd_ref, o_ref, t_ref = refs
    x = x_ref[...]
    h = _rms(x, g_ref[...]).astype(BF16)
    for c in range(d_ff // ff_chunk):
        lo, hi = c * ff_chunk, (c + 1) * ff_chunk
        a = _dot(h, wg_ref[:, lo:hi])
        b = _dot(h, wu_ref[:, lo:hi])
        t_ref[:, lo:hi] = (jax.nn.silu(a) * b).astype(BF16)
    y = x + 0.5 * _dot(t_ref[...], wd_ref[...])
    if final:
        y = _rms(y, gf_ref[...])
    o_ref[...] = y


def _ffn(x, g, wg, wu, wd, g_final=None, *, tm):
    t, d = x.shape
    d_ff = wg.shape[1]
    final = g_final is not None
    row = pl.BlockSpec((tm, d), lambda i: (i, 0))
    in_specs = [row, _resident((1, d)), _resident((d, d_ff)), _resident((d, d_ff)),
                _resident((d_ff, d))]
    args = [x, g.reshape(1, d), wg.astype(BF16), wu.astype(BF16), wd.astype(BF16)]
    if final:
        in_specs.append(_resident((1, d)))
        args.append(g_final.reshape(1, d))
    return pl.pallas_call(
        functools.partial(_ffn_body, d_ff=d_ff, ff_chunk=MXU_DIM, final=final),
        grid=(t // tm,),
        in_specs=in_specs,
        out_specs=row,
        out_shape=jax.ShapeDtypeStruct((t, d), F32),
        scratch_shapes=[pltpu.VMEM((tm, d_ff), BF16)],
        compiler_params=_params("parallel"),
        name="ffn_final" if final else "ffn",
    )(*args)


def _norm_proj_body(x_ref, g_ref, *refs, n):
    h = _rms(x_ref[...], g_ref[...]).astype(BF16)
    for w_ref, o_ref in zip(refs[:n], refs[n:]):
        o_ref[...] = _dot(h, w_ref[...])


def _norm_proj(x, g, ws, *, tm):
    t, d = x.shape
    n = len(ws)
    in_specs = [pl.BlockSpec((tm, d), lambda i: (i, 0)), _resident((1, d))]
    in_specs += [_resident(w.shape) for w in ws]
    return pl.pallas_call(
        functools.partial(_norm_proj_body, n=n),
        grid=(t // tm,),
        in_specs=in_specs,
        out_specs=[pl.BlockSpec((tm, w.shape[1]), lambda i: (i, 0)) for w in ws],
        out_shape=[jax.ShapeDtypeStruct((t, w.shape[1]), F32) for w in ws],
        compiler_params=_params("parallel"),
        name="norm_proj",
    )(x, g.reshape(1, d), *[w.astype(BF16) for w in ws])


def _fox_split(w_in, b_f, heads, head_dim):
    w = heads * head_dim
    wq, wk, wv = w_in[:, :w], w_in[:, w:2 * w], w_in[:, 2 * w:3 * w]
    wf = jnp.pad(w_in[:, 3 * w:3 * w + heads], ((0, 0), (0, LANES - heads)))
    wc = w_in[:, 3 * w + heads:]
    bf = jnp.pad(b_f, (0, LANES - heads)).reshape(1, LANES)
    return wq, wk, wv, wf, wc, bf


def _fox_proj_sample_body(x_ref, g_ref, wq_ref, wk_ref, wv_ref, wc_ref, wf_ref, wft_ref, bf_ref, bft_ref,
                          q_ref, k_ref, v_ref, qc_ref, lf_ref, lft_ref, *, scale):
    h = _rms(x_ref[...], g_ref[...]).astype(BF16)
    q_ref[...] = _dot(h, wq_ref[...]) * scale
    k_ref[...] = _dot(h, wk_ref[...])
    v_ref[...] = _dot(h, wv_ref[...])
    qc_ref[...] = _dot(h, wc_ref[...])
    lf_ref[...] = _log_sigmoid(_dot(h, wf_ref[...]) + bf_ref[...])
    lft_ref[...] = _log_sigmoid(_dot_nt(wft_ref[...], h) + bft_ref[...])


def _fox_proj_sample(x, g, w_in, b_f, *, heads, head_dim, cross_width, tm):
    t, d = x.shape
    w = heads * head_dim
    hp = 2 * SUBLANES
    wq, wk, wv, wf, wc, bf = _fox_split(w_in, b_f, heads, head_dim)
    wft = wf[:, :hp].T
    bft = bf[0, :hp].reshape(hp, 1)
    row = lambda n: pl.BlockSpec((tm, n), lambda i: (i, 0))
    outs = [w, w, w, cross_width, LANES]
    return pl.pallas_call(
        functools.partial(_fox_proj_sample_body, scale=head_dim ** -0.5),
        grid=(t // tm,),
        in_specs=[row(d), _resident((1, d)), _resident((d, w)), _resident((d, w)),
                  _resident((d, w)), _resident((d, cross_width)), _resident((d, LANES)),
                  _resident((hp, d)), _resident((1, LANES)), _resident((hp, 1))],
        out_specs=[row(n) for n in outs] + [pl.BlockSpec((hp, tm), lambda i: (0, i))],
        out_shape=[jax.ShapeDtypeStruct((t, n), F32) for n in outs]
        + [jax.ShapeDtypeStruct((hp, t), F32)],
        compiler_params=_params("parallel"),
        name="fox_proj_sample",
    )(x, g.reshape(1, d), wq.astype(BF16), wk.astype(BF16), wv.astype(BF16), wc.astype(BF16),
      wf.astype(BF16), wft.astype(BF16), bf, bft)


N_BIAS = 3


def _fox_proj_prompt_body(x_ref, g_ref, wqa_ref, wk_ref, wka_ref, wv_ref, wvat_ref, wc_ref, wf_ref,
                          bf_ref, place_ref, oneq_ref, onev_ref,
                          k_ref, v_ref, qc_ref, lf_ref, qa_ref, ka_ref, vta_ref, carry_ref,
                          *, scale, tiles_per_seq):
    @pl.when(pl.program_id(0) % tiles_per_seq == 0)
    def _():
        carry_ref[...] = jnp.zeros_like(carry_ref)

    h = _rms(x_ref[...], g_ref[...]).astype(BF16)
    qa_ref[...] = (_dot(h, wqa_ref[...]) * scale + oneq_ref[...]).astype(BF16)
    k_ref[...] = _dot(h, wk_ref[...])
    v_ref[...] = _dot(h, wv_ref[...])
    qc_ref[...] = _dot(h, wc_ref[...])
    lf = _log_sigmoid(_dot(h, wf_ref[...]) + bf_ref[...])
    lf_ref[...] = lf
    tm = lf.shape[0]
    r = lax.broadcasted_iota(jnp.int32, (tm, tm), 0)
    c = lax.broadcasted_iota(jnp.int32, (tm, tm), 1)
    low = jnp.where(c <= r, 1.0, 0.0).astype(BF16)
    cum = carry_ref[0:1, :] + sum(_dot(low, piece) for piece in _split3(lf))
    carry_ref[...] = jnp.broadcast_to(cum[tm - 1:, :], carry_ref.shape)
    bias = sum(_dot(piece, place_ref[j]) for j, piece in enumerate(_split3(-cum)))
    ka_ref[...] = (_dot(h, wka_ref[...]) + bias).astype(BF16)
    vta_ref[...] = (_dot_nt(wvat_ref[...], h) + onev_ref[...]).astype(BF16)


def _fox_proj_prompt(x, g, w_in, b_f, *, heads, head_dim, cross_width, tm, tiles_per_seq):
    t, d = x.shape
    w = heads * head_dim
    wa = heads * LANES
    wq, wk, wv, wf, wc, bf = _fox_split(w_in, b_f, heads, head_dim)
    aug = lambda a: jnp.pad(a.reshape(d, heads, head_dim), ((0, 0), (0, 0), (0, LANES - head_dim))).reshape(d, wa)
    head = jnp.arange(heads)
    place = jnp.zeros((N_BIAS, LANES, wa), F32)
    oneq = jnp.zeros((1, wa), F32)
    for j in range(N_BIAS):
        place = place.at[j, head, head * LANES + head_dim + j].set(1.0)
        oneq = oneq.at[0, head * LANES + head_dim + j].set(1.0)
    onev = jnp.zeros((wa, 1), F32).at[head * LANES + head_dim, 0].set(1.0)
    row = lambda n: pl.BlockSpec((tm, n), lambda i: (i, 0))
    outs = [(w, F32), (w, F32), (cross_width, F32), (LANES, F32), (wa, BF16), (wa, BF16)]
    return pl.pallas_call(
        functools.partial(_fox_proj_prompt_body, scale=head_dim ** -0.5, tiles_per_seq=tiles_per_seq),
        grid=(t // tm,),
        in_specs=[row(d), _resident((1, d)), _resident((d, wa)), _resident((d, w)), _resident((d, wa)),
                  _resident((d, w)), _resident((wa, d)), _resident((d, cross_width)),
                  _resident((d, LANES)), _resident((1, LANES)), _resident(place.shape),
                  _resident((1, wa)), _resident((wa, 1))],
        out_specs=[row(n) for n, _ in outs] + [pl.BlockSpec((wa, tm), lambda i: (0, i))],
        out_shape=[jax.ShapeDtypeStruct((t, n), dt) for n, dt in outs]
        + [jax.ShapeDtypeStruct((wa, t), BF16)],
        scratch_shapes=[pltpu.VMEM((SUBLANES, LANES), F32)],
        compiler_params=_params("arbitrary"),
        name="fox_proj_prompt",
    )(x, g.reshape(1, d), aug(wq).astype(BF16), wk.astype(BF16), aug(wk).astype(BF16),
      wv.astype(BF16), aug(wv).T.astype(BF16), wc.astype(BF16), wf.astype(BF16), bf,
      place.astype(BF16), oneq, onev)


def _s5_param_body(are_ref, aim_ref, ldt_ref, bre_ref, bim_ref,
                   abr_ref, abi_ref, bbr_ref, bbi_ref, pwr_ref, pwi_ref):
    a_re = are_ref[...]
    a_im = aim_ref[...]
    dt = jnp.exp(ldt_ref[...])
    mag = jnp.exp(dt * a_re)
    ab_re = mag * jnp.cos(dt * a_im)
    ab_im = mag * jnp.sin(dt * a_im)
    den = a_re * a_re + a_im * a_im
    nr = ab_re - 1.0
    ni = ab_im
    zr = (nr * a_re + ni * a_im) / den
    zi = (ni * a_re - nr * a_im) / den
    b_re = bre_ref[...]
    b_im = bim_ref[...]
    abr_ref[...] = ab_re
    abi_ref[...] = ab_im
    bbr_ref[...] = zr * b_re - zi * b_im
    bbi_ref[...] = zr * b_im + zi * b_re
    pr, pi = ab_re, ab_im
    for r in range(SUBLANES):
        pwr_ref[r] = pr
        pwi_ref[r] = pi
        pr, pi = pr * ab_re - pi * ab_im, pr * ab_im + pi * ab_re


def _s5_params(a_re, a_im, log_dt, b_re, b_im, c_re, c_im):
    g, p, c = b_re.shape
    rows = g * c
    rep = lambda a: jnp.repeat(a, c, axis=0)
    bt = lambda b: b.transpose(0, 2, 1).reshape(rows, p)
    full = pl.BlockSpec((rows, p), lambda: (0, 0))
    pw_spec = pl.BlockSpec((SUBLANES, rows, p), lambda: (0, 0, 0))
    abr, abi, bbr, bbi, pwr, pwi = pl.pallas_call(
        _s5_param_body,
        in_specs=[full, full, pl.BlockSpec((rows, 1), lambda: (0, 0)), full, full],
        out_specs=[full, full, full, full, pw_spec, pw_spec],
        out_shape=[jax.ShapeDtypeStruct((rows, p), F32)] * 4
        + [jax.ShapeDtypeStruct((SUBLANES, rows, p), F32)] * 2,
        name="s5_params",
    )(rep(a_re), rep(a_im), rep(log_dt.reshape(g, 1)), bt(b_re), bt(b_im))
    width = g * p
    flat = lambda a: a[::c].reshape(1, width)
    flat_pw = lambda a: a[:, ::c].reshape(SUBLANES, width)
    gpt = MXU_DIM // c
    n_tiles = g // gpt
    eye = jnp.eye(gpt, dtype=F32)

    def in_tiles(bb):
        blocks = bb.reshape(n_tiles, gpt, c, p)
        return jnp.einsum("tgcp,gh->tgchp", blocks, eye).reshape(n_tiles, gpt * c, gpt * p)

    def out_tiles(cc):
        blocks = cc.reshape(n_tiles, gpt, c, p)
        return jnp.einsum("tgcp,gh->tgphc", blocks, eye).reshape(n_tiles, gpt * p, gpt * c)

    w_in = jnp.concatenate([in_tiles(bbr), in_tiles(bbi)], axis=-1).astype(BF16)
    return (flat(abr), flat(abi), flat_pw(pwr), flat_pw(pwi), w_in,
            out_tiles(c_re).astype(BF16), out_tiles(c_im).astype(BF16))


def _s5_in_proj(u, wb_ref, bre_ref, bim_ref):
    ub = u.astype(BF16)
    n_tiles, ch, two_w = wb_ref.shape
    w = two_w // 2
    for t in range(n_tiles):
        bu = _dot(ub[:, t * ch:(t + 1) * ch], wb_ref[t])
        bre_ref[:, t * w:(t + 1) * w] = bu[:, :w]
        bim_ref[:, t * w:(t + 1) * w] = bu[:, w:]


def _s5_out(u, hre_ref, him_ref, wcr_ref, wci_ref, d_ref, wglu_ref):
    n_tiles, w, _ = wcr_ref.shape
    ys = []
    for t in range(n_tiles):
        hr = hre_ref[:, t * w:(t + 1) * w].astype(BF16)
        hi = him_ref[:, t * w:(t + 1) * w].astype(BF16)
        ys.append(_dot(hr, wcr_ref[t]) - _dot(hi, wci_ref[t]))
    y = jax.nn.gelu(jnp.concatenate(ys, axis=-1) + d_ref[...] * u)
    z = _dot(y.astype(BF16), wglu_ref[...])
    half = z.shape[-1] // 2
    return z[:, :half] * jax.nn.sigmoid(z[:, half:])


def _s5_prompt_body(u_ref, wb_ref, wcr_ref, wci_ref, d_ref, wglu_ref, tab_ref,
                    o_ref, hre_ref, him_ref, bre_ref, bim_ref, cre_ref, cim_ref, *, lane_group):
    ci = pl.program_id(1)
    rows, width = bre_ref.shape

    @pl.when(ci == 0)
    def _():
        cre_ref[...] = jnp.zeros_like(cre_ref)
        cim_ref[...] = jnp.zeros_like(cim_ref)

    u = u_ref[0]
    _s5_in_proj(u, wb_ref, bre_ref, bim_ref)

    for lg in range(width // lane_group):
        ls = slice(lg * lane_group, (lg + 1) * lane_group)

        def tile(i, carry, ls=ls):
            cr, cim = carry
            r0 = pl.multiple_of(i * SUBLANES, SUBLANES)
            xr = bre_ref[pl.ds(r0, SUBLANES), ls]
            xi = bim_ref[pl.ds(r0, SUBLANES), ls]
            for j, shift in enumerate((1, 2, 4)):
                rr = pltpu.roll(xr, shift, 0)
                ri = pltpu.roll(xi, shift, 0)
                ar = tab_ref[2 * j, :, ls]
                ai = tab_ref[2 * j + 1, :, ls]
                xr, xi = xr + ar * rr - ai * ri, xi + ar * ri + ai * rr
            pr = tab_ref[6, :, ls]
            pi = tab_ref[7, :, ls]
            xr, xi = xr + pr * cr - pi * cim, xi + pr * cim + pi * cr
            bre_ref[pl.ds(r0, SUBLANES), ls] = xr
            bim_ref[pl.ds(r0, SUBLANES), ls] = xi
            last = SUBLANES - 1
            return (jnp.broadcast_to(xr[last:, :], xr.shape), jnp.broadcast_to(xi[last:, :], xi.shape))

        cr, cim = lax.fori_loop(0, rows // SUBLANES, tile, (cre_ref[:, ls], cim_ref[:, ls]))
        cre_ref[:, ls] = cr
        cim_ref[:, ls] = cim

    o_ref[0] = _s5_out(u, bre_ref, bim_ref, wcr_ref, wci_ref, d_ref, wglu_ref)

    @pl.when(ci == pl.num_programs(1) - 1)
    def _():
        hre_ref[0] = bre_ref[rows - 1:rows, :]
        him_ref[0] = bim_ref[rows - 1:rows, :]


def _s5_prompt(u, params, d_skip, w_glu, *, rows):
    ab_re, ab_im, pw_re, pw_im, w_in, wc_re, wc_im = params
    nb, length, ch = u.shape
    width = ab_re.shape[1]
    row_idx = jnp.arange(SUBLANES)[:, None]
    tabs = []
    for shift in (1, 2, 4):
        keep = row_idx >= shift
        tabs += [jnp.where(keep, pw_re[shift - 1][None, :], 0.0),
                 jnp.where(keep, pw_im[shift - 1][None, :], 0.0)]
    tab = jnp.stack(tabs + [pw_re, pw_im])
    state = jax.ShapeDtypeStruct((nb, 1, width), F32)
    state_spec = pl.BlockSpec((1, 1, width), lambda b, c: (b, 0, 0))
    tok_spec = pl.BlockSpec((1, rows, ch), lambda b, c: (b, c, 0))
    return pl.pallas_call(
        functools.partial(_s5_prompt_body, lane_group=4 * LANES),
        grid=(nb, length // rows),
        in_specs=[tok_spec, _resident(w_in.shape), _resident(wc_re.shape), _resident(wc_im.shape),
                  _resident((1, ch)), _resident(w_glu.shape), _resident(tab.shape)],
        out_specs=[tok_spec, state_spec, state_spec],
        out_shape=[jax.ShapeDtypeStruct((nb, length, ch), F32), state, state],
        scratch_shapes=[pltpu.VMEM((rows, width), F32), pltpu.VMEM((rows, width), F32),
                        pltpu.VMEM((SUBLANES, width), F32), pltpu.VMEM((SUBLANES, width), F32)],
        compiler_params=_params("parallel", "arbitrary"),
        name="s5_prompt",
    )(u, w_in, wc_re, wc_im, d_skip.reshape(1, ch), w_glu.astype(BF16), tab)


def _s5_sample_body(u_ref, h0r_ref, h0i_ref, wb_ref, wcr_ref, wci_ref, d_ref, wglu_ref, ab_ref,
                    o_ref, hre_ref, him_ref, bre_ref, bim_ref):
    hre_ref[...] = h0r_ref[...]
    him_ref[...] = h0i_ref[...]
    ar = ab_ref[0:1, :]
    ai = ab_ref[1:2, :]
    for t in range(u_ref.shape[0]):
        u = u_ref[t]
        _s5_in_proj(u, wb_ref, bre_ref, bim_ref)
        hr = hre_ref[...]
        hi = him_ref[...]
        hre_ref[...] = ar * hr - ai * hi + bre_ref[...]
        him_ref[...] = ar * hi + ai * hr + bim_ref[...]
        o_ref[t] = _s5_out(u, hre_ref, him_ref, wcr_ref, wci_ref, d_ref, wglu_ref)


def _s5_sample(u, h0_re, h0_im, params, d_skip, w_glu):
    ab_re, ab_im, _, _, w_in, wc_re, wc_im = params
    nt, nb, ch = u.shape
    width = ab_re.shape[1]
    ab = jnp.concatenate([ab_re, ab_im], axis=0)
    whole = lambda shape: pl.BlockSpec(shape, lambda: (0,) * len(shape))
    state = jax.ShapeDtypeStruct((nb, width), F32)
    return pl.pallas_call(
        _s5_sample_body,
        in_specs=[whole(u.shape), whole((nb, width)), whole((nb, width)), whole(w_in.shape),
                  whole(wc_re.shape), whole(wc_im.shape), whole((1, ch)), whole(w_glu.shape),
                  whole(ab.shape)],
        out_specs=[whole(u.shape), whole((nb, width)), whole((nb, width))],
        out_shape=[jax.ShapeDtypeStruct(u.shape, F32), state, state],
        scratch_shapes=[pltpu.VMEM((nb, width), F32), pltpu.VMEM((nb, width), F32)],
        compiler_params=pltpu.CompilerParams(vmem_limit_bytes=VMEM_LIMIT_BYTES),
        name="s5_sample",
    )(u, h0_re, h0_im, w_in, wc_re, wc_im, d_skip.reshape(1, ch), w_glu.astype(BF16), ab)


HEADS_PER_STEP = 2


def _fox_prompt_body(q_ref, k_ref, vt_ref, o_ref, m_ref, acc_ref, *, head_dim, q_chunk):
    qi = pl.program_id(2)
    tq = q_ref.shape[1]
    m_ref[...] = jnp.full_like(m_ref, NEG_INF)
    acc_ref[...] = jnp.zeros_like(acc_ref)

    def block(kj, diagonal):
        ks = pl.multiple_of(kj * tq, tq)
        chains = [(h, slice(h * LANES, (h + 1) * LANES), slice(c * q_chunk, (c + 1) * q_chunk))
                  for h in range(HEADS_PER_STEP) for c in range(tq // q_chunk)]
        scores = [_dot_nt(k_ref[0, pl.ds(ks, tq), hl], q_ref[0, qs, hl]) for _, hl, qs in chains]
        if diagonal:
            key = lax.broadcasted_iota(jnp.int32, (tq, q_chunk), 0)
            qry = lax.broadcasted_iota(jnp.int32, (tq, q_chunk), 1)
            scores = [jnp.where(key <= qry + qs.start, s, NEG_INF) for s, (_, _, qs) in zip(scores, chains)]
        probs, alphas = [], []
        for s, (h, _, qs) in zip(scores, chains):
            m_old = m_ref[h, :, qs]
            m_new = jnp.maximum(m_old, jnp.max(s, axis=0, keepdims=True))
            alphas.append(jnp.exp(m_old - m_new))
            probs.append(jnp.exp(s - m_new).astype(BF16))
            m_ref[h, :, qs] = m_new
        for p, alpha, (h, hl, qs) in zip(probs, alphas, chains):
            acc_ref[h, :, qs] = alpha * acc_ref[h, :, qs] + _dot(vt_ref[hl, pl.ds(ks, tq)], p)

    def body(kj, carry):
        block(kj, False)
        return carry

    lax.fori_loop(0, qi, body, 0)
    block(qi, True)
    outs = [acc_ref[h, :head_dim, :] / acc_ref[h, head_dim:head_dim + 1, :] for h in range(HEADS_PER_STEP)]
    o_ref[0] = jnp.concatenate(outs, axis=0).T


def _fox_prompt(qa, ka, vta, *, heads, head_dim, tq):
    nb, length, _ = qa.shape
    wide = HEADS_PER_STEP * LANES
    return pl.pallas_call(
        functools.partial(_fox_prompt_body, head_dim=head_dim, q_chunk=MXU_DIM),
        grid=(nb, heads // HEADS_PER_STEP, length // tq),
        in_specs=[pl.BlockSpec((1, tq, wide), lambda b, hp, i: (b, i, hp)),
                  pl.BlockSpec((1, length, wide), lambda b, hp, i: (b, 0, hp)),
                  pl.BlockSpec((wide, length), lambda b, hp, i: (hp, b))],
        out_specs=pl.BlockSpec((1, tq, HEADS_PER_STEP * head_dim), lambda b, hp, i: (b, i, hp)),
        out_shape=jax.ShapeDtypeStruct((nb, length, heads * head_dim), F32),
        scratch_shapes=[pltpu.VMEM((HEADS_PER_STEP, 1, tq), F32),
                        pltpu.VMEM((HEADS_PER_STEP, LANES, tq), F32)],
        compiler_params=_params("parallel", "parallel", "arbitrary"),
        name="fox_prompt",
    )(qa, ka, vta)


def _fox_sample_body(pt_ref, q_ref, *refs, pages_per_step):
    n = pages_per_step
    k_refs, v_refs, lf_refs = refs[:n], refs[n:2 * n], refs[2 * n:3 * n]
    kn_ref, vn_ref, lfn_ref, expand_ref, o_ref, q_scr, m_ref, l_ref, acc_ref, carry_ref = refs[3 * n:]
    del pt_ref
    g = pl.program_id(1)
    n_tok, _, head_dim = q_ref.shape
    page = k_refs[0].shape[0]
    n_grp, rows, _ = q_scr.shape
    tri = _upper_tri()

    @pl.when(g == 0)
    def _():
        for gi, qg in enumerate(_head_groups(q_ref)):
            q_scr[gi] = qg.astype(BF16)
        m_ref[...] = jnp.full_like(m_ref, NEG_INF)
        l_ref[...] = jnp.zeros_like(l_ref)
        acc_ref[...] = jnp.zeros_like(acc_ref)
        carry_ref[...] = jnp.zeros_like(carry_ref)

    def update(pages, n_pos, causal):
        lanes = pages[0][0][0].shape[0]
        row = lax.broadcasted_iota(jnp.int32, (rows, lanes), 0)
        lane = lax.broadcasted_iota(jnp.int32, (rows, lanes), 1)
        valid = (lane % SUBLANES) == (row % SUBLANES)
        if causal:
            valid = valid & (lane // SUBLANES <= row // SUBLANES) & (lane < n_pos * SUBLANES)
        cums, base = [], carry_ref[...]
        for _, _, lf in pages:
            c = _cumsum_lanes(lf, tri) + base
            base = jnp.broadcast_to(c[:, LANES - 1:], c.shape)
            cums.append(c)
        carry_ref[...] = base
        c_all = cums[0] if len(cums) == 1 else jnp.concatenate(cums, axis=0)
        bias_all = sum(_dot(piece, expand_ref[:, :lanes]) for piece in _split3(c_all))
        hp = cums[0].shape[0]
        for gi in range(n_grp):
            scores = []
            for pi, (k_groups, _, _) in enumerate(pages):
                r0 = pi * hp + gi * SUBLANES
                bias = bias_all[r0:r0 + SUBLANES, :]
                s = _dot_nt(q_scr[gi], k_groups[gi].astype(BF16)) - jnp.concatenate([bias] * n_tok, axis=0)
                scores.append(jnp.where(valid, s, NEG_INF))
            m_old = m_ref[gi]
            m_new = functools.reduce(jnp.maximum, [jnp.max(s, axis=1, keepdims=True) for s in scores], m_old)
            alpha = jnp.exp(m_old - m_new)
            probs = [jnp.exp(s - m_new) for s in scores]
            l_ref[gi] = alpha * l_ref[gi] + sum(jnp.sum(p, axis=1, keepdims=True) for p in probs)
            acc_ref[gi] = alpha * acc_ref[gi] + sum(
                _dot(p.astype(BF16), v_groups[gi].astype(BF16)) for p, (_, v_groups, _) in zip(probs, pages))
            m_ref[gi] = m_new

    update([(_head_groups(k_refs[i]), _head_groups(v_refs[i]), lf_refs[i][...]) for i in range(n)], page, False)

    @pl.when(g == pl.num_programs(1) - 1)
    def _():
        def padded(ref):
            pad = jnp.zeros((LANES - rows, head_dim), F32)
            return [jnp.concatenate([grp, pad], axis=0) for grp in _head_groups(ref)]

        update([(padded(kn_ref), padded(vn_ref), lfn_ref[...])], n_tok, True)
        for gi in range(n_grp):
            o_ref[gi] = acc_ref[gi] / l_ref[gi]


def _head_groups(ref):
    n_pos, heads, dim = ref.shape
    groups = []
    for h0 in range(0, heads, SUBLANES):
        h1 = min(h0 + SUBLANES, heads)
        blk = ref[:, h0:h1, :]
        if h1 - h0 < SUBLANES:
            blk = jnp.concatenate([blk, jnp.zeros((n_pos, SUBLANES - (h1 - h0), dim), blk.dtype)], axis=1)
        groups.append(blk.reshape(n_pos * SUBLANES, dim))
    return groups


def _fox_sample(q, k_new, v_new, lft_new, cache_k, cache_v, cache_lft, page_table, *, layer, pages_per_step):
    ns, n_tok, heads, head_dim = q.shape
    n_pages = page_table.shape[1]
    page = cache_k.shape[2]
    hp = cache_lft.shape[1]
    n = pages_per_step
    n_grp = -(-heads // SUBLANES)
    rows = n_tok * SUBLANES
    pos = jnp.arange(page)[:, None]
    lane = jnp.arange(page * SUBLANES)[None, :]
    expand = (lane // SUBLANES == pos).astype(BF16)

    def kv_page(i):
        return pl.BlockSpec((None, None, page, heads, head_dim),
                            lambda s, g, pt: (layer, pt[s, g * n + i], 0, 0, 0))

    def lf_page(i):
        return pl.BlockSpec((None, hp, page), lambda s, g, pt: (pt[s, g * n + i], 0, 0))

    tok = pl.BlockSpec((None, n_tok, heads, head_dim), lambda s, g, pt: (s, 0, 0, 0))
    in_specs = [tok] + [kv_page(i) for i in range(n)] * 2 + [lf_page(i) for i in range(n)]
    in_specs += [tok, tok, pl.BlockSpec((None, hp, page), lambda s, g, pt: (s, 0, 0)),
                 pl.BlockSpec(expand.shape, lambda s, g, pt: (0, 0))]
    return pl.pallas_call(
        functools.partial(_fox_sample_body, pages_per_step=n),
        grid_spec=pltpu.PrefetchScalarGridSpec(
            num_scalar_prefetch=1,
            grid=(ns, n_pages // n),
            in_specs=in_specs,
            out_specs=pl.BlockSpec((None, n_grp, rows, head_dim), lambda s, g, pt: (s, 0, 0, 0)),
            scratch_shapes=[pltpu.VMEM((n_grp, rows, head_dim), BF16), pltpu.VMEM((n_grp, rows, 1), F32),
                            pltpu.VMEM((n_grp, rows, 1), F32), pltpu.VMEM((n_grp, rows, head_dim), F32),
                            pltpu.VMEM((hp, page), F32)],
        ),
        out_shape=jax.ShapeDtypeStruct((ns, n_grp, rows, head_dim), F32),
        compiler_params=_params("parallel", "arbitrary"),
        name="fox_sample",
    )(page_table, q, *([cache_k] * n), *([cache_v] * n), *([cache_lft] * n), k_new, v_new, lft_new, expand)


def _softmax_pv(s, vb):
    m = jnp.max(s, axis=-1, keepdims=True)
    p = jnp.exp(s - m)
    return _dot(p.astype(BF16), vb) / jnp.sum(p, axis=-1, keepdims=True)


def _cross_out_prompt_body(x_ref, mix_ref, qc_ref, mk_ref, mv_ref, wom_ref, woc_ref, o_ref,
                           *, heads, head_dim):
    tm, cw = qc_ref.shape
    q = qc_ref[...] * head_dim ** -0.5
    head_of_lane = lax.broadcasted_iota(jnp.int32, (1, cw), 1) // head_dim
    zero = jnp.zeros_like(q)
    q4 = jnp.concatenate([jnp.where(head_of_lane == h, q, zero) for h in range(heads)], axis=0)
    s = _dot_nt(q4.astype(BF16), mk_ref[0].astype(BF16))
    o4 = _softmax_pv(s, mv_ref[0].astype(BF16))
    cross = zero
    for h in range(heads):
        cross = cross + jnp.where(head_of_lane == h, o4[h * tm:(h + 1) * tm, :], zero)
    o_ref[...] = (x_ref[...] + _dot(mix_ref[...].astype(BF16), wom_ref[...])
                  + _dot(cross.astype(BF16), woc_ref[...]))


def _cross_out_prompt(x, mix, qc, mk, mv, wom, woc, *, heads, head_dim, tm):
    t, d = x.shape
    nb, n_mem, cw = mk.shape
    per_b = t // nb // tm
    row = lambda n: pl.BlockSpec((tm, n), lambda i: (i, 0))
    mem = pl.BlockSpec((1, n_mem, cw), lambda i: (i // per_b, 0, 0))
    return pl.pallas_call(
        functools.partial(_cross_out_prompt_body, heads=heads, head_dim=head_dim),
        grid=(t // tm,),
        in_specs=[row(d), row(mix.shape[1]), row(cw), mem, mem, _resident(wom.shape),
                  _resident(woc.shape)],
        out_specs=row(d),
        out_shape=jax.ShapeDtypeStruct((t, d), F32),
        compiler_params=_params("parallel"),
        name="cross_out_prompt",
    )(x, mix, qc, mk, mv, wom, woc)


def _cross_out_sample_body(x_ref, mix_ref, qc_ref, mk_ref, mv_ref, wom_ref, woc_ref, o_ref, cross_ref,
                           *, heads, head_dim, n_tok):
    n_seq, _, cw = mk_ref.shape
    scale = head_dim ** -0.5
    head_of_lane = lax.broadcasted_iota(jnp.int32, (SUBLANES, cw), 1) // head_dim
    head_mask = head_of_lane == lax.broadcasted_iota(jnp.int32, (SUBLANES, cw), 0)
    zero = jnp.zeros((SUBLANES, cw), F32)
    for i in range(n_seq):
        rows = [jnp.where(head_mask, jnp.broadcast_to(qc_ref[pl.ds(i * n_tok + t, 1), :] * scale,
                                                     (SUBLANES, cw)), zero) for t in range(n_tok)]
        q = jnp.concatenate(rows, axis=0).astype(BF16)
        o = _softmax_pv(_dot_nt(q, mk_ref[i].astype(BF16)), mv_ref[i].astype(BF16))
        for t in range(n_tok):
            picked = jnp.where(head_mask, o[t * SUBLANES:(t + 1) * SUBLANES, :], zero)
            cross_ref[pl.ds(i * n_tok + t, 1), :] = jnp.sum(picked, axis=0, keepdims=True)
    o_ref[...] = (x_ref[...] + _dot(mix_ref[...].astype(BF16), wom_ref[...])
                  + _dot(cross_ref[...].astype(BF16), woc_ref[...]))


def _cross_out_sample(x, mix, qc, mk, mv, wom, woc, *, heads, head_dim, n_tok, seqs_per_step):
    t, d = x.shape
    ns, n_mem, cw = mk.shape
    tm = seqs_per_step * n_tok
    row = lambda n: pl.BlockSpec((tm, n), lambda i: (i, 0))
    mem = pl.BlockSpec((seqs_per_step, n_mem, cw), lambda i: (i, 0, 0))
    return pl.pallas_call(
        functools.partial(_cross_out_sample_body, heads=heads, head_dim=head_dim, n_tok=n_tok),
        grid=(ns // seqs_per_step,),
        in_specs=[row(d), row(mix.shape[1]), row(cw), mem, mem, _resident(wom.shape),
                  _resident(woc.shape)],
        out_specs=row(d),
        out_shape=jax.ShapeDtypeStruct((t, d), F32),
        scratch_shapes=[pltpu.VMEM((tm, cw), F32)],
        compiler_params=_params("parallel"),
        name="cross_out_sample",
    )(x, mix, qc, mk, mv, wom, woc)


def kernel(x_prompt, x_sample, mem_prompt, state_s5_re, state_s5_im, cache_fox_k, cache_fox_v, cache_fox_logf, cache_mem_k, cache_mem_v, page_table, ffn_norm, ffn_w_gate, ffn_w_up, ffn_w_down, norm_mix, norm_mem, w_mem_kv, w_in_s5, s5_a_re, s5_a_im, s5_log_dt, s5_b_re, s5_b_im, s5_c_re, s5_c_im, s5_d, s5_w_glu, w_in_fox, fox_b_f, w_out, norm_final):
    nb, seq, d = x_prompt.shape
    ns, n_tok, _ = x_sample.shape
    depth = ffn_norm.shape[0]
    n_mem = mem_prompt.shape[1]
    cross_heads, head_dim = cache_mem_k.shape[3], cache_mem_k.shape[4]
    cross_width = cross_heads * head_dim
    mixer_width = w_out.shape[1] - cross_width
    fox_heads = cache_fox_k.shape[3]
    n_phys, page = cache_fox_k.shape[1], cache_fox_k.shape[2]
    groups, n_state = state_s5_re.shape[2], state_s5_re.shape[3]
    hp = 2 * SUBLANES

    tm_p = 512
    tm_s = ns * n_tok
    xp = x_prompt.reshape(nb * seq, d)
    xs = x_sample.reshape(tm_s, d)
    mem = mem_prompt.reshape(nb * n_mem, d)

    s5_re_p, s5_im_p, s5_re_s, s5_im_s = [], [], [], []
    fk_p, fv_p, fl_p, fk_s, fv_s, fl_s = [], [], [], [], [], []
    mk_list, mv_list = [], []
    for i in range(depth):
        j = i // N_MIXERS
        ffn_w = lambda half: (ffn_norm[i, half], ffn_w_gate[i, half], ffn_w_up[i, half], ffn_w_down[i, half])
        xp = _ffn(xp, *ffn_w(0), tm=tm_p)
        xs = _ffn(xs, *ffn_w(0), tm=tm_s)

        mk_p, mv_p = _norm_proj(mem, norm_mem[i],
                                [w_mem_kv[i][:, :cross_width], w_mem_kv[i][:, cross_width:]], tm=nb * n_mem)
        mk_list.append(mk_p.reshape(nb, n_mem, cross_heads, head_dim))
        mv_list.append(mv_p.reshape(nb, n_mem, cross_heads, head_dim))
        mk_s = cache_mem_k[i].reshape(ns, n_mem, cross_width)
        mv_s = cache_mem_v[i].reshape(ns, n_mem, cross_width)

        if i % N_MIXERS == 0:
            w_in = [w_in_s5[j][:, :mixer_width], w_in_s5[j][:, mixer_width:]]
            u_p, qc_p = _norm_proj(xp, norm_mix[i], w_in, tm=tm_p)
            u_s, qc_s = _norm_proj(xs, norm_mix[i], w_in, tm=tm_s)
            params = _s5_params(s5_a_re[j], s5_a_im[j], s5_log_dt[j], s5_b_re[j], s5_b_im[j],
                                s5_c_re[j], s5_c_im[j])
            mix_p, hr_p, hi_p = _s5_prompt(u_p.reshape(nb, seq, mixer_width), params, s5_d[j],
                                           s5_w_glu[j], rows=tm_p)
            mix_p = mix_p.reshape(nb * seq, mixer_width)
            u_t = u_s.reshape(ns, n_tok, mixer_width).transpose(1, 0, 2)
            mix_t, hr_s, hi_s = _s5_sample(u_t, state_s5_re[j].reshape(ns, groups * n_state),
                                           state_s5_im[j].reshape(ns, groups * n_state),
                                           params, s5_d[j], s5_w_glu[j])
            mix_s = mix_t.transpose(1, 0, 2).reshape(tm_s, mixer_width)
            s5_re_p.append(hr_p.reshape(nb, groups, n_state))
            s5_im_p.append(hi_p.reshape(nb, groups, n_state))
            s5_re_s.append(hr_s.reshape(ns, groups, n_state))
            s5_im_s.append(hi_s.reshape(ns, groups, n_state))
        else:
            fox = dict(heads=fox_heads, head_dim=head_dim, cross_width=cross_width)
            k_p, v_p, qc_p, lf_p, qa_p, ka_p, vta_p = _fox_proj_prompt(
                xp, norm_mix[i], w_in_fox[j], fox_b_f[j], tm=tm_p, tiles_per_seq=seq // tm_p, **fox)
            q_s, k_s, v_s, qc_s, lf_s, lft_s = _fox_proj_sample(
                xs, norm_mix[i], w_in_fox[j], fox_b_f[j], tm=tm_s, **fox)

            seq3 = lambda a: a.reshape(nb, seq, a.shape[-1])
            mix_p = _fox_prompt(seq3(qa_p), seq3(ka_p), vta_p, heads=fox_heads, head_dim=head_dim, tq=tm_p)
            mix_p = mix_p.reshape(nb * seq, mixer_width)

            tok4 = lambda a: a.reshape(ns, n_tok, fox_heads, head_dim)
            k_s, v_s = tok4(k_s), tok4(v_s)
            lft_new = jnp.pad(lft_s.reshape(hp, ns, n_tok).transpose(1, 0, 2),
                              ((0, 0), (0, 0), (0, page - n_tok)))
            cache_lft = jnp.pad(cache_fox_logf[j].transpose(0, 2, 1), ((0, 0), (0, hp - fox_heads), (0, 0)))
            mix_s = _fox_sample(tok4(q_s), k_s, v_s, lft_new, cache_fox_k, cache_fox_v, cache_lft, page_table,
                                layer=j, pages_per_step=4)
            mix_s = mix_s.reshape(ns, -1, n_tok, SUBLANES, head_dim).transpose(0, 2, 1, 3, 4)
            mix_s = mix_s.reshape(ns, n_tok, -1, head_dim)[:, :, :fox_heads].reshape(tm_s, mixer_width)

            fk_p.append(k_p.reshape(nb, seq, fox_heads, head_dim))
            fv_p.append(v_p.reshape(nb, seq, fox_heads, head_dim))
            fl_p.append(lf_p[:, :fox_heads].reshape(nb, seq, fox_heads))
            fk_s.append(k_s)
            fv_s.append(v_s)
            fl_s.append(lf_s[:, :fox_heads].reshape(ns, n_tok, fox_heads))

        wom = w_out[i][:mixer_width].astype(BF16)
        woc = w_out[i][mixer_width:].astype(BF16)
        xp = _cross_out_prompt(xp, mix_p, qc_p, mk_p.reshape(nb, n_mem, cross_width),
                               mv_p.reshape(nb, n_mem, cross_width), wom, woc,
                               heads=cross_heads, head_dim=head_dim, tm=tm_p)
        xs = _cross_out_sample(xs, mix_s, qc_s, mk_s, mv_s, wom, woc,
                               heads=cross_heads, head_dim=head_dim, n_tok=n_tok, seqs_per_step=8)

        g_final = norm_final if i == depth - 1 else None
        xp = _ffn(xp, *ffn_w(1), g_final, tm=tm_p)
        xs = _ffn(xs, *ffn_w(1), g_final, tm=tm_s)

    return (xp.reshape(nb, seq, d), xs.reshape(ns, n_tok, d),
            jnp.stack(s5_re_p), jnp.stack(s5_im_p), jnp.stack(s5_re_s), jnp.stack(s5_im_s),
            jnp.stack(fk_p), jnp.stack(fv_p), jnp.stack(fl_p),
            jnp.stack(fk_s), jnp.stack(fv_s), jnp.stack(fl_s),
            jnp.stack(mk_list), jnp.stack(mv_list))
```

```python
import functools

import jax
import jax.numpy as jnp
from jax import lax
from jax.experimental import pallas as pl
from jax.experimental.pallas import tpu as pltpu

F32 = jnp.float32
BF16 = jnp.bfloat16

RMS_EPS = 1e-6
NEG_INF = -1e30
N_MIXERS = 2

LANES = 128
SUBLANES = 8
MXU_DIM = 256
VMEM_LIMIT_BYTES = 56 * 1024 * 1024

NT_DIMS = (((1,), (1,)), ((), ()))


def _params(*sem):
    return pltpu.CompilerParams(dimension_semantics=sem, vmem_limit_bytes=VMEM_LIMIT_BYTES)


def _resident(shape):
    nd = len(shape)
    return pl.BlockSpec(shape, lambda *_: (0,) * nd, pipeline_mode=pl.Buffered(1))


def _rms(x, g):
    return x * lax.rsqrt(jnp.mean(x * x, axis=-1, keepdims=True) + RMS_EPS) * g


def _dot(a, b):
    return jnp.dot(a, b, preferred_element_type=F32)


def _dot_nt(a, b):
    return lax.dot_general(a, b, NT_DIMS, preferred_element_type=F32)


def _log_sigmoid(x):
    return jnp.minimum(x, 0.0) - jnp.log1p(jnp.exp(-jnp.abs(x)))


def _split3(x):
    hi = x.astype(BF16)
    r1 = x - hi.astype(F32)
    mid = r1.astype(BF16)
    lo = (r1 - mid.astype(F32)).astype(BF16)
    return hi, mid, lo


def _cumsum_lanes(x, tri):
    hi, mid, lo = _split3(x)
    return _dot(hi, tri) + _dot(mid, tri) + _dot(lo, tri)


def _upper_tri():
    r = lax.broadcasted_iota(jnp.int32, (LANES, LANES), 0)
    c = lax.broadcasted_iota(jnp.int32, (LANES, LANES), 1)
    return jnp.where(r <= c, 1.0, 0.0).astype(BF16)


def _ffn_body(*refs, d_ff, ff_chunk, final):
    if final:
        x_ref, g_ref, wg_ref, wu_ref, wd_ref, gf_ref, o_ref, t_ref = refs
    else:
        x_ref, g_ref, wg_ref, wu_ref, wd_ref, o_ref, t_ref = refs
    x = x_ref[...]
    h = _rms(x, g_ref[...]).astype(BF16)
    for c in range(d_ff // ff_chunk):
        lo, hi = c * ff_chunk, (c + 1) * ff_chunk
        a = _dot(h, wg_ref[:, lo:hi])
        b = _dot(h, wu_ref[:, lo:hi])
        t_ref[:, lo:hi] = (jax.nn.silu(a) * b).astype(BF16)
    y = x + 0.5 * _dot(t_ref[...], wd_ref[...])
    if final:
        y = _rms(y, gf_ref[...])
    o_ref[...] = y


def _ffn(x, g, wg, wu, wd, g_final=None, *, tm):
    t, d = x.shape
    d_ff = wg.shape[1]
    final = g_final is not None
    row = pl.BlockSpec((tm, d), lambda i: (i, 0))
    in_specs = [row, _resident((1, d)), _resident((d, d_ff)), _resident((d, d_ff)),
                _resident((d_ff, d))]
    args = [x, g.reshape(1, d), wg.astype(BF16), wu.astype(BF16), wd.astype(BF16)]
    if final:
        in_specs.append(_resident((1, d)))
        args.append(g_final.reshape(1, d))
    return pl.pallas_call(
        functools.partial(_ffn_body, d_ff=d_ff, ff_chunk=MXU_DIM, final=final),
        grid=(t // tm,),
        in_specs=in_specs,
        out_specs=row,
        out_shape=jax.ShapeDtypeStruct((t, d), F32),
        scratch_shapes=[pltpu.VMEM((tm, d_ff), BF16)],
        compiler_params=_params("parallel"),
        name="ffn_final" if final else "ffn",
    )(*args)


def _norm_proj_body(x_ref, g_ref, *refs, n):
    h = _rms(x_ref[...], g_ref[...]).astype(BF16)
    for w_ref, o_ref in zip(refs[:n], refs[n:]):
        o_ref[...] = _dot(h, w_ref[...])


def _norm_proj(x, g, ws, *, tm):
    t, d = x.shape
    n = len(ws)
    in_specs = [pl.BlockSpec((tm, d), lambda i: (i, 0)), _resident((1, d))]
    in_specs += [_resident(w.shape) for w in ws]
    return pl.pallas_call(
        functools.partial(_norm_proj_body, n=n),
        grid=(t // tm,),
        in_specs=in_specs,
        out_specs=[pl.BlockSpec((tm, w.shape[1]), lambda i: (i, 0)) for w in ws],
        out_shape=[jax.ShapeDtypeStruct((t, w.shape[1]), F32) for w in ws],
        compiler_params=_params("parallel"),
        name="norm_proj",
    )(x, g.reshape(1, d), *[w.astype(BF16) for w in ws])


def _fox_split(w_in, b_f, heads, head_dim):
    w = heads * head_dim
    wq, wk, wv = w_in[:, :w], w_in[:, w:2 * w], w_in[:, 2 * w:3 * w]
    wf = jnp.pad(w_in[:, 3 * w:3 * w + heads], ((0, 0), (0, LANES - heads)))
    wc = w_in[:, 3 * w + heads:]
    bf = jnp.pad(b_f, (0, LANES - heads)).reshape(1, LANES)
    return wq, wk, wv, wf, wc, bf


def _fox_proj_sample_body(x_ref, g_ref, wq_ref, wk_ref, wv_ref, wc_ref, wf_ref, wft_ref, bf_ref, bft_ref,
                          q_ref, k_ref, v_ref, qc_ref, lf_ref, lft_ref, *, scale):
    h = _rms(x_ref[...], g_ref[...]).astype(BF16)
    q_ref[...] = _dot(h, wq_ref[...]) * scale
    k_ref[...] = _dot(h, wk_ref[...])
    v_ref[...] = _dot(h, wv_ref[...])
    qc_ref[...] = _dot(h, wc_ref[...])
    lf_ref[...] = _log_sigmoid(_dot(h, wf_ref[...]) + bf_ref[...])
    lft_ref[...] = _log_sigmoid(_dot_nt(wft_ref[...], h) + bft_ref[...])


def _fox_proj_sample(x, g, w_in, b_f, *, heads, head_dim, cross_width, tm):
    t, d = x.shape
    w = heads * head_dim
    hp = 2 * SUBLANES
    wq, wk, wv, wf, wc, bf = _fox_split(w_in, b_f, heads, head_dim)
    wft = wf[:, :hp].T
    bft = bf[0, :hp].reshape(hp, 1)
    row = lambda n: pl.BlockSpec((tm, n), lambda i: (i, 0))
    outs = [w, w, w, cross_width, LANES]
    return pl.pallas_call(
        functools.partial(_fox_proj_sample_body, scale=head_dim ** -0.5),
        grid=(t // tm,),
        in_specs=[row(d), _resident((1, d)), _resident((d, w)), _resident((d, w)),
                  _resident((d, w)), _resident((d, cross_width)), _resident((d, LANES)),
                  _resident((hp, d)), _resident((1, LANES)), _resident((hp, 1))],
        out_specs=[row(n) for n in outs] + [pl.BlockSpec((hp, tm), lambda i: (0, i))],
        out_shape=[jax.ShapeDtypeStruct((t, n), F32) for n in outs]
        + [jax.ShapeDtypeStruct((hp, t), F32)],
        compiler_params=_params("parallel"),
        name="fox_proj_sample",
    )(x, g.reshape(1, d), wq.astype(BF16), wk.astype(BF16), wv.astype(BF16), wc.astype(BF16),
      wf.astype(BF16), wft.astype(BF16), bf, bft)


N_BIAS = 3


def _fox_proj_prompt_body(x_ref, g_ref, wqa_ref, wkt_ref, wka_ref, wvt_ref, wc_ref, wf_ref,
                          bf_ref, place_ref, oneq_ref,
                          kt_ref, vt_ref, qc_ref, lf_ref, qa_ref, ka_ref, vta_ref, carry_ref,
                          *, scale, tiles_per_seq, head_dim):
    @pl.when(pl.program_id(0) % tiles_per_seq == 0)
    def _():
        carry_ref[...] = jnp.zeros_like(carry_ref)

    h = _rms(x_ref[...], g_ref[...]).astype(BF16)
    qa_ref[...] = (_dot(h, wqa_ref[...]) * scale + oneq_ref[...]).astype(BF16)
    kt_ref[...] = _dot_nt(wkt_ref[...], h)
    vt = _dot_nt(wvt_ref[...], h)
    vt_ref[...] = vt
    tm = vt.shape[1]
    tail = jnp.where(lax.broadcasted_iota(jnp.int32, (LANES - head_dim, tm), 0) == 0, 1.0, 0.0)
    blocks = []
    for r0 in range(0, vt.shape[0], head_dim):
        blocks += [vt[r0:r0 + head_dim, :], tail]
    vta_ref[...] = jnp.concatenate(blocks, axis=0).astype(BF16)
    qc_ref[...] = _dot(h, wc_ref[...])
    lf = _log_sigmoid(_dot(h, wf_ref[...]) + bf_ref[...])
    lf_ref[...] = lf
    r = lax.broadcasted_iota(jnp.int32, (tm, tm), 0)
    c = lax.broadcasted_iota(jnp.int32, (tm, tm), 1)
    low = jnp.where(c <= r, 1.0, 0.0).astype(BF16)
    cum = carry_ref[0:1, :] + sum(_dot(low, piece) for piece in _split3(lf))
    carry_ref[...] = jnp.broadcast_to(cum[tm - 1:, :], carry_ref.shape)
    bias = sum(_dot(piece, place_ref[j]) for j, piece in enumerate(_split3(-cum)))
    ka_ref[...] = (_dot(h, wka_ref[...]) + bias).astype(BF16)


def _fox_proj_prompt(x, g, w_in, b_f, *, heads, head_dim, cross_width, tm, tiles_per_seq):
    t, d = x.shape
    w = heads * head_dim
    wa = heads * LANES
    wq, wk, wv, wf, wc, bf = _fox_split(w_in, b_f, heads, head_dim)
    aug = lambda a: jnp.pad(a.reshape(d, heads, head_dim), ((0, 0), (0, 0), (0, LANES - head_dim))).reshape(d, wa)
    head = jnp.arange(heads)
    place = jnp.zeros((N_BIAS, LANES, wa), F32)
    oneq = jnp.zeros((1, wa), F32)
    for j in range(N_BIAS):
        place = place.at[j, head, head * LANES + head_dim + j].set(1.0)
        oneq = oneq.at[0, head * LANES + head_dim + j].set(1.0)
    row = lambda n: pl.BlockSpec((tm, n), lambda i: (i, 0))
    col = lambda n: pl.BlockSpec((None, n, tm), lambda i: (i // tiles_per_seq, 0, i % tiles_per_seq))
    seq_len = tm * tiles_per_seq
    outs = [(cross_width, F32), (LANES, F32), (wa, BF16), (wa, BF16)]
    return pl.pallas_call(
        functools.partial(_fox_proj_prompt_body, scale=head_dim ** -0.5, tiles_per_seq=tiles_per_seq,
                          head_dim=head_dim),
        grid=(t // tm,),
        in_specs=[row(d), _resident((1, d)), _resident((d, wa)), _resident((w, d)), _resident((d, wa)),
                  _resident((w, d)), _resident((d, cross_width)),
                  _resident((d, LANES)), _resident((1, LANES)), _resident(place.shape),
                  _resident((1, wa))],
        out_specs=[col(w), col(w)] + [row(n) for n, _ in outs] + [pl.BlockSpec((wa, tm), lambda i: (0, i))],
        out_shape=[jax.ShapeDtypeStruct((t // seq_len, w, seq_len), F32)] * 2
        + [jax.ShapeDtypeStruct((t, n), dt) for n, dt in outs]
        + [jax.ShapeDtypeStruct((wa, t), BF16)],
        scratch_shapes=[pltpu.VMEM((SUBLANES, LANES), F32)],
        compiler_params=_params("arbitrary"),
        name="fox_proj_prompt",
    )(x, g.reshape(1, d), aug(wq).astype(BF16), wk.T.astype(BF16), aug(wk).astype(BF16),
      wv.T.astype(BF16), wc.astype(BF16), wf.astype(BF16), bf, place.astype(BF16), oneq)


def _s5_param_body(are_ref, aim_ref, ldt_ref, bre_ref, bim_ref,
                   abr_ref, abi_ref, bbr_ref, bbi_ref, pwr_ref, pwi_ref):
    a_re = are_ref[...]
    a_im = aim_ref[...]
    dt = jnp.exp(ldt_ref[...])
    mag = jnp.exp(dt * a_re)
    ab_re = mag * jnp.cos(dt * a_im)
    ab_im = mag * jnp.sin(dt * a_im)
    den = a_re * a_re + a_im * a_im
    nr = ab_re - 1.0
    ni = ab_im
    zr = (nr * a_re + ni * a_im) / den
    zi = (ni * a_re - nr * a_im) / den
    b_re = bre_ref[...]
    b_im = bim_ref[...]
    abr_ref[...] = ab_re
    abi_ref[...] = ab_im
    bbr_ref[...] = zr * b_re - zi * b_im
    bbi_ref[...] = zr * b_im + zi * b_re
    pr, pi = ab_re, ab_im
    for r in range(SUBLANES):
        pwr_ref[r] = pr
        pwi_ref[r] = pi
        pr, pi = pr * ab_re - pi * ab_im, pr * ab_im + pi * ab_re


def _s5_params(a_re, a_im, log_dt, b_re, b_im, c_re, c_im):
    g, p, c = b_re.shape
    rows = g * c
    rep = lambda a: jnp.repeat(a, c, axis=0)
    bt = lambda b: b.transpose(0, 2, 1).reshape(rows, p)
    full = pl.BlockSpec((rows, p), lambda: (0, 0))
    pw_spec = pl.BlockSpec((SUBLANES, rows, p), lambda: (0, 0, 0))
    abr, abi, bbr, bbi, pwr, pwi = pl.pallas_call(
        _s5_param_body,
        in_specs=[full, full, pl.BlockSpec((rows, 1), lambda: (0, 0)), full, full],
        out_specs=[full, full, full, full, pw_spec, pw_spec],
        out_shape=[jax.ShapeDtypeStruct((rows, p), F32)] * 4
        + [jax.ShapeDtypeStruct((SUBLANES, rows, p), F32)] * 2,
        name="s5_params",
    )(rep(a_re), rep(a_im), rep(log_dt.reshape(g, 1)), bt(b_re), bt(b_im))
    width = g * p
    flat = lambda a: a[::c].reshape(1, width)
    flat_pw = lambda a: a[:, ::c].reshape(SUBLANES, width)
    gpt = MXU_DIM // c
    n_tiles = g // gpt
    eye = jnp.eye(gpt, dtype=F32)

    def in_tiles(bb):
        blocks = bb.reshape(n_tiles, gpt, c, p)
        return jnp.einsum("tgcp,gh->tgchp", blocks, eye).reshape(n_tiles, gpt * c, gpt * p)

    def out_tiles(cc):
        blocks = cc.reshape(n_tiles, gpt, c, p)
        return jnp.einsum("tgcp,gh->tgphc", blocks, eye).reshape(n_tiles, gpt * p, gpt * c)

    w_in = jnp.concatenate([in_tiles(bbr), in_tiles(bbi)], axis=-1).astype(BF16)
    return (flat(abr), flat(abi), flat_pw(pwr), flat_pw(pwi), w_in,
            out_tiles(c_re).astype(BF16), out_tiles(c_im).astype(BF16))


def _s5_in_proj(u, wb_ref, bre_ref, bim_ref):
    ub = u.astype(BF16)
    n_tiles, ch, two_w = wb_ref.shape
    w = two_w // 2
    for t in range(n_tiles):
        bu = _dot(ub[:, t * ch:(t + 1) * ch], wb_ref[t])
        bre_ref[:, t * w:(t + 1) * w] = bu[:, :w]
        bim_ref[:, t * w:(t + 1) * w] = bu[:, w:]


def _s5_out(u, hre_ref, him_ref, wcr_ref, wci_ref, d_ref, wglu_ref):
    n_tiles, w, _ = wcr_ref.shape
    ys = []
    for t in range(n_tiles):
        hr = hre_ref[:, t * w:(t + 1) * w].astype(BF16)
        hi = him_ref[:, t * w:(t + 1) * w].astype(BF16)
        ys.append(_dot(hr, wcr_ref[t]) - _dot(hi, wci_ref[t]))
    y = jax.nn.gelu(jnp.concatenate(ys, axis=-1) + d_ref[...] * u)
    z = _dot(y.astype(BF16), wglu_ref[...])
    half = z.shape[-1] // 2
    return z[:, :half] * jax.nn.sigmoid(z[:, half:])


def _s5_prompt_body(u_ref, wb_ref, wcr_ref, wci_ref, d_ref, wglu_ref, tab_ref,
                    o_ref, hre_ref, him_ref, bre_ref, bim_ref, cre_ref, cim_ref, *, lane_group):
    ci = pl.program_id(1)
    rows, width = bre_ref.shape

    @pl.when(ci == 0)
    def _():
        cre_ref[...] = jnp.zeros_like(cre_ref)
        cim_ref[...] = jnp.zeros_like(cim_ref)

    u = u_ref[0]
    _s5_in_proj(u, wb_ref, bre_ref, bim_ref)

    for lg in range(width // lane_group):
        ls = slice(lg * lane_group, (lg + 1) * lane_group)

        def tile(i, carry, ls=ls):
            cr, cim = carry
            r0 = pl.multiple_of(i * SUBLANES, SUBLANES)
            xr = bre_ref[pl.ds(r0, SUBLANES), ls]
            xi = bim_ref[pl.ds(r0, SUBLANES), ls]
            for j, shift in enumerate((1, 2, 4)):
                rr = pltpu.roll(xr, shift, 0)
                ri = pltpu.roll(xi, shift, 0)
                ar = tab_ref[2 * j, :, ls]
                ai = tab_ref[2 * j + 1, :, ls]
                xr, xi = xr + ar * rr - ai * ri, xi + ar * ri + ai * rr
            pr = tab_ref[6, :, ls]
            pi = tab_ref[7, :, ls]
            xr, xi = xr + pr * cr - pi * cim, xi + pr * cim + pi * cr
            bre_ref[pl.ds(r0, SUBLANES), ls] = xr
            bim_ref[pl.ds(r0, SUBLANES), ls] = xi
            last = SUBLANES - 1
            return (jnp.broadcast_to(xr[last:, :], xr.shape), jnp.broadcast_to(xi[last:, :], xi.shape))

        cr, cim = lax.fori_loop(0, rows // SUBLANES, tile, (cre_ref[:, ls], cim_ref[:, ls]))
        cre_ref[:, ls] = cr
        cim_ref[:, ls] = cim

    o_ref[0] = _s5_out(u, bre_ref, bim_ref, wcr_ref, wci_ref, d_ref, wglu_ref)

    @pl.when(ci == pl.num_programs(1) - 1)
    def _():
        hre_ref[0] = bre_ref[rows - 1:rows, :]
        him_ref[0] = bim_ref[rows - 1:rows, :]


def _s5_prompt(u, params, d_skip, w_glu, *, rows):
    ab_re, ab_im, pw_re, pw_im, w_in, wc_re, wc_im = params
    nb, length, ch = u.shape
    width = ab_re.shape[1]
    row_idx = jnp.arange(SUBLANES)[:, None]
    tabs = []
    for shift in (1, 2, 4):
        keep = row_idx >= shift
        tabs += [jnp.where(keep, pw_re[shift - 1][None, :], 0.0),
                 jnp.where(keep, pw_im[shift - 1][None, :], 0.0)]
    tab = jnp.stack(tabs + [pw_re, pw_im])
    state = jax.ShapeDtypeStruct((nb, 1, width), F32)
    state_spec = pl.BlockSpec((1, 1, width), lambda b, c: (b, 0, 0))
    tok_spec = pl.BlockSpec((1, rows, ch), lambda b, c: (b, c, 0))
    return pl.pallas_call(
        functools.partial(_s5_prompt_body, lane_group=4 * LANES),
        grid=(nb, length // rows),
        in_specs=[tok_spec, _resident(w_in.shape), _resident(wc_re.shape), _resident(wc_im.shape),
                  _resident((1, ch)), _resident(w_glu.shape), _resident(tab.shape)],
        out_specs=[tok_spec, state_spec, state_spec],
        out_shape=[jax.ShapeDtypeStruct((nb, length, ch), F32), state, state],
        scratch_shapes=[pltpu.VMEM((rows, width), F32), pltpu.VMEM((rows, width), F32),
                        pltpu.VMEM((SUBLANES, width), F32), pltpu.VMEM((SUBLANES, width), F32)],
        compiler_params=_params("parallel", "arbitrary"),
        name="s5_prompt",
    )(u, w_in, wc_re, wc_im, d_skip.reshape(1, ch), w_glu.astype(BF16), tab)


def _s5_sample_body(u_ref, h0r_ref, h0i_ref, wb_ref, wcr_ref, wci_ref, d_ref, wglu_ref, ab_ref,
                    o_ref, hre_ref, him_ref, bre_ref, bim_ref):
    hre_ref[...] = h0r_ref[...]
    him_ref[...] = h0i_ref[...]
    ar = ab_ref[0:1, :]
    ai = ab_ref[1:2, :]
    for t in range(u_ref.shape[0]):
        u = u_ref[t]
        _s5_in_proj(u, wb_ref, bre_ref, bim_ref)
        hr = hre_ref[...]
        hi = him_ref[...]
        hre_ref[...] = ar * hr - ai * hi + bre_ref[...]
        him_ref[...] = ar * hi + ai * hr + bim_ref[...]
        o_ref[t] = _s5_out(u, hre_ref, him_ref, wcr_ref, wci_ref, d_ref, wglu_ref)


def _s5_sample(u, h0_re, h0_im, params, d_skip, w_glu):
    ab_re, ab_im, _, _, w_in, wc_re, wc_im = params
    nt, nb, ch = u.shape
    width = ab_re.shape[1]
    ab = jnp.concatenate([ab_re, ab_im], axis=0)
    whole = lambda shape: pl.BlockSpec(shape, lambda: (0,) * len(shape))
    state = jax.ShapeDtypeStruct((nb, width), F32)
    return pl.pallas_call(
        _s5_sample_body,
        in_specs=[whole(u.shape), whole((nb, width)), whole((nb, width)), whole(w_in.shape),
                  whole(wc_re.shape), whole(wc_im.shape), whole((1, ch)), whole(w_glu.shape),
                  whole(ab.shape)],
        out_specs=[whole(u.shape), whole((nb, width)), whole((nb, width))],
        out_shape=[jax.ShapeDtypeStruct(u.shape, F32), state, state],
        scratch_shapes=[pltpu.VMEM((nb, width), F32), pltpu.VMEM((nb, width), F32)],
        compiler_params=pltpu.CompilerParams(vmem_limit_bytes=VMEM_LIMIT_BYTES),
        name="s5_sample",
    )(u, h0_re, h0_im, w_in, wc_re, wc_im, d_skip.reshape(1, ch), w_glu.astype(BF16), ab)


HEADS_PER_STEP = 2


def _fox_prompt_body(q_ref, k_ref, vt_ref, o_ref, m_ref, acc_ref, *, head_dim, q_chunk):
    qi = pl.program_id(2)
    tq = q_ref.shape[1]
    m_ref[...] = jnp.full_like(m_ref, NEG_INF)
    acc_ref[...] = jnp.zeros_like(acc_ref)

    def block(kj, diagonal):
        ks = pl.multiple_of(kj * tq, tq)
        chains = [(h, slice(h * LANES, (h + 1) * LANES), slice(c * q_chunk, (c + 1) * q_chunk))
                  for h in range(HEADS_PER_STEP) for c in range(tq // q_chunk)]
        scores = [_dot_nt(k_ref[0, pl.ds(ks, tq), hl], q_ref[0, qs, hl]) for _, hl, qs in chains]
        if diagonal:
            key = lax.broadcasted_iota(jnp.int32, (tq, q_chunk), 0)
            qry = lax.broadcasted_iota(jnp.int32, (tq, q_chunk), 1)
            scores = [jnp.where(key <= qry + qs.start, s, NEG_INF) for s, (_, _, qs) in zip(scores, chains)]
        probs, alphas = [], []
        for s, (h, _, qs) in zip(scores, chains):
            m_old = m_ref[h, :, qs]
            m_new = jnp.maximum(m_old, jnp.max(s, axis=0, keepdims=True))
            alphas.append(jnp.exp(m_old - m_new))
            probs.append(jnp.exp(s - m_new).astype(BF16))
            m_ref[h, :, qs] = m_new
        for p, alpha, (h, hl, qs) in zip(probs, alphas, chains):
            acc_ref[h, :, qs] = alpha * acc_ref[h, :, qs] + _dot(vt_ref[hl, pl.ds(ks, tq)], p)

    def body(kj, carry):
        block(kj, False)
        return carry

    lax.fori_loop(0, qi, body, 0)
    block(qi, True)
    outs = [acc_ref[h, :head_dim, :] / acc_ref[h, head_dim:head_dim + 1, :] for h in range(HEADS_PER_STEP)]
    o_ref[0] = jnp.concatenate(outs, axis=0).T


def _fox_prompt(qa, ka, vta, *, heads, head_dim, tq):
    nb, length, _ = qa.shape
    wide = HEADS_PER_STEP * LANES
    return pl.pallas_call(
        functools.partial(_fox_prompt_body, head_dim=head_dim, q_chunk=MXU_DIM),
        grid=(nb, heads // HEADS_PER_STEP, length // tq),
        in_specs=[pl.BlockSpec((1, tq, wide), lambda b, hp, i: (b, i, hp)),
                  pl.BlockSpec((1, length, wide), lambda b, hp, i: (b, 0, hp)),
                  pl.BlockSpec((wide, length), lambda b, hp, i: (hp, b))],
        out_specs=pl.BlockSpec((1, tq, HEADS_PER_STEP * head_dim), lambda b, hp, i: (b, i, hp)),
        out_shape=jax.ShapeDtypeStruct((nb, length, heads * head_dim), F32),
        scratch_shapes=[pltpu.VMEM((HEADS_PER_STEP, 1, tq), F32),
                        pltpu.VMEM((HEADS_PER_STEP, LANES, tq), F32)],
        compiler_params=_params("parallel", "parallel", "arbitrary"),
        name="fox_prompt",
    )(qa, ka, vta)


def _fox_sample_body(pt_ref, q_ref, *refs, pages_per_step, head_dim):
    n = pages_per_step
    kt_refs, vt_refs, lf_refs = refs[:n], refs[n:2 * n], refs[2 * n:3 * n]
    kn_ref, vn_ref, lfn_ref, o_ref, qrow_ref, m_ref, l_ref, acc_ref, carry_ref = refs[3 * n:]
    del pt_ref
    g = pl.program_id(1)
    n_tok, w = q_ref.shape
    hp, page = lf_refs[0].shape
    head_of_lane = lax.broadcasted_iota(jnp.int32, (hp, w), 1) // head_dim
    head_mask = head_of_lane == lax.broadcasted_iota(jnp.int32, (hp, w), 0)
    tri = _upper_tri()

    @pl.when(g == 0)
    def _():
        zero = jnp.zeros((hp, w), F32)
        rows = [jnp.where(head_mask, jnp.broadcast_to(q_ref[t:t + 1, :], (hp, w)), zero)
                for t in range(n_tok)]
        qrow_ref[...] = jnp.concatenate(rows, axis=0).astype(BF16)
        m_ref[...] = jnp.full_like(m_ref, NEG_INF)
        l_ref[...] = jnp.zeros_like(l_ref)
        acc_ref[...] = jnp.zeros_like(acc_ref)
        carry_ref[...] = jnp.zeros_like(carry_ref)

    def update(pages, valid=None):
        scores, base = [], carry_ref[...]
        for s, lf, _ in pages:
            c = _cumsum_lanes(lf, tri) + base
            base = jnp.broadcast_to(c[:, page - 1:], c.shape)
            s = s - jnp.concatenate([c] * n_tok, axis=0)
            scores.append(s if valid is None else jnp.where(valid, s, NEG_INF))
        carry_ref[...] = base
        m_old = m_ref[...]
        m_new = functools.reduce(jnp.maximum, [jnp.max(s, axis=1, keepdims=True) for s in scores], m_old)
        alpha = jnp.exp(m_old - m_new)
        probs = [jnp.exp(s - m_new) for s in scores]
        l_ref[...] = alpha * l_ref[...] + sum(jnp.sum(p, axis=1, keepdims=True) for p in probs)
        acc_ref[...] = alpha * acc_ref[...] + sum(pv(p.astype(BF16)) for p, (_, _, pv) in zip(probs, pages))
        m_ref[...] = m_new

    def cached(i):
        kt = kt_refs[i][...].reshape(w, page).astype(BF16)
        vt = vt_refs[i][...].reshape(w, page).astype(BF16)
        return _dot(qrow_ref[...], kt), lf_refs[i][...], lambda p: _dot_nt(p, vt)

    update([cached(i) for i in range(n)])

    @pl.when(g == pl.num_programs(1) - 1)
    def _():
        pad = jnp.zeros((page - kn_ref.shape[0], w), F32)
        kn = jnp.concatenate([kn_ref[...], pad], axis=0).astype(BF16)
        vn = jnp.concatenate([vn_ref[...], pad], axis=0).astype(BF16)
        key = lax.broadcasted_iota(jnp.int32, (n_tok * hp, page), 1)
        tok = lax.broadcasted_iota(jnp.int32, (n_tok * hp, page), 0) // hp
        update([(_dot_nt(qrow_ref[...], kn), lfn_ref[...], lambda p: _dot(p, vn))], key <= tok)
        out = acc_ref[...] / l_ref[...]
        for t in range(n_tok):
            picked = jnp.where(head_mask, out[t * hp:(t + 1) * hp, :], 0.0)
            o_ref[t:t + 1, :] = jnp.sum(picked, axis=0, keepdims=True)


def _fox_sample(q, k_new, v_new, lft_new, cache_kt, cache_vt, cache_lft, page_table, *, layer, pages_per_step):
    ns, n_tok, w = q.shape
    n_pages = page_table.shape[1]
    _, _, heads, head_dim, page = cache_kt.shape
    hp = cache_lft.shape[1]
    n = pages_per_step
    pad_rows = lambda a: jnp.pad(a, ((0, 0), (0, SUBLANES - n_tok), (0, 0)))

    def kv_page(i):
        return pl.BlockSpec((None, None, heads, head_dim, page),
                            lambda s, g, pt: (layer, pt[s, g * n + i], 0, 0, 0))

    def lf_page(i):
        return pl.BlockSpec((None, hp, page), lambda s, g, pt: (pt[s, g * n + i], 0, 0))

    per_seq = lambda shape: pl.BlockSpec((None,) + shape, lambda s, g, pt: (s, 0, 0))
    in_specs = [per_seq((n_tok, w))] + [kv_page(i) for i in range(n)] * 2 + [lf_page(i) for i in range(n)]
    in_specs += [per_seq((SUBLANES, w)), per_seq((SUBLANES, w)), per_seq((hp, page))]
    rows = n_tok * hp
    return pl.pallas_call(
        functools.partial(_fox_sample_body, pages_per_step=n, head_dim=head_dim),
        grid_spec=pltpu.PrefetchScalarGridSpec(
            num_scalar_prefetch=1,
            grid=(ns, n_pages // n),
            in_specs=in_specs,
            out_specs=per_seq((n_tok, w)),
            scratch_shapes=[pltpu.VMEM((rows, w), BF16), pltpu.VMEM((rows, 1), F32),
                            pltpu.VMEM((rows, 1), F32), pltpu.VMEM((rows, w), F32),
                            pltpu.VMEM((hp, page), F32)],
        ),
        out_shape=jax.ShapeDtypeStruct((ns, n_tok, w), F32),
        compiler_params=_params("parallel", "arbitrary"),
        name="fox_sample",
    )(page_table, q, *([cache_kt] * n), *([cache_vt] * n), *([cache_lft] * n),
      pad_rows(k_new), pad_rows(v_new), lft_new)


def _softmax_pv(s, vb):
    m = jnp.max(s, axis=-1, keepdims=True)
    p = jnp.exp(s - m)
    return _dot(p.astype(BF16), vb) / jnp.sum(p, axis=-1, keepdims=True)


def _cross_out_prompt_body(x_ref, mix_ref, qc_ref, mk_ref, mv_ref, wom_ref, woc_ref, o_ref,
                           *, heads, head_dim):
    tm, cw = qc_ref.shape
    q = qc_ref[...] * head_dim ** -0.5
    head_of_lane = lax.broadcasted_iota(jnp.int32, (1, cw), 1) // head_dim
    zero = jnp.zeros_like(q)
    q4 = jnp.concatenate([jnp.where(head_of_lane == h, q, zero) for h in range(heads)], axis=0)
    s = _dot_nt(q4.astype(BF16), mk_ref[0].astype(BF16))
    o4 = _softmax_pv(s, mv_ref[0].astype(BF16))
    cross = zero
    for h in range(heads):
        cross = cross + jnp.where(head_of_lane == h, o4[h * tm:(h + 1) * tm, :], zero)
    o_ref[...] = (x_ref[...] + _dot(mix_ref[...].astype(BF16), wom_ref[...])
                  + _dot(cross.astype(BF16), woc_ref[...]))


def _cross_out_prompt(x, mix, qc, mk, mv, wom, woc, *, heads, head_dim, tm):
    t, d = x.shape
    nb, n_mem, cw = mk.shape
    per_b = t // nb // tm
    row = lambda n: pl.BlockSpec((tm, n), lambda i: (i, 0))
    mem = pl.BlockSpec((1, n_mem, cw), lambda i: (i // per_b, 0, 0))
    return pl.pallas_call(
        functools.partial(_cross_out_prompt_body, heads=heads, head_dim=head_dim),
        grid=(t // tm,),
        in_specs=[row(d), row(mix.shape[1]), row(cw), mem, mem, _resident(wom.shape),
                  _resident(woc.shape)],
        out_specs=row(d),
        out_shape=jax.ShapeDtypeStruct((t, d), F32),
        compiler_params=_params("parallel"),
        name="cross_out_prompt",
    )(x, mix, qc, mk, mv, wom, woc)


def _cross_out_sample_body(x_ref, mix_ref, qc_ref, mkt_ref, mvt_ref, wom_ref, woc_ref, o_ref, cross_ref,
                           *, n_tok):
    n_seq, _, head_dim, n_mem = mkt_ref.shape
    cw = qc_ref.shape[1]
    scale = head_dim ** -0.5
    head_of_lane = lax.broadcasted_iota(jnp.int32, (SUBLANES, cw), 1) // head_dim
    head_mask = head_of_lane == lax.broadcasted_iota(jnp.int32, (SUBLANES, cw), 0)
    zero = jnp.zeros((SUBLANES, cw), F32)
    for i in range(n_seq):
        rows = [jnp.where(head_mask, jnp.broadcast_to(qc_ref[pl.ds(i * n_tok + t, 1), :] * scale,
                                                     (SUBLANES, cw)), zero) for t in range(n_tok)]
        q = jnp.concatenate(rows, axis=0).astype(BF16)
        s = _dot(q, mkt_ref[i].reshape(cw, n_mem).astype(BF16))
        p = jnp.exp(s - jnp.max(s, axis=-1, keepdims=True))
        o = _dot_nt(p.astype(BF16), mvt_ref[i].reshape(cw, n_mem).astype(BF16)) / jnp.sum(p, axis=-1, keepdims=True)
        for t in range(n_tok):
            picked = jnp.where(head_mask, o[t * SUBLANES:(t + 1) * SUBLANES, :], zero)
            cross_ref[pl.ds(i * n_tok + t, 1), :] = jnp.sum(picked, axis=0, keepdims=True)
    o_ref[...] = (x_ref[...] + _dot(mix_ref[...].astype(BF16), wom_ref[...])
                  + _dot(cross_ref[...].astype(BF16), woc_ref[...]))


def _cross_out_sample(x, mix, qc, mkt, mvt, wom, woc, *, layer, n_tok, seqs_per_step):
    t, d = x.shape
    _, ns, heads, head_dim, n_mem = mkt.shape
    cw = heads * head_dim
    tm = seqs_per_step * n_tok
    row = lambda n: pl.BlockSpec((tm, n), lambda i: (i, 0))
    mem = pl.BlockSpec((None, seqs_per_step, heads, head_dim, n_mem), lambda i: (layer, i, 0, 0, 0))
    return pl.pallas_call(
        functools.partial(_cross_out_sample_body, n_tok=n_tok),
        grid=(ns // seqs_per_step,),
        in_specs=[row(d), row(mix.shape[1]), row(cw), mem, mem, _resident(wom.shape),
                  _resident(woc.shape)],
        out_specs=row(d),
        out_shape=jax.ShapeDtypeStruct((t, d), F32),
        scratch_shapes=[pltpu.VMEM((tm, cw), F32)],
        compiler_params=_params("parallel"),
        name="cross_out_sample",
    )(x, mix, qc, mkt, mvt, wom, woc)


def kernel(x_prompt, x_sample, mem_prompt, state_s5_re, state_s5_im, cache_fox_k, cache_fox_v, cache_fox_logf, cache_mem_k, cache_mem_v, page_table, ffn_norm, ffn_w_gate, ffn_w_up, ffn_w_down, norm_mix, norm_mem, w_mem_kv, w_in_s5, s5_a_re, s5_a_im, s5_log_dt, s5_b_re, s5_b_im, s5_c_re, s5_c_im, s5_d, s5_w_glu, w_in_fox, fox_b_f, w_out, norm_final):
    nb, seq, d = x_prompt.shape
    ns, n_tok, _ = x_sample.shape
    depth = ffn_norm.shape[0]
    n_mem = mem_prompt.shape[1]
    cross_heads, head_dim = cache_mem_k.shape[3], cache_mem_k.shape[4]
    cross_width = cross_heads * head_dim
    mixer_width = w_out.shape[1] - cross_width
    fox_heads = cache_fox_k.shape[3]
    n_phys, page = cache_fox_k.shape[1], cache_fox_k.shape[2]
    groups, n_state = state_s5_re.shape[2], state_s5_re.shape[3]
    hp = 2 * SUBLANES

    tm_p = 512
    tm_s = ns * n_tok
    xp = x_prompt.reshape(nb * seq, d)
    xs = x_sample.reshape(tm_s, d)
    mem = mem_prompt.reshape(nb * n_mem, d)

    s5_re_p, s5_im_p, s5_re_s, s5_im_s = [], [], [], []
    fk_p, fv_p, fl_p, fk_s, fv_s, fl_s = [], [], [], [], [], []
    mk_list, mv_list = [], []
    for i in range(depth):
        j = i // N_MIXERS
        ffn_w = lambda half: (ffn_norm[i, half], ffn_w_gate[i, half], ffn_w_up[i, half], ffn_w_down[i, half])
        xp = _ffn(xp, *ffn_w(0), tm=tm_p)
        xs = _ffn(xs, *ffn_w(0), tm=tm_s)

        mk_p, mv_p = _norm_proj(mem, norm_mem[i],
                                [w_mem_kv[i][:, :cross_width], w_mem_kv[i][:, cross_width:]], tm=nb * n_mem)
        mk_list.append(mk_p.reshape(nb, n_mem, cross_heads, head_dim))
        mv_list.append(mv_p.reshape(nb, n_mem, cross_heads, head_dim))

        if i % N_MIXERS == 0:
            w_in = [w_in_s5[j][:, :mixer_width], w_in_s5[j][:, mixer_width:]]
            u_p, qc_p = _norm_proj(xp, norm_mix[i], w_in, tm=tm_p)
            u_s, qc_s = _norm_proj(xs, norm_mix[i], w_in, tm=tm_s)
            params = _s5_params(s5_a_re[j], s5_a_im[j], s5_log_dt[j], s5_b_re[j], s5_b_im[j],
                                s5_c_re[j], s5_c_im[j])
            mix_p, hr_p, hi_p = _s5_prompt(u_p.reshape(nb, seq, mixer_width), params, s5_d[j],
                                           s5_w_glu[j], rows=tm_p)
            mix_p = mix_p.reshape(nb * seq, mixer_width)
            u_t = u_s.reshape(ns, n_tok, mixer_width).transpose(1, 0, 2)
            mix_t, hr_s, hi_s = _s5_sample(u_t, state_s5_re[j].reshape(ns, groups * n_state),
                                           state_s5_im[j].reshape(ns, groups * n_state),
                                           params, s5_d[j], s5_w_glu[j])
            mix_s = mix_t.transpose(1, 0, 2).reshape(tm_s, mixer_width)
            s5_re_p.append(hr_p.reshape(nb, groups, n_state))
            s5_im_p.append(hi_p.reshape(nb, groups, n_state))
            s5_re_s.append(hr_s.reshape(ns, groups, n_state))
            s5_im_s.append(hi_s.reshape(ns, groups, n_state))
        else:
            fox = dict(heads=fox_heads, head_dim=head_dim, cross_width=cross_width)
            kt_p, vt_p, qc_p, lf_p, qa_p, ka_p, vta_p = _fox_proj_prompt(
                xp, norm_mix[i], w_in_fox[j], fox_b_f[j], tm=tm_p, tiles_per_seq=seq // tm_p, **fox)
            q_s, k_s, v_s, qc_s, lf_s, lft_s = _fox_proj_sample(
                xs, norm_mix[i], w_in_fox[j], fox_b_f[j], tm=tm_s, **fox)

            seq3 = lambda a: a.reshape(nb, seq, a.shape[-1])
            mix_p = _fox_prompt(seq3(qa_p), seq3(ka_p), vta_p, heads=fox_heads, head_dim=head_dim, tq=tm_p)
            mix_p = mix_p.reshape(nb * seq, mixer_width)

            tok3 = lambda a: a.reshape(ns, n_tok, mixer_width)
            lft_new = jnp.pad(lft_s.reshape(hp, ns, n_tok).transpose(1, 0, 2),
                              ((0, 0), (0, 0), (0, page - n_tok)))
            cache_lft = jnp.pad(cache_fox_logf[j].transpose(0, 2, 1), ((0, 0), (0, hp - fox_heads), (0, 0)))
            to_stored = lambda a: a.transpose(0, 1, 3, 4, 2)
            mix_s = _fox_sample(tok3(q_s), tok3(k_s), tok3(v_s), lft_new, to_stored(cache_fox_k),
                                to_stored(cache_fox_v), cache_lft, page_table, layer=j, pages_per_step=4)
            mix_s = mix_s.reshape(tm_s, mixer_width)

            from_stored = lambda a: a.reshape(nb, fox_heads, head_dim, seq).transpose(0, 3, 1, 2)
            fk_p.append(from_stored(kt_p))
            fv_p.append(from_stored(vt_p))
            fl_p.append(lf_p[:, :fox_heads].reshape(nb, seq, fox_heads))
            fk_s.append(k_s.reshape(ns, n_tok, fox_heads, head_dim))
            fv_s.append(v_s.reshape(ns, n_tok, fox_heads, head_dim))
            fl_s.append(lf_s[:, :fox_heads].reshape(ns, n_tok, fox_heads))

        wom = w_out[i][:mixer_width].astype(BF16)
        woc = w_out[i][mixer_width:].astype(BF16)
        xp = _cross_out_prompt(xp, mix_p, qc_p, mk_p.reshape(nb, n_mem, cross_width),
                               mv_p.reshape(nb, n_mem, cross_width), wom, woc,
                               heads=cross_heads, head_dim=head_dim, tm=tm_p)
        xs = _cross_out_sample(xs, mix_s, qc_s, cache_mem_k.transpose(0, 1, 3, 4, 2),
                               cache_mem_v.transpose(0, 1, 3, 4, 2), wom, woc,
                               layer=i, n_tok=n_tok, seqs_per_step=8)

        g_final = norm_final if i == depth - 1 else None
        xp = _ffn(xp, *ffn_w(1), g_final, tm=tm_p)
        xs = _ffn(xs, *ffn_w(1), g_final, tm=tm_s)

    return (xp.reshape(nb, seq, d), xs.reshape(ns, n_tok, d),
            jnp.stack(s5_re_p), jnp.stack(s5_im_p), jnp.stack(s5_re_s), jnp.stack(s5_im_s),
            jnp.stack(fk_p), jnp.stack(fv_p), jnp.stack(fl_p),
            jnp.stack(fk_s), jnp.stack(fv_s), jnp.stack(fl_s),
            jnp.stack(mk_list), jnp.stack(mv_list))
```

```python
import functools

import jax
import jax.numpy as jnp
from jax import lax
from jax.experimental import pallas as pl
from jax.experimental.pallas import tpu as pltpu

F32 = jnp.float32
BF16 = jnp.bfloat16

RMS_EPS = 1e-6
NEG_INF = -1e30
N_MIXERS = 2

LANES = 128
SUBLANES = 8
MXU_DIM = 256
VMEM_LIMIT_BYTES = 56 * 1024 * 1024

NT_DIMS = (((1,), (1,)), ((), ()))


def _params(*sem):
    return pltpu.CompilerParams(dimension_semantics=sem, vmem_limit_bytes=VMEM_LIMIT_BYTES)


def _resident(shape):
    nd = len(shape)
    return pl.BlockSpec(shape, lambda *_: (0,) * nd, pipeline_mode=pl.Buffered(1))


def _rms(x, g):
    return x * lax.rsqrt(jnp.mean(x * x, axis=-1, keepdims=True) + RMS_EPS) * g


def _dot(a, b):
    return jnp.dot(a, b, preferred_element_type=F32)


def _dot_nt(a, b):
    return lax.dot_general(a, b, NT_DIMS, preferred_element_type=F32)


def _log_sigmoid(x):
    return jnp.minimum(x, 0.0) - jnp.log1p(jnp.exp(-jnp.abs(x)))


def _split3(x):
    hi = x.astype(BF16)
    r1 = x - hi.astype(F32)
    mid = r1.astype(BF16)
    lo = (r1 - mid.astype(F32)).astype(BF16)
    return hi, mid, lo


def _cumsum_lanes(x, tri):
    hi, mid, lo = _split3(x)
    return _dot(hi, tri) + _dot(mid, tri) + _dot(lo, tri)


def _upper_tri():
    r = lax.broadcasted_iota(jnp.int32, (LANES, LANES), 0)
    c = lax.broadcasted_iota(jnp.int32, (LANES, LANES), 1)
    return jnp.where(r <= c, 1.0, 0.0).astype(BF16)


def _ffn_body(*refs, d_ff, ff_chunk, final):
    if final:
        x_ref, g_ref, wg_ref, wu_ref, wd_ref, gf_ref, o_ref, t_ref = refs
    else:
        x_ref, g_ref, wg_ref, wu_ref, wd_ref, o_ref, t_ref = refs
    x = x_ref[...]
    h = _rms(x, g_ref[...]).astype(BF16)
    for c in range(d_ff // ff_chunk):
        lo, hi = c * ff_chunk, (c + 1) * ff_chunk
        a = _dot(h, wg_ref[:, lo:hi])
        b = _dot(h, wu_ref[:, lo:hi])
        t_ref[:, lo:hi] = (jax.nn.silu(a) * b).astype(BF16)
    y = x + 0.5 * _dot(t_ref[...], wd_ref[...])
    if final:
        y = _rms(y, gf_ref[...])
    o_ref[...] = y


def _cast_body(x_ref, o_ref):
    o_ref[...] = x_ref[...].astype(o_ref.dtype)


def _to_bf16(w, *, rows):
    n, r, c = w.shape
    spec = pl.BlockSpec((None, rows, c), lambda i, j: (i, j, 0))
    return pl.pallas_call(
        _cast_body, grid=(n, r // rows), in_specs=[spec], out_specs=spec,
        out_shape=jax.ShapeDtypeStruct(w.shape, BF16),
        compiler_params=_params("parallel", "parallel"), name="to_bf16",
    )(w)


def _ffn(x, g, wg, wu, wd, idx, g_final=None, *, tm):
    t, d = x.shape
    d_ff = wg.shape[2]
    final = g_final is not None
    row = pl.BlockSpec((tm, d), lambda i: (i, 0))
    pick = lambda r, c: pl.BlockSpec((None, r, c), lambda i: (idx, 0, 0), pipeline_mode=pl.Buffered(1))
    in_specs = [row, _resident((1, d)), pick(d, d_ff), pick(d, d_ff), pick(d_ff, d)]
    args = [x, g.reshape(1, d), wg, wu, wd]
    if final:
        in_specs.append(_resident((1, d)))
        args.append(g_final.reshape(1, d))
    return pl.pallas_call(
        functools.partial(_ffn_body, d_ff=d_ff, ff_chunk=MXU_DIM, final=final),
        grid=(t // tm,),
        in_specs=in_specs,
        out_specs=row,
        out_shape=jax.ShapeDtypeStruct((t, d), F32),
        scratch_shapes=[pltpu.VMEM((tm, d_ff), BF16)],
        compiler_params=_params("parallel"),
        name="ffn_final" if final else "ffn",
    )(*args)


def _norm_proj_body(x_ref, g_ref, *refs, n):
    h = _rms(x_ref[...], g_ref[...]).astype(BF16)
    for w_ref, o_ref in zip(refs[:n], refs[n:]):
        o_ref[...] = _dot(h, w_ref[...])


def _norm_proj(x, g, ws, *, tm):
    t, d = x.shape
    n = len(ws)
    in_specs = [pl.BlockSpec((tm, d), lambda i: (i, 0)), _resident((1, d))]
    in_specs += [_resident(w.shape) for w in ws]
    return pl.pallas_call(
        functools.partial(_norm_proj_body, n=n),
        grid=(t // tm,),
        in_specs=in_specs,
        out_specs=[pl.BlockSpec((tm, w.shape[1]), lambda i: (i, 0)) for w in ws],
        out_shape=[jax.ShapeDtypeStruct((t, w.shape[1]), F32) for w in ws],
        compiler_params=_params("parallel"),
        name="norm_proj",
    )(x, g.reshape(1, d), *[w.astype(BF16) for w in ws])


def _fox_split(w_in, b_f, heads, head_dim):
    w = heads * head_dim
    wq, wk, wv = w_in[:, :w], w_in[:, w:2 * w], w_in[:, 2 * w:3 * w]
    wf = jnp.pad(w_in[:, 3 * w:3 * w + heads], ((0, 0), (0, LANES - heads)))
    wc = w_in[:, 3 * w + heads:]
    bf = jnp.pad(b_f, (0, LANES - heads)).reshape(1, LANES)
    return wq, wk, wv, wf, wc, bf


def _fox_proj_sample_body(x_ref, g_ref, wq_ref, wk_ref, wv_ref, wc_ref, wf_ref, wft_ref, bf_ref, bft_ref,
                          q_ref, k_ref, v_ref, qc_ref, lf_ref, lft_ref, *, scale):
    h = _rms(x_ref[...], g_ref[...]).astype(BF16)
    q_ref[...] = _dot(h, wq_ref[...]) * scale
    k_ref[...] = _dot(h, wk_ref[...])
    v_ref[...] = _dot(h, wv_ref[...])
    qc_ref[...] = _dot(h, wc_ref[...])
    lf_ref[...] = _log_sigmoid(_dot(h, wf_ref[...]) + bf_ref[...])
    lft_ref[...] = _log_sigmoid(_dot_nt(wft_ref[...], h) + bft_ref[...])


def _fox_proj_sample(x, g, w_in, b_f, *, heads, head_dim, cross_width, tm):
    t, d = x.shape
    w = heads * head_dim
    hp = 2 * SUBLANES
    wq, wk, wv, wf, wc, bf = _fox_split(w_in, b_f, heads, head_dim)
    wft = wf[:, :hp].T
    bft = bf[0, :hp].reshape(hp, 1)
    row = lambda n: pl.BlockSpec((tm, n), lambda i: (i, 0))
    outs = [w, w, w, cross_width, LANES]
    return pl.pallas_call(
        functools.partial(_fox_proj_sample_body, scale=head_dim ** -0.5),
        grid=(t // tm,),
        in_specs=[row(d), _resident((1, d)), _resident((d, w)), _resident((d, w)),
                  _resident((d, w)), _resident((d, cross_width)), _resident((d, LANES)),
                  _resident((hp, d)), _resident((1, LANES)), _resident((hp, 1))],
        out_specs=[row(n) for n in outs] + [pl.BlockSpec((hp, tm), lambda i: (0, i))],
        out_shape=[jax.ShapeDtypeStruct((t, n), F32) for n in outs]
        + [jax.ShapeDtypeStruct((hp, t), F32)],
        compiler_params=_params("parallel"),
        name="fox_proj_sample",
    )(x, g.reshape(1, d), wq.astype(BF16), wk.astype(BF16), wv.astype(BF16), wc.astype(BF16),
      wf.astype(BF16), wft.astype(BF16), bf, bft)


N_BIAS = 3


def _fox_proj_prompt_body(x_ref, g_ref, wqa_ref, wkt_ref, wka_ref, wvt_ref, wc_ref, wf_ref,
                          bf_ref, place_ref, oneq_ref,
                          kt_ref, vt_ref, qc_ref, lf_ref, qa_ref, ka_ref, vta_ref, carry_ref,
                          *, scale, tiles_per_seq, head_dim):
    @pl.when(pl.program_id(0) % tiles_per_seq == 0)
    def _():
        carry_ref[...] = jnp.zeros_like(carry_ref)

    h = _rms(x_ref[...], g_ref[...]).astype(BF16)
    qa_ref[...] = (_dot(h, wqa_ref[...]) * scale + oneq_ref[...]).astype(BF16)
    kt_ref[...] = _dot_nt(wkt_ref[...], h)
    vt = _dot_nt(wvt_ref[...], h)
    vt_ref[...] = vt
    tm = vt.shape[1]
    tail = jnp.where(lax.broadcasted_iota(jnp.int32, (LANES - head_dim, tm), 0) == 0, 1.0, 0.0)
    blocks = []
    for r0 in range(0, vt.shape[0], head_dim):
        blocks += [vt[r0:r0 + head_dim, :], tail]
    vta_ref[...] = jnp.concatenate(blocks, axis=0).astype(BF16)
    qc_ref[...] = _dot(h, wc_ref[...])
    lf = _log_sigmoid(_dot(h, wf_ref[...]) + bf_ref[...])
    lf_ref[...] = lf
    r = lax.broadcasted_iota(jnp.int32, (tm, tm), 0)
    c = lax.broadcasted_iota(jnp.int32, (tm, tm), 1)
    low = jnp.where(c <= r, 1.0, 0.0).astype(BF16)
    cum = carry_ref[0:1, :] + sum(_dot(low, piece) for piece in _split3(lf))
    carry_ref[...] = jnp.broadcast_to(cum[tm - 1:, :], carry_ref.shape)
    bias = sum(_dot(piece, place_ref[j]) for j, piece in enumerate(_split3(-cum)))
    ka_ref[...] = (_dot(h, wka_ref[...]) + bias).astype(BF16)


def _fox_proj_prompt(x, g, w_in, b_f, *, heads, head_dim, cross_width, tm, tiles_per_seq):
    t, d = x.shape
    w = heads * head_dim
    wa = heads * LANES
    wq, wk, wv, wf, wc, bf = _fox_split(w_in, b_f, heads, head_dim)
    aug = lambda a: jnp.pad(a.reshape(d, heads, head_dim), ((0, 0), (0, 0), (0, LANES - head_dim))).reshape(d, wa)
    head = jnp.arange(heads)
    place = jnp.zeros((N_BIAS, LANES, wa), F32)
    oneq = jnp.zeros((1, wa), F32)
    for j in range(N_BIAS):
        place = place.at[j, head, head * LANES + head_dim + j].set(1.0)
        oneq = oneq.at[0, head * LANES + head_dim + j].set(1.0)
    row = lambda n: pl.BlockSpec((tm, n), lambda i: (i, 0))
    col = lambda n: pl.BlockSpec((None, n, tm), lambda i: (i // tiles_per_seq, 0, i % tiles_per_seq))
    seq_len = tm * tiles_per_seq
    outs = [(cross_width, F32), (LANES, F32), (wa, BF16), (wa, BF16)]
    return pl.pallas_call(
        functools.partial(_fox_proj_prompt_body, scale=head_dim ** -0.5, tiles_per_seq=tiles_per_seq,
                          head_dim=head_dim),
        grid=(t // tm,),
        in_specs=[row(d), _resident((1, d)), _resident((d, wa)), _resident((w, d)), _resident((d, wa)),
                  _resident((w, d)), _resident((d, cross_width)),
                  _resident((d, LANES)), _resident((1, LANES)), _resident(place.shape),
                  _resident((1, wa))],
        out_specs=[col(w), col(w)] + [row(n) for n, _ in outs] + [pl.BlockSpec((wa, tm), lambda i: (0, i))],
        out_shape=[jax.ShapeDtypeStruct((t // seq_len, w, seq_len), F32)] * 2
        + [jax.ShapeDtypeStruct((t, n), dt) for n, dt in outs]
        + [jax.ShapeDtypeStruct((wa, t), BF16)],
        scratch_shapes=[pltpu.VMEM((SUBLANES, LANES), F32)],
        compiler_params=_params("arbitrary"),
        name="fox_proj_prompt",
    )(x, g.reshape(1, d), aug(wq).astype(BF16), wk.T.astype(BF16), aug(wk).astype(BF16),
      wv.T.astype(BF16), wc.astype(BF16), wf.astype(BF16), bf, place.astype(BF16), oneq)


def _s5_param_body(are_ref, aim_ref, ldt_ref, bre_ref, bim_ref,
                   abr_ref, abi_ref, bbr_ref, bbi_ref, pwr_ref, pwi_ref):
    a_re = are_ref[...]
    a_im = aim_ref[...]
    dt = jnp.exp(ldt_ref[...])
    mag = jnp.exp(dt * a_re)
    ab_re = mag * jnp.cos(dt * a_im)
    ab_im = mag * jnp.sin(dt * a_im)
    den = a_re * a_re + a_im * a_im
    nr = ab_re - 1.0
    ni = ab_im
    zr = (nr * a_re + ni * a_im) / den
    zi = (ni * a_re - nr * a_im) / den
    b_re = bre_ref[...]
    b_im = bim_ref[...]
    abr_ref[...] = ab_re
    abi_ref[...] = ab_im
    bbr_ref[...] = zr * b_re - zi * b_im
    bbi_ref[...] = zr * b_im + zi * b_re
    pr, pi = ab_re, ab_im
    for r in range(SUBLANES):
        pwr_ref[r] = pr
        pwi_ref[r] = pi
        pr, pi = pr * ab_re - pi * ab_im, pr * ab_im + pi * ab_re


def _s5_params(a_re, a_im, log_dt, b_re, b_im, c_re, c_im):
    g, p, c = b_re.shape
    rows = g * c
    rep = lambda a: jnp.repeat(a, c, axis=0)
    bt = lambda b: b.transpose(0, 2, 1).reshape(rows, p)
    full = pl.BlockSpec((rows, p), lambda: (0, 0))
    pw_spec = pl.BlockSpec((SUBLANES, rows, p), lambda: (0, 0, 0))
    abr, abi, bbr, bbi, pwr, pwi = pl.pallas_call(
        _s5_param_body,
        in_specs=[full, full, pl.BlockSpec((rows, 1), lambda: (0, 0)), full, full],
        out_specs=[full, full, full, full, pw_spec, pw_spec],
        out_shape=[jax.ShapeDtypeStruct((rows, p), F32)] * 4
        + [jax.ShapeDtypeStruct((SUBLANES, rows, p), F32)] * 2,
        name="s5_params",
    )(rep(a_re), rep(a_im), rep(log_dt.reshape(g, 1)), bt(b_re), bt(b_im))
    width = g * p
    flat = lambda a: a[::c].reshape(1, width)
    flat_pw = lambda a: a[:, ::c].reshape(SUBLANES, width)
    gpt = MXU_DIM // c
    n_tiles = g // gpt
    eye = jnp.eye(gpt, dtype=F32)

    def in_tiles(bb):
        blocks = bb.reshape(n_tiles, gpt, c, p)
        return jnp.einsum("tgcp,gh->tgchp", blocks, eye).reshape(n_tiles, gpt * c, gpt * p)

    def out_tiles(cc):
        blocks = cc.reshape(n_tiles, gpt, c, p)
        return jnp.einsum("tgcp,gh->tgphc", blocks, eye).reshape(n_tiles, gpt * p, gpt * c)

    w_in = jnp.concatenate([in_tiles(bbr), in_tiles(bbi)], axis=-1).astype(BF16)
    return (flat(abr), flat(abi), flat_pw(pwr), flat_pw(pwi), w_in,
            out_tiles(c_re).astype(BF16), out_tiles(c_im).astype(BF16))


def _s5_in_proj(u, wb_ref, bre_ref, bim_ref):
    ub = u.astype(BF16)
    n_tiles, ch, two_w = wb_ref.shape
    w = two_w // 2
    for t in range(n_tiles):
        bu = _dot(ub[:, t * ch:(t + 1) * ch], wb_ref[t])
        bre_ref[:, t * w:(t + 1) * w] = bu[:, :w]
        bim_ref[:, t * w:(t + 1) * w] = bu[:, w:]


def _s5_out(u, hre_ref, him_ref, wcr_ref, wci_ref, d_ref, wglu_ref):
    n_tiles, w, _ = wcr_ref.shape
    ys = []
    for t in range(n_tiles):
        hr = hre_ref[:, t * w:(t + 1) * w].astype(BF16)
        hi = him_ref[:, t * w:(t + 1) * w].astype(BF16)
        ys.append(_dot(hr, wcr_ref[t]) - _dot(hi, wci_ref[t]))
    y = jax.nn.gelu(jnp.concatenate(ys, axis=-1) + d_ref[...] * u)
    z = _dot(y.astype(BF16), wglu_ref[...])
    half = z.shape[-1] // 2
    return z[:, :half] * jax.nn.sigmoid(z[:, half:])


def _s5_prompt_body(u_ref, wb_ref, wcr_ref, wci_ref, d_ref, wglu_ref, tab_ref,
                    o_ref, hre_ref, him_ref, bre_ref, bim_ref, cre_ref, cim_ref, *, lane_group):
    ci = pl.program_id(1)
    rows, width = bre_ref.shape

    @pl.when(ci == 0)
    def _():
        cre_ref[...] = jnp.zeros_like(cre_ref)
        cim_ref[...] = jnp.zeros_like(cim_ref)

    u = u_ref[0]
    _s5_in_proj(u, wb_ref, bre_ref, bim_ref)

    for lg in range(width // lane_group):
        ls = slice(lg * lane_group, (lg + 1) * lane_group)

        def tile(i, carry, ls=ls):
            cr, cim = carry
            r0 = pl.multiple_of(i * SUBLANES, SUBLANES)
            xr = bre_ref[pl.ds(r0, SUBLANES), ls]
            xi = bim_ref[pl.ds(r0, SUBLANES), ls]
            for j, shift in enumerate((1, 2, 4)):
                rr = pltpu.roll(xr, shift, 0)
                ri = pltpu.roll(xi, shift, 0)
                ar = tab_ref[2 * j, :, ls]
                ai = tab_ref[2 * j + 1, :, ls]
                xr, xi = xr + ar * rr - ai * ri, xi + ar * ri + ai * rr
            pr = tab_ref[6, :, ls]
            pi = tab_ref[7, :, ls]
            xr, xi = xr + pr * cr - pi * cim, xi + pr * cim + pi * cr
            bre_ref[pl.ds(r0, SUBLANES), ls] = xr
            bim_ref[pl.ds(r0, SUBLANES), ls] = xi
            last = SUBLANES - 1
            return (jnp.broadcast_to(xr[last:, :], xr.shape), jnp.broadcast_to(xi[last:, :], xi.shape))

        cr, cim = lax.fori_loop(0, rows // SUBLANES, tile, (cre_ref[:, ls], cim_ref[:, ls]))
        cre_ref[:, ls] = cr
        cim_ref[:, ls] = cim

    o_ref[0] = _s5_out(u, bre_ref, bim_ref, wcr_ref, wci_ref, d_ref, wglu_ref)

    @pl.when(ci == pl.num_programs(1) - 1)
    def _():
        hre_ref[0] = bre_ref[rows - 1:rows, :]
        him_ref[0] = bim_ref[rows - 1:rows, :]


def _s5_prompt(u, params, d_skip, w_glu, *, rows):
    ab_re, ab_im, pw_re, pw_im, w_in, wc_re, wc_im = params
    nb, length, ch = u.shape
    width = ab_re.shape[1]
    row_idx = jnp.arange(SUBLANES)[:, None]
    tabs = []
    for shift in (1, 2, 4):
        keep = row_idx >= shift
        tabs += [jnp.where(keep, pw_re[shift - 1][None, :], 0.0),
                 jnp.where(keep, pw_im[shift - 1][None, :], 0.0)]
    tab = jnp.stack(tabs + [pw_re, pw_im])
    state = jax.ShapeDtypeStruct((nb, 1, width), F32)
    state_spec = pl.BlockSpec((1, 1, width), lambda b, c: (b, 0, 0))
    tok_spec = pl.BlockSpec((1, rows, ch), lambda b, c: (b, c, 0))
    return pl.pallas_call(
        functools.partial(_s5_prompt_body, lane_group=4 * LANES),
        grid=(nb, length // rows),
        in_specs=[tok_spec, _resident(w_in.shape), _resident(wc_re.shape), _resident(wc_im.shape),
                  _resident((1, ch)), _resident(w_glu.shape), _resident(tab.shape)],
        out_specs=[tok_spec, state_spec, state_spec],
        out_shape=[jax.ShapeDtypeStruct((nb, length, ch), F32), state, state],
        scratch_shapes=[pltpu.VMEM((rows, width), F32), pltpu.VMEM((rows, width), F32),
                        pltpu.VMEM((SUBLANES, width), F32), pltpu.VMEM((SUBLANES, width), F32)],
        compiler_params=_params("parallel", "arbitrary"),
        name="s5_prompt",
    )(u, w_in, wc_re, wc_im, d_skip.reshape(1, ch), w_glu.astype(BF16), tab)


def _s5_sample_body(u_ref, h0r_ref, h0i_ref, wb_ref, wcr_ref, wci_ref, d_ref, wglu_ref, ab_ref,
                    o_ref, hre_ref, him_ref, bre_ref, bim_ref):
    hre_ref[...] = h0r_ref[...]
    him_ref[...] = h0i_ref[...]
    ar = ab_ref[0:1, :]
    ai = ab_ref[1:2, :]
    for t in range(u_ref.shape[0]):
        u = u_ref[t]
        _s5_in_proj(u, wb_ref, bre_ref, bim_ref)
        hr = hre_ref[...]
        hi = him_ref[...]
        hre_ref[...] = ar * hr - ai * hi + bre_ref[...]
        him_ref[...] = ar * hi + ai * hr + bim_ref[...]
        o_ref[t] = _s5_out(u, hre_ref, him_ref, wcr_ref, wci_ref, d_ref, wglu_ref)


def _s5_sample(u, h0_re, h0_im, params, d_skip, w_glu):
    ab_re, ab_im, _, _, w_in, wc_re, wc_im = params
    nt, nb, ch = u.shape
    width = ab_re.shape[1]
    ab = jnp.concatenate([ab_re, ab_im], axis=0)
    whole = lambda shape: pl.BlockSpec(shape, lambda: (0,) * len(shape))
    state = jax.ShapeDtypeStruct((nb, width), F32)
    return pl.pallas_call(
        _s5_sample_body,
        in_specs=[whole(u.shape), whole((nb, width)), whole((nb, width)), whole(w_in.shape),
                  whole(wc_re.shape), whole(wc_im.shape), whole((1, ch)), whole(w_glu.shape),
                  whole(ab.shape)],
        out_specs=[whole(u.shape), whole((nb, width)), whole((nb, width))],
        out_shape=[jax.ShapeDtypeStruct(u.shape, F32), state, state],
        scratch_shapes=[pltpu.VMEM((nb, width), F32), pltpu.VMEM((nb, width), F32)],
        compiler_params=pltpu.CompilerParams(vmem_limit_bytes=VMEM_LIMIT_BYTES),
        name="s5_sample",
    )(u, h0_re, h0_im, w_in, wc_re, wc_im, d_skip.reshape(1, ch), w_glu.astype(BF16), ab)


HEADS_PER_STEP = 2


def _fox_prompt_body(q_ref, k_ref, vt_ref, o_ref, m_ref, acc_ref, *, head_dim, tk):
    qi = pl.program_id(2)
    tq = q_ref.shape[1]
    n_chunk = tq // tk
    m_ref[...] = jnp.full_like(m_ref, NEG_INF)
    acc_ref[...] = jnp.zeros_like(acc_ref)

    def block(ks, first_chunk, masked_chunk):
        chains = [(h, slice(h * LANES, (h + 1) * LANES), c, slice(c * tk, (c + 1) * tk))
                  for h in range(HEADS_PER_STEP) for c in range(first_chunk, n_chunk)]
        scores = []
        for _, hl, c, qs in chains:
            s = _dot_nt(k_ref[0, pl.ds(ks, tk), hl], q_ref[0, qs, hl])
            if c == masked_chunk:
                key = lax.broadcasted_iota(jnp.int32, s.shape, 0)
                qry = lax.broadcasted_iota(jnp.int32, s.shape, 1)
                s = jnp.where(key <= qry, s, NEG_INF)
            scores.append(s)
        probs, alphas = [], []
        for s, (h, _, _, qs) in zip(scores, chains):
            m_old = m_ref[h, :, qs]
            m_new = jnp.maximum(m_old, jnp.max(s, axis=0, keepdims=True))
            alphas.append(jnp.exp(m_old - m_new))
            probs.append(jnp.exp(s - m_new).astype(BF16))
            m_ref[h, :, qs] = m_new
        for p, alpha, (h, hl, _, qs) in zip(probs, alphas, chains):
            acc_ref[h, :, qs] = alpha * acc_ref[h, :, qs] + _dot(vt_ref[hl, pl.ds(ks, tk)], p)

    def body(kj, carry):
        block(pl.multiple_of(kj * tk, tk), 0, None)
        return carry

    lax.fori_loop(0, qi * n_chunk, body, 0)
    for c in range(n_chunk):
        block(pl.multiple_of((qi * n_chunk + c) * tk, tk), c, c)
    outs = [acc_ref[h, :head_dim, :] / acc_ref[h, head_dim:head_dim + 1, :] for h in range(HEADS_PER_STEP)]
    o_ref[0] = jnp.concatenate(outs, axis=0).T


def _fox_prompt(qa, ka, vta, *, heads, head_dim, tq, tk):
    nb, length, _ = qa.shape
    wide = HEADS_PER_STEP * LANES
    return pl.pallas_call(
        functools.partial(_fox_prompt_body, head_dim=head_dim, tk=tk),
        grid=(nb, heads // HEADS_PER_STEP, length // tq),
        in_specs=[pl.BlockSpec((1, tq, wide), lambda b, hp, i: (b, i, hp)),
                  pl.BlockSpec((1, length, wide), lambda b, hp, i: (b, 0, hp)),
                  pl.BlockSpec((wide, length), lambda b, hp, i: (hp, b))],
        out_specs=pl.BlockSpec((1, tq, HEADS_PER_STEP * head_dim), lambda b, hp, i: (b, i, hp)),
        out_shape=jax.ShapeDtypeStruct((nb, length, heads * head_dim), F32),
        scratch_shapes=[pltpu.VMEM((HEADS_PER_STEP, 1, tq), F32),
                        pltpu.VMEM((HEADS_PER_STEP, LANES, tq), F32)],
        compiler_params=_params("parallel", "parallel", "arbitrary"),
        name="fox_prompt",
    )(qa, ka, vta)


def _fox_sample_body(pt_ref, q_ref, *refs, pages_per_step, head_dim):
    n = pages_per_step
    kt_refs, vt_refs, lf_refs = refs[:n], refs[n:2 * n], refs[2 * n:3 * n]
    kn_ref, vn_ref, lfn_ref, o_ref, qrow_ref, m_ref, l_ref, acc_ref, carry_ref = refs[3 * n:]
    del pt_ref
    g = pl.program_id(1)
    n_tok, w = q_ref.shape
    hp, page = lf_refs[0].shape
    head_of_lane = lax.broadcasted_iota(jnp.int32, (hp, w), 1) // head_dim
    head_mask = head_of_lane == lax.broadcasted_iota(jnp.int32, (hp, w), 0)
    tri = _upper_tri()

    @pl.when(g == 0)
    def _():
        zero = jnp.zeros((hp, w), F32)
        rows = [jnp.where(head_mask, jnp.broadcast_to(q_ref[t:t + 1, :], (hp, w)), zero)
                for t in range(n_tok)]
        qrow_ref[...] = jnp.concatenate(rows, axis=0).astype(BF16)
        m_ref[...] = jnp.full_like(m_ref, NEG_INF)
        l_ref[...] = jnp.zeros_like(l_ref)
        acc_ref[...] = jnp.zeros_like(acc_ref)
        carry_ref[...] = jnp.zeros_like(carry_ref)

    def update(pages, valid=None):
        scores, base = [], carry_ref[...]
        for s, lf, _ in pages:
            c = _cumsum_lanes(lf, tri) + base
            base = jnp.broadcast_to(c[:, page - 1:], c.shape)
            s = s - jnp.concatenate([c] * n_tok, axis=0)
            scores.append(s if valid is None else jnp.where(valid, s, NEG_INF))
        carry_ref[...] = base
        m_old = m_ref[...]
        m_new = functools.reduce(jnp.maximum, [jnp.max(s, axis=1, keepdims=True) for s in scores], m_old)
        alpha = jnp.exp(m_old - m_new)
        probs = [jnp.exp(s - m_new) for s in scores]
        l_ref[...] = alpha * l_ref[...] + sum(jnp.sum(p, axis=1, keepdims=True) for p in probs)
        acc_ref[...] = alpha * acc_ref[...] + sum(pv(p.astype(BF16)) for p, (_, _, pv) in zip(probs, pages))
        m_ref[...] = m_new

    def cached(i):
        kt = kt_refs[i][...].reshape(w, page).astype(BF16)
        vt = vt_refs[i][...].reshape(w, page).astype(BF16)
        return _dot(qrow_ref[...], kt), lf_refs[i][...], lambda p: _dot_nt(p, vt)

    update([cached(i) for i in range(n)])

    @pl.when(g == pl.num_programs(1) - 1)
    def _():
        pad = jnp.zeros((page - kn_ref.shape[0], w), F32)
        kn = jnp.concatenate([kn_ref[...], pad], axis=0).astype(BF16)
        vn = jnp.concatenate([vn_ref[...], pad], axis=0).astype(BF16)
        key = lax.broadcasted_iota(jnp.int32, (n_tok * hp, page), 1)
        tok = lax.broadcasted_iota(jnp.int32, (n_tok * hp, page), 0) // hp
        update([(_dot_nt(qrow_ref[...], kn), lfn_ref[...], lambda p: _dot(p, vn))], key <= tok)
        out = acc_ref[...] / l_ref[...]
        for t in range(n_tok):
            picked = jnp.where(head_mask, out[t * hp:(t + 1) * hp, :], 0.0)
            o_ref[t:t + 1, :] = jnp.sum(picked, axis=0, keepdims=True)


def _fox_sample(q, k_new, v_new, lft_new, cache_kt, cache_vt, cache_lft, page_table, *, layer, pages_per_step):
    ns, n_tok, w = q.shape
    n_pages = page_table.shape[1]
    _, _, heads, head_dim, page = cache_kt.shape
    hp = cache_lft.shape[1]
    n = pages_per_step
    pad_rows = lambda a: jnp.pad(a, ((0, 0), (0, SUBLANES - n_tok), (0, 0)))

    def kv_page(i):
        return pl.BlockSpec((None, None, heads, head_dim, page),
                            lambda s, g, pt: (layer, pt[s, g * n + i], 0, 0, 0))

    def lf_page(i):
        return pl.BlockSpec((None, hp, page), lambda s, g, pt: (pt[s, g * n + i], 0, 0))

    per_seq = lambda shape: pl.BlockSpec((None,) + shape, lambda s, g, pt: (s, 0, 0))
    in_specs = [per_seq((n_tok, w))] + [kv_page(i) for i in range(n)] * 2 + [lf_page(i) for i in range(n)]
    in_specs += [per_seq((SUBLANES, w)), per_seq((SUBLANES, w)), per_seq((hp, page))]
    rows = n_tok * hp
    return pl.pallas_call(
        functools.partial(_fox_sample_body, pages_per_step=n, head_dim=head_dim),
        grid_spec=pltpu.PrefetchScalarGridSpec(
            num_scalar_prefetch=1,
            grid=(ns, n_pages // n),
            in_specs=in_specs,
            out_specs=per_seq((n_tok, w)),
            scratch_shapes=[pltpu.VMEM((rows, w), BF16), pltpu.VMEM((rows, 1), F32),
                            pltpu.VMEM((rows, 1), F32), pltpu.VMEM((rows, w), F32),
                            pltpu.VMEM((hp, page), F32)],
        ),
        out_shape=jax.ShapeDtypeStruct((ns, n_tok, w), F32),
        compiler_params=_params("parallel", "arbitrary"),
        name="fox_sample",
    )(page_table, q, *([cache_kt] * n), *([cache_vt] * n), *([cache_lft] * n),
      pad_rows(k_new), pad_rows(v_new), lft_new)


def _softmax_pv(s, vb):
    m = jnp.max(s, axis=-1, keepdims=True)
    p = jnp.exp(s - m)
    return _dot(p.astype(BF16), vb) / jnp.sum(p, axis=-1, keepdims=True)


def _cross_out_prompt_body(x_ref, mix_ref, qc_ref, mk_ref, mv_ref, wom_ref, woc_ref, o_ref,
                           *, heads, head_dim):
    tm, cw = qc_ref.shape
    q = qc_ref[...] * head_dim ** -0.5
    head_of_lane = lax.broadcasted_iota(jnp.int32, (1, cw), 1) // head_dim
    zero = jnp.zeros_like(q)
    q4 = jnp.concatenate([jnp.where(head_of_lane == h, q, zero) for h in range(heads)], axis=0)
    s = _dot_nt(q4.astype(BF16), mk_ref[0].astype(BF16))
    o4 = _softmax_pv(s, mv_ref[0].astype(BF16))
    cross = zero
    for h in range(heads):
        cross = cross + jnp.where(head_of_lane == h, o4[h * tm:(h + 1) * tm, :], zero)
    o_ref[...] = (x_ref[...] + _dot(mix_ref[...].astype(BF16), wom_ref[...])
                  + _dot(cross.astype(BF16), woc_ref[...]))


def _cross_out_prompt(x, mix, qc, mk, mv, wom, woc, *, heads, head_dim, tm):
    t, d = x.shape
    nb, n_mem, cw = mk.shape
    per_b = t // nb // tm
    row = lambda n: pl.BlockSpec((tm, n), lambda i: (i, 0))
    mem = pl.BlockSpec((1, n_mem, cw), lambda i: (i // per_b, 0, 0))
    return pl.pallas_call(
        functools.partial(_cross_out_prompt_body, heads=heads, head_dim=head_dim),
        grid=(t // tm,),
        in_specs=[row(d), row(mix.shape[1]), row(cw), mem, mem, _resident(wom.shape),
                  _resident(woc.shape)],
        out_specs=row(d),
        out_shape=jax.ShapeDtypeStruct((t, d), F32),
        compiler_params=_params("parallel"),
        name="cross_out_prompt",
    )(x, mix, qc, mk, mv, wom, woc)


def _cross_out_sample_body(x_ref, mix_ref, qc_ref, mkt_ref, mvt_ref, wom_ref, woc_ref, o_ref, cross_ref,
                           *, n_tok):
    n_seq, _, head_dim, n_mem = mkt_ref.shape
    cw = qc_ref.shape[1]
    scale = head_dim ** -0.5
    head_of_lane = lax.broadcasted_iota(jnp.int32, (SUBLANES, cw), 1) // head_dim
    head_mask = head_of_lane == lax.broadcasted_iota(jnp.int32, (SUBLANES, cw), 0)
    zero = jnp.zeros((SUBLANES, cw), F32)
    for i in range(n_seq):
        rows = [jnp.where(head_mask, jnp.broadcast_to(qc_ref[pl.ds(i * n_tok + t, 1), :] * scale,
                                                     (SUBLANES, cw)), zero) for t in range(n_tok)]
        q = jnp.concatenate(rows, axis=0).astype(BF16)
        s = _dot(q, mkt_ref[i].reshape(cw, n_mem).astype(BF16))
        p = jnp.exp(s - jnp.max(s, axis=-1, keepdims=True))
        o = _dot_nt(p.astype(BF16), mvt_ref[i].reshape(cw, n_mem).astype(BF16)) / jnp.sum(p, axis=-1, keepdims=True)
        for t in range(n_tok):
            picked = jnp.where(head_mask, o[t * SUBLANES:(t + 1) * SUBLANES, :], zero)
            cross_ref[pl.ds(i * n_tok + t, 1), :] = jnp.sum(picked, axis=0, keepdims=True)
    o_ref[...] = (x_ref[...] + _dot(mix_ref[...].astype(BF16), wom_ref[...])
                  + _dot(cross_ref[...].astype(BF16), woc_ref[...]))


def _cross_out_sample(x, mix, qc, mkt, mvt, wom, woc, *, layer, n_tok, seqs_per_step):
    t, d = x.shape
    _, ns, heads, head_dim, n_mem = mkt.shape
    cw = heads * head_dim
    tm = seqs_per_step * n_tok
    row = lambda n: pl.BlockSpec((tm, n), lambda i: (i, 0))
    mem = pl.BlockSpec((None, seqs_per_step, heads, head_dim, n_mem), lambda i: (layer, i, 0, 0, 0))
    return pl.pallas_call(
        functools.partial(_cross_out_sample_body, n_tok=n_tok),
        grid=(ns // seqs_per_step,),
        in_specs=[row(d), row(mix.shape[1]), row(cw), mem, mem, _resident(wom.shape),
                  _resident(woc.shape)],
        out_specs=row(d),
        out_shape=jax.ShapeDtypeStruct((t, d), F32),
        scratch_shapes=[pltpu.VMEM((tm, cw), F32)],
        compiler_params=_params("parallel"),
        name="cross_out_sample",
    )(x, mix, qc, mkt, mvt, wom, woc)


def kernel(x_prompt, x_sample, mem_prompt, state_s5_re, state_s5_im, cache_fox_k, cache_fox_v, cache_fox_logf, cache_mem_k, cache_mem_v, page_table, ffn_norm, ffn_w_gate, ffn_w_up, ffn_w_down, norm_mix, norm_mem, w_mem_kv, w_in_s5, s5_a_re, s5_a_im, s5_log_dt, s5_b_re, s5_b_im, s5_c_re, s5_c_im, s5_d, s5_w_glu, w_in_fox, fox_b_f, w_out, norm_final):
    nb, seq, d = x_prompt.shape
    ns, n_tok, _ = x_sample.shape
    depth = ffn_norm.shape[0]
    n_mem = mem_prompt.shape[1]
    cross_heads, head_dim = cache_mem_k.shape[3], cache_mem_k.shape[4]
    cross_width = cross_heads * head_dim
    mixer_width = w_out.shape[1] - cross_width
    fox_heads = cache_fox_k.shape[3]
    n_phys, page = cache_fox_k.shape[1], cache_fox_k.shape[2]
    groups, n_state = state_s5_re.shape[2], state_s5_re.shape[3]
    hp = 2 * SUBLANES

    tm_p = 512
    tm_s = ns * n_tok
    xp = x_prompt.reshape(nb * seq, d)
    xs = x_sample.reshape(tm_s, d)
    mem = mem_prompt.reshape(nb * n_mem, d)

    s5_re_p, s5_im_p, s5_re_s, s5_im_s = [], [], [], []
    fk_p, fv_p, fl_p, fk_s, fv_s, fl_s = [], [], [], [], [], []
    mk_list, mv_list = [], []
    d_ff = ffn_w_gate.shape[-1]
    wg_all = _to_bf16(ffn_w_gate.reshape(2 * depth, d, d_ff), rows=tm_p)
    wu_all = _to_bf16(ffn_w_up.reshape(2 * depth, d, d_ff), rows=tm_p)
    wd_all = _to_bf16(ffn_w_down.reshape(2 * depth, d_ff, d), rows=d_ff // 2)
    for i in range(depth):
        j = i // N_MIXERS
        ffn_w = lambda half: (ffn_norm[i, half], wg_all, wu_all, wd_all, 2 * i + half)
        xp = _ffn(xp, *ffn_w(0), tm=tm_p)
        xs = _ffn(xs, *ffn_w(0), tm=tm_s)

        mk_p, mv_p = _norm_proj(mem, norm_mem[i],
                                [w_mem_kv[i][:, :cross_width], w_mem_kv[i][:, cross_width:]], tm=nb * n_mem)
        mk_list.append(mk_p.reshape(nb, n_mem, cross_heads, head_dim))
        mv_list.append(mv_p.reshape(nb, n_mem, cross_heads, head_dim))

        if i % N_MIXERS == 0:
            w_in = [w_in_s5[j][:, :mixer_width], w_in_s5[j][:, mixer_width:]]
            u_p, qc_p = _norm_proj(xp, norm_mix[i], w_in, tm=tm_p)
            u_s, qc_s = _norm_proj(xs, norm_mix[i], w_in, tm=tm_s)
            params = _s5_params(s5_a_re[j], s5_a_im[j], s5_log_dt[j], s5_b_re[j], s5_b_im[j],
                                s5_c_re[j], s5_c_im[j])
            mix_p, hr_p, hi_p = _s5_prompt(u_p.reshape(nb, seq, mixer_width), params, s5_d[j],
                                           s5_w_glu[j], rows=tm_p)
            mix_p = mix_p.reshape(nb * seq, mixer_width)
            u_t = u_s.reshape(ns, n_tok, mixer_width).transpose(1, 0, 2)
            mix_t, hr_s, hi_s = _s5_sample(u_t, state_s5_re[j].reshape(ns, groups * n_state),
                                           state_s5_im[j].reshape(ns, groups * n_state),
                                           params, s5_d[j], s5_w_glu[j])
            mix_s = mix_t.transpose(1, 0, 2).reshape(tm_s, mixer_width)
            s5_re_p.append(hr_p.reshape(nb, groups, n_state))
            s5_im_p.append(hi_p.reshape(nb, groups, n_state))
            s5_re_s.append(hr_s.reshape(ns, groups, n_state))
            s5_im_s.append(hi_s.reshape(ns, groups, n_state))
        else:
            fox = dict(heads=fox_heads, head_dim=head_dim, cross_width=cross_width)
            kt_p, vt_p, qc_p, lf_p, qa_p, ka_p, vta_p = _fox_proj_prompt(
                xp, norm_mix[i], w_in_fox[j], fox_b_f[j], tm=tm_p, tiles_per_seq=seq // tm_p, **fox)
            q_s, k_s, v_s, qc_s, lf_s, lft_s = _fox_proj_sample(
                xs, norm_mix[i], w_in_fox[j], fox_b_f[j], tm=tm_s, **fox)

            seq3 = lambda a: a.reshape(nb, seq, a.shape[-1])
            mix_p = _fox_prompt(seq3(qa_p), seq3(ka_p), vta_p, heads=fox_heads, head_dim=head_dim,
                                tq=min(seq, 4 * tm_p), tk=tm_p)
            mix_p = mix_p.reshape(nb * seq, mixer_width)

            tok3 = lambda a: a.reshape(ns, n_tok, mixer_width)
            lft_new = jnp.pad(lft_s.reshape(hp, ns, n_tok).transpose(1, 0, 2),
                              ((0, 0), (0, 0), (0, page - n_tok)))
            cache_lft = jnp.pad(cache_fox_logf[j].transpose(0, 2, 1), ((0, 0), (0, hp - fox_heads), (0, 0)))
            to_stored = lambda a: a.transpose(0, 1, 3, 4, 2)
            mix_s = _fox_sample(tok3(q_s), tok3(k_s), tok3(v_s), lft_new, to_stored(cache_fox_k),
                                to_stored(cache_fox_v), cache_lft, page_table, layer=j,
                                pages_per_step=page_table.shape[1])
            mix_s = mix_s.reshape(tm_s, mixer_width)

            from_stored = lambda a: a.reshape(nb, fox_heads, head_dim, seq).transpose(0, 3, 1, 2)
            fk_p.append(from_stored(kt_p))
            fv_p.append(from_stored(vt_p))
            fl_p.append(lf_p[:, :fox_heads].reshape(nb, seq, fox_heads))
            fk_s.append(k_s.reshape(ns, n_tok, fox_heads, head_dim))
            fv_s.append(v_s.reshape(ns, n_tok, fox_heads, head_dim))
            fl_s.append(lf_s[:, :fox_heads].reshape(ns, n_tok, fox_heads))

        wom = w_out[i][:mixer_width].astype(BF16)
        woc = w_out[i][mixer_width:].astype(BF16)
        xp = _cross_out_prompt(xp, mix_p, qc_p, mk_p.reshape(nb, n_mem, cross_width),
                               mv_p.reshape(nb, n_mem, cross_width), wom, woc,
                               heads=cross_heads, head_dim=head_dim, tm=tm_p)
        xs = _cross_out_sample(xs, mix_s, qc_s, cache_mem_k.transpose(0, 1, 3, 4, 2),
                               cache_mem_v.transpose(0, 1, 3, 4, 2), wom, woc,
                               layer=i, n_tok=n_tok, seqs_per_step=8)

        g_final = norm_final if i == depth - 1 else None
        xp = _ffn(xp, *ffn_w(1), g_final, tm=tm_p)
        xs = _ffn(xs, *ffn_w(1), g_final, tm=tm_s)

    return (xp.reshape(nb, seq, d), xs.reshape(ns, n_tok, d),
            jnp.stack(s5_re_p), jnp.stack(s5_im_p), jnp.stack(s5_re_s), jnp.stack(s5_im_s),
            jnp.stack(fk_p), jnp.stack(fv_p), jnp.stack(fl_p),
            jnp.stack(fk_s), jnp.stack(fv_s), jnp.stack(fl_s),
            jnp.stack(mk_list), jnp.stack(mv_list))
```

```python
import functools

import jax
import jax.numpy as jnp
from jax import lax
from jax.experimental import pallas as pl
from jax.experimental.pallas import tpu as pltpu

F32 = jnp.float32
BF16 = jnp.bfloat16

RMS_EPS = 1e-6
NEG_INF = -1e30
N_MIXERS = 2

LANES = 128
SUBLANES = 8
MXU_DIM = 256
VMEM_LIMIT_BYTES = 56 * 1024 * 1024

NT_DIMS = (((1,), (1,)), ((), ()))


def _params(*sem):
    return pltpu.CompilerParams(dimension_semantics=sem, vmem_limit_bytes=VMEM_LIMIT_BYTES)


def _resident(shape):
    nd = len(shape)
    return pl.BlockSpec(shape, lambda *_: (0,) * nd, pipeline_mode=pl.Buffered(1))


def _rms(x, g):
    return x * lax.rsqrt(jnp.mean(x * x, axis=-1, keepdims=True) + RMS_EPS) * g


def _dot(a, b):
    return jnp.dot(a, b, preferred_element_type=F32)


def _dot_nt(a, b):
    return lax.dot_general(a, b, NT_DIMS, preferred_element_type=F32)


def _log_sigmoid(x):
    return jnp.minimum(x, 0.0) - jnp.log1p(jnp.exp(-jnp.abs(x)))


def _split3(x):
    hi = x.astype(BF16)
    r1 = x - hi.astype(F32)
    mid = r1.astype(BF16)
    lo = (r1 - mid.astype(F32)).astype(BF16)
    return hi, mid, lo


def _cumsum_lanes(x, tri):
    hi, mid, lo = _split3(x)
    return _dot(hi, tri) + _dot(mid, tri) + _dot(lo, tri)


def _upper_tri():
    r = lax.broadcasted_iota(jnp.int32, (LANES, LANES), 0)
    c = lax.broadcasted_iota(jnp.int32, (LANES, LANES), 1)
    return jnp.where(r <= c, 1.0, 0.0).astype(BF16)


def _ffn_body(*refs, d_ff, ff_chunk, final):
    if final:
        x_ref, g_ref, wg_ref, wu_ref, wd_ref, gf_ref, o_ref, t_ref = refs
    else:
        x_ref, g_ref, wg_ref, wu_ref, wd_ref, o_ref, t_ref = refs
    x = x_ref[...]
    h = _rms(x, g_ref[...]).astype(BF16)
    for c in range(d_ff // ff_chunk):
        lo, hi = c * ff_chunk, (c + 1) * ff_chunk
        a = _dot(h, wg_ref[:, lo:hi])
        b = _dot(h, wu_ref[:, lo:hi])
        t_ref[:, lo:hi] = (jax.nn.silu(a) * b).astype(BF16)
    y = x + 0.5 * _dot(t_ref[...], wd_ref[...])
    if final:
        y = _rms(y, gf_ref[...])
    o_ref[...] = y


def _cast_body(x_ref, o_ref):
    o_ref[...] = x_ref[...].astype(o_ref.dtype)


def _to_bf16(w, *, rows):
    n, r, c = w.shape
    spec = pl.BlockSpec((None, rows, c), lambda i, j: (i, j, 0))
    return pl.pallas_call(
        _cast_body, grid=(n, r // rows), in_specs=[spec], out_specs=spec,
        out_shape=jax.ShapeDtypeStruct(w.shape, BF16),
        compiler_params=_params("parallel", "parallel"), name="to_bf16",
    )(w)


def _ffn(x, g, wg, wu, wd, idx, g_final=None, *, tm):
    t, d = x.shape
    d_ff = wg.shape[2]
    final = g_final is not None
    row = pl.BlockSpec((tm, d), lambda i: (i, 0))
    pick = lambda r, c: pl.BlockSpec((None, r, c), lambda i: (idx, 0, 0), pipeline_mode=pl.Buffered(1))
    in_specs = [row, _resident((1, d)), pick(d, d_ff), pick(d, d_ff), pick(d_ff, d)]
    args = [x, g.reshape(1, d), wg, wu, wd]
    if final:
        in_specs.append(_resident((1, d)))
        args.append(g_final.reshape(1, d))
    return pl.pallas_call(
        functools.partial(_ffn_body, d_ff=d_ff, ff_chunk=MXU_DIM, final=final),
        grid=(t // tm,),
        in_specs=in_specs,
        out_specs=row,
        out_shape=jax.ShapeDtypeStruct((t, d), F32),
        scratch_shapes=[pltpu.VMEM((tm, d_ff), BF16)],
        compiler_params=_params("parallel"),
        name="ffn_final" if final else "ffn",
    )(*args)


def _norm_proj_body(x_ref, g_ref, *refs, n):
    h = _rms(x_ref[...], g_ref[...]).astype(BF16)
    for w_ref, o_ref in zip(refs[:n], refs[n:]):
        o_ref[...] = _dot(h, w_ref[...])


def _norm_proj(x, g, ws, *, tm):
    t, d = x.shape
    n = len(ws)
    in_specs = [pl.BlockSpec((tm, d), lambda i: (i, 0)), _resident((1, d))]
    in_specs += [_resident(w.shape) for w in ws]
    return pl.pallas_call(
        functools.partial(_norm_proj_body, n=n),
        grid=(t // tm,),
        in_specs=in_specs,
        out_specs=[pl.BlockSpec((tm, w.shape[1]), lambda i: (i, 0)) for w in ws],
        out_shape=[jax.ShapeDtypeStruct((t, w.shape[1]), F32) for w in ws],
        compiler_params=_params("parallel"),
        name="norm_proj",
    )(x, g.reshape(1, d), *[w.astype(BF16) for w in ws])


def _fox_split(w_in, b_f, heads, head_dim):
    w = heads * head_dim
    wq, wk, wv = w_in[:, :w], w_in[:, w:2 * w], w_in[:, 2 * w:3 * w]
    wf = jnp.pad(w_in[:, 3 * w:3 * w + heads], ((0, 0), (0, LANES - heads)))
    wc = w_in[:, 3 * w + heads:]
    bf = jnp.pad(b_f, (0, LANES - heads)).reshape(1, LANES)
    return wq, wk, wv, wf, wc, bf


def _fox_proj_sample_body(x_ref, g_ref, wq_ref, wk_ref, wv_ref, wc_ref, wf_ref, wft_ref, bf_ref, bft_ref,
                          q_ref, k_ref, v_ref, qc_ref, lf_ref, lft_ref, *, scale):
    h = _rms(x_ref[...], g_ref[...]).astype(BF16)
    q_ref[...] = _dot(h, wq_ref[...]) * scale
    k_ref[...] = _dot(h, wk_ref[...])
    v_ref[...] = _dot(h, wv_ref[...])
    qc_ref[...] = _dot(h, wc_ref[...])
    lf_ref[...] = _log_sigmoid(_dot(h, wf_ref[...]) + bf_ref[...])
    lft_ref[...] = _log_sigmoid(_dot_nt(wft_ref[...], h) + bft_ref[...])


def _fox_proj_sample(x, g, w_in, b_f, *, heads, head_dim, cross_width, tm):
    t, d = x.shape
    w = heads * head_dim
    hp = 2 * SUBLANES
    wq, wk, wv, wf, wc, bf = _fox_split(w_in, b_f, heads, head_dim)
    wft = wf[:, :hp].T
    bft = bf[0, :hp].reshape(hp, 1)
    row = lambda n: pl.BlockSpec((tm, n), lambda i: (i, 0))
    outs = [w, w, w, cross_width, LANES]
    return pl.pallas_call(
        functools.partial(_fox_proj_sample_body, scale=head_dim ** -0.5),
        grid=(t // tm,),
        in_specs=[row(d), _resident((1, d)), _resident((d, w)), _resident((d, w)),
                  _resident((d, w)), _resident((d, cross_width)), _resident((d, LANES)),
                  _resident((hp, d)), _resident((1, LANES)), _resident((hp, 1))],
        out_specs=[row(n) for n in outs] + [pl.BlockSpec((hp, tm), lambda i: (0, i))],
        out_shape=[jax.ShapeDtypeStruct((t, n), F32) for n in outs]
        + [jax.ShapeDtypeStruct((hp, t), F32)],
        compiler_params=_params("parallel"),
        name="fox_proj_sample",
    )(x, g.reshape(1, d), wq.astype(BF16), wk.astype(BF16), wv.astype(BF16), wc.astype(BF16),
      wf.astype(BF16), wft.astype(BF16), bf, bft)


N_BIAS = 3


NORM_SLACK = 1.02


def _fox_proj_prompt_body(x_ref, g_ref, wqa_ref, wkt_ref, wka_ref, wvt_ref, wc_ref, wf_ref,
                          bf_ref, place_ref, oneq_ref, headsum_ref,
                          kt_ref, vt_ref, qc_ref, lf_ref, qa_ref, ka_ref, vta_ref, stats_ref, carry_ref,
                          *, scale, tiles_per_seq, head_dim):
    tile = pl.program_id(0)

    @pl.when(tile % tiles_per_seq == 0)
    def _():
        carry_ref[...] = jnp.zeros_like(carry_ref)

    def max_norm(x):
        sq = _dot((x * x).astype(BF16), headsum_ref[...])
        return jnp.sqrt(jnp.max(sq, axis=0, keepdims=True)) * NORM_SLACK

    h = _rms(x_ref[...], g_ref[...]).astype(BF16)
    q_aug = _dot(h, wqa_ref[...]) * scale
    qa_ref[...] = (q_aug + oneq_ref[...]).astype(BF16)
    stats_ref[0, pl.ds(tile, 1), :] = max_norm(q_aug)
    kt_ref[...] = _dot_nt(wkt_ref[...], h)
    vt = _dot_nt(wvt_ref[...], h)
    vt_ref[...] = vt
    tm = vt.shape[1]
    tail = jnp.where(lax.broadcasted_iota(jnp.int32, (LANES - head_dim, tm), 0) == 0, 1.0, 0.0)
    blocks = []
    for r0 in range(0, vt.shape[0], head_dim):
        blocks += [vt[r0:r0 + head_dim, :], tail]
    vta_ref[...] = jnp.concatenate(blocks, axis=0).astype(BF16)
    qc_ref[...] = _dot(h, wc_ref[...])
    lf = _log_sigmoid(_dot(h, wf_ref[...]) + bf_ref[...])
    lf_ref[...] = lf
    r = lax.broadcasted_iota(jnp.int32, (tm, tm), 0)
    c = lax.broadcasted_iota(jnp.int32, (tm, tm), 1)
    low = jnp.where(c <= r, 1.0, 0.0).astype(BF16)
    cum = carry_ref[0:1, :] + sum(_dot(low, piece) for piece in _split3(lf))
    carry_ref[...] = jnp.broadcast_to(cum[tm - 1:, :], carry_ref.shape)
    bias = sum(_dot(piece, place_ref[j]) for j, piece in enumerate(_split3(-cum)))
    k_aug = _dot(h, wka_ref[...])
    ka_ref[...] = (k_aug + bias).astype(BF16)
    stats_ref[1, pl.ds(tile, 1), :] = max_norm(k_aug)
    stats_ref[2, pl.ds(tile, 1), :] = cum[0:1, :]
    stats_ref[3, pl.ds(tile, 1), :] = cum[tm - 1:, :]


def _fox_proj_prompt(x, g, w_in, b_f, *, heads, head_dim, cross_width, tm, tiles_per_seq):
    t, d = x.shape
    w = heads * head_dim
    wa = heads * LANES
    wq, wk, wv, wf, wc, bf = _fox_split(w_in, b_f, heads, head_dim)
    aug = lambda a: jnp.pad(a.reshape(d, heads, head_dim), ((0, 0), (0, 0), (0, LANES - head_dim))).reshape(d, wa)
    head = jnp.arange(heads)
    place = jnp.zeros((N_BIAS, LANES, wa), F32)
    oneq = jnp.zeros((1, wa), F32)
    for j in range(N_BIAS):
        place = place.at[j, head, head * LANES + head_dim + j].set(1.0)
        oneq = oneq.at[0, head * LANES + head_dim + j].set(1.0)
    headsum = (jnp.arange(wa)[:, None] // LANES == jnp.arange(LANES)[None, :]).astype(BF16)
    row = lambda n: pl.BlockSpec((tm, n), lambda i: (i, 0))
    col = lambda n: pl.BlockSpec((None, n, tm), lambda i: (i // tiles_per_seq, 0, i % tiles_per_seq))
    seq_len = tm * tiles_per_seq
    n_tiles = t // tm
    outs = [(cross_width, F32), (LANES, F32), (wa, BF16), (wa, BF16)]
    stats_shape = (4, n_tiles, LANES)
    return pl.pallas_call(
        functools.partial(_fox_proj_prompt_body, scale=head_dim ** -0.5, tiles_per_seq=tiles_per_seq,
                          head_dim=head_dim),
        grid=(n_tiles,),
        in_specs=[row(d), _resident((1, d)), _resident((d, wa)), _resident((w, d)), _resident((d, wa)),
                  _resident((w, d)), _resident((d, cross_width)),
                  _resident((d, LANES)), _resident((1, LANES)), _resident(place.shape),
                  _resident((1, wa)), _resident(headsum.shape)],
        out_specs=[col(w), col(w)] + [row(n) for n, _ in outs]
        + [pl.BlockSpec((wa, tm), lambda i: (0, i)), pl.BlockSpec(stats_shape, lambda i: (0, 0, 0))],
        out_shape=[jax.ShapeDtypeStruct((t // seq_len, w, seq_len), F32)] * 2
        + [jax.ShapeDtypeStruct((t, n), dt) for n, dt in outs]
        + [jax.ShapeDtypeStruct((wa, t), BF16), jax.ShapeDtypeStruct(stats_shape, F32)],
        scratch_shapes=[pltpu.VMEM((SUBLANES, LANES), F32)],
        compiler_params=_params("arbitrary"),
        name="fox_proj_prompt",
    )(x, g.reshape(1, d), aug(wq).astype(BF16), wk.T.astype(BF16), aug(wk).astype(BF16),
      wv.T.astype(BF16), wc.astype(BF16), wf.astype(BF16), bf, place.astype(BF16), oneq, headsum)


def _s5_param_body(are_ref, aim_ref, ldt_ref, bre_ref, bim_ref,
                   abr_ref, abi_ref, bbr_ref, bbi_ref, pwr_ref, pwi_ref):
    a_re = are_ref[...]
    a_im = aim_ref[...]
    dt = jnp.exp(ldt_ref[...])
    mag = jnp.exp(dt * a_re)
    ab_re = mag * jnp.cos(dt * a_im)
    ab_im = mag * jnp.sin(dt * a_im)
    den = a_re * a_re + a_im * a_im
    nr = ab_re - 1.0
    ni = ab_im
    zr = (nr * a_re + ni * a_im) / den
    zi = (ni * a_re - nr * a_im) / den
    b_re = bre_ref[...]
    b_im = bim_ref[...]
    abr_ref[...] = ab_re
    abi_ref[...] = ab_im
    bbr_ref[...] = zr * b_re - zi * b_im
    bbi_ref[...] = zr * b_im + zi * b_re
    pr, pi = ab_re, ab_im
    for r in range(SUBLANES):
        pwr_ref[r] = pr
        pwi_ref[r] = pi
        pr, pi = pr * ab_re - pi * ab_im, pr * ab_im + pi * ab_re


def _s5_params(a_re, a_im, log_dt, b_re, b_im, c_re, c_im):
    g, p, c = b_re.shape
    rows = g * c
    rep = lambda a: jnp.repeat(a, c, axis=0)
    bt = lambda b: b.transpose(0, 2, 1).reshape(rows, p)
    full = pl.BlockSpec((rows, p), lambda: (0, 0))
    pw_spec = pl.BlockSpec((SUBLANES, rows, p), lambda: (0, 0, 0))
    abr, abi, bbr, bbi, pwr, pwi = pl.pallas_call(
        _s5_param_body,
        in_specs=[full, full, pl.BlockSpec((rows, 1), lambda: (0, 0)), full, full],
        out_specs=[full, full, full, full, pw_spec, pw_spec],
        out_shape=[jax.ShapeDtypeStruct((rows, p), F32)] * 4
        + [jax.ShapeDtypeStruct((SUBLANES, rows, p), F32)] * 2,
        name="s5_params",
    )(rep(a_re), rep(a_im), rep(log_dt.reshape(g, 1)), bt(b_re), bt(b_im))
    width = g * p
    flat = lambda a: a[::c].reshape(1, width)
    flat_pw = lambda a: a[:, ::c].reshape(SUBLANES, width)
    gpt = MXU_DIM // c
    n_tiles = g // gpt
    eye = jnp.eye(gpt, dtype=F32)

    def in_tiles(bb):
        blocks = bb.reshape(n_tiles, gpt, c, p)
        return jnp.einsum("tgcp,gh->tgchp", blocks, eye).reshape(n_tiles, gpt * c, gpt * p)

    def out_tiles(cc):
        blocks = cc.reshape(n_tiles, gpt, c, p)
        return jnp.einsum("tgcp,gh->tgphc", blocks, eye).reshape(n_tiles, gpt * p, gpt * c)

    w_in = jnp.concatenate([in_tiles(bbr), in_tiles(bbi)], axis=-1).astype(BF16)
    return (flat(abr), flat(abi), flat_pw(pwr), flat_pw(pwi), w_in,
            out_tiles(c_re).astype(BF16), out_tiles(c_im).astype(BF16))


def _s5_in_proj(u, wb_ref, bre_ref, bim_ref):
    ub = u.astype(BF16)
    n_tiles, ch, two_w = wb_ref.shape
    w = two_w // 2
    for t in range(n_tiles):
        bu = _dot(ub[:, t * ch:(t + 1) * ch], wb_ref[t])
        bre_ref[:, t * w:(t + 1) * w] = bu[:, :w]
        bim_ref[:, t * w:(t + 1) * w] = bu[:, w:]


def _s5_out(u, hre_ref, him_ref, wcr_ref, wci_ref, d_ref, wglu_ref):
    n_tiles, w, _ = wcr_ref.shape
    ys = []
    for t in range(n_tiles):
        hr = hre_ref[:, t * w:(t + 1) * w].astype(BF16)
        hi = him_ref[:, t * w:(t + 1) * w].astype(BF16)
        ys.append(_dot(hr, wcr_ref[t]) - _dot(hi, wci_ref[t]))
    y = jax.nn.gelu(jnp.concatenate(ys, axis=-1) + d_ref[...] * u)
    z = _dot(y.astype(BF16), wglu_ref[...])
    half = z.shape[-1] // 2
    return z[:, :half] * jax.nn.sigmoid(z[:, half:])


def _s5_prompt_body(u_ref, wb_ref, wcr_ref, wci_ref, d_ref, wglu_ref, tab_ref,
                    o_ref, hre_ref, him_ref, bre_ref, bim_ref, cre_ref, cim_ref, *, lane_group):
    ci = pl.program_id(1)
    rows, width = bre_ref.shape

    @pl.when(ci == 0)
    def _():
        cre_ref[...] = jnp.zeros_like(cre_ref)
        cim_ref[...] = jnp.zeros_like(cim_ref)

    u = u_ref[0]
    _s5_in_proj(u, wb_ref, bre_ref, bim_ref)

    for lg in range(width // lane_group):
        ls = slice(lg * lane_group, (lg + 1) * lane_group)

        def tile(i, carry, ls=ls):
            cr, cim = carry
            r0 = pl.multiple_of(i * SUBLANES, SUBLANES)
            xr = bre_ref[pl.ds(r0, SUBLANES), ls]
            xi = bim_ref[pl.ds(r0, SUBLANES), ls]
            for j, shift in enumerate((1, 2, 4)):
                rr = pltpu.roll(xr, shift, 0)
                ri = pltpu.roll(xi, shift, 0)
                ar = tab_ref[2 * j, :, ls]
                ai = tab_ref[2 * j + 1, :, ls]
                xr, xi = xr + ar * rr - ai * ri, xi + ar * ri + ai * rr
            pr = tab_ref[6, :, ls]
            pi = tab_ref[7, :, ls]
            xr, xi = xr + pr * cr - pi * cim, xi + pr * cim + pi * cr
            bre_ref[pl.ds(r0, SUBLANES), ls] = xr
            bim_ref[pl.ds(r0, SUBLANES), ls] = xi
            last = SUBLANES - 1
            return (jnp.broadcast_to(xr[last:, :], xr.shape), jnp.broadcast_to(xi[last:, :], xi.shape))

        cr, cim = lax.fori_loop(0, rows // SUBLANES, tile, (cre_ref[:, ls], cim_ref[:, ls]))
        cre_ref[:, ls] = cr
        cim_ref[:, ls] = cim

    o_ref[0] = _s5_out(u, bre_ref, bim_ref, wcr_ref, wci_ref, d_ref, wglu_ref)

    @pl.when(ci == pl.num_programs(1) - 1)
    def _():
        hre_ref[0] = bre_ref[rows - 1:rows, :]
        him_ref[0] = bim_ref[rows - 1:rows, :]


def _s5_prompt(u, params, d_skip, w_glu, *, rows):
    ab_re, ab_im, pw_re, pw_im, w_in, wc_re, wc_im = params
    nb, length, ch = u.shape
    width = ab_re.shape[1]
    row_idx = jnp.arange(SUBLANES)[:, None]
    tabs = []
    for shift in (1, 2, 4):
        keep = row_idx >= shift
        tabs += [jnp.where(keep, pw_re[shift - 1][None, :], 0.0),
                 jnp.where(keep, pw_im[shift - 1][None, :], 0.0)]
    tab = jnp.stack(tabs + [pw_re, pw_im])
    state = jax.ShapeDtypeStruct((nb, 1, width), F32)
    state_spec = pl.BlockSpec((1, 1, width), lambda b, c: (b, 0, 0))
    tok_spec = pl.BlockSpec((1, rows, ch), lambda b, c: (b, c, 0))
    return pl.pallas_call(
        functools.partial(_s5_prompt_body, lane_group=4 * LANES),
        grid=(nb, length // rows),
        in_specs=[tok_spec, _resident(w_in.shape), _resident(wc_re.shape), _resident(wc_im.shape),
                  _resident((1, ch)), _resident(w_glu.shape), _resident(tab.shape)],
        out_specs=[tok_spec, state_spec, state_spec],
        out_shape=[jax.ShapeDtypeStruct((nb, length, ch), F32), state, state],
        scratch_shapes=[pltpu.VMEM((rows, width), F32), pltpu.VMEM((rows, width), F32),
                        pltpu.VMEM((SUBLANES, width), F32), pltpu.VMEM((SUBLANES, width), F32)],
        compiler_params=_params("parallel", "arbitrary"),
        name="s5_prompt",
    )(u, w_in, wc_re, wc_im, d_skip.reshape(1, ch), w_glu.astype(BF16), tab)


def _s5_sample_body(u_ref, h0r_ref, h0i_ref, wb_ref, wcr_ref, wci_ref, d_ref, wglu_ref, ab_ref,
                    o_ref, hre_ref, him_ref, bre_ref, bim_ref):
    hre_ref[...] = h0r_ref[...]
    him_ref[...] = h0i_ref[...]
    ar = ab_ref[0:1, :]
    ai = ab_ref[1:2, :]
    for t in range(u_ref.shape[0]):
        u = u_ref[t]
        _s5_in_proj(u, wb_ref, bre_ref, bim_ref)
        hr = hre_ref[...]
        hi = him_ref[...]
        hre_ref[...] = ar * hr - ai * hi + bre_ref[...]
        him_ref[...] = ar * hi + ai * hr + bim_ref[...]
        o_ref[t] = _s5_out(u, hre_ref, him_ref, wcr_ref, wci_ref, d_ref, wglu_ref)


def _s5_sample(u, h0_re, h0_im, params, d_skip, w_glu):
    ab_re, ab_im, _, _, w_in, wc_re, wc_im = params
    nt, nb, ch = u.shape
    width = ab_re.shape[1]
    ab = jnp.concatenate([ab_re, ab_im], axis=0)
    whole = lambda shape: pl.BlockSpec(shape, lambda: (0,) * len(shape))
    state = jax.ShapeDtypeStruct((nb, width), F32)
    return pl.pallas_call(
        _s5_sample_body,
        in_specs=[whole(u.shape), whole((nb, width)), whole((nb, width)), whole(w_in.shape),
                  whole(wc_re.shape), whole(wc_im.shape), whole((1, ch)), whole(w_glu.shape),
                  whole(ab.shape)],
        out_specs=[whole(u.shape), whole((nb, width)), whole((nb, width))],
        out_shape=[jax.ShapeDtypeStruct(u.shape, F32), state, state],
        scratch_shapes=[pltpu.VMEM((nb, width), F32), pltpu.VMEM((nb, width), F32)],
        compiler_params=pltpu.CompilerParams(vmem_limit_bytes=VMEM_LIMIT_BYTES),
        name="s5_sample",
    )(u, h0_re, h0_im, w_in, wc_re, wc_im, d_skip.reshape(1, ch), w_glu.astype(BF16), ab)


HEADS_PER_STEP = 2


UNDERFLOW_MARGIN = 110.0


def _first_live_block(st_ref, base, n_seq_blk, first_chunk, n_chunk, pair):
    kn = st_ref[1, pl.ds(base, n_seq_blk), :]
    c_last = st_ref[3, pl.ds(base, n_seq_blk), :]
    rows = [kn[0:1]]
    for j in range(1, n_seq_blk):
        rows.append(jnp.maximum(rows[-1], kn[j:j + 1]))
    kn_run = jnp.concatenate(rows, axis=0)
    worst = None
    for c in range(n_chunk):
        r = base + first_chunk + c
        term = st_ref[0, pl.ds(r, 1), :] * (kn_run + st_ref[1, pl.ds(r, 1), :]) + st_ref[2, pl.ds(r, 1), :]
        worst = term if worst is None else jnp.maximum(worst, term)
    blk = lax.broadcasted_iota(jnp.int32, worst.shape, 0)
    lane = lax.broadcasted_iota(jnp.int32, (1, LANES), 1)
    dead = (worst - c_last < -UNDERFLOW_MARGIN) & (blk < first_chunk)
    count = jnp.sum(jnp.where(dead, 1, 0), axis=0, keepdims=True)
    return jnp.min(jnp.where(lane // HEADS_PER_STEP == pair, count, n_seq_blk))


def _fox_prompt_body(q_ref, k_ref, vt_ref, st_ref, o_ref, m_ref, acc_ref, *, head_dim, tk):
    qi = pl.program_id(2)
    tq = q_ref.shape[1]
    n_chunk = tq // tk
    n_seq_blk = k_ref.shape[1] // tk
    first_live = _first_live_block(st_ref, pl.program_id(0) * n_seq_blk, n_seq_blk, qi * n_chunk, n_chunk,
                                   pl.program_id(1))
    m_ref[...] = jnp.full_like(m_ref, NEG_INF)
    acc_ref[...] = jnp.zeros_like(acc_ref)

    def block(ks, first_chunk, masked_chunk):
        chains = [(h, slice(h * LANES, (h + 1) * LANES), c, slice(c * tk, (c + 1) * tk))
                  for h in range(HEADS_PER_STEP) for c in range(first_chunk, n_chunk)]
        scores = []
        for _, hl, c, qs in chains:
            s = _dot_nt(k_ref[0, pl.ds(ks, tk), hl], q_ref[0, qs, hl])
            if c == masked_chunk:
                key = lax.broadcasted_iota(jnp.int32, s.shape, 0)
                qry = lax.broadcasted_iota(jnp.int32, s.shape, 1)
                s = jnp.where(key <= qry, s, NEG_INF)
            scores.append(s)
        probs, alphas = [], []
        for s, (h, _, _, qs) in zip(scores, chains):
            m_old = m_ref[h, :, qs]
            m_new = jnp.maximum(m_old, jnp.max(s, axis=0, keepdims=True))
            alphas.append(jnp.exp(m_old - m_new))
            probs.append(jnp.exp(s - m_new).astype(BF16))
            m_ref[h, :, qs] = m_new
        for p, alpha, (h, hl, _, qs) in zip(probs, alphas, chains):
            acc_ref[h, :, qs] = alpha * acc_ref[h, :, qs] + _dot(vt_ref[hl, pl.ds(ks, tk)], p)

    def body(kj, carry):
        block(pl.multiple_of(kj * tk, tk), 0, None)
        return carry

    lax.fori_loop(first_live, qi * n_chunk, body, 0)
    for c in range(n_chunk):
        block(pl.multiple_of((qi * n_chunk + c) * tk, tk), c, c)
    outs = [acc_ref[h, :head_dim, :] / acc_ref[h, head_dim:head_dim + 1, :] for h in range(HEADS_PER_STEP)]
    o_ref[0] = jnp.concatenate(outs, axis=0).T


def _fox_prompt(qa, ka, vta, stats, *, heads, head_dim, tq, tk):
    nb, length, _ = qa.shape
    wide = HEADS_PER_STEP * LANES
    return pl.pallas_call(
        functools.partial(_fox_prompt_body, head_dim=head_dim, tk=tk),
        grid=(nb, heads // HEADS_PER_STEP, length // tq),
        in_specs=[pl.BlockSpec((1, tq, wide), lambda b, hp, i: (b, i, hp)),
                  pl.BlockSpec((1, length, wide), lambda b, hp, i: (b, 0, hp)),
                  pl.BlockSpec((wide, length), lambda b, hp, i: (hp, b)),
                  pl.BlockSpec(stats.shape, lambda b, hp, i: (0, 0, 0))],
        out_specs=pl.BlockSpec((1, tq, HEADS_PER_STEP * head_dim), lambda b, hp, i: (b, i, hp)),
        out_shape=jax.ShapeDtypeStruct((nb, length, heads * head_dim), F32),
        scratch_shapes=[pltpu.VMEM((HEADS_PER_STEP, 1, tq), F32),
                        pltpu.VMEM((HEADS_PER_STEP, LANES, tq), F32)],
        compiler_params=_params("parallel", "parallel", "arbitrary"),
        name="fox_prompt",
    )(qa, ka, vta, stats)


def _fox_sample_body(pt_ref, q_ref, *refs, pages_per_step, head_dim):
    n = pages_per_step
    kt_refs, vt_refs, lf_refs = refs[:n], refs[n:2 * n], refs[2 * n:3 * n]
    kn_ref, vn_ref, lfn_ref, o_ref, qrow_ref, m_ref, l_ref, acc_ref, carry_ref = refs[3 * n:]
    del pt_ref
    g = pl.program_id(1)
    n_tok, w = q_ref.shape
    hp, page = lf_refs[0].shape
    head_of_lane = lax.broadcasted_iota(jnp.int32, (hp, w), 1) // head_dim
    head_mask = head_of_lane == lax.broadcasted_iota(jnp.int32, (hp, w), 0)
    tri = _upper_tri()

    @pl.when(g == 0)
    def _():
        zero = jnp.zeros((hp, w), F32)
        rows = [jnp.where(head_mask, jnp.broadcast_to(q_ref[t:t + 1, :], (hp, w)), zero)
                for t in range(n_tok)]
        qrow_ref[...] = jnp.concatenate(rows, axis=0).astype(BF16)
        m_ref[...] = jnp.full_like(m_ref, NEG_INF)
        l_ref[...] = jnp.zeros_like(l_ref)
        acc_ref[...] = jnp.zeros_like(acc_ref)
        carry_ref[...] = jnp.zeros_like(carry_ref)

    def update(pages, valid=None):
        scores, base = [], carry_ref[...]
        for s, lf, _ in pages:
            c = _cumsum_lanes(lf, tri) + base
            base = jnp.broadcast_to(c[:, page - 1:], c.shape)
            s = s - jnp.concatenate([c] * n_tok, axis=0)
            scores.append(s if valid is None else jnp.where(valid, s, NEG_INF))
        carry_ref[...] = base
        m_old = m_ref[...]
        m_new = functools.reduce(jnp.maximum, [jnp.max(s, axis=1, keepdims=True) for s in scores], m_old)
        alpha = jnp.exp(m_old - m_new)
        probs = [jnp.exp(s - m_new) for s in scores]
        l_ref[...] = alpha * l_ref[...] + sum(jnp.sum(p, axis=1, keepdims=True) for p in probs)
        acc_ref[...] = alpha * acc_ref[...] + sum(pv(p.astype(BF16)) for p, (_, _, pv) in zip(probs, pages))
        m_ref[...] = m_new

    def cached(i):
        kt = kt_refs[i][...].reshape(w, page).astype(BF16)
        vt = vt_refs[i][...].reshape(w, page).astype(BF16)
        return _dot(qrow_ref[...], kt), lf_refs[i][...], lambda p: _dot_nt(p, vt)

    update([cached(i) for i in range(n)])

    @pl.when(g == pl.num_programs(1) - 1)
    def _():
        pad = jnp.zeros((page - kn_ref.shape[0], w), F32)
        kn = jnp.concatenate([kn_ref[...], pad], axis=0).astype(BF16)
        vn = jnp.concatenate([vn_ref[...], pad], axis=0).astype(BF16)
        key = lax.broadcasted_iota(jnp.int32, (n_tok * hp, page), 1)
        tok = lax.broadcasted_iota(jnp.int32, (n_tok * hp, page), 0) // hp
        update([(_dot_nt(qrow_ref[...], kn), lfn_ref[...], lambda p: _dot(p, vn))], key <= tok)
        out = acc_ref[...] / l_ref[...]
        for t in range(n_tok):
            picked = jnp.where(head_mask, out[t * hp:(t + 1) * hp, :], 0.0)
            o_ref[t:t + 1, :] = jnp.sum(picked, axis=0, keepdims=True)


def _fox_sample(q, k_new, v_new, lft_new, cache_kt, cache_vt, cache_lft, page_table, *, layer, pages_per_step):
    ns, n_tok, w = q.shape
    n_pages = page_table.shape[1]
    _, _, heads, head_dim, page = cache_kt.shape
    hp = cache_lft.shape[1]
    n = pages_per_step
    pad_rows = lambda a: jnp.pad(a, ((0, 0), (0, SUBLANES - n_tok), (0, 0)))

    def kv_page(i):
        return pl.BlockSpec((None, None, heads, head_dim, page),
                            lambda s, g, pt: (layer, pt[s, g * n + i], 0, 0, 0))

    def lf_page(i):
        return pl.BlockSpec((None, hp, page), lambda s, g, pt: (pt[s, g * n + i], 0, 0))

    per_seq = lambda shape: pl.BlockSpec((None,) + shape, lambda s, g, pt: (s, 0, 0))
    in_specs = [per_seq((n_tok, w))] + [kv_page(i) for i in range(n)] * 2 + [lf_page(i) for i in range(n)]
    in_specs += [per_seq((SUBLANES, w)), per_seq((SUBLANES, w)), per_seq((hp, page))]
    rows = n_tok * hp
    return pl.pallas_call(
        functools.partial(_fox_sample_body, pages_per_step=n, head_dim=head_dim),
        grid_spec=pltpu.PrefetchScalarGridSpec(
            num_scalar_prefetch=1,
            grid=(ns, n_pages // n),
            in_specs=in_specs,
            out_specs=per_seq((n_tok, w)),
            scratch_shapes=[pltpu.VMEM((rows, w), BF16), pltpu.VMEM((rows, 1), F32),
                            pltpu.VMEM((rows, 1), F32), pltpu.VMEM((rows, w), F32),
                            pltpu.VMEM((hp, page), F32)],
        ),
        out_shape=jax.ShapeDtypeStruct((ns, n_tok, w), F32),
        compiler_params=_params("parallel", "arbitrary"),
        name="fox_sample",
    )(page_table, q, *([cache_kt] * n), *([cache_vt] * n), *([cache_lft] * n),
      pad_rows(k_new), pad_rows(v_new), lft_new)


def _softmax_pv(s, vb):
    m = jnp.max(s, axis=-1, keepdims=True)
    p = jnp.exp(s - m)
    return _dot(p.astype(BF16), vb) / jnp.sum(p, axis=-1, keepdims=True)


def _cross_out_prompt_body(x_ref, mix_ref, qc_ref, mk_ref, mv_ref, wom_ref, woc_ref, o_ref,
                           *, heads, head_dim):
    tm, cw = qc_ref.shape
    q = qc_ref[...] * head_dim ** -0.5
    head_of_lane = lax.broadcasted_iota(jnp.int32, (1, cw), 1) // head_dim
    zero = jnp.zeros_like(q)
    q4 = jnp.concatenate([jnp.where(head_of_lane == h, q, zero) for h in range(heads)], axis=0)
    s = _dot_nt(q4.astype(BF16), mk_ref[0].astype(BF16))
    o4 = _softmax_pv(s, mv_ref[0].astype(BF16))
    cross = zero
    for h in range(heads):
        cross = cross + jnp.where(head_of_lane == h, o4[h * tm:(h + 1) * tm, :], zero)
    o_ref[...] = (x_ref[...] + _dot(mix_ref[...].astype(BF16), wom_ref[...])
                  + _dot(cross.astype(BF16), woc_ref[...]))


def _cross_out_prompt(x, mix, qc, mk, mv, wom, woc, *, heads, head_dim, tm):
    t, d = x.shape
    nb, n_mem, cw = mk.shape
    per_b = t // nb // tm
    row = lambda n: pl.BlockSpec((tm, n), lambda i: (i, 0))
    mem = pl.BlockSpec((1, n_mem, cw), lambda i: (i // per_b, 0, 0))
    return pl.pallas_call(
        functools.partial(_cross_out_prompt_body, heads=heads, head_dim=head_dim),
        grid=(t // tm,),
        in_specs=[row(d), row(mix.shape[1]), row(cw), mem, mem, _resident(wom.shape),
                  _resident(woc.shape)],
        out_specs=row(d),
        out_shape=jax.ShapeDtypeStruct((t, d), F32),
        compiler_params=_params("parallel"),
        name="cross_out_prompt",
    )(x, mix, qc, mk, mv, wom, woc)


def _cross_out_sample_body(x_ref, mix_ref, qc_ref, mkt_ref, mvt_ref, wom_ref, woc_ref, o_ref, cross_ref,
                           *, n_tok):
    n_seq, _, head_dim, n_mem = mkt_ref.shape
    cw = qc_ref.shape[1]
    scale = head_dim ** -0.5
    head_of_lane = lax.broadcasted_iota(jnp.int32, (SUBLANES, cw), 1) // head_dim
    head_mask = head_of_lane == lax.broadcasted_iota(jnp.int32, (SUBLANES, cw), 0)
    zero = jnp.zeros((SUBLANES, cw), F32)
    for i in range(n_seq):
        rows = [jnp.where(head_mask, jnp.broadcast_to(qc_ref[pl.ds(i * n_tok + t, 1), :] * scale,
                                                     (SUBLANES, cw)), zero) for t in range(n_tok)]
        q = jnp.concatenate(rows, axis=0).astype(BF16)
        s = _dot(q, mkt_ref[i].reshape(cw, n_mem).astype(BF16))
        p = jnp.exp(s - jnp.max(s, axis=-1, keepdims=True))
        o = _dot_nt(p.astype(BF16), mvt_ref[i].reshape(cw, n_mem).astype(BF16)) / jnp.sum(p, axis=-1, keepdims=True)
        for t in range(n_tok):
            picked = jnp.where(head_mask, o[t * SUBLANES:(t + 1) * SUBLANES, :], zero)
            cross_ref[pl.ds(i * n_tok + t, 1), :] = jnp.sum(picked, axis=0, keepdims=True)
    o_ref[...] = (x_ref[...] + _dot(mix_ref[...].astype(BF16), wom_ref[...])
                  + _dot(cross_ref[...].astype(BF16), woc_ref[...]))


def _cross_out_sample(x, mix, qc, mkt, mvt, wom, woc, *, layer, n_tok, seqs_per_step):
    t, d = x.shape
    _, ns, heads, head_dim, n_mem = mkt.shape
    cw = heads * head_dim
    tm = seqs_per_step * n_tok
    row = lambda n: pl.BlockSpec((tm, n), lambda i: (i, 0))
    mem = pl.BlockSpec((None, seqs_per_step, heads, head_dim, n_mem), lambda i: (layer, i, 0, 0, 0))
    return pl.pallas_call(
        functools.partial(_cross_out_sample_body, n_tok=n_tok),
        grid=(ns // seqs_per_step,),
        in_specs=[row(d), row(mix.shape[1]), row(cw), mem, mem, _resident(wom.shape),
                  _resident(woc.shape)],
        out_specs=row(d),
        out_shape=jax.ShapeDtypeStruct((t, d), F32),
        scratch_shapes=[pltpu.VMEM((tm, cw), F32)],
        compiler_params=_params("parallel"),
        name="cross_out_sample",
    )(x, mix, qc, mkt, mvt, wom, woc)


def kernel(x_prompt, x_sample, mem_prompt, state_s5_re, state_s5_im, cache_fox_k, cache_fox_v, cache_fox_logf, cache_mem_k, cache_mem_v, page_table, ffn_norm, ffn_w_gate, ffn_w_up, ffn_w_down, norm_mix, norm_mem, w_mem_kv, w_in_s5, s5_a_re, s5_a_im, s5_log_dt, s5_b_re, s5_b_im, s5_c_re, s5_c_im, s5_d, s5_w_glu, w_in_fox, fox_b_f, w_out, norm_final):
    nb, seq, d = x_prompt.shape
    ns, n_tok, _ = x_sample.shape
    depth = ffn_norm.shape[0]
    n_mem = mem_prompt.shape[1]
    cross_heads, head_dim = cache_mem_k.shape[3], cache_mem_k.shape[4]
    cross_width = cross_heads * head_dim
    mixer_width = w_out.shape[1] - cross_width
    fox_heads = cache_fox_k.shape[3]
    n_phys, page = cache_fox_k.shape[1], cache_fox_k.shape[2]
    groups, n_state = state_s5_re.shape[2], state_s5_re.shape[3]
    hp = 2 * SUBLANES

    tm_p = 512
    tm_s = ns * n_tok
    xp = x_prompt.reshape(nb * seq, d)
    xs = x_sample.reshape(tm_s, d)
    mem = mem_prompt.reshape(nb * n_mem, d)

    s5_re_p, s5_im_p, s5_re_s, s5_im_s = [], [], [], []
    fk_p, fv_p, fl_p, fk_s, fv_s, fl_s = [], [], [], [], [], []
    mk_list, mv_list = [], []
    d_ff = ffn_w_gate.shape[-1]
    wg_all = _to_bf16(ffn_w_gate.reshape(2 * depth, d, d_ff), rows=tm_p)
    wu_all = _to_bf16(ffn_w_up.reshape(2 * depth, d, d_ff), rows=tm_p)
    wd_all = _to_bf16(ffn_w_down.reshape(2 * depth, d_ff, d), rows=d_ff // 2)
    for i in range(depth):
        j = i // N_MIXERS
        ffn_w = lambda half: (ffn_norm[i, half], wg_all, wu_all, wd_all, 2 * i + half)
        xp = _ffn(xp, *ffn_w(0), tm=tm_p)
        xs = _ffn(xs, *ffn_w(0), tm=tm_s)

        mk_p, mv_p = _norm_proj(mem, norm_mem[i],
                                [w_mem_kv[i][:, :cross_width], w_mem_kv[i][:, cross_width:]], tm=nb * n_mem)
        mk_list.append(mk_p.reshape(nb, n_mem, cross_heads, head_dim))
        mv_list.append(mv_p.reshape(nb, n_mem, cross_heads, head_dim))

        if i % N_MIXERS == 0:
            w_in = [w_in_s5[j][:, :mixer_width], w_in_s5[j][:, mixer_width:]]
            u_p, qc_p = _norm_proj(xp, norm_mix[i], w_in, tm=tm_p)
            u_s, qc_s = _norm_proj(xs, norm_mix[i], w_in, tm=tm_s)
            params = _s5_params(s5_a_re[j], s5_a_im[j], s5_log_dt[j], s5_b_re[j], s5_b_im[j],
                                s5_c_re[j], s5_c_im[j])
            mix_p, hr_p, hi_p = _s5_prompt(u_p.reshape(nb, seq, mixer_width), params, s5_d[j],
                                           s5_w_glu[j], rows=tm_p)
            mix_p = mix_p.reshape(nb * seq, mixer_width)
            u_t = u_s.reshape(ns, n_tok, mixer_width).transpose(1, 0, 2)
            mix_t, hr_s, hi_s = _s5_sample(u_t, state_s5_re[j].reshape(ns, groups * n_state),
                                           state_s5_im[j].reshape(ns, groups * n_state),
                                           params, s5_d[j], s5_w_glu[j])
            mix_s = mix_t.transpose(1, 0, 2).reshape(tm_s, mixer_width)
            s5_re_p.append(hr_p.reshape(nb, groups, n_state))
            s5_im_p.append(hi_p.reshape(nb, groups, n_state))
            s5_re_s.append(hr_s.reshape(ns, groups, n_state))
            s5_im_s.append(hi_s.reshape(ns, groups, n_state))
        else:
            fox = dict(heads=fox_heads, head_dim=head_dim, cross_width=cross_width)
            kt_p, vt_p, qc_p, lf_p, qa_p, ka_p, vta_p, stats_p = _fox_proj_prompt(
                xp, norm_mix[i], w_in_fox[j], fox_b_f[j], tm=tm_p, tiles_per_seq=seq // tm_p, **fox)
            q_s, k_s, v_s, qc_s, lf_s, lft_s = _fox_proj_sample(
                xs, norm_mix[i], w_in_fox[j], fox_b_f[j], tm=tm_s, **fox)

            seq3 = lambda a: a.reshape(nb, seq, a.shape[-1])
            mix_p = _fox_prompt(seq3(qa_p), seq3(ka_p), vta_p, stats_p, heads=fox_heads, head_dim=head_dim,
                                tq=min(seq, 4 * tm_p), tk=tm_p)
            mix_p = mix_p.reshape(nb * seq, mixer_width)

            tok3 = lambda a: a.reshape(ns, n_tok, mixer_width)
            lft_new = jnp.pad(lft_s.reshape(hp, ns, n_tok).transpose(1, 0, 2),
                              ((0, 0), (0, 0), (0, page - n_tok)))
            cache_lft = jnp.pad(cache_fox_logf[j].transpose(0, 2, 1), ((0, 0), (0, hp - fox_heads), (0, 0)))
            to_stored = lambda a: a.transpose(0, 1, 3, 4, 2)
            mix_s = _fox_sample(tok3(q_s), tok3(k_s), tok3(v_s), lft_new, to_stored(cache_fox_k),
                                to_stored(cache_fox_v), cache_lft, page_table, layer=j,
                                pages_per_step=page_table.shape[1])
            mix_s = mix_s.reshape(tm_s, mixer_width)

            from_stored = lambda a: a.reshape(nb, fox_heads, head_dim, seq).transpose(0, 3, 1, 2)
            fk_p.append(from_stored(kt_p))
            fv_p.append(from_stored(vt_p))
            fl_p.append(lf_p[:, :fox_heads].reshape(nb, seq, fox_heads))
            fk_s.append(k_s.reshape(ns, n_tok, fox_heads, head_dim))
            fv_s.append(v_s.reshape(ns, n_tok, fox_heads, head_dim))
            fl_s.append(lf_s[:, :fox_heads].reshape(ns, n_tok, fox_heads))

        wom = w_out[i][:mixer_width].astype(BF16)
        woc = w_out[i][mixer_width:].astype(BF16)
        xp = _cross_out_prompt(xp, mix_p, qc_p, mk_p.reshape(nb, n_mem, cross_width),
                               mv_p.reshape(nb, n_mem, cross_width), wom, woc,
                               heads=cross_heads, head_dim=head_dim, tm=tm_p)
        xs = _cross_out_sample(xs, mix_s, qc_s, cache_mem_k.transpose(0, 1, 3, 4, 2),
                               cache_mem_v.transpose(0, 1, 3, 4, 2), wom, woc,
                               layer=i, n_tok=n_tok, seqs_per_step=8)

        g_final = norm_final if i == depth - 1 else None
        xp = _ffn(xp, *ffn_w(1), g_final, tm=tm_p)
        xs = _ffn(xs, *ffn_w(1), g_final, tm=tm_s)

    return (xp.reshape(nb, seq, d), xs.reshape(ns, n_tok, d),
            jnp.stack(s5_re_p), jnp.stack(s5_im_p), jnp.stack(s5_re_s), jnp.stack(s5_im_s),
            jnp.stack(fk_p), jnp.stack(fv_p), jnp.stack(fl_p),
            jnp.stack(fk_s), jnp.stack(fv_s), jnp.stack(fl_s),
            jnp.stack(mk_list), jnp.stack(mv_list))
```

```python
import functools

import jax
import jax.numpy as jnp
from jax import lax
from jax.experimental import pallas as pl
from jax.experimental.pallas import tpu as pltpu

F32 = jnp.float32
BF16 = jnp.bfloat16

RMS_EPS = 1e-6
NEG_INF = -1e30
N_MIXERS = 2

LANES = 128
SUBLANES = 8
MXU_DIM = 256
VMEM_LIMIT_BYTES = 56 * 1024 * 1024

NT_DIMS = (((1,), (1,)), ((), ()))


def _params(*sem):
    return pltpu.CompilerParams(dimension_semantics=sem, vmem_limit_bytes=VMEM_LIMIT_BYTES)


def _resident(shape):
    nd = len(shape)
    return pl.BlockSpec(shape, lambda *_: (0,) * nd, pipeline_mode=pl.Buffered(1))


def _rms(x, g):
    return x * lax.rsqrt(jnp.mean(x * x, axis=-1, keepdims=True) + RMS_EPS) * g


def _dot(a, b):
    return jnp.dot(a, b, preferred_element_type=F32)


def _dot_nt(a, b):
    return lax.dot_general(a, b, NT_DIMS, preferred_element_type=F32)


def _log_sigmoid(x):
    return jnp.minimum(x, 0.0) - jnp.log1p(jnp.exp(-jnp.abs(x)))


def _split3(x):
    hi = x.astype(BF16)
    r1 = x - hi.astype(F32)
    mid = r1.astype(BF16)
    lo = (r1 - mid.astype(F32)).astype(BF16)
    return hi, mid, lo


def _cumsum_lanes(x, tri):
    hi, mid, lo = _split3(x)
    return _dot(hi, tri) + _dot(mid, tri) + _dot(lo, tri)


def _upper_tri():
    r = lax.broadcasted_iota(jnp.int32, (LANES, LANES), 0)
    c = lax.broadcasted_iota(jnp.int32, (LANES, LANES), 1)
    return jnp.where(r <= c, 1.0, 0.0).astype(BF16)


def _ffn_body(*refs, d_ff, ff_chunk, final):
    if final:
        x_ref, g_ref, wg_ref, wu_ref, wd_ref, gf_ref, o_ref, t_ref = refs
    else:
        x_ref, g_ref, wg_ref, wu_ref, wd_ref, o_ref, t_ref = refs
    x = x_ref[...]
    h = _rms(x, g_ref[...]).astype(BF16)
    for c in range(d_ff // ff_chunk):
        lo, hi = c * ff_chunk, (c + 1) * ff_chunk
        a = _dot(h, wg_ref[:, lo:hi])
        b = _dot(h, wu_ref[:, lo:hi])
        t_ref[:, lo:hi] = (jax.nn.silu(a) * b).astype(BF16)
    y = x + 0.5 * _dot(t_ref[...], wd_ref[...])
    if final:
        y = _rms(y, gf_ref[...])
    o_ref[...] = y


def _cast_body(x_ref, o_ref):
    o_ref[...] = x_ref[...].astype(o_ref.dtype)


def _to_bf16(w, *, rows):
    n, r, c = w.shape
    spec = pl.BlockSpec((None, rows, c), lambda i, j: (i, j, 0))
    return pl.pallas_call(
        _cast_body, grid=(n, r // rows), in_specs=[spec], out_specs=spec,
        out_shape=jax.ShapeDtypeStruct(w.shape, BF16),
        compiler_params=_params("parallel", "parallel"), name="to_bf16",
    )(w)


def _ffn(x, g, wg, wu, wd, idx, g_final=None, *, tm):
    t, d = x.shape
    d_ff = wg.shape[2]
    final = g_final is not None
    row = pl.BlockSpec((tm, d), lambda i: (i, 0))
    pick = lambda r, c: pl.BlockSpec((None, r, c), lambda i: (idx, 0, 0), pipeline_mode=pl.Buffered(1))
    in_specs = [row, _resident((1, d)), pick(d, d_ff), pick(d, d_ff), pick(d_ff, d)]
    args = [x, g.reshape(1, d), wg, wu, wd]
    if final:
        in_specs.append(_resident((1, d)))
        args.append(g_final.reshape(1, d))
    return pl.pallas_call(
        functools.partial(_ffn_body, d_ff=d_ff, ff_chunk=MXU_DIM, final=final),
        grid=(t // tm,),
        in_specs=in_specs,
        out_specs=row,
        out_shape=jax.ShapeDtypeStruct((t, d), F32),
        scratch_shapes=[pltpu.VMEM((tm, d_ff), BF16)],
        compiler_params=_params("parallel"),
        name="ffn_final" if final else "ffn",
    )(*args)


def _norm_proj_body(x_ref, g_ref, *refs, n):
    h = _rms(x_ref[...], g_ref[...]).astype(BF16)
    for w_ref, o_ref in zip(refs[:n], refs[n:]):
        o_ref[...] = _dot(h, w_ref[...])


def _norm_proj(x, g, ws, *, tm):
    t, d = x.shape
    n = len(ws)
    in_specs = [pl.BlockSpec((tm, d), lambda i: (i, 0)), _resident((1, d))]
    in_specs += [_resident(w.shape) for w in ws]
    return pl.pallas_call(
        functools.partial(_norm_proj_body, n=n),
        grid=(t // tm,),
        in_specs=in_specs,
        out_specs=[pl.BlockSpec((tm, w.shape[1]), lambda i: (i, 0)) for w in ws],
        out_shape=[jax.ShapeDtypeStruct((t, w.shape[1]), F32) for w in ws],
        compiler_params=_params("parallel"),
        name="norm_proj",
    )(x, g.reshape(1, d), *[w.astype(BF16) for w in ws])


def _fox_split(w_in, b_f, heads, head_dim):
    w = heads * head_dim
    wq, wk, wv = w_in[:, :w], w_in[:, w:2 * w], w_in[:, 2 * w:3 * w]
    wf = jnp.pad(w_in[:, 3 * w:3 * w + heads], ((0, 0), (0, LANES - heads)))
    wc = w_in[:, 3 * w + heads:]
    bf = jnp.pad(b_f, (0, LANES - heads)).reshape(1, LANES)
    return wq, wk, wv, wf, wc, bf


def _fox_proj_sample_body(x_ref, g_ref, wq_ref, wk_ref, wv_ref, wc_ref, wf_ref, wft_ref, bf_ref, bft_ref,
                          q_ref, k_ref, v_ref, qc_ref, lf_ref, lft_ref, *, scale):
    h = _rms(x_ref[...], g_ref[...]).astype(BF16)
    q_ref[...] = _dot(h, wq_ref[...]) * scale
    k_ref[...] = _dot(h, wk_ref[...])
    v_ref[...] = _dot(h, wv_ref[...])
    qc_ref[...] = _dot(h, wc_ref[...])
    lf_ref[...] = _log_sigmoid(_dot(h, wf_ref[...]) + bf_ref[...])
    lft_ref[...] = _log_sigmoid(_dot_nt(wft_ref[...], h) + bft_ref[...])


def _fox_proj_sample(x, g, w_in, b_f, *, heads, head_dim, cross_width, tm):
    t, d = x.shape
    w = heads * head_dim
    hp = 2 * SUBLANES
    wq, wk, wv, wf, wc, bf = _fox_split(w_in, b_f, heads, head_dim)
    wft = wf[:, :hp].T
    bft = bf[0, :hp].reshape(hp, 1)
    row = lambda n: pl.BlockSpec((tm, n), lambda i: (i, 0))
    outs = [w, w, w, cross_width, LANES]
    return pl.pallas_call(
        functools.partial(_fox_proj_sample_body, scale=head_dim ** -0.5),
        grid=(t // tm,),
        in_specs=[row(d), _resident((1, d)), _resident((d, w)), _resident((d, w)),
                  _resident((d, w)), _resident((d, cross_width)), _resident((d, LANES)),
                  _resident((hp, d)), _resident((1, LANES)), _resident((hp, 1))],
        out_specs=[row(n) for n in outs] + [pl.BlockSpec((hp, tm), lambda i: (0, i))],
        out_shape=[jax.ShapeDtypeStruct((t, n), F32) for n in outs]
        + [jax.ShapeDtypeStruct((hp, t), F32)],
        compiler_params=_params("parallel"),
        name="fox_proj_sample",
    )(x, g.reshape(1, d), wq.astype(BF16), wk.astype(BF16), wv.astype(BF16), wc.astype(BF16),
      wf.astype(BF16), wft.astype(BF16), bf, bft)


N_BIAS = 3


NORM_SLACK = 1.02


def _fox_proj_prompt_body(x_ref, g_ref, wqa_ref, wkt_ref, wka_ref, wvt_ref, wc_ref, wf_ref,
                          bf_ref, place_ref, oneq_ref, headsum_ref,
                          kt_ref, vt_ref, qc_ref, lf_ref, qa_ref, ka_ref, vta_ref, stats_ref, carry_ref,
                          *, scale, tiles_per_seq, head_dim):
    tile = pl.program_id(0)

    @pl.when(tile % tiles_per_seq == 0)
    def _():
        carry_ref[...] = jnp.zeros_like(carry_ref)

    def max_norm(x):
        sq = _dot((x * x).astype(BF16), headsum_ref[...])
        return jnp.sqrt(jnp.max(sq, axis=0, keepdims=True)) * NORM_SLACK

    h = _rms(x_ref[...], g_ref[...]).astype(BF16)
    q_aug = _dot(h, wqa_ref[...]) * scale
    qa_ref[...] = (q_aug + oneq_ref[...]).astype(BF16)
    stats_ref[0, pl.ds(tile, 1), :] = max_norm(q_aug)
    kt_ref[...] = _dot_nt(wkt_ref[...], h)
    vt = _dot_nt(wvt_ref[...], h)
    vt_ref[...] = vt
    tm = vt.shape[1]
    tail = jnp.where(lax.broadcasted_iota(jnp.int32, (LANES - head_dim, tm), 0) == 0, 1.0, 0.0)
    blocks = []
    for r0 in range(0, vt.shape[0], head_dim):
        blocks += [vt[r0:r0 + head_dim, :], tail]
    vta_ref[...] = jnp.concatenate(blocks, axis=0).astype(BF16)
    qc_ref[...] = _dot(h, wc_ref[...])
    lf = _log_sigmoid(_dot(h, wf_ref[...]) + bf_ref[...])
    lf_ref[...] = lf
    r = lax.broadcasted_iota(jnp.int32, (tm, tm), 0)
    c = lax.broadcasted_iota(jnp.int32, (tm, tm), 1)
    low = jnp.where(c <= r, 1.0, 0.0).astype(BF16)
    cum = carry_ref[0:1, :] + sum(_dot(low, piece) for piece in _split3(lf))
    carry_ref[...] = jnp.broadcast_to(cum[tm - 1:, :], carry_ref.shape)
    bias = sum(_dot(piece, place_ref[j]) for j, piece in enumerate(_split3(-cum)))
    k_aug = _dot(h, wka_ref[...])
    ka_ref[...] = (k_aug + bias).astype(BF16)
    stats_ref[1, pl.ds(tile, 1), :] = max_norm(k_aug)
    stats_ref[2, pl.ds(tile, 1), :] = cum[0:1, :]
    stats_ref[3, pl.ds(tile, 1), :] = cum[tm - 1:, :]


def _fox_proj_prompt(x, g, w_in, b_f, *, heads, head_dim, cross_width, tm, tiles_per_seq):
    t, d = x.shape
    w = heads * head_dim
    wa = heads * LANES
    wq, wk, wv, wf, wc, bf = _fox_split(w_in, b_f, heads, head_dim)
    aug = lambda a: jnp.pad(a.reshape(d, heads, head_dim), ((0, 0), (0, 0), (0, LANES - head_dim))).reshape(d, wa)
    head = jnp.arange(heads)
    place = jnp.zeros((N_BIAS, LANES, wa), F32)
    oneq = jnp.zeros((1, wa), F32)
    for j in range(N_BIAS):
        place = place.at[j, head, head * LANES + head_dim + j].set(1.0)
        oneq = oneq.at[0, head * LANES + head_dim + j].set(1.0)
    headsum = (jnp.arange(wa)[:, None] // LANES == jnp.arange(LANES)[None, :]).astype(BF16)
    row = lambda n: pl.BlockSpec((tm, n), lambda i: (i, 0))
    col = lambda n: pl.BlockSpec((None, n, tm), lambda i: (i // tiles_per_seq, 0, i % tiles_per_seq))
    seq_len = tm * tiles_per_seq
    n_tiles = t // tm
    outs = [(cross_width, F32), (LANES, F32), (wa, BF16), (wa, BF16)]
    stats_shape = (4, n_tiles, LANES)
    return pl.pallas_call(
        functools.partial(_fox_proj_prompt_body, scale=head_dim ** -0.5, tiles_per_seq=tiles_per_seq,
                          head_dim=head_dim),
        grid=(n_tiles,),
        in_specs=[row(d), _resident((1, d)), _resident((d, wa)), _resident((w, d)), _resident((d, wa)),
                  _resident((w, d)), _resident((d, cross_width)),
                  _resident((d, LANES)), _resident((1, LANES)), _resident(place.shape),
                  _resident((1, wa)), _resident(headsum.shape)],
        out_specs=[col(w), col(w)] + [row(n) for n, _ in outs]
        + [pl.BlockSpec((wa, tm), lambda i: (0, i)), pl.BlockSpec(stats_shape, lambda i: (0, 0, 0))],
        out_shape=[jax.ShapeDtypeStruct((t // seq_len, w, seq_len), F32)] * 2
        + [jax.ShapeDtypeStruct((t, n), dt) for n, dt in outs]
        + [jax.ShapeDtypeStruct((wa, t), BF16), jax.ShapeDtypeStruct(stats_shape, F32)],
        scratch_shapes=[pltpu.VMEM((SUBLANES, LANES), F32)],
        compiler_params=_params("arbitrary"),
        name="fox_proj_prompt",
    )(x, g.reshape(1, d), aug(wq).astype(BF16), wk.T.astype(BF16), aug(wk).astype(BF16),
      wv.T.astype(BF16), wc.astype(BF16), wf.astype(BF16), bf, place.astype(BF16), oneq, headsum)


def _s5_discretise(a_re, a_im, log_dt):
    dt = jnp.exp(log_dt)
    mag = jnp.exp(dt * a_re)
    ab_re = mag * jnp.cos(dt * a_im)
    ab_im = mag * jnp.sin(dt * a_im)
    den = a_re * a_re + a_im * a_im
    nr = ab_re - 1.0
    ni = ab_im
    return ab_re, ab_im, (nr * a_re + ni * a_im) / den, (ni * a_re - nr * a_im) / den


def _s5_param_body(are_ref, aim_ref, ldt_ref, arex_ref, aimx_ref, ldtx_ref, bre_ref, bim_ref,
                   bbr_ref, bbi_ref, pwr_ref, pwi_ref):
    _, _, zr, zi = _s5_discretise(arex_ref[...], aimx_ref[...], ldtx_ref[...])
    b_re = bre_ref[...]
    b_im = bim_ref[...]
    bbr_ref[...] = zr * b_re - zi * b_im
    bbi_ref[...] = zr * b_im + zi * b_re
    ab_re, ab_im, _, _ = _s5_discretise(are_ref[...], aim_ref[...], ldt_ref[...])
    pr, pi = ab_re, ab_im
    for r in range(pwr_ref.shape[0]):
        pwr_ref[r] = pr
        pwi_ref[r] = pi
        pr, pi = pr * ab_re - pi * ab_im, pr * ab_im + pi * ab_re


def _s5_params(a_re, a_im, log_dt, b_re, b_im, c_re, c_im, *, n_pow):
    g, p, c = b_re.shape
    rows = g * c
    rep = lambda a: jnp.repeat(a, c, axis=0)
    bt = lambda b: b.transpose(0, 2, 1).reshape(rows, p)
    whole = lambda shape: pl.BlockSpec(shape, lambda: (0,) * len(shape))
    log_dt = log_dt.reshape(g, 1)
    bbr, bbi, pwr, pwi = pl.pallas_call(
        _s5_param_body,
        in_specs=[whole((g, p)), whole((g, p)), whole((g, 1)), whole((rows, p)), whole((rows, p)),
                  whole((rows, 1)), whole((rows, p)), whole((rows, p))],
        out_specs=[whole((rows, p)), whole((rows, p)), whole((n_pow, g, p)), whole((n_pow, g, p))],
        out_shape=[jax.ShapeDtypeStruct((rows, p), F32)] * 2 + [jax.ShapeDtypeStruct((n_pow, g, p), F32)] * 2,
        name="s5_params",
    )(a_re, a_im, log_dt, rep(a_re), rep(a_im), rep(log_dt), bt(b_re), bt(b_im))
    width = g * p
    pw_re, pw_im = pwr.reshape(n_pow, width), pwi.reshape(n_pow, width)
    gpt = MXU_DIM // c
    n_tiles = g // gpt
    eye = jnp.eye(gpt, dtype=F32)

    def in_tiles(bb):
        blocks = bb.reshape(n_tiles, gpt, c, p)
        return jnp.einsum("tgcp,gh->tgchp", blocks, eye).reshape(n_tiles, gpt * c, gpt * p)

    def out_tiles(cc):
        blocks = cc.reshape(n_tiles, gpt, c, p)
        return jnp.einsum("tgcp,gh->tgphc", blocks, eye).reshape(n_tiles, gpt * p, gpt * c)

    w_in = jnp.concatenate([in_tiles(bbr), in_tiles(bbi)], axis=-1).astype(BF16)
    return (pw_re[0:1], pw_im[0:1], pw_re, pw_im, w_in,
            out_tiles(c_re).astype(BF16), out_tiles(c_im).astype(BF16))


def _s5_in_proj(u, wb_ref, bre_ref, bim_ref):
    ub = u.astype(BF16)
    n_tiles, ch, two_w = wb_ref.shape
    w = two_w // 2
    for t in range(n_tiles):
        bu = _dot(ub[:, t * ch:(t + 1) * ch], wb_ref[t])
        bre_ref[:, t * w:(t + 1) * w] = bu[:, :w]
        bim_ref[:, t * w:(t + 1) * w] = bu[:, w:]


def _s5_out(u, hre_ref, him_ref, wcr_ref, wci_ref, d_ref, wglu_ref):
    n_tiles, w, _ = wcr_ref.shape
    ys = []
    for t in range(n_tiles):
        hr = hre_ref[:, t * w:(t + 1) * w].astype(BF16)
        hi = him_ref[:, t * w:(t + 1) * w].astype(BF16)
        ys.append(_dot(hr, wcr_ref[t]) - _dot(hi, wci_ref[t]))
    y = jax.nn.gelu(jnp.concatenate(ys, axis=-1) + d_ref[...] * u)
    z = _dot(y.astype(BF16), wglu_ref[...])
    half = z.shape[-1] // 2
    return z[:, :half] * jax.nn.sigmoid(z[:, half:])


SEGMENT_PITCH_PAD = SUBLANES


def _s5_prompt_body(u_ref, wb_ref, wcr_ref, wci_ref, d_ref, wglu_ref, ab8_ref, aseg_ref,
                    o_ref, hre_ref, him_ref, up_ref, slab_ref, bre_ref, bim_ref, cin_ref, carry_ref,
                    *, lane_group):
    ci = pl.program_id(1)
    rows, width = bre_ref.shape
    ch = up_ref.shape[1]
    seg = rows // SUBLANES
    pitch = slab_ref.shape[1] // SUBLANES
    n_slab = ch // LANES
    lanes = [slice(j * LANES, (j + 1) * LANES) for j in range(n_slab)]
    groups = [slice(g * lane_group, (g + 1) * lane_group) for g in range(width // lane_group)]
    tile = lambda k: pl.ds(pl.multiple_of(k * SUBLANES, SUBLANES), SUBLANES)

    @pl.when(ci == 0)
    def _():
        carry_ref[...] = jnp.zeros_like(carry_ref)

    for s in range(SUBLANES):
        for j, ls in enumerate(lanes):
            slab_ref[j, pitch * s:pitch * s + seg, :] = u_ref[0, seg * s:seg * (s + 1), ls]

    def gather(k, carry):
        for j, ls in enumerate(lanes):
            up_ref[tile(k), ls] = slab_ref[j, pl.ds(k, SUBLANES, stride=pitch), :]
        return carry

    lax.fori_loop(0, seg, gather, 0)
    u = up_ref[...]
    _s5_in_proj(u, wb_ref, bre_ref, bim_ref)

    for ls in groups:
        ar = ab8_ref[0, :, ls]
        ai = ab8_ref[1, :, ls]

        def step(k, carry, ls=ls, ar=ar, ai=ai):
            hr, hi = carry
            hr, hi = ar * hr - ai * hi + bre_ref[tile(k), ls], ar * hi + ai * hr + bim_ref[tile(k), ls]
            bre_ref[tile(k), ls] = hr
            bim_ref[tile(k), ls] = hi
            return hr, hi

        zero = jnp.zeros((SUBLANES, lane_group), F32)
        lax.fori_loop(0, seg, step, (zero, zero))

    er, ei = carry_ref[0, 0:1, :], carry_ref[1, 0:1, :]
    sr, si = aseg_ref[0], aseg_ref[1]
    for s in range(SUBLANES):
        cin_ref[0, s:s + 1, :] = er
        cin_ref[1, s:s + 1, :] = ei
        end = rows - SUBLANES + s
        er, ei = (bre_ref[end:end + 1, :] + sr * er - si * ei, bim_ref[end:end + 1, :] + sr * ei + si * er)
    carry_ref[0, 0:1, :] = er
    carry_ref[1, 0:1, :] = ei

    for ls in groups:
        ar = ab8_ref[0, :, ls]
        ai = ab8_ref[1, :, ls]

        def fix(k, carry, ls=ls, ar=ar, ai=ai):
            gr, gi = carry
            gr, gi = ar * gr - ai * gi, ar * gi + ai * gr
            bre_ref[tile(k), ls] = bre_ref[tile(k), ls] + gr
            bim_ref[tile(k), ls] = bim_ref[tile(k), ls] + gi
            return gr, gi

        lax.fori_loop(0, seg, fix, (cin_ref[0, :, ls], cin_ref[1, :, ls]))

    up_ref[...] = _s5_out(u, bre_ref, bim_ref, wcr_ref, wci_ref, d_ref, wglu_ref)

    def scatter(k, carry):
        for j, ls in enumerate(lanes):
            slab_ref[j, pl.ds(k, SUBLANES, stride=pitch), :] = up_ref[tile(k), ls]
        return carry

    lax.fori_loop(0, seg, scatter, 0)
    for s in range(SUBLANES):
        for j, ls in enumerate(lanes):
            o_ref[0, seg * s:seg * (s + 1), ls] = slab_ref[j, pitch * s:pitch * s + seg, :]

    @pl.when(ci == pl.num_programs(1) - 1)
    def _():
        hre_ref[0] = er
        him_ref[0] = ei


def _s5_prompt(u, params, d_skip, w_glu, *, rows):
    ab_re, ab_im, pw_re, pw_im, w_in, wc_re, wc_im = params
    nb, length, ch = u.shape
    width = ab_re.shape[1]
    seg = rows // SUBLANES
    ab8 = jnp.stack([jnp.broadcast_to(ab_re, (SUBLANES, width)), jnp.broadcast_to(ab_im, (SUBLANES, width))])
    a_seg = jnp.stack([pw_re[seg - 1:seg], pw_im[seg - 1:seg]])
    state = jax.ShapeDtypeStruct((nb, 1, width), F32)
    state_spec = pl.BlockSpec((1, 1, width), lambda b, c: (b, 0, 0))
    tok_spec = pl.BlockSpec((1, rows, ch), lambda b, c: (b, c, 0))
    slab_rows = SUBLANES * (seg + SEGMENT_PITCH_PAD)
    return pl.pallas_call(
        functools.partial(_s5_prompt_body, lane_group=4 * LANES),
        grid=(nb, length // rows),
        in_specs=[tok_spec, _resident(w_in.shape), _resident(wc_re.shape), _resident(wc_im.shape),
                  _resident((1, ch)), _resident(w_glu.shape), _resident(ab8.shape), _resident(a_seg.shape)],
        out_specs=[tok_spec, state_spec, state_spec],
        out_shape=[jax.ShapeDtypeStruct((nb, length, ch), F32), state, state],
        scratch_shapes=[pltpu.VMEM((rows, ch), F32), pltpu.VMEM((ch // LANES, slab_rows, LANES), F32),
                        pltpu.VMEM((rows, width), F32), pltpu.VMEM((rows, width), F32),
                        pltpu.VMEM((2, SUBLANES, width), F32), pltpu.VMEM((2, SUBLANES, width), F32)],
        compiler_params=_params("parallel", "arbitrary"),
        name="s5_prompt",
    )(u, w_in, wc_re, wc_im, d_skip.reshape(1, ch), w_glu.astype(BF16), ab8, a_seg)


def _s5_sample_body(u_ref, h0r_ref, h0i_ref, wb_ref, wcr_ref, wci_ref, d_ref, wglu_ref, ab_ref,
                    o_ref, hre_ref, him_ref, bre_ref, bim_ref):
    hre_ref[...] = h0r_ref[...]
    him_ref[...] = h0i_ref[...]
    ar = ab_ref[0:1, :]
    ai = ab_ref[1:2, :]
    for t in range(u_ref.shape[0]):
        u = u_ref[t]
        _s5_in_proj(u, wb_ref, bre_ref, bim_ref)
        hr = hre_ref[...]
        hi = him_ref[...]
        hre_ref[...] = ar * hr - ai * hi + bre_ref[...]
        him_ref[...] = ar * hi + ai * hr + bim_ref[...]
        o_ref[t] = _s5_out(u, hre_ref, him_ref, wcr_ref, wci_ref, d_ref, wglu_ref)


def _s5_sample(u, h0_re, h0_im, params, d_skip, w_glu):
    ab_re, ab_im, _, _, w_in, wc_re, wc_im = params
    nt, nb, ch = u.shape
    width = ab_re.shape[1]
    ab = jnp.concatenate([ab_re, ab_im], axis=0)
    whole = lambda shape: pl.BlockSpec(shape, lambda: (0,) * len(shape))
    state = jax.ShapeDtypeStruct((nb, width), F32)
    return pl.pallas_call(
        _s5_sample_body,
        in_specs=[whole(u.shape), whole((nb, width)), whole((nb, width)), whole(w_in.shape),
                  whole(wc_re.shape), whole(wc_im.shape), whole((1, ch)), whole(w_glu.shape),
                  whole(ab.shape)],
        out_specs=[whole(u.shape), whole((nb, width)), whole((nb, width))],
        out_shape=[jax.ShapeDtypeStruct(u.shape, F32), state, state],
        scratch_shapes=[pltpu.VMEM((nb, width), F32), pltpu.VMEM((nb, width), F32)],
        compiler_params=pltpu.CompilerParams(vmem_limit_bytes=VMEM_LIMIT_BYTES),
        name="s5_sample",
    )(u, h0_re, h0_im, w_in, wc_re, wc_im, d_skip.reshape(1, ch), w_glu.astype(BF16), ab)


HEADS_PER_STEP = 2


UNDERFLOW_MARGIN = 110.0


def _first_live_block(st_ref, base, n_seq_blk, first_chunk, n_chunk, pair):
    kn = st_ref[1, pl.ds(base, n_seq_blk), :]
    c_last = st_ref[3, pl.ds(base, n_seq_blk), :]
    rows = [kn[0:1]]
    for j in range(1, n_seq_blk):
        rows.append(jnp.maximum(rows[-1], kn[j:j + 1]))
    kn_run = jnp.concatenate(rows, axis=0)
    worst = None
    for c in range(n_chunk):
        r = base + first_chunk + c
        term = st_ref[0, pl.ds(r, 1), :] * (kn_run + st_ref[1, pl.ds(r, 1), :]) + st_ref[2, pl.ds(r, 1), :]
        worst = term if worst is None else jnp.maximum(worst, term)
    blk = lax.broadcasted_iota(jnp.int32, worst.shape, 0)
    lane = lax.broadcasted_iota(jnp.int32, (1, LANES), 1)
    dead = (worst - c_last < -UNDERFLOW_MARGIN) & (blk < first_chunk)
    count = jnp.sum(jnp.where(dead, 1, 0), axis=0, keepdims=True)
    return jnp.min(jnp.where(lane // HEADS_PER_STEP == pair, count, n_seq_blk))


def _fox_prompt_body(q_ref, k_ref, vt_ref, st_ref, o_ref, m_ref, acc_ref, *, head_dim, tk):
    qi = pl.program_id(2)
    tq = q_ref.shape[1]
    n_chunk = tq // tk
    n_seq_blk = k_ref.shape[1] // tk
    first_live = _first_live_block(st_ref, pl.program_id(0) * n_seq_blk, n_seq_blk, qi * n_chunk, n_chunk,
                                   pl.program_id(1))
    m_ref[...] = jnp.full_like(m_ref, NEG_INF)
    acc_ref[...] = jnp.zeros_like(acc_ref)

    def block(ks, first_chunk, masked_chunk):
        chains = [(h, slice(h * LANES, (h + 1) * LANES), c, slice(c * tk, (c + 1) * tk))
                  for h in range(HEADS_PER_STEP) for c in range(first_chunk, n_chunk)]
        scores = []
        for _, hl, c, qs in chains:
            s = _dot_nt(k_ref[0, pl.ds(ks, tk), hl], q_ref[0, qs, hl])
            if c == masked_chunk:
                key = lax.broadcasted_iota(jnp.int32, s.shape, 0)
                qry = lax.broadcasted_iota(jnp.int32, s.shape, 1)
                s = jnp.where(key <= qry, s, NEG_INF)
            scores.append(s)
        probs, alphas = [], []
        for s, (h, _, _, qs) in zip(scores, chains):
            m_old = m_ref[h, :, qs]
            m_new = jnp.maximum(m_old, jnp.max(s, axis=0, keepdims=True))
            alphas.append(jnp.exp(m_old - m_new))
            probs.append(jnp.exp(s - m_new).astype(BF16))
            m_ref[h, :, qs] = m_new
        for p, alpha, (h, hl, _, qs) in zip(probs, alphas, chains):
            acc_ref[h, :, qs] = alpha * acc_ref[h, :, qs] + _dot(vt_ref[hl, pl.ds(ks, tk)], p)

    def body(kj, carry):
        block(pl.multiple_of(kj * tk, tk), 0, None)
        return carry

    lax.fori_loop(first_live, qi * n_chunk, body, 0)
    for c in range(n_chunk):
        block(pl.multiple_of((qi * n_chunk + c) * tk, tk), c, c)
    outs = [acc_ref[h, :head_dim, :] / acc_ref[h, head_dim:head_dim + 1, :] for h in range(HEADS_PER_STEP)]
    o_ref[0] = jnp.concatenate(outs, axis=0).T


def _fox_prompt(qa, ka, vta, stats, *, heads, head_dim, tq, tk):
    nb, length, _ = qa.shape
    wide = HEADS_PER_STEP * LANES
    return pl.pallas_call(
        functools.partial(_fox_prompt_body, head_dim=head_dim, tk=tk),
        grid=(nb, heads // HEADS_PER_STEP, length // tq),
        in_specs=[pl.BlockSpec((1, tq, wide), lambda b, hp, i: (b, i, hp)),
                  pl.BlockSpec((1, length, wide), lambda b, hp, i: (b, 0, hp)),
                  pl.BlockSpec((wide, length), lambda b, hp, i: (hp, b)),
                  pl.BlockSpec(stats.shape, lambda b, hp, i: (0, 0, 0))],
        out_specs=pl.BlockSpec((1, tq, HEADS_PER_STEP * head_dim), lambda b, hp, i: (b, i, hp)),
        out_shape=jax.ShapeDtypeStruct((nb, length, heads * head_dim), F32),
        scratch_shapes=[pltpu.VMEM((HEADS_PER_STEP, 1, tq), F32),
                        pltpu.VMEM((HEADS_PER_STEP, LANES, tq), F32)],
        compiler_params=_params("parallel", "parallel", "arbitrary"),
        name="fox_prompt",
    )(qa, ka, vta, stats)


def _fox_sample_body(pt_ref, q_ref, *refs, pages_per_step, head_dim):
    n = pages_per_step
    kt_refs, vt_refs, lf_refs = refs[:n], refs[n:2 * n], refs[2 * n:3 * n]
    kn_ref, vn_ref, lfn_ref, o_ref, qrow_ref, m_ref, l_ref, acc_ref, carry_ref = refs[3 * n:]
    del pt_ref
    g = pl.program_id(1)
    n_tok, w = q_ref.shape
    hp, page = lf_refs[0].shape
    head_of_lane = lax.broadcasted_iota(jnp.int32, (hp, w), 1) // head_dim
    head_mask = head_of_lane == lax.broadcasted_iota(jnp.int32, (hp, w), 0)
    tri = _upper_tri()

    @pl.when(g == 0)
    def _():
        zero = jnp.zeros((hp, w), F32)
        rows = [jnp.where(head_mask, jnp.broadcast_to(q_ref[t:t + 1, :], (hp, w)), zero)
                for t in range(n_tok)]
        qrow_ref[...] = jnp.concatenate(rows, axis=0).astype(BF16)
        m_ref[...] = jnp.full_like(m_ref, NEG_INF)
        l_ref[...] = jnp.zeros_like(l_ref)
        acc_ref[...] = jnp.zeros_like(acc_ref)
        carry_ref[...] = jnp.zeros_like(carry_ref)

    def update(pages, valid=None):
        scores, base = [], carry_ref[...]
        for s, lf, _ in pages:
            c = _cumsum_lanes(lf, tri) + base
            base = jnp.broadcast_to(c[:, page - 1:], c.shape)
            s = s - jnp.concatenate([c] * n_tok, axis=0)
            scores.append(s if valid is None else jnp.where(valid, s, NEG_INF))
        carry_ref[...] = base
        m_old = m_ref[...]
        m_new = functools.reduce(jnp.maximum, [jnp.max(s, axis=1, keepdims=True) for s in scores], m_old)
        alpha = jnp.exp(m_old - m_new)
        probs = [jnp.exp(s - m_new) for s in scores]
        l_ref[...] = alpha * l_ref[...] + sum(jnp.sum(p, axis=1, keepdims=True) for p in probs)
        acc_ref[...] = alpha * acc_ref[...] + sum(pv(p.astype(BF16)) for p, (_, _, pv) in zip(probs, pages))
        m_ref[...] = m_new

    def cached(i):
        kt = kt_refs[i][...].reshape(w, page).astype(BF16)
        vt = vt_refs[i][...].reshape(w, page).astype(BF16)
        return _dot(qrow_ref[...], kt), lf_refs[i][...], lambda p: _dot_nt(p, vt)

    update([cached(i) for i in range(n)])

    @pl.when(g == pl.num_programs(1) - 1)
    def _():
        pad = jnp.zeros((page - kn_ref.shape[0], w), F32)
        kn = jnp.concatenate([kn_ref[...], pad], axis=0).astype(BF16)
        vn = jnp.concatenate([vn_ref[...], pad], axis=0).astype(BF16)
        key = lax.broadcasted_iota(jnp.int32, (n_tok * hp, page), 1)
        tok = lax.broadcasted_iota(jnp.int32, (n_tok * hp, page), 0) // hp
        update([(_dot_nt(qrow_ref[...], kn), lfn_ref[...], lambda p: _dot(p, vn))], key <= tok)
        out = acc_ref[...] / l_ref[...]
        for t in range(n_tok):
            picked = jnp.where(head_mask, out[t * hp:(t + 1) * hp, :], 0.0)
            o_ref[t:t + 1, :] = jnp.sum(picked, axis=0, keepdims=True)


def _fox_sample(q, k_new, v_new, lft_new, cache_kt, cache_vt, cache_lft, page_table, *, layer, pages_per_step):
    ns, n_tok, w = q.shape
    n_pages = page_table.shape[1]
    _, _, heads, head_dim, page = cache_kt.shape
    hp = cache_lft.shape[1]
    n = pages_per_step
    pad_rows = lambda a: jnp.pad(a, ((0, 0), (0, SUBLANES - n_tok), (0, 0)))

    def kv_page(i):
        return pl.BlockSpec((None, None, heads, head_dim, page),
                            lambda s, g, pt: (layer, pt[s, g * n + i], 0, 0, 0))

    def lf_page(i):
        return pl.BlockSpec((None, hp, page), lambda s, g, pt: (pt[s, g * n + i], 0, 0))

    per_seq = lambda shape: pl.BlockSpec((None,) + shape, lambda s, g, pt: (s, 0, 0))
    in_specs = [per_seq((n_tok, w))] + [kv_page(i) for i in range(n)] * 2 + [lf_page(i) for i in range(n)]
    in_specs += [per_seq((SUBLANES, w)), per_seq((SUBLANES, w)), per_seq((hp, page))]
    rows = n_tok * hp
    return pl.pallas_call(
        functools.partial(_fox_sample_body, pages_per_step=n, head_dim=head_dim),
        grid_spec=pltpu.PrefetchScalarGridSpec(
            num_scalar_prefetch=1,
            grid=(ns, n_pages // n),
            in_specs=in_specs,
            out_specs=per_seq((n_tok, w)),
            scratch_shapes=[pltpu.VMEM((rows, w), BF16), pltpu.VMEM((rows, 1), F32),
                            pltpu.VMEM((rows, 1), F32), pltpu.VMEM((rows, w), F32),
                            pltpu.VMEM((hp, page), F32)],
        ),
        out_shape=jax.ShapeDtypeStruct((ns, n_tok, w), F32),
        compiler_params=_params("parallel", "arbitrary"),
        name="fox_sample",
    )(page_table, q, *([cache_kt] * n), *([cache_vt] * n), *([cache_lft] * n),
      pad_rows(k_new), pad_rows(v_new), lft_new)


def _softmax_pv(s, vb):
    m = jnp.max(s, axis=-1, keepdims=True)
    p = jnp.exp(s - m)
    return _dot(p.astype(BF16), vb) / jnp.sum(p, axis=-1, keepdims=True)


def _cross_out_prompt_body(x_ref, mix_ref, qc_ref, mk_ref, mv_ref, wom_ref, woc_ref, o_ref,
                           *, heads, head_dim):
    tm, cw = qc_ref.shape
    q = qc_ref[...] * head_dim ** -0.5
    head_of_lane = lax.broadcasted_iota(jnp.int32, (1, cw), 1) // head_dim
    zero = jnp.zeros_like(q)
    q4 = jnp.concatenate([jnp.where(head_of_lane == h, q, zero) for h in range(heads)], axis=0)
    s = _dot_nt(q4.astype(BF16), mk_ref[0].astype(BF16))
    o4 = _softmax_pv(s, mv_ref[0].astype(BF16))
    cross = zero
    for h in range(heads):
        cross = cross + jnp.where(head_of_lane == h, o4[h * tm:(h + 1) * tm, :], zero)
    o_ref[...] = (x_ref[...] + _dot(mix_ref[...].astype(BF16), wom_ref[...])
                  + _dot(cross.astype(BF16), woc_ref[...]))


def _cross_out_prompt(x, mix, qc, mk, mv, wom, woc, *, heads, head_dim, tm):
    t, d = x.shape
    nb, n_mem, cw = mk.shape
    per_b = t // nb // tm
    row = lambda n: pl.BlockSpec((tm, n), lambda i: (i, 0))
    mem = pl.BlockSpec((1, n_mem, cw), lambda i: (i // per_b, 0, 0))
    return pl.pallas_call(
        functools.partial(_cross_out_prompt_body, heads=heads, head_dim=head_dim),
        grid=(t // tm,),
        in_specs=[row(d), row(mix.shape[1]), row(cw), mem, mem, _resident(wom.shape),
                  _resident(woc.shape)],
        out_specs=row(d),
        out_shape=jax.ShapeDtypeStruct((t, d), F32),
        compiler_params=_params("parallel"),
        name="cross_out_prompt",
    )(x, mix, qc, mk, mv, wom, woc)


def _cross_out_sample_body(x_ref, mix_ref, qc_ref, mkt_ref, mvt_ref, wom_ref, woc_ref, o_ref, cross_ref,
                           *, n_tok):
    n_seq, _, head_dim, n_mem = mkt_ref.shape
    cw = qc_ref.shape[1]
    scale = head_dim ** -0.5
    head_of_lane = lax.broadcasted_iota(jnp.int32, (SUBLANES, cw), 1) // head_dim
    head_mask = head_of_lane == lax.broadcasted_iota(jnp.int32, (SUBLANES, cw), 0)
    zero = jnp.zeros((SUBLANES, cw), F32)
    for i in range(n_seq):
        rows = [jnp.where(head_mask, jnp.broadcast_to(qc_ref[pl.ds(i * n_tok + t, 1), :] * scale,
                                                     (SUBLANES, cw)), zero) for t in range(n_tok)]
        q = jnp.concatenate(rows, axis=0).astype(BF16)
        s = _dot(q, mkt_ref[i].reshape(cw, n_mem).astype(BF16))
        p = jnp.exp(s - jnp.max(s, axis=-1, keepdims=True))
        o = _dot_nt(p.astype(BF16), mvt_ref[i].reshape(cw, n_mem).astype(BF16)) / jnp.sum(p, axis=-1, keepdims=True)
        for t in range(n_tok):
            picked = jnp.where(head_mask, o[t * SUBLANES:(t + 1) * SUBLANES, :], zero)
            cross_ref[pl.ds(i * n_tok + t, 1), :] = jnp.sum(picked, axis=0, keepdims=True)
    o_ref[...] = (x_ref[...] + _dot(mix_ref[...].astype(BF16), wom_ref[...])
                  + _dot(cross_ref[...].astype(BF16), woc_ref[...]))


def _cross_out_sample(x, mix, qc, mkt, mvt, wom, woc, *, layer, n_tok, seqs_per_step):
    t, d = x.shape
    _, ns, heads, head_dim, n_mem = mkt.shape
    cw = heads * head_dim
    tm = seqs_per_step * n_tok
    row = lambda n: pl.BlockSpec((tm, n), lambda i: (i, 0))
    mem = pl.BlockSpec((None, seqs_per_step, heads, head_dim, n_mem), lambda i: (layer, i, 0, 0, 0))
    return pl.pallas_call(
        functools.partial(_cross_out_sample_body, n_tok=n_tok),
        grid=(ns // seqs_per_step,),
        in_specs=[row(d), row(mix.shape[1]), row(cw), mem, mem, _resident(wom.shape),
                  _resident(woc.shape)],
        out_specs=row(d),
        out_shape=jax.ShapeDtypeStruct((t, d), F32),
        scratch_shapes=[pltpu.VMEM((tm, cw), F32)],
        compiler_params=_params("parallel"),
        name="cross_out_sample",
    )(x, mix, qc, mkt, mvt, wom, woc)


def kernel(x_prompt, x_sample, mem_prompt, state_s5_re, state_s5_im, cache_fox_k, cache_fox_v, cache_fox_logf, cache_mem_k, cache_mem_v, page_table, ffn_norm, ffn_w_gate, ffn_w_up, ffn_w_down, norm_mix, norm_mem, w_mem_kv, w_in_s5, s5_a_re, s5_a_im, s5_log_dt, s5_b_re, s5_b_im, s5_c_re, s5_c_im, s5_d, s5_w_glu, w_in_fox, fox_b_f, w_out, norm_final):
    nb, seq, d = x_prompt.shape
    ns, n_tok, _ = x_sample.shape
    depth = ffn_norm.shape[0]
    n_mem = mem_prompt.shape[1]
    cross_heads, head_dim = cache_mem_k.shape[3], cache_mem_k.shape[4]
    cross_width = cross_heads * head_dim
    mixer_width = w_out.shape[1] - cross_width
    fox_heads = cache_fox_k.shape[3]
    n_phys, page = cache_fox_k.shape[1], cache_fox_k.shape[2]
    groups, n_state = state_s5_re.shape[2], state_s5_re.shape[3]
    hp = 2 * SUBLANES

    tm_p = 512
    tm_s = ns * n_tok
    xp = x_prompt.reshape(nb * seq, d)
    xs = x_sample.reshape(tm_s, d)
    mem = mem_prompt.reshape(nb * n_mem, d)

    s5_re_p, s5_im_p, s5_re_s, s5_im_s = [], [], [], []
    fk_p, fv_p, fl_p, fk_s, fv_s, fl_s = [], [], [], [], [], []
    mk_list, mv_list = [], []
    d_ff = ffn_w_gate.shape[-1]
    wg_all = _to_bf16(ffn_w_gate.reshape(2 * depth, d, d_ff), rows=tm_p)
    wu_all = _to_bf16(ffn_w_up.reshape(2 * depth, d, d_ff), rows=tm_p)
    wd_all = _to_bf16(ffn_w_down.reshape(2 * depth, d_ff, d), rows=d_ff // 2)
    for i in range(depth):
        j = i // N_MIXERS
        ffn_w = lambda half: (ffn_norm[i, half], wg_all, wu_all, wd_all, 2 * i + half)
        xp = _ffn(xp, *ffn_w(0), tm=tm_p)
        xs = _ffn(xs, *ffn_w(0), tm=tm_s)

        mk_p, mv_p = _norm_proj(mem, norm_mem[i],
                                [w_mem_kv[i][:, :cross_width], w_mem_kv[i][:, cross_width:]], tm=nb * n_mem)
        mk_list.append(mk_p.reshape(nb, n_mem, cross_heads, head_dim))
        mv_list.append(mv_p.reshape(nb, n_mem, cross_heads, head_dim))

        if i % N_MIXERS == 0:
            w_in = [w_in_s5[j][:, :mixer_width], w_in_s5[j][:, mixer_width:]]
            u_p, qc_p = _norm_proj(xp, norm_mix[i], w_in, tm=tm_p)
            u_s, qc_s = _norm_proj(xs, norm_mix[i], w_in, tm=tm_s)
            params = _s5_params(s5_a_re[j], s5_a_im[j], s5_log_dt[j], s5_b_re[j], s5_b_im[j],
                                s5_c_re[j], s5_c_im[j], n_pow=tm_p // SUBLANES)
            mix_p, hr_p, hi_p = _s5_prompt(u_p.reshape(nb, seq, mixer_width), params, s5_d[j],
                                           s5_w_glu[j], rows=tm_p)
            mix_p = mix_p.reshape(nb * seq, mixer_width)
            u_t = u_s.reshape(ns, n_tok, mixer_width).transpose(1, 0, 2)
            mix_t, hr_s, hi_s = _s5_sample(u_t, state_s5_re[j].reshape(ns, groups * n_state),
                                           state_s5_im[j].reshape(ns, groups * n_state),
                                           params, s5_d[j], s5_w_glu[j])
            mix_s = mix_t.transpose(1, 0, 2).reshape(tm_s, mixer_width)
            s5_re_p.append(hr_p.reshape(nb, groups, n_state))
            s5_im_p.append(hi_p.reshape(nb, groups, n_state))
            s5_re_s.append(hr_s.reshape(ns, groups, n_state))
            s5_im_s.append(hi_s.reshape(ns, groups, n_state))
        else:
            fox = dict(heads=fox_heads, head_dim=head_dim, cross_width=cross_width)
            kt_p, vt_p, qc_p, lf_p, qa_p, ka_p, vta_p, stats_p = _fox_proj_prompt(
                xp, norm_mix[i], w_in_fox[j], fox_b_f[j], tm=tm_p, tiles_per_seq=seq // tm_p, **fox)
            q_s, k_s, v_s, qc_s, lf_s, lft_s = _fox_proj_sample(
                xs, norm_mix[i], w_in_fox[j], fox_b_f[j], tm=tm_s, **fox)

            seq3 = lambda a: a.reshape(nb, seq, a.shape[-1])
            mix_p = _fox_prompt(seq3(qa_p), seq3(ka_p), vta_p, stats_p, heads=fox_heads, head_dim=head_dim,
                                tq=min(seq, 4 * tm_p), tk=tm_p)
            mix_p = mix_p.reshape(nb * seq, mixer_width)

            tok3 = lambda a: a.reshape(ns, n_tok, mixer_width)
            lft_new = jnp.pad(lft_s.reshape(hp, ns, n_tok).transpose(1, 0, 2),
                              ((0, 0), (0, 0), (0, page - n_tok)))
            cache_lft = jnp.pad(cache_fox_logf[j].transpose(0, 2, 1), ((0, 0), (0, hp - fox_heads), (0, 0)))
            to_stored = lambda a: a.transpose(0, 1, 3, 4, 2)
            mix_s = _fox_sample(tok3(q_s), tok3(k_s), tok3(v_s), lft_new, to_stored(cache_fox_k),
                                to_stored(cache_fox_v), cache_lft, page_table, layer=j,
                                pages_per_step=page_table.shape[1])
            mix_s = mix_s.reshape(tm_s, mixer_width)

            from_stored = lambda a: a.reshape(nb, fox_heads, head_dim, seq).transpose(0, 3, 1, 2)
            fk_p.append(from_stored(kt_p))
            fv_p.append(from_stored(vt_p))
            fl_p.append(lf_p[:, :fox_heads].reshape(nb, seq, fox_heads))
            fk_s.append(k_s.reshape(ns, n_tok, fox_heads, head_dim))
            fv_s.append(v_s.reshape(ns, n_tok, fox_heads, head_dim))
            fl_s.append(lf_s[:, :fox_heads].reshape(ns, n_tok, fox_heads))

        wom = w_out[i][:mixer_width].astype(BF16)
        woc = w_out[i][mixer_width:].astype(BF16)
        xp = _cross_out_prompt(xp, mix_p, qc_p, mk_p.reshape(nb, n_mem, cross_width),
                               mv_p.reshape(nb, n_mem, cross_width), wom, woc,
                               heads=cross_heads, head_dim=head_dim, tm=tm_p)
        xs = _cross_out_sample(xs, mix_s, qc_s, cache_mem_k.transpose(0, 1, 3, 4, 2),
                               cache_mem_v.transpose(0, 1, 3, 4, 2), wom, woc,
                               layer=i, n_tok=n_tok, seqs_per_step=8)

        g_final = norm_final if i == depth - 1 else None
        xp = _ffn(xp, *ffn_w(1), g_final, tm=tm_p)
        xs = _ffn(xs, *ffn_w(1), g_final, tm=tm_s)

    return (xp.reshape(nb, seq, d), xs.reshape(ns, n_tok, d),
            jnp.stack(s5_re_p), jnp.stack(s5_im_p), jnp.stack(s5_re_s), jnp.stack(s5_im_s),
            jnp.stack(fk_p), jnp.stack(fv_p), jnp.stack(fl_p),
            jnp.stack(fk_s), jnp.stack(fv_s), jnp.stack(fl_s),
            jnp.stack(mk_list), jnp.stack(mv_list))
```

```python
import functools

import jax
import jax.numpy as jnp
import numpy as np
from jax import lax
from jax.experimental import pallas as pl
from jax.experimental.pallas import tpu as pltpu

F32 = jnp.float32
BF16 = jnp.bfloat16

RMS_EPS = 1e-6
NEG_INF = -1e30
N_MIXERS = 2

LANES = 128
SUBLANES = 8
MXU_DIM = 256
VMEM_LIMIT_BYTES = 56 * 1024 * 1024

NT_DIMS = (((1,), (1,)), ((), ()))


def _params(*sem):
    return pltpu.CompilerParams(dimension_semantics=sem, vmem_limit_bytes=VMEM_LIMIT_BYTES)


def _resident(shape):
    nd = len(shape)
    return pl.BlockSpec(shape, lambda *_: (0,) * nd, pipeline_mode=pl.Buffered(1))


def _rms(x, g):
    return x * lax.rsqrt(jnp.mean(x * x, axis=-1, keepdims=True) + RMS_EPS) * g


def _dot(a, b):
    return jnp.dot(a, b, preferred_element_type=F32)


def _dot_nt(a, b):
    return lax.dot_general(a, b, NT_DIMS, preferred_element_type=F32)


def _log_sigmoid(x):
    return jnp.minimum(x, 0.0) - jnp.log1p(jnp.exp(-jnp.abs(x)))


def _split3(x):
    hi = x.astype(BF16)
    r1 = x - hi.astype(F32)
    mid = r1.astype(BF16)
    lo = (r1 - mid.astype(F32)).astype(BF16)
    return hi, mid, lo


def _cumsum_lanes(x, tri):
    hi, mid, lo = _split3(x)
    return _dot(hi, tri) + _dot(mid, tri) + _dot(lo, tri)


def _upper_tri():
    r = lax.broadcasted_iota(jnp.int32, (LANES, LANES), 0)
    c = lax.broadcasted_iota(jnp.int32, (LANES, LANES), 1)
    return jnp.where(r <= c, 1.0, 0.0).astype(BF16)


def _ffn_body(*refs, d_ff, ff_chunk, final):
    if final:
        x_ref, g_ref, wg_ref, wu_ref, wd_ref, gf_ref, o_ref, t_ref = refs
    else:
        x_ref, g_ref, wg_ref, wu_ref, wd_ref, o_ref, t_ref = refs
    x = x_ref[...]
    h = _rms(x, g_ref[...]).astype(BF16)
    for c in range(d_ff // ff_chunk):
        lo, hi = c * ff_chunk, (c + 1) * ff_chunk
        a = _dot(h, wg_ref[:, lo:hi])
        b = _dot(h, wu_ref[:, lo:hi])
        t_ref[:, lo:hi] = (jax.nn.silu(a) * b).astype(BF16)
    y = x + 0.5 * _dot(t_ref[...], wd_ref[...])
    if final:
        y = _rms(y, gf_ref[...])
    o_ref[...] = y


def _cast_body(x_ref, o_ref):
    o_ref[...] = x_ref[...].astype(o_ref.dtype)


def _to_bf16(w, *, rows):
    n, r, c = w.shape
    spec = pl.BlockSpec((None, rows, c), lambda i, j: (i, j, 0))
    return pl.pallas_call(
        _cast_body, grid=(n, r // rows), in_specs=[spec], out_specs=spec,
        out_shape=jax.ShapeDtypeStruct(w.shape, BF16),
        compiler_params=_params("parallel", "parallel"), name="to_bf16",
    )(w)


def _ffn(x, g, wg, wu, wd, idx, g_final=None, *, tm):
    t, d = x.shape
    d_ff = wg.shape[2]
    final = g_final is not None
    row = pl.BlockSpec((tm, d), lambda i: (i, 0))
    pick = lambda r, c: pl.BlockSpec((None, r, c), lambda i: (idx, 0, 0), pipeline_mode=pl.Buffered(1))
    in_specs = [row, _resident((1, d)), pick(d, d_ff), pick(d, d_ff), pick(d_ff, d)]
    args = [x, g.reshape(1, d), wg, wu, wd]
    if final:
        in_specs.append(_resident((1, d)))
        args.append(g_final.reshape(1, d))
    return pl.pallas_call(
        functools.partial(_ffn_body, d_ff=d_ff, ff_chunk=MXU_DIM, final=final),
        grid=(t // tm,),
        in_specs=in_specs,
        out_specs=row,
        out_shape=jax.ShapeDtypeStruct((t, d), F32),
        scratch_shapes=[pltpu.VMEM((tm, d_ff), BF16)],
        compiler_params=_params("parallel"),
        name="ffn_final" if final else "ffn",
    )(*args)


def _norm_proj_body(x_ref, g_ref, *refs, n):
    h = _rms(x_ref[...], g_ref[...]).astype(BF16)
    for w_ref, o_ref in zip(refs[:n], refs[n:]):
        o_ref[...] = _dot(h, w_ref[...])


def _norm_proj(x, g, ws, *, tm):
    t, d = x.shape
    n = len(ws)
    in_specs = [pl.BlockSpec((tm, d), lambda i: (i, 0)), _resident((1, d))]
    in_specs += [_resident(w.shape) for w in ws]
    return pl.pallas_call(
        functools.partial(_norm_proj_body, n=n),
        grid=(t // tm,),
        in_specs=in_specs,
        out_specs=[pl.BlockSpec((tm, w.shape[1]), lambda i: (i, 0)) for w in ws],
        out_shape=[jax.ShapeDtypeStruct((t, w.shape[1]), F32) for w in ws],
        compiler_params=_params("parallel"),
        name="norm_proj",
    )(x, g.reshape(1, d), *[w.astype(BF16) for w in ws])


def _fox_split(w_in, b_f, heads, head_dim):
    w = heads * head_dim
    wq, wk, wv = w_in[:, :w], w_in[:, w:2 * w], w_in[:, 2 * w:3 * w]
    wf = jnp.pad(w_in[:, 3 * w:3 * w + heads], ((0, 0), (0, LANES - heads)))
    wc = w_in[:, 3 * w + heads:]
    bf = jnp.pad(b_f, (0, LANES - heads)).reshape(1, LANES)
    return wq, wk, wv, wf, wc, bf


def _fox_proj_sample_body(x_ref, g_ref, wq_ref, wk_ref, wv_ref, wc_ref, wf_ref, wft_ref, bf_ref, bft_ref,
                          q_ref, k_ref, v_ref, qc_ref, lf_ref, lft_ref, *, scale):
    h = _rms(x_ref[...], g_ref[...]).astype(BF16)
    q_ref[...] = _dot(h, wq_ref[...]) * scale
    k_ref[...] = _dot(h, wk_ref[...])
    v_ref[...] = _dot(h, wv_ref[...])
    qc_ref[...] = _dot(h, wc_ref[...])
    lf_ref[...] = _log_sigmoid(_dot(h, wf_ref[...]) + bf_ref[...])
    lft_ref[...] = _log_sigmoid(_dot_nt(wft_ref[...], h) + bft_ref[...])


def _fox_proj_sample(x, g, w_in, b_f, *, heads, head_dim, cross_width, tm):
    t, d = x.shape
    w = heads * head_dim
    hp = 2 * SUBLANES
    wq, wk, wv, wf, wc, bf = _fox_split(w_in, b_f, heads, head_dim)
    wft = wf[:, :hp].T
    bft = bf[0, :hp].reshape(hp, 1)
    row = lambda n: pl.BlockSpec((tm, n), lambda i: (i, 0))
    outs = [w, w, w, cross_width, LANES]
    return pl.pallas_call(
        functools.partial(_fox_proj_sample_body, scale=head_dim ** -0.5),
        grid=(t // tm,),
        in_specs=[row(d), _resident((1, d)), _resident((d, w)), _resident((d, w)),
                  _resident((d, w)), _resident((d, cross_width)), _resident((d, LANES)),
                  _resident((hp, d)), _resident((1, LANES)), _resident((hp, 1))],
        out_specs=[row(n) for n in outs] + [pl.BlockSpec((hp, tm), lambda i: (0, i))],
        out_shape=[jax.ShapeDtypeStruct((t, n), F32) for n in outs]
        + [jax.ShapeDtypeStruct((hp, t), F32)],
        compiler_params=_params("parallel"),
        name="fox_proj_sample",
    )(x, g.reshape(1, d), wq.astype(BF16), wk.astype(BF16), wv.astype(BF16), wc.astype(BF16),
      wf.astype(BF16), wft.astype(BF16), bf, bft)


N_BIAS = 3


NORM_SLACK = 1.02


def _fox_proj_prompt_body(x_ref, g_ref, wqa_ref, wkt_ref, wka_ref, wvt_ref, wc_ref, wf_ref,
                          bf_ref, place_ref, oneq_ref, headsum_ref,
                          kt_ref, vt_ref, qc_ref, lf_ref, qa_ref, ka_ref, vta_ref, stats_ref, carry_ref,
                          *, scale, tiles_per_seq, head_dim):
    tile = pl.program_id(0)

    @pl.when(tile % tiles_per_seq == 0)
    def _():
        carry_ref[...] = jnp.zeros_like(carry_ref)

    def max_norm(x):
        sq = _dot((x * x).astype(BF16), headsum_ref[...])
        return jnp.sqrt(jnp.max(sq, axis=0, keepdims=True)) * NORM_SLACK

    h = _rms(x_ref[...], g_ref[...]).astype(BF16)
    q_aug = _dot(h, wqa_ref[...]) * scale
    qa_ref[...] = (q_aug + oneq_ref[...]).astype(BF16)
    stats_ref[0, pl.ds(tile, 1), :] = max_norm(q_aug)
    kt = _dot_nt(wkt_ref[...], h)
    kt_ref[...] = kt
    head_lane = lax.broadcasted_iota(jnp.int32, (1, LANES), 1)
    k_norm = jnp.zeros((1, LANES), F32)
    for hd in range(kt.shape[0] // head_dim):
        rows = kt[hd * head_dim:(hd + 1) * head_dim, :]
        sq = jnp.max(jnp.sum(rows * rows, axis=0, keepdims=True), axis=1, keepdims=True)
        k_norm = jnp.where(head_lane == hd, jnp.sqrt(sq) * NORM_SLACK, k_norm)
    stats_ref[1, pl.ds(tile, 1), :] = k_norm
    vt = _dot_nt(wvt_ref[...], h)
    vt_ref[...] = vt
    tm = vt.shape[1]
    tail = jnp.where(lax.broadcasted_iota(jnp.int32, (LANES - head_dim, tm), 0) == 0, 1.0, 0.0)
    blocks = []
    for r0 in range(0, vt.shape[0], head_dim):
        blocks += [vt[r0:r0 + head_dim, :], tail]
    vta_ref[...] = jnp.concatenate(blocks, axis=0).astype(BF16)
    qc_ref[...] = _dot(h, wc_ref[...])
    lf = _log_sigmoid(_dot(h, wf_ref[...]) + bf_ref[...])
    lf_ref[...] = lf
    r = lax.broadcasted_iota(jnp.int32, (tm, tm), 0)
    c = lax.broadcasted_iota(jnp.int32, (tm, tm), 1)
    low = jnp.where(c <= r, 1.0, 0.0).astype(BF16)
    cum = carry_ref[0:1, :] + sum(_dot(low, piece) for piece in _split3(lf))
    carry_ref[...] = jnp.broadcast_to(cum[tm - 1:, :], carry_ref.shape)
    bias = sum(_dot(piece, place_ref[j]) for j, piece in enumerate(_split3(-cum)))
    ka_ref[...] = (_dot(h, wka_ref[...]) + bias).astype(BF16)
    stats_ref[2, pl.ds(tile, 1), :] = cum[0:1, :]
    stats_ref[3, pl.ds(tile, 1), :] = cum[tm - 1:, :]


def _fox_proj_prompt(x, g, w_in, b_f, *, heads, head_dim, cross_width, tm, tiles_per_seq):
    t, d = x.shape
    w = heads * head_dim
    wa = heads * LANES
    wq, wk, wv, wf, wc, bf = _fox_split(w_in, b_f, heads, head_dim)
    aug = lambda a: jnp.pad(a.reshape(d, heads, head_dim), ((0, 0), (0, 0), (0, LANES - head_dim))).reshape(d, wa)
    head = np.arange(heads)
    place = np.zeros((N_BIAS, LANES, wa), np.float32)
    oneq = np.zeros((1, wa), np.float32)
    for j in range(N_BIAS):
        place[j, head, head * LANES + head_dim + j] = 1.0
        oneq[0, head * LANES + head_dim + j] = 1.0
    place = jnp.asarray(place)
    headsum = jnp.asarray(np.arange(wa)[:, None] // LANES == np.arange(LANES)[None, :], BF16)
    row = lambda n: pl.BlockSpec((tm, n), lambda i: (i, 0))
    col = lambda n: pl.BlockSpec((None, n, tm), lambda i: (i // tiles_per_seq, 0, i % tiles_per_seq))
    seq_len = tm * tiles_per_seq
    n_tiles = t // tm
    outs = [(cross_width, F32), (LANES, F32), (wa, BF16), (wa, BF16)]
    stats_shape = (4, n_tiles, LANES)
    return pl.pallas_call(
        functools.partial(_fox_proj_prompt_body, scale=head_dim ** -0.5, tiles_per_seq=tiles_per_seq,
                          head_dim=head_dim),
        grid=(n_tiles,),
        in_specs=[row(d), _resident((1, d)), _resident((d, wa)), _resident((w, d)), _resident((d, wa)),
                  _resident((w, d)), _resident((d, cross_width)),
                  _resident((d, LANES)), _resident((1, LANES)), _resident(place.shape),
                  _resident((1, wa)), _resident(headsum.shape)],
        out_specs=[col(w), col(w)] + [row(n) for n, _ in outs]
        + [pl.BlockSpec((wa, tm), lambda i: (0, i)), pl.BlockSpec(stats_shape, lambda i: (0, 0, 0))],
        out_shape=[jax.ShapeDtypeStruct((t // seq_len, w, seq_len), F32)] * 2
        + [jax.ShapeDtypeStruct((t, n), dt) for n, dt in outs]
        + [jax.ShapeDtypeStruct((wa, t), BF16), jax.ShapeDtypeStruct(stats_shape, F32)],
        scratch_shapes=[pltpu.VMEM((SUBLANES, LANES), F32)],
        compiler_params=_params("arbitrary"),
        name="fox_proj_prompt",
    )(x, g.reshape(1, d), aug(wq).astype(BF16), wk.T.astype(BF16), aug(wk).astype(BF16),
      wv.T.astype(BF16), wc.astype(BF16), wf.astype(BF16), bf, place.astype(BF16), oneq, headsum)


def _s5_discretise(a_re, a_im, log_dt):
    dt = jnp.exp(log_dt)
    mag = jnp.exp(dt * a_re)
    ab_re = mag * jnp.cos(dt * a_im)
    ab_im = mag * jnp.sin(dt * a_im)
    den = a_re * a_re + a_im * a_im
    nr = ab_re - 1.0
    ni = ab_im
    return ab_re, ab_im, (nr * a_re + ni * a_im) / den, (ni * a_re - nr * a_im) / den


def _s5_param_body(are_ref, aim_ref, ldt_ref, arex_ref, aimx_ref, ldtx_ref, bre_ref, bim_ref,
                   bbr_ref, bbi_ref, pwr_ref, pwi_ref):
    _, _, zr, zi = _s5_discretise(arex_ref[...], aimx_ref[...], ldtx_ref[...])
    b_re = bre_ref[...]
    b_im = bim_ref[...]
    bbr_ref[...] = zr * b_re - zi * b_im
    bbi_ref[...] = zr * b_im + zi * b_re
    ab_re, ab_im, _, _ = _s5_discretise(are_ref[...], aim_ref[...], ldt_ref[...])
    pr, pi = ab_re, ab_im
    for r in range(pwr_ref.shape[0]):
        pwr_ref[r] = pr
        pwi_ref[r] = pi
        pr, pi = pr * ab_re - pi * ab_im, pr * ab_im + pi * ab_re


def _s5_params(a_re, a_im, log_dt, b_re, b_im, c_re, c_im, *, n_pow):
    g, p, c = b_re.shape
    rows = g * c
    rep = lambda a: jnp.repeat(a, c, axis=0)
    bt = lambda b: b.transpose(0, 2, 1).reshape(rows, p)
    whole = lambda shape: pl.BlockSpec(shape, lambda: (0,) * len(shape))
    log_dt = log_dt.reshape(g, 1)
    bbr, bbi, pwr, pwi = pl.pallas_call(
        _s5_param_body,
        in_specs=[whole((g, p)), whole((g, p)), whole((g, 1)), whole((rows, p)), whole((rows, p)),
                  whole((rows, 1)), whole((rows, p)), whole((rows, p))],
        out_specs=[whole((rows, p)), whole((rows, p)), whole((n_pow, g, p)), whole((n_pow, g, p))],
        out_shape=[jax.ShapeDtypeStruct((rows, p), F32)] * 2 + [jax.ShapeDtypeStruct((n_pow, g, p), F32)] * 2,
        name="s5_params",
    )(a_re, a_im, log_dt, rep(a_re), rep(a_im), rep(log_dt), bt(b_re), bt(b_im))
    width = g * p
    pw_re, pw_im = pwr.reshape(n_pow, width), pwi.reshape(n_pow, width)
    gpt = MXU_DIM // c
    n_tiles = g // gpt
    eye = jnp.eye(gpt, dtype=F32)

    def in_tiles(bb):
        blocks = bb.reshape(n_tiles, gpt, c, p)
        return jnp.einsum("tgcp,gh->tgchp", blocks, eye).reshape(n_tiles, gpt * c, gpt * p)

    def out_tiles(cc):
        blocks = cc.reshape(n_tiles, gpt, c, p)
        return jnp.einsum("tgcp,gh->tgphc", blocks, eye).reshape(n_tiles, gpt * p, gpt * c)

    w_in = jnp.concatenate([in_tiles(bbr), in_tiles(bbi)], axis=-1).astype(BF16)
    return (pw_re[0:1], pw_im[0:1], pw_re, pw_im, w_in,
            out_tiles(c_re).astype(BF16), out_tiles(c_im).astype(BF16))


def _s5_in_proj(u, wb_ref, bre_ref, bim_ref):
    ub = u.astype(BF16)
    n_tiles, ch, two_w = wb_ref.shape
    w = two_w // 2
    for t in range(n_tiles):
        bu = _dot(ub[:, t * ch:(t + 1) * ch], wb_ref[t])
        bre_ref[:, t * w:(t + 1) * w] = bu[:, :w]
        bim_ref[:, t * w:(t + 1) * w] = bu[:, w:]


def _s5_out(u, hre_ref, him_ref, wcr_ref, wci_ref, d_ref, wglu_ref):
    n_tiles, w, _ = wcr_ref.shape
    ys = []
    for t in range(n_tiles):
        hr = hre_ref[:, t * w:(t + 1) * w].astype(BF16)
        hi = him_ref[:, t * w:(t + 1) * w].astype(BF16)
        ys.append(_dot(hr, wcr_ref[t]) - _dot(hi, wci_ref[t]))
    y = jax.nn.gelu(jnp.concatenate(ys, axis=-1) + d_ref[...] * u)
    z = _dot(y.astype(BF16), wglu_ref[...])
    half = z.shape[-1] // 2
    return z[:, :half] * jax.nn.sigmoid(z[:, half:])


SEGMENT_PITCH_PAD = SUBLANES


def _s5_prompt_body(u_ref, wb_ref, wcr_ref, wci_ref, d_ref, wglu_ref, ab8_ref, aseg_ref,
                    o_ref, hre_ref, him_ref, up_ref, slab_ref, bre_ref, bim_ref, cin_ref, carry_ref,
                    *, lane_group):
    ci = pl.program_id(1)
    rows, width = bre_ref.shape
    ch = up_ref.shape[1]
    seg = rows // SUBLANES
    pitch = slab_ref.shape[1] // SUBLANES
    n_slab = ch // LANES
    lanes = [slice(j * LANES, (j + 1) * LANES) for j in range(n_slab)]
    groups = [slice(g * lane_group, (g + 1) * lane_group) for g in range(width // lane_group)]
    tile = lambda k: pl.ds(pl.multiple_of(k * SUBLANES, SUBLANES), SUBLANES)

    @pl.when(ci == 0)
    def _():
        carry_ref[...] = jnp.zeros_like(carry_ref)

    for s in range(SUBLANES):
        for j, ls in enumerate(lanes):
            slab_ref[j, pitch * s:pitch * s + seg, :] = u_ref[0, seg * s:seg * (s + 1), ls]

    def gather(k, carry):
        for j, ls in enumerate(lanes):
            up_ref[tile(k), ls] = slab_ref[j, pl.ds(k, SUBLANES, stride=pitch), :]
        return carry

    lax.fori_loop(0, seg, gather, 0)
    u = up_ref[...]
    _s5_in_proj(u, wb_ref, bre_ref, bim_ref)

    for ls in groups:
        ar = ab8_ref[0, :, ls]
        ai = ab8_ref[1, :, ls]

        def step(k, carry, ls=ls, ar=ar, ai=ai):
            hr, hi = carry
            hr, hi = ar * hr - ai * hi + bre_ref[tile(k), ls], ar * hi + ai * hr + bim_ref[tile(k), ls]
            bre_ref[tile(k), ls] = hr
            bim_ref[tile(k), ls] = hi
            return hr, hi

        zero = jnp.zeros((SUBLANES, lane_group), F32)
        lax.fori_loop(0, seg, step, (zero, zero))

    er, ei = carry_ref[0, 0:1, :], carry_ref[1, 0:1, :]
    sr, si = aseg_ref[0], aseg_ref[1]
    for s in range(SUBLANES):
        cin_ref[0, s:s + 1, :] = er
        cin_ref[1, s:s + 1, :] = ei
        end = rows - SUBLANES + s
        er, ei = (bre_ref[end:end + 1, :] + sr * er - si * ei, bim_ref[end:end + 1, :] + sr * ei + si * er)
    carry_ref[0, 0:1, :] = er
    carry_ref[1, 0:1, :] = ei

    for ls in groups:
        ar = ab8_ref[0, :, ls]
        ai = ab8_ref[1, :, ls]

        def fix(k, carry, ls=ls, ar=ar, ai=ai):
            gr, gi = carry
            gr, gi = ar * gr - ai * gi, ar * gi + ai * gr
            bre_ref[tile(k), ls] = bre_ref[tile(k), ls] + gr
            bim_ref[tile(k), ls] = bim_ref[tile(k), ls] + gi
            return gr, gi

        lax.fori_loop(0, seg, fix, (cin_ref[0, :, ls], cin_ref[1, :, ls]))

    up_ref[...] = _s5_out(u, bre_ref, bim_ref, wcr_ref, wci_ref, d_ref, wglu_ref)

    def scatter(k, carry):
        for j, ls in enumerate(lanes):
            slab_ref[j, pl.ds(k, SUBLANES, stride=pitch), :] = up_ref[tile(k), ls]
        return carry

    lax.fori_loop(0, seg, scatter, 0)
    for s in range(SUBLANES):
        for j, ls in enumerate(lanes):
            o_ref[0, seg * s:seg * (s + 1), ls] = slab_ref[j, pitch * s:pitch * s + seg, :]

    @pl.when(ci == pl.num_programs(1) - 1)
    def _():
        hre_ref[0] = er
        him_ref[0] = ei


def _s5_prompt(u, params, d_skip, w_glu, *, rows):
    ab_re, ab_im, pw_re, pw_im, w_in, wc_re, wc_im = params
    nb, length, ch = u.shape
    width = ab_re.shape[1]
    seg = rows // SUBLANES
    ab8 = jnp.stack([jnp.broadcast_to(ab_re, (SUBLANES, width)), jnp.broadcast_to(ab_im, (SUBLANES, width))])
    a_seg = jnp.stack([pw_re[seg - 1:seg], pw_im[seg - 1:seg]])
    state = jax.ShapeDtypeStruct((nb, 1, width), F32)
    state_spec = pl.BlockSpec((1, 1, width), lambda b, c: (b, 0, 0))
    tok_spec = pl.BlockSpec((1, rows, ch), lambda b, c: (b, c, 0))
    slab_rows = SUBLANES * (seg + SEGMENT_PITCH_PAD)
    return pl.pallas_call(
        functools.partial(_s5_prompt_body, lane_group=4 * LANES),
        grid=(nb, length // rows),
        in_specs=[tok_spec, _resident(w_in.shape), _resident(wc_re.shape), _resident(wc_im.shape),
                  _resident((1, ch)), _resident(w_glu.shape), _resident(ab8.shape), _resident(a_seg.shape)],
        out_specs=[tok_spec, state_spec, state_spec],
        out_shape=[jax.ShapeDtypeStruct((nb, length, ch), F32), state, state],
        scratch_shapes=[pltpu.VMEM((rows, ch), F32), pltpu.VMEM((ch // LANES, slab_rows, LANES), F32),
                        pltpu.VMEM((rows, width), F32), pltpu.VMEM((rows, width), F32),
                        pltpu.VMEM((2, SUBLANES, width), F32), pltpu.VMEM((2, SUBLANES, width), F32)],
        compiler_params=_params("parallel", "arbitrary"),
        name="s5_prompt",
    )(u, w_in, wc_re, wc_im, d_skip.reshape(1, ch), w_glu.astype(BF16), ab8, a_seg)


def _s5_sample_body(u_ref, h0r_ref, h0i_ref, wb_ref, wcr_ref, wci_ref, d_ref, wglu_ref, ab_ref,
                    o_ref, hre_ref, him_ref, bre_ref, bim_ref):
    hre_ref[...] = h0r_ref[...]
    him_ref[...] = h0i_ref[...]
    ar = ab_ref[0:1, :]
    ai = ab_ref[1:2, :]
    for t in range(u_ref.shape[0]):
        u = u_ref[t]
        _s5_in_proj(u, wb_ref, bre_ref, bim_ref)
        hr = hre_ref[...]
        hi = him_ref[...]
        hre_ref[...] = ar * hr - ai * hi + bre_ref[...]
        him_ref[...] = ar * hi + ai * hr + bim_ref[...]
        o_ref[t] = _s5_out(u, hre_ref, him_ref, wcr_ref, wci_ref, d_ref, wglu_ref)


def _s5_sample(u, h0_re, h0_im, params, d_skip, w_glu):
    ab_re, ab_im, _, _, w_in, wc_re, wc_im = params
    nt, nb, ch = u.shape
    width = ab_re.shape[1]
    ab = jnp.concatenate([ab_re, ab_im], axis=0)
    whole = lambda shape: pl.BlockSpec(shape, lambda: (0,) * len(shape))
    state = jax.ShapeDtypeStruct((nb, width), F32)
    return pl.pallas_call(
        _s5_sample_body,
        in_specs=[whole(u.shape), whole((nb, width)), whole((nb, width)), whole(w_in.shape),
                  whole(wc_re.shape), whole(wc_im.shape), whole((1, ch)), whole(w_glu.shape),
                  whole(ab.shape)],
        out_specs=[whole(u.shape), whole((nb, width)), whole((nb, width))],
        out_shape=[jax.ShapeDtypeStruct(u.shape, F32), state, state],
        scratch_shapes=[pltpu.VMEM((nb, width), F32), pltpu.VMEM((nb, width), F32)],
        compiler_params=pltpu.CompilerParams(vmem_limit_bytes=VMEM_LIMIT_BYTES),
        name="s5_sample",
    )(u, h0_re, h0_im, w_in, wc_re, wc_im, d_skip.reshape(1, ch), w_glu.astype(BF16), ab)


HEADS_PER_STEP = 2


UNDERFLOW_MARGIN = 110.0


def _first_live_block(st_ref, base, n_seq_blk, first_chunk, n_chunk, pair):
    kn = st_ref[1, pl.ds(base, n_seq_blk), :]
    c_last = st_ref[3, pl.ds(base, n_seq_blk), :]
    rows = [kn[0:1]]
    for j in range(1, n_seq_blk):
        rows.append(jnp.maximum(rows[-1], kn[j:j + 1]))
    kn_run = jnp.concatenate(rows, axis=0)
    worst = None
    for c in range(n_chunk):
        r = base + first_chunk + c
        term = st_ref[0, pl.ds(r, 1), :] * (kn_run + st_ref[1, pl.ds(r, 1), :]) + st_ref[2, pl.ds(r, 1), :]
        worst = term if worst is None else jnp.maximum(worst, term)
    blk = lax.broadcasted_iota(jnp.int32, worst.shape, 0)
    lane = lax.broadcasted_iota(jnp.int32, (1, LANES), 1)
    dead = (worst - c_last < -UNDERFLOW_MARGIN) & (blk < first_chunk)
    count = jnp.sum(jnp.where(dead, 1, 0), axis=0, keepdims=True)
    return jnp.min(jnp.where(lane // HEADS_PER_STEP == pair, count, n_seq_blk))


def _fox_prompt_body(q_ref, k_ref, vt_ref, st_ref, o_ref, m_ref, acc_ref, *, head_dim, tk):
    qi = pl.program_id(2)
    tq = q_ref.shape[1]
    n_chunk = tq // tk
    n_seq_blk = k_ref.shape[1] // tk
    first_live = _first_live_block(st_ref, pl.program_id(0) * n_seq_blk, n_seq_blk, qi * n_chunk, n_chunk,
                                   pl.program_id(1))
    m_ref[...] = jnp.full_like(m_ref, NEG_INF)
    acc_ref[...] = jnp.zeros_like(acc_ref)

    def block(ks, first_chunk, masked_chunk):
        chains = [(h, slice(h * LANES, (h + 1) * LANES), c, slice(c * tk, (c + 1) * tk))
                  for h in range(HEADS_PER_STEP) for c in range(first_chunk, n_chunk)]
        scores = []
        for _, hl, c, qs in chains:
            s = _dot_nt(k_ref[0, pl.ds(ks, tk), hl], q_ref[0, qs, hl])
            if c == masked_chunk:
                key = lax.broadcasted_iota(jnp.int32, s.shape, 0)
                qry = lax.broadcasted_iota(jnp.int32, s.shape, 1)
                s = jnp.where(key <= qry, s, NEG_INF)
            scores.append(s)
        probs, alphas = [], []
        for s, (h, _, _, qs) in zip(scores, chains):
            m_old = m_ref[h, :, qs]
            m_new = jnp.maximum(m_old, jnp.max(s, axis=0, keepdims=True))
            alphas.append(jnp.exp(m_old - m_new))
            probs.append(jnp.exp(s - m_new).astype(BF16))
            m_ref[h, :, qs] = m_new
        for p, alpha, (h, hl, _, qs) in zip(probs, alphas, chains):
            acc_ref[h, :, qs] = alpha * acc_ref[h, :, qs] + _dot(vt_ref[hl, pl.ds(ks, tk)], p)

    def body(kj, carry):
        block(pl.multiple_of(kj * tk, tk), 0, None)
        return carry

    lax.fori_loop(first_live, qi * n_chunk, body, 0)
    for c in range(n_chunk):
        block(pl.multiple_of((qi * n_chunk + c) * tk, tk), c, c)
    outs = [acc_ref[h, :head_dim, :] / acc_ref[h, head_dim:head_dim + 1, :] for h in range(HEADS_PER_STEP)]
    o_ref[0] = jnp.concatenate(outs, axis=0).T


def _fox_prompt(qa, ka, vta, stats, *, heads, head_dim, tq, tk):
    nb, length, _ = qa.shape
    wide = HEADS_PER_STEP * LANES
    return pl.pallas_call(
        functools.partial(_fox_prompt_body, head_dim=head_dim, tk=tk),
        grid=(nb, heads // HEADS_PER_STEP, length // tq),
        in_specs=[pl.BlockSpec((1, tq, wide), lambda b, hp, i: (b, i, hp)),
                  pl.BlockSpec((1, length, wide), lambda b, hp, i: (b, 0, hp)),
                  pl.BlockSpec((wide, length), lambda b, hp, i: (hp, b)),
                  pl.BlockSpec(stats.shape, lambda b, hp, i: (0, 0, 0))],
        out_specs=pl.BlockSpec((1, tq, HEADS_PER_STEP * head_dim), lambda b, hp, i: (b, i, hp)),
        out_shape=jax.ShapeDtypeStruct((nb, length, heads * head_dim), F32),
        scratch_shapes=[pltpu.VMEM((HEADS_PER_STEP, 1, tq), F32),
                        pltpu.VMEM((HEADS_PER_STEP, LANES, tq), F32)],
        compiler_params=_params("parallel", "parallel", "arbitrary"),
        name="fox_prompt",
    )(qa, ka, vta, stats)


def _fox_sample_body(pt_ref, q_ref, *refs, pages_per_step, head_dim):
    n = pages_per_step
    kt_refs, vt_refs, lf_refs = refs[:n], refs[n:2 * n], refs[2 * n:3 * n]
    kn_ref, vn_ref, lfn_ref, o_ref, qrow_ref, m_ref, l_ref, acc_ref, carry_ref = refs[3 * n:]
    del pt_ref
    g = pl.program_id(1)
    n_tok, w = q_ref.shape
    hp, page = lf_refs[0].shape
    head_of_lane = lax.broadcasted_iota(jnp.int32, (hp, w), 1) // head_dim
    head_mask = head_of_lane == lax.broadcasted_iota(jnp.int32, (hp, w), 0)
    tri = _upper_tri()

    @pl.when(g == 0)
    def _():
        zero = jnp.zeros((hp, w), F32)
        rows = [jnp.where(head_mask, jnp.broadcast_to(q_ref[t:t + 1, :], (hp, w)), zero)
                for t in range(n_tok)]
        qrow_ref[...] = jnp.concatenate(rows, axis=0).astype(BF16)
        m_ref[...] = jnp.full_like(m_ref, NEG_INF)
        l_ref[...] = jnp.zeros_like(l_ref)
        acc_ref[...] = jnp.zeros_like(acc_ref)
        carry_ref[...] = jnp.zeros_like(carry_ref)

    def update(pages, valid=None):
        scores, base = [], carry_ref[:, 0:1]
        for s, lf, _ in pages:
            local = _cumsum_lanes(lf, tri)
            c = local + base
            base = base + local[:, page - 1:]
            s = s - jnp.concatenate([c] * n_tok, axis=0)
            scores.append(s if valid is None else jnp.where(valid, s, NEG_INF))
        carry_ref[...] = jnp.broadcast_to(base, carry_ref.shape)
        m_old = m_ref[...]
        m_new = functools.reduce(jnp.maximum, [jnp.max(s, axis=1, keepdims=True) for s in scores], m_old)
        alpha = jnp.exp(m_old - m_new)
        probs = [jnp.exp(s - m_new) for s in scores]
        l_ref[...] = alpha * l_ref[...] + sum(jnp.sum(p, axis=1, keepdims=True) for p in probs)
        acc_ref[...] = alpha * acc_ref[...] + sum(pv(p.astype(BF16)) for p, (_, _, pv) in zip(probs, pages))
        m_ref[...] = m_new

    def cached(i):
        kt = kt_refs[i][...].reshape(w, page).astype(BF16)
        vt = vt_refs[i][...].reshape(w, page).astype(BF16)
        return _dot(qrow_ref[...], kt), lf_refs[i][...], lambda p: _dot_nt(p, vt)

    update([cached(i) for i in range(n)])

    @pl.when(g == pl.num_programs(1) - 1)
    def _():
        pad = jnp.zeros((page - kn_ref.shape[0], w), F32)
        kn = jnp.concatenate([kn_ref[...], pad], axis=0).astype(BF16)
        vn = jnp.concatenate([vn_ref[...], pad], axis=0).astype(BF16)
        key = lax.broadcasted_iota(jnp.int32, (n_tok * hp, page), 1)
        tok = lax.broadcasted_iota(jnp.int32, (n_tok * hp, page), 0) // hp
        update([(_dot_nt(qrow_ref[...], kn), lfn_ref[...], lambda p: _dot(p, vn))], key <= tok)
        out = acc_ref[...] / l_ref[...]
        for t in range(n_tok):
            picked = jnp.where(head_mask, out[t * hp:(t + 1) * hp, :], 0.0)
            o_ref[t:t + 1, :] = jnp.sum(picked, axis=0, keepdims=True)


def _fox_sample(q, k_new, v_new, lft_new, cache_kt, cache_vt, cache_lft, page_table, *, layer, pages_per_step):
    ns, n_tok, w = q.shape
    n_pages = page_table.shape[1]
    _, _, heads, head_dim, page = cache_kt.shape
    hp = cache_lft.shape[1]
    n = pages_per_step
    pad_rows = lambda a: jnp.pad(a, ((0, 0), (0, SUBLANES - n_tok), (0, 0)))

    def kv_page(i):
        return pl.BlockSpec((None, None, heads, head_dim, page),
                            lambda s, g, pt: (layer, pt[s, g * n + i], 0, 0, 0))

    def lf_page(i):
        return pl.BlockSpec((None, hp, page), lambda s, g, pt: (pt[s, g * n + i], 0, 0))

    per_seq = lambda shape: pl.BlockSpec((None,) + shape, lambda s, g, pt: (s, 0, 0))
    in_specs = [per_seq((n_tok, w))] + [kv_page(i) for i in range(n)] * 2 + [lf_page(i) for i in range(n)]
    in_specs += [per_seq((SUBLANES, w)), per_seq((SUBLANES, w)), per_seq((hp, page))]
    rows = n_tok * hp
    return pl.pallas_call(
        functools.partial(_fox_sample_body, pages_per_step=n, head_dim=head_dim),
        grid_spec=pltpu.PrefetchScalarGridSpec(
            num_scalar_prefetch=1,
            grid=(ns, n_pages // n),
            in_specs=in_specs,
            out_specs=per_seq((n_tok, w)),
            scratch_shapes=[pltpu.VMEM((rows, w), BF16), pltpu.VMEM((rows, 1), F32),
                            pltpu.VMEM((rows, 1), F32), pltpu.VMEM((rows, w), F32),
                            pltpu.VMEM((hp, page), F32)],
        ),
        out_shape=jax.ShapeDtypeStruct((ns, n_tok, w), F32),
        compiler_params=_params("parallel", "arbitrary"),
        name="fox_sample",
    )(page_table, q, *([cache_kt] * n), *([cache_vt] * n), *([cache_lft] * n),
      pad_rows(k_new), pad_rows(v_new), lft_new)


def _softmax_pv(s, vb):
    m = jnp.max(s, axis=-1, keepdims=True)
    p = jnp.exp(s - m)
    return _dot(p.astype(BF16), vb) / jnp.sum(p, axis=-1, keepdims=True)


def _cross_out_prompt_body(x_ref, mix_ref, qc_ref, mk_ref, mv_ref, wom_ref, woc_ref, o_ref,
                           *, heads, head_dim):
    tm, cw = qc_ref.shape
    q = qc_ref[...] * head_dim ** -0.5
    head_of_lane = lax.broadcasted_iota(jnp.int32, (1, cw), 1) // head_dim
    zero = jnp.zeros_like(q)
    q4 = jnp.concatenate([jnp.where(head_of_lane == h, q, zero) for h in range(heads)], axis=0)
    s = _dot_nt(q4.astype(BF16), mk_ref[0].astype(BF16))
    o4 = _softmax_pv(s, mv_ref[0].astype(BF16))
    cross = zero
    for h in range(heads):
        cross = cross + jnp.where(head_of_lane == h, o4[h * tm:(h + 1) * tm, :], zero)
    o_ref[...] = (x_ref[...] + _dot(mix_ref[...].astype(BF16), wom_ref[...])
                  + _dot(cross.astype(BF16), woc_ref[...]))


def _cross_out_prompt(x, mix, qc, mk, mv, wom, woc, *, heads, head_dim, tm):
    t, d = x.shape
    nb, n_mem, cw = mk.shape
    per_b = t // nb // tm
    row = lambda n: pl.BlockSpec((tm, n), lambda i: (i, 0))
    mem = pl.BlockSpec((1, n_mem, cw), lambda i: (i // per_b, 0, 0))
    return pl.pallas_call(
        functools.partial(_cross_out_prompt_body, heads=heads, head_dim=head_dim),
        grid=(t // tm,),
        in_specs=[row(d), row(mix.shape[1]), row(cw), mem, mem, _resident(wom.shape),
                  _resident(woc.shape)],
        out_specs=row(d),
        out_shape=jax.ShapeDtypeStruct((t, d), F32),
        compiler_params=_params("parallel"),
        name="cross_out_prompt",
    )(x, mix, qc, mk, mv, wom, woc)


def _cross_out_sample_body(x_ref, mix_ref, qc_ref, mkt_ref, mvt_ref, wom_ref, woc_ref, o_ref, cross_ref,
                           *, n_tok):
    n_seq, _, head_dim, n_mem = mkt_ref.shape
    cw = qc_ref.shape[1]
    scale = head_dim ** -0.5
    head_of_lane = lax.broadcasted_iota(jnp.int32, (SUBLANES, cw), 1) // head_dim
    head_mask = head_of_lane == lax.broadcasted_iota(jnp.int32, (SUBLANES, cw), 0)
    zero = jnp.zeros((SUBLANES, cw), F32)
    for i in range(n_seq):
        rows = [jnp.where(head_mask, jnp.broadcast_to(qc_ref[pl.ds(i * n_tok + t, 1), :] * scale,
                                                     (SUBLANES, cw)), zero) for t in range(n_tok)]
        q = jnp.concatenate(rows, axis=0).astype(BF16)
        s = _dot(q, mkt_ref[i].reshape(cw, n_mem).astype(BF16))
        p = jnp.exp(s - jnp.max(s, axis=-1, keepdims=True))
        o = _dot_nt(p.astype(BF16), mvt_ref[i].reshape(cw, n_mem).astype(BF16)) / jnp.sum(p, axis=-1, keepdims=True)
        for t in range(n_tok):
            picked = jnp.where(head_mask, o[t * SUBLANES:(t + 1) * SUBLANES, :], zero)
            cross_ref[pl.ds(i * n_tok + t, 1), :] = jnp.sum(picked, axis=0, keepdims=True)
    o_ref[...] = (x_ref[...] + _dot(mix_ref[...].astype(BF16), wom_ref[...])
                  + _dot(cross_ref[...].astype(BF16), woc_ref[...]))


def _cross_out_sample(x, mix, qc, mkt, mvt, wom, woc, *, layer, n_tok, seqs_per_step):
    t, d = x.shape
    _, ns, heads, head_dim, n_mem = mkt.shape
    cw = heads * head_dim
    tm = seqs_per_step * n_tok
    row = lambda n: pl.BlockSpec((tm, n), lambda i: (i, 0))
    mem = pl.BlockSpec((None, seqs_per_step, heads, head_dim, n_mem), lambda i: (layer, i, 0, 0, 0))
    return pl.pallas_call(
        functools.partial(_cross_out_sample_body, n_tok=n_tok),
        grid=(ns // seqs_per_step,),
        in_specs=[row(d), row(mix.shape[1]), row(cw), mem, mem, _resident(wom.shape),
                  _resident(woc.shape)],
        out_specs=row(d),
        out_shape=jax.ShapeDtypeStruct((t, d), F32),
        scratch_shapes=[pltpu.VMEM((tm, cw), F32)],
        compiler_params=_params("parallel"),
        name="cross_out_sample",
    )(x, mix, qc, mkt, mvt, wom, woc)


def kernel(x_prompt, x_sample, mem_prompt, state_s5_re, state_s5_im, cache_fox_k, cache_fox_v, cache_fox_logf, cache_mem_k, cache_mem_v, page_table, ffn_norm, ffn_w_gate, ffn_w_up, ffn_w_down, norm_mix, norm_mem, w_mem_kv, w_in_s5, s5_a_re, s5_a_im, s5_log_dt, s5_b_re, s5_b_im, s5_c_re, s5_c_im, s5_d, s5_w_glu, w_in_fox, fox_b_f, w_out, norm_final):
    nb, seq, d = x_prompt.shape
    ns, n_tok, _ = x_sample.shape
    depth = ffn_norm.shape[0]
    n_mem = mem_prompt.shape[1]
    cross_heads, head_dim = cache_mem_k.shape[3], cache_mem_k.shape[4]
    cross_width = cross_heads * head_dim
    mixer_width = w_out.shape[1] - cross_width
    fox_heads = cache_fox_k.shape[3]
    n_phys, page = cache_fox_k.shape[1], cache_fox_k.shape[2]
    groups, n_state = state_s5_re.shape[2], state_s5_re.shape[3]
    hp = 2 * SUBLANES

    tm_p = 512
    tm_s = ns * n_tok
    xp = x_prompt.reshape(nb * seq, d)
    xs = x_sample.reshape(tm_s, d)
    mem = mem_prompt.reshape(nb * n_mem, d)

    s5_re_p, s5_im_p, s5_re_s, s5_im_s = [], [], [], []
    fk_p, fv_p, fl_p, fk_s, fv_s, fl_s = [], [], [], [], [], []
    mk_list, mv_list = [], []
    d_ff = ffn_w_gate.shape[-1]
    wg_all = _to_bf16(ffn_w_gate.reshape(2 * depth, d, d_ff), rows=tm_p)
    wu_all = _to_bf16(ffn_w_up.reshape(2 * depth, d, d_ff), rows=tm_p)
    wd_all = _to_bf16(ffn_w_down.reshape(2 * depth, d_ff, d), rows=d_ff // 2)
    for i in range(depth):
        j = i // N_MIXERS
        ffn_w = lambda half: (ffn_norm[i, half], wg_all, wu_all, wd_all, 2 * i + half)
        xp = _ffn(xp, *ffn_w(0), tm=tm_p)
        xs = _ffn(xs, *ffn_w(0), tm=tm_s)

        mk_p, mv_p = _norm_proj(mem, norm_mem[i],
                                [w_mem_kv[i][:, :cross_width], w_mem_kv[i][:, cross_width:]], tm=nb * n_mem)
        mk_list.append(mk_p.reshape(nb, n_mem, cross_heads, head_dim))
        mv_list.append(mv_p.reshape(nb, n_mem, cross_heads, head_dim))

        if i % N_MIXERS == 0:
            w_in = [w_in_s5[j][:, :mixer_width], w_in_s5[j][:, mixer_width:]]
            u_p, qc_p = _norm_proj(xp, norm_mix[i], w_in, tm=tm_p)
            u_s, qc_s = _norm_proj(xs, norm_mix[i], w_in, tm=tm_s)
            params = _s5_params(s5_a_re[j], s5_a_im[j], s5_log_dt[j], s5_b_re[j], s5_b_im[j],
                                s5_c_re[j], s5_c_im[j], n_pow=tm_p // SUBLANES)
            mix_p, hr_p, hi_p = _s5_prompt(u_p.reshape(nb, seq, mixer_width), params, s5_d[j],
                                           s5_w_glu[j], rows=tm_p)
            mix_p = mix_p.reshape(nb * seq, mixer_width)
            u_t = u_s.reshape(ns, n_tok, mixer_width).transpose(1, 0, 2)
            mix_t, hr_s, hi_s = _s5_sample(u_t, state_s5_re[j].reshape(ns, groups * n_state),
                                           state_s5_im[j].reshape(ns, groups * n_state),
                                           params, s5_d[j], s5_w_glu[j])
            mix_s = mix_t.transpose(1, 0, 2).reshape(tm_s, mixer_width)
            s5_re_p.append(hr_p.reshape(nb, groups, n_state))
            s5_im_p.append(hi_p.reshape(nb, groups, n_state))
            s5_re_s.append(hr_s.reshape(ns, groups, n_state))
            s5_im_s.append(hi_s.reshape(ns, groups, n_state))
        else:
            fox = dict(heads=fox_heads, head_dim=head_dim, cross_width=cross_width)
            kt_p, vt_p, qc_p, lf_p, qa_p, ka_p, vta_p, stats_p = _fox_proj_prompt(
                xp, norm_mix[i], w_in_fox[j], fox_b_f[j], tm=tm_p, tiles_per_seq=seq // tm_p, **fox)
            q_s, k_s, v_s, qc_s, lf_s, lft_s = _fox_proj_sample(
                xs, norm_mix[i], w_in_fox[j], fox_b_f[j], tm=tm_s, **fox)

            seq3 = lambda a: a.reshape(nb, seq, a.shape[-1])
            mix_p = _fox_prompt(seq3(qa_p), seq3(ka_p), vta_p, stats_p, heads=fox_heads, head_dim=head_dim,
                                tq=min(seq, 4 * tm_p), tk=tm_p)
            mix_p = mix_p.reshape(nb * seq, mixer_width)

            tok3 = lambda a: a.reshape(ns, n_tok, mixer_width)
            lft_new = jnp.pad(lft_s.reshape(hp, ns, n_tok).transpose(1, 0, 2),
                              ((0, 0), (0, 0), (0, page - n_tok)))
            cache_lft = jnp.pad(cache_fox_logf[j].transpose(0, 2, 1), ((0, 0), (0, hp - fox_heads), (0, 0)))
            to_stored = lambda a: a.transpose(0, 1, 3, 4, 2)
            mix_s = _fox_sample(tok3(q_s), tok3(k_s), tok3(v_s), lft_new, to_stored(cache_fox_k),
                                to_stored(cache_fox_v), cache_lft, page_table, layer=j,
                                pages_per_step=page_table.shape[1])
            mix_s = mix_s.reshape(tm_s, mixer_width)

            from_stored = lambda a: a.reshape(nb, fox_heads, head_dim, seq).transpose(0, 3, 1, 2)
            fk_p.append(from_stored(kt_p))
            fv_p.append(from_stored(vt_p))
            fl_p.append(lf_p[:, :fox_heads].reshape(nb, seq, fox_heads))
            fk_s.append(k_s.reshape(ns, n_tok, fox_heads, head_dim))
            fv_s.append(v_s.reshape(ns, n_tok, fox_heads, head_dim))
            fl_s.append(lf_s[:, :fox_heads].reshape(ns, n_tok, fox_heads))

        wom = w_out[i][:mixer_width].astype(BF16)
        woc = w_out[i][mixer_width:].astype(BF16)
        xp = _cross_out_prompt(xp, mix_p, qc_p, mk_p.reshape(nb, n_mem, cross_width),
                               mv_p.reshape(nb, n_mem, cross_width), wom, woc,
                               heads=cross_heads, head_dim=head_dim, tm=tm_p)
        xs = _cross_out_sample(xs, mix_s, qc_s, cache_mem_k.transpose(0, 1, 3, 4, 2),
                               cache_mem_v.transpose(0, 1, 3, 4, 2), wom, woc,
                               layer=i, n_tok=n_tok, seqs_per_step=8)

        g_final = norm_final if i == depth - 1 else None
        xp = _ffn(xp, *ffn_w(1), g_final, tm=tm_p)
        xs = _ffn(xs, *ffn_w(1), g_final, tm=tm_s)

    return (xp.reshape(nb, seq, d), xs.reshape(ns, n_tok, d),
            jnp.stack(s5_re_p), jnp.stack(s5_im_p), jnp.stack(s5_re_s), jnp.stack(s5_im_s),
            jnp.stack(fk_p), jnp.stack(fv_p), jnp.stack(fl_p),
            jnp.stack(fk_s), jnp.stack(fv_s), jnp.stack(fl_s),
            jnp.stack(mk_list), jnp.stack(mv_list))
```

```python
import functools

import jax
import jax.numpy as jnp
import numpy as np
from jax import lax
from jax.experimental import pallas as pl
from jax.experimental.pallas import tpu as pltpu

F32 = jnp.float32
BF16 = jnp.bfloat16

RMS_EPS = 1e-6
NEG_INF = -1e30
N_MIXERS = 2

LANES = 128
SUBLANES = 8
MXU_DIM = 256
VMEM_LIMIT_BYTES = 56 * 1024 * 1024

NT_DIMS = (((1,), (1,)), ((), ()))


def _params(*sem):
    return pltpu.CompilerParams(dimension_semantics=sem, vmem_limit_bytes=VMEM_LIMIT_BYTES)


def _resident(shape):
    nd = len(shape)
    return pl.BlockSpec(shape, lambda *_: (0,) * nd, pipeline_mode=pl.Buffered(1))


def _rms(x, g):
    return x * lax.rsqrt(jnp.mean(x * x, axis=-1, keepdims=True) + RMS_EPS) * g


def _dot(a, b):
    return jnp.dot(a, b, preferred_element_type=F32)


def _dot_nt(a, b):
    return lax.dot_general(a, b, NT_DIMS, preferred_element_type=F32)


def _log_sigmoid(x):
    return jnp.minimum(x, 0.0) - jnp.log1p(jnp.exp(-jnp.abs(x)))


def _split3(x):
    hi = x.astype(BF16)
    r1 = x - hi.astype(F32)
    mid = r1.astype(BF16)
    lo = (r1 - mid.astype(F32)).astype(BF16)
    return hi, mid, lo


def _cumsum_lanes(x, tri):
    hi, mid, lo = _split3(x)
    return _dot(hi, tri) + _dot(mid, tri) + _dot(lo, tri)


def _upper_tri():
    r = lax.broadcasted_iota(jnp.int32, (LANES, LANES), 0)
    c = lax.broadcasted_iota(jnp.int32, (LANES, LANES), 1)
    return jnp.where(r <= c, 1.0, 0.0).astype(BF16)


def _half_ffn(x, g_ref, wg_ref, wu_ref, wd_ref, t_ref, ff_chunk):
    h = _rms(x, g_ref[...]).astype(BF16)
    for lo in range(0, wg_ref.shape[1], ff_chunk):
        a = _dot(h, wg_ref[:, lo:lo + ff_chunk])
        b = _dot(h, wu_ref[:, lo:lo + ff_chunk])
        t_ref[:, lo:lo + ff_chunk] = (jax.nn.silu(a) * b).astype(BF16)
    return x + 0.5 * _dot(t_ref[...], wd_ref[...])


def _ffn_body(*refs, ff_chunk, final):
    if final:
        x_ref, g_ref, wg_ref, wu_ref, wd_ref, gf_ref, o_ref, t_ref = refs
    else:
        x_ref, g_ref, wg_ref, wu_ref, wd_ref, o_ref, t_ref = refs
    y = _half_ffn(x_ref[...], g_ref, wg_ref, wu_ref, wd_ref, t_ref, ff_chunk)
    if final:
        y = _rms(y, gf_ref[...])
    o_ref[...] = y


def _cast_body(x_ref, o_ref):
    o_ref[...] = x_ref[...].astype(o_ref.dtype)


def _to_bf16(w, *, rows):
    n, r, c = w.shape
    spec = pl.BlockSpec((None, rows, c), lambda i, j: (i, j, 0))
    return pl.pallas_call(
        _cast_body, grid=(n, r // rows), in_specs=[spec], out_specs=spec,
        out_shape=jax.ShapeDtypeStruct(w.shape, BF16),
        compiler_params=_params("parallel", "parallel"), name="to_bf16",
    )(w)


def _ffn(x, g, wg, wu, wd, idx, g_final=None, *, tm):
    t, d = x.shape
    d_ff = wg.shape[2]
    final = g_final is not None
    row = pl.BlockSpec((tm, d), lambda i: (i, 0))
    pick = lambda r, c: pl.BlockSpec((None, r, c), lambda i: (idx, 0, 0), pipeline_mode=pl.Buffered(1))
    in_specs = [row, _resident((1, d)), pick(d, d_ff), pick(d, d_ff), pick(d_ff, d)]
    args = [x, g.reshape(1, d), wg, wu, wd]
    if final:
        in_specs.append(_resident((1, d)))
        args.append(g_final.reshape(1, d))
    return pl.pallas_call(
        functools.partial(_ffn_body, ff_chunk=MXU_DIM, final=final),
        grid=(t // tm,),
        in_specs=in_specs,
        out_specs=row,
        out_shape=jax.ShapeDtypeStruct((t, d), F32),
        scratch_shapes=[pltpu.VMEM((tm, d_ff), BF16)],
        compiler_params=_params("parallel"),
        name="ffn_final" if final else "ffn",
    )(*args)


def _norm_proj_body(x_ref, g_ref, *refs, n):
    h = _rms(x_ref[...], g_ref[...]).astype(BF16)
    for w_ref, o_ref in zip(refs[:n], refs[n:]):
        o_ref[...] = _dot(h, w_ref[...])


def _norm_proj(x, g, ws, *, tm):
    t, d = x.shape
    n = len(ws)
    in_specs = [pl.BlockSpec((tm, d), lambda i: (i, 0)), _resident((1, d))]
    in_specs += [_resident(w.shape) for w in ws]
    return pl.pallas_call(
        functools.partial(_norm_proj_body, n=n),
        grid=(t // tm,),
        in_specs=in_specs,
        out_specs=[pl.BlockSpec((tm, w.shape[1]), lambda i: (i, 0)) for w in ws],
        out_shape=[jax.ShapeDtypeStruct((t, w.shape[1]), F32) for w in ws],
        compiler_params=_params("parallel"),
        name="norm_proj",
    )(x, g.reshape(1, d), *[w.astype(BF16) for w in ws])


def _fox_split(w_in, b_f, heads, head_dim):
    w = heads * head_dim
    wq, wk, wv = w_in[:, :w], w_in[:, w:2 * w], w_in[:, 2 * w:3 * w]
    wf = jnp.pad(w_in[:, 3 * w:3 * w + heads], ((0, 0), (0, LANES - heads)))
    wc = w_in[:, 3 * w + heads:]
    bf = jnp.pad(b_f, (0, LANES - heads)).reshape(1, LANES)
    return wq, wk, wv, wf, wc, bf


def _fox_proj_sample_body(x_ref, g_ref, wq_ref, wk_ref, wv_ref, wc_ref, wf_ref, wft_ref, bf_ref, bft_ref,
                          q_ref, k_ref, v_ref, qc_ref, lf_ref, lft_ref, *, scale):
    h = _rms(x_ref[...], g_ref[...]).astype(BF16)
    q_ref[...] = _dot(h, wq_ref[...]) * scale
    k_ref[...] = _dot(h, wk_ref[...])
    v_ref[...] = _dot(h, wv_ref[...])
    qc_ref[...] = _dot(h, wc_ref[...])
    lf_ref[...] = _log_sigmoid(_dot(h, wf_ref[...]) + bf_ref[...])
    lft_ref[...] = _log_sigmoid(_dot_nt(wft_ref[...], h) + bft_ref[...])


def _fox_proj_sample(x, g, w_in, b_f, *, heads, head_dim, cross_width, tm):
    t, d = x.shape
    w = heads * head_dim
    hp = 2 * SUBLANES
    wq, wk, wv, wf, wc, bf = _fox_split(w_in, b_f, heads, head_dim)
    wft = wf[:, :hp].T
    bft = bf[0, :hp].reshape(hp, 1)
    row = lambda n: pl.BlockSpec((tm, n), lambda i: (i, 0))
    outs = [w, w, w, cross_width, LANES]
    return pl.pallas_call(
        functools.partial(_fox_proj_sample_body, scale=head_dim ** -0.5),
        grid=(t // tm,),
        in_specs=[row(d), _resident((1, d)), _resident((d, w)), _resident((d, w)),
                  _resident((d, w)), _resident((d, cross_width)), _resident((d, LANES)),
                  _resident((hp, d)), _resident((1, LANES)), _resident((hp, 1))],
        out_specs=[row(n) for n in outs] + [pl.BlockSpec((hp, tm), lambda i: (0, i))],
        out_shape=[jax.ShapeDtypeStruct((t, n), F32) for n in outs]
        + [jax.ShapeDtypeStruct((hp, t), F32)],
        compiler_params=_params("parallel"),
        name="fox_proj_sample",
    )(x, g.reshape(1, d), wq.astype(BF16), wk.astype(BF16), wv.astype(BF16), wc.astype(BF16),
      wf.astype(BF16), wft.astype(BF16), bf, bft)


N_BIAS = 3


NORM_SLACK = 1.02


def _fox_proj_prompt_body(x_ref, g_ref, wqa_ref, wkt_ref, wka_ref, wvt_ref, wc_ref, wf_ref,
                          bf_ref, place_ref, oneq_ref, headsum_ref,
                          kt_ref, vt_ref, qc_ref, lf_ref, qa_ref, ka_ref, vta_ref, stats_ref, carry_ref,
                          *, scale, tiles_per_seq, head_dim):
    tile = pl.program_id(0)

    @pl.when(tile % tiles_per_seq == 0)
    def _():
        carry_ref[...] = jnp.zeros_like(carry_ref)

    def max_norm(x):
        sq = _dot((x * x).astype(BF16), headsum_ref[...])
        return jnp.sqrt(jnp.max(sq, axis=0, keepdims=True)) * NORM_SLACK

    h = _rms(x_ref[...], g_ref[...]).astype(BF16)
    q_aug = _dot(h, wqa_ref[...]) * scale
    qa_ref[...] = (q_aug + oneq_ref[...]).astype(BF16)
    stats_ref[0, pl.ds(tile, 1), :] = max_norm(q_aug)
    kt = _dot_nt(wkt_ref[...], h)
    kt_ref[...] = kt
    head_lane = lax.broadcasted_iota(jnp.int32, (1, LANES), 1)
    k_norm = jnp.zeros((1, LANES), F32)
    for hd in range(kt.shape[0] // head_dim):
        rows = kt[hd * head_dim:(hd + 1) * head_dim, :]
        sq = jnp.max(jnp.sum(rows * rows, axis=0, keepdims=True), axis=1, keepdims=True)
        k_norm = jnp.where(head_lane == hd, jnp.sqrt(sq) * NORM_SLACK, k_norm)
    stats_ref[1, pl.ds(tile, 1), :] = k_norm
    vt = _dot_nt(wvt_ref[...], h)
    vt_ref[...] = vt
    tm = vt.shape[1]
    tail = jnp.where(lax.broadcasted_iota(jnp.int32, (LANES - head_dim, tm), 0) == 0, 1.0, 0.0)
    blocks = []
    for r0 in range(0, vt.shape[0], head_dim):
        blocks += [vt[r0:r0 + head_dim, :], tail]
    vta_ref[...] = jnp.concatenate(blocks, axis=0).astype(BF16)
    qc_ref[...] = _dot(h, wc_ref[...])
    lf = _log_sigmoid(_dot(h, wf_ref[...]) + bf_ref[...])
    lf_ref[...] = lf
    r = lax.broadcasted_iota(jnp.int32, (tm, tm), 0)
    c = lax.broadcasted_iota(jnp.int32, (tm, tm), 1)
    low = jnp.where(c <= r, 1.0, 0.0).astype(BF16)
    cum = carry_ref[0:1, :] + sum(_dot(low, piece) for piece in _split3(lf))
    carry_ref[...] = jnp.broadcast_to(cum[tm - 1:, :], carry_ref.shape)
    bias = sum(_dot(piece, place_ref[j]) for j, piece in enumerate(_split3(-cum)))
    ka_ref[...] = (_dot(h, wka_ref[...]) + bias).astype(BF16)
    stats_ref[2, pl.ds(tile, 1), :] = cum[0:1, :]
    stats_ref[3, pl.ds(tile, 1), :] = cum[tm - 1:, :]


def _fox_proj_prompt(x, g, w_in, b_f, *, heads, head_dim, cross_width, tm, tiles_per_seq):
    t, d = x.shape
    w = heads * head_dim
    wa = heads * LANES
    wq, wk, wv, wf, wc, bf = _fox_split(w_in, b_f, heads, head_dim)
    aug = lambda a: jnp.pad(a.reshape(d, heads, head_dim), ((0, 0), (0, 0), (0, LANES - head_dim))).reshape(d, wa)
    head = np.arange(heads)
    place = np.zeros((N_BIAS, LANES, wa), np.float32)
    oneq = np.zeros((1, wa), np.float32)
    for j in range(N_BIAS):
        place[j, head, head * LANES + head_dim + j] = 1.0
        oneq[0, head * LANES + head_dim + j] = 1.0
    place = jnp.asarray(place)
    headsum = jnp.asarray(np.arange(wa)[:, None] // LANES == np.arange(LANES)[None, :], BF16)
    row = lambda n: pl.BlockSpec((tm, n), lambda i: (i, 0))
    col = lambda n: pl.BlockSpec((None, n, tm), lambda i: (i // tiles_per_seq, 0, i % tiles_per_seq))
    seq_len = tm * tiles_per_seq
    n_tiles = t // tm
    outs = [(cross_width, F32), (LANES, F32), (wa, BF16), (wa, BF16)]
    stats_shape = (4, n_tiles, LANES)
    return pl.pallas_call(
        functools.partial(_fox_proj_prompt_body, scale=head_dim ** -0.5, tiles_per_seq=tiles_per_seq,
                          head_dim=head_dim),
        grid=(n_tiles,),
        in_specs=[row(d), _resident((1, d)), _resident((d, wa)), _resident((w, d)), _resident((d, wa)),
                  _resident((w, d)), _resident((d, cross_width)),
                  _resident((d, LANES)), _resident((1, LANES)), _resident(place.shape),
                  _resident((1, wa)), _resident(headsum.shape)],
        out_specs=[col(w), col(w)] + [row(n) for n, _ in outs]
        + [pl.BlockSpec((wa, tm), lambda i: (0, i)), pl.BlockSpec(stats_shape, lambda i: (0, 0, 0))],
        out_shape=[jax.ShapeDtypeStruct((t // seq_len, w, seq_len), F32)] * 2
        + [jax.ShapeDtypeStruct((t, n), dt) for n, dt in outs]
        + [jax.ShapeDtypeStruct((wa, t), BF16), jax.ShapeDtypeStruct(stats_shape, F32)],
        scratch_shapes=[pltpu.VMEM((SUBLANES, LANES), F32)],
        compiler_params=_params("arbitrary"),
        name="fox_proj_prompt",
    )(x, g.reshape(1, d), aug(wq).astype(BF16), wk.T.astype(BF16), aug(wk).astype(BF16),
      wv.T.astype(BF16), wc.astype(BF16), wf.astype(BF16), bf, place.astype(BF16), oneq, headsum)


def _s5_discretise(a_re, a_im, log_dt):
    dt = jnp.exp(log_dt)
    mag = jnp.exp(dt * a_re)
    ab_re = mag * jnp.cos(dt * a_im)
    ab_im = mag * jnp.sin(dt * a_im)
    den = a_re * a_re + a_im * a_im
    nr = ab_re - 1.0
    ni = ab_im
    return ab_re, ab_im, (nr * a_re + ni * a_im) / den, (ni * a_re - nr * a_im) / den


def _s5_param_body(are_ref, aim_ref, ldt_ref, arex_ref, aimx_ref, ldtx_ref, bre_ref, bim_ref,
                   bbr_ref, bbi_ref, pwr_ref, pwi_ref):
    _, _, zr, zi = _s5_discretise(arex_ref[...], aimx_ref[...], ldtx_ref[...])
    b_re = bre_ref[...]
    b_im = bim_ref[...]
    bbr_ref[...] = zr * b_re - zi * b_im
    bbi_ref[...] = zr * b_im + zi * b_re
    ab_re, ab_im, _, _ = _s5_discretise(are_ref[...], aim_ref[...], ldt_ref[...])
    pr, pi = ab_re, ab_im
    for r in range(pwr_ref.shape[0]):
        pwr_ref[r] = pr
        pwi_ref[r] = pi
        pr, pi = pr * ab_re - pi * ab_im, pr * ab_im + pi * ab_re


def _s5_params(a_re, a_im, log_dt, b_re, b_im, c_re, c_im, *, n_pow):
    g, p, c = b_re.shape
    rows = g * c
    rep = lambda a: jnp.repeat(a, c, axis=0)
    bt = lambda b: b.transpose(0, 2, 1).reshape(rows, p)
    whole = lambda shape: pl.BlockSpec(shape, lambda: (0,) * len(shape))
    log_dt = log_dt.reshape(g, 1)
    bbr, bbi, pwr, pwi = pl.pallas_call(
        _s5_param_body,
        in_specs=[whole((g, p)), whole((g, p)), whole((g, 1)), whole((rows, p)), whole((rows, p)),
                  whole((rows, 1)), whole((rows, p)), whole((rows, p))],
        out_specs=[whole((rows, p)), whole((rows, p)), whole((n_pow, g, p)), whole((n_pow, g, p))],
        out_shape=[jax.ShapeDtypeStruct((rows, p), F32)] * 2 + [jax.ShapeDtypeStruct((n_pow, g, p), F32)] * 2,
        name="s5_params",
    )(a_re, a_im, log_dt, rep(a_re), rep(a_im), rep(log_dt), bt(b_re), bt(b_im))
    width = g * p
    pw_re, pw_im = pwr.reshape(n_pow, width), pwi.reshape(n_pow, width)
    gpt = MXU_DIM // c
    n_tiles = g // gpt
    eye = jnp.eye(gpt, dtype=F32)

    def in_tiles(bb):
        blocks = bb.reshape(n_tiles, gpt, c, p)
        return jnp.einsum("tgcp,gh->tgchp", blocks, eye).reshape(n_tiles, gpt * c, gpt * p)

    def out_tiles(cc):
        blocks = cc.reshape(n_tiles, gpt, c, p)
        return jnp.einsum("tgcp,gh->tgphc", blocks, eye).reshape(n_tiles, gpt * p, gpt * c)

    w_in = jnp.concatenate([in_tiles(bbr), in_tiles(bbi)], axis=-1).astype(BF16)
    return (pw_re[0:1], pw_im[0:1], pw_re, pw_im, w_in,
            out_tiles(c_re).astype(BF16), out_tiles(c_im).astype(BF16))


def _s5_in_proj(u, wb_ref, bre_ref, bim_ref):
    ub = u.astype(BF16)
    n_tiles, ch, two_w = wb_ref.shape
    w = two_w // 2
    for t in range(n_tiles):
        bu = _dot(ub[:, t * ch:(t + 1) * ch], wb_ref[t])
        bre_ref[:, t * w:(t + 1) * w] = bu[:, :w]
        bim_ref[:, t * w:(t + 1) * w] = bu[:, w:]


def _s5_out(u, hre_ref, him_ref, wcr_ref, wci_ref, d_ref, wglu_ref):
    n_tiles, w, _ = wcr_ref.shape
    ys = []
    for t in range(n_tiles):
        hr = hre_ref[:, t * w:(t + 1) * w].astype(BF16)
        hi = him_ref[:, t * w:(t + 1) * w].astype(BF16)
        ys.append(_dot(hr, wcr_ref[t]) - _dot(hi, wci_ref[t]))
    y = jax.nn.gelu(jnp.concatenate(ys, axis=-1) + d_ref[...] * u)
    z = _dot(y.astype(BF16), wglu_ref[...])
    half = z.shape[-1] // 2
    return z[:, :half] * jax.nn.sigmoid(z[:, half:])


SEGMENT_PITCH_PAD = SUBLANES


def _s5_prompt_body(u_ref, wb_ref, wcr_ref, wci_ref, d_ref, wglu_ref, ab8_ref, aseg_ref,
                    o_ref, hre_ref, him_ref, up_ref, slab_ref, bre_ref, bim_ref, cin_ref, carry_ref,
                    *, lane_group):
    ci = pl.program_id(1)
    rows, width = bre_ref.shape
    ch = up_ref.shape[1]
    seg = rows // SUBLANES
    pitch = slab_ref.shape[1] // SUBLANES
    n_slab = ch // LANES
    lanes = [slice(j * LANES, (j + 1) * LANES) for j in range(n_slab)]
    groups = [slice(g * lane_group, (g + 1) * lane_group) for g in range(width // lane_group)]
    tile = lambda k: pl.ds(pl.multiple_of(k * SUBLANES, SUBLANES), SUBLANES)

    @pl.when(ci == 0)
    def _():
        carry_ref[...] = jnp.zeros_like(carry_ref)

    for s in range(SUBLANES):
        for j, ls in enumerate(lanes):
            slab_ref[j, pitch * s:pitch * s + seg, :] = u_ref[0, seg * s:seg * (s + 1), ls]

    def gather(k, carry):
        for j, ls in enumerate(lanes):
            up_ref[tile(k), ls] = slab_ref[j, pl.ds(k, SUBLANES, stride=pitch), :]
        return carry

    lax.fori_loop(0, seg, gather, 0)
    u = up_ref[...]
    _s5_in_proj(u, wb_ref, bre_ref, bim_ref)

    for ls in groups:
        ar = ab8_ref[0, :, ls]
        ai = ab8_ref[1, :, ls]

        def step(k, carry, ls=ls, ar=ar, ai=ai):
            hr, hi = carry
            hr, hi = ar * hr - ai * hi + bre_ref[tile(k), ls], ar * hi + ai * hr + bim_ref[tile(k), ls]
            bre_ref[tile(k), ls] = hr
            bim_ref[tile(k), ls] = hi
            return hr, hi

        zero = jnp.zeros((SUBLANES, lane_group), F32)
        lax.fori_loop(0, seg, step, (zero, zero))

    er, ei = carry_ref[0, 0:1, :], carry_ref[1, 0:1, :]
    sr, si = aseg_ref[0], aseg_ref[1]
    for s in range(SUBLANES):
        cin_ref[0, s:s + 1, :] = er
        cin_ref[1, s:s + 1, :] = ei
        end = rows - SUBLANES + s
        er, ei = (bre_ref[end:end + 1, :] + sr * er - si * ei, bim_ref[end:end + 1, :] + sr * ei + si * er)
    carry_ref[0, 0:1, :] = er
    carry_ref[1, 0:1, :] = ei

    for ls in groups:
        ar = ab8_ref[0, :, ls]
        ai = ab8_ref[1, :, ls]

        def fix(k, carry, ls=ls, ar=ar, ai=ai):
            gr, gi = carry
            gr, gi = ar * gr - ai * gi, ar * gi + ai * gr
            bre_ref[tile(k), ls] = bre_ref[tile(k), ls] + gr
            bim_ref[tile(k), ls] = bim_ref[tile(k), ls] + gi
            return gr, gi

        lax.fori_loop(0, seg, fix, (cin_ref[0, :, ls], cin_ref[1, :, ls]))

    up_ref[...] = _s5_out(u, bre_ref, bim_ref, wcr_ref, wci_ref, d_ref, wglu_ref)

    def scatter(k, carry):
        for j, ls in enumerate(lanes):
            slab_ref[j, pl.ds(k, SUBLANES, stride=pitch), :] = up_ref[tile(k), ls]
        return carry

    lax.fori_loop(0, seg, scatter, 0)
    for s in range(SUBLANES):
        for j, ls in enumerate(lanes):
            o_ref[0, seg * s:seg * (s + 1), ls] = slab_ref[j, pitch * s:pitch * s + seg, :]

    @pl.when(ci == pl.num_programs(1) - 1)
    def _():
        hre_ref[0] = er
        him_ref[0] = ei


def _s5_prompt(u, params, d_skip, w_glu, *, rows):
    ab_re, ab_im, pw_re, pw_im, w_in, wc_re, wc_im = params
    nb, length, ch = u.shape
    width = ab_re.shape[1]
    seg = rows // SUBLANES
    ab8 = jnp.stack([jnp.broadcast_to(ab_re, (SUBLANES, width)), jnp.broadcast_to(ab_im, (SUBLANES, width))])
    a_seg = jnp.stack([pw_re[seg - 1:seg], pw_im[seg - 1:seg]])
    state = jax.ShapeDtypeStruct((nb, 1, width), F32)
    state_spec = pl.BlockSpec((1, 1, width), lambda b, c: (b, 0, 0))
    tok_spec = pl.BlockSpec((1, rows, ch), lambda b, c: (b, c, 0))
    slab_rows = SUBLANES * (seg + SEGMENT_PITCH_PAD)
    return pl.pallas_call(
        functools.partial(_s5_prompt_body, lane_group=4 * LANES),
        grid=(nb, length // rows),
        in_specs=[tok_spec, _resident(w_in.shape), _resident(wc_re.shape), _resident(wc_im.shape),
                  _resident((1, ch)), _resident(w_glu.shape), _resident(ab8.shape), _resident(a_seg.shape)],
        out_specs=[tok_spec, state_spec, state_spec],
        out_shape=[jax.ShapeDtypeStruct((nb, length, ch), F32), state, state],
        scratch_shapes=[pltpu.VMEM((rows, ch), F32), pltpu.VMEM((ch // LANES, slab_rows, LANES), F32),
                        pltpu.VMEM((rows, width), F32), pltpu.VMEM((rows, width), F32),
                        pltpu.VMEM((2, SUBLANES, width), F32), pltpu.VMEM((2, SUBLANES, width), F32)],
        compiler_params=_params("parallel", "arbitrary"),
        name="s5_prompt",
    )(u, w_in, wc_re, wc_im, d_skip.reshape(1, ch), w_glu.astype(BF16), ab8, a_seg)


def _s5_sample_body(u_ref, h0r_ref, h0i_ref, wb_ref, wcr_ref, wci_ref, d_ref, wglu_ref, ab_ref,
                    o_ref, hre_ref, him_ref, bre_ref, bim_ref):
    hre_ref[...] = h0r_ref[...]
    him_ref[...] = h0i_ref[...]
    ar = ab_ref[0:1, :]
    ai = ab_ref[1:2, :]
    for t in range(u_ref.shape[0]):
        u = u_ref[t]
        _s5_in_proj(u, wb_ref, bre_ref, bim_ref)
        hr = hre_ref[...]
        hi = him_ref[...]
        hre_ref[...] = ar * hr - ai * hi + bre_ref[...]
        him_ref[...] = ar * hi + ai * hr + bim_ref[...]
        o_ref[t] = _s5_out(u, hre_ref, him_ref, wcr_ref, wci_ref, d_ref, wglu_ref)


def _s5_sample(u, h0_re, h0_im, params, d_skip, w_glu):
    ab_re, ab_im, _, _, w_in, wc_re, wc_im = params
    nt, nb, ch = u.shape
    width = ab_re.shape[1]
    ab = jnp.concatenate([ab_re, ab_im], axis=0)
    whole = lambda shape: pl.BlockSpec(shape, lambda: (0,) * len(shape))
    state = jax.ShapeDtypeStruct((nb, width), F32)
    return pl.pallas_call(
        _s5_sample_body,
        in_specs=[whole(u.shape), whole((nb, width)), whole((nb, width)), whole(w_in.shape),
                  whole(wc_re.shape), whole(wc_im.shape), whole((1, ch)), whole(w_glu.shape),
                  whole(ab.shape)],
        out_specs=[whole(u.shape), whole((nb, width)), whole((nb, width))],
        out_shape=[jax.ShapeDtypeStruct(u.shape, F32), state, state],
        scratch_shapes=[pltpu.VMEM((nb, width), F32), pltpu.VMEM((nb, width), F32)],
        compiler_params=pltpu.CompilerParams(vmem_limit_bytes=VMEM_LIMIT_BYTES),
        name="s5_sample",
    )(u, h0_re, h0_im, w_in, wc_re, wc_im, d_skip.reshape(1, ch), w_glu.astype(BF16), ab)


HEADS_PER_STEP = 2


UNDERFLOW_MARGIN = 110.0


def _first_live_block(st_ref, base, n_seq_blk, first_chunk, n_chunk, pair):
    kn = st_ref[1, pl.ds(base, n_seq_blk), :]
    c_last = st_ref[3, pl.ds(base, n_seq_blk), :]
    rows = [kn[0:1]]
    for j in range(1, n_seq_blk):
        rows.append(jnp.maximum(rows[-1], kn[j:j + 1]))
    kn_run = jnp.concatenate(rows, axis=0)
    worst = None
    for c in range(n_chunk):
        r = base + first_chunk + c
        term = st_ref[0, pl.ds(r, 1), :] * (kn_run + st_ref[1, pl.ds(r, 1), :]) + st_ref[2, pl.ds(r, 1), :]
        worst = term if worst is None else jnp.maximum(worst, term)
    blk = lax.broadcasted_iota(jnp.int32, worst.shape, 0)
    lane = lax.broadcasted_iota(jnp.int32, (1, LANES), 1)
    dead = (worst - c_last < -UNDERFLOW_MARGIN) & (blk < first_chunk)
    count = jnp.sum(jnp.where(dead, 1, 0), axis=0, keepdims=True)
    return jnp.min(jnp.where(lane // HEADS_PER_STEP == pair, count, n_seq_blk))


def _fox_prompt_body(q_ref, k_ref, vt_ref, st_ref, o_ref, m_ref, acc_ref, *, head_dim, tk):
    qi = pl.program_id(2)
    tq = q_ref.shape[1]
    n_chunk = tq // tk
    n_seq_blk = k_ref.shape[1] // tk
    first_live = _first_live_block(st_ref, pl.program_id(0) * n_seq_blk, n_seq_blk, qi * n_chunk, n_chunk,
                                   pl.program_id(1))
    m_ref[...] = jnp.full_like(m_ref, NEG_INF)
    acc_ref[...] = jnp.zeros_like(acc_ref)

    def block(ks, first_chunk, masked_chunk):
        chains = [(h, slice(h * LANES, (h + 1) * LANES), c, slice(c * tk, (c + 1) * tk))
                  for h in range(HEADS_PER_STEP) for c in range(first_chunk, n_chunk)]
        scores = []
        for _, hl, c, qs in chains:
            s = _dot_nt(k_ref[0, pl.ds(ks, tk), hl], q_ref[0, qs, hl])
            if c == masked_chunk:
                key = lax.broadcasted_iota(jnp.int32, s.shape, 0)
                qry = lax.broadcasted_iota(jnp.int32, s.shape, 1)
                s = jnp.where(key <= qry, s, NEG_INF)
            scores.append(s)
        probs, alphas = [], []
        for s, (h, _, _, qs) in zip(scores, chains):
            m_old = m_ref[h, :, qs]
            m_new = jnp.maximum(m_old, jnp.max(s, axis=0, keepdims=True))
            alphas.append(jnp.exp(m_old - m_new))
            probs.append(jnp.exp(s - m_new).astype(BF16))
            m_ref[h, :, qs] = m_new
        for p, alpha, (h, hl, _, qs) in zip(probs, alphas, chains):
            acc_ref[h, :, qs] = alpha * acc_ref[h, :, qs] + _dot(vt_ref[hl, pl.ds(ks, tk)], p)

    def body(kj, carry):
        block(pl.multiple_of(kj * tk, tk), 0, None)
        return carry

    lax.fori_loop(first_live, qi * n_chunk, body, 0)
    for c in range(n_chunk):
        block(pl.multiple_of((qi * n_chunk + c) * tk, tk), c, c)
    outs = [acc_ref[h, :head_dim, :] / acc_ref[h, head_dim:head_dim + 1, :] for h in range(HEADS_PER_STEP)]
    o_ref[0] = jnp.concatenate(outs, axis=0).T


def _fox_prompt(qa, ka, vta, stats, *, heads, head_dim, tq, tk):
    nb, length, _ = qa.shape
    wide = HEADS_PER_STEP * LANES
    return pl.pallas_call(
        functools.partial(_fox_prompt_body, head_dim=head_dim, tk=tk),
        grid=(nb, heads // HEADS_PER_STEP, length // tq),
        in_specs=[pl.BlockSpec((1, tq, wide), lambda b, hp, i: (b, i, hp)),
                  pl.BlockSpec((1, length, wide), lambda b, hp, i: (b, 0, hp)),
                  pl.BlockSpec((wide, length), lambda b, hp, i: (hp, b)),
                  pl.BlockSpec(stats.shape, lambda b, hp, i: (0, 0, 0))],
        out_specs=pl.BlockSpec((1, tq, HEADS_PER_STEP * head_dim), lambda b, hp, i: (b, i, hp)),
        out_shape=jax.ShapeDtypeStruct((nb, length, heads * head_dim), F32),
        scratch_shapes=[pltpu.VMEM((HEADS_PER_STEP, 1, tq), F32),
                        pltpu.VMEM((HEADS_PER_STEP, LANES, tq), F32)],
        compiler_params=_params("parallel", "parallel", "arbitrary"),
        name="fox_prompt",
    )(qa, ka, vta, stats)


def _fox_sample_body(pt_ref, q_ref, *refs, pages_per_step, head_dim):
    n = pages_per_step
    kt_refs, vt_refs, lf_refs = refs[:n], refs[n:2 * n], refs[2 * n:3 * n]
    kn_ref, vn_ref, lfn_ref, o_ref, qrow_ref, m_ref, l_ref, acc_ref, carry_ref = refs[3 * n:]
    del pt_ref
    g = pl.program_id(1)
    n_tok, w = q_ref.shape
    hp, page = lf_refs[0].shape
    head_of_lane = lax.broadcasted_iota(jnp.int32, (hp, w), 1) // head_dim
    head_mask = head_of_lane == lax.broadcasted_iota(jnp.int32, (hp, w), 0)
    tri = _upper_tri()

    @pl.when(g == 0)
    def _():
        zero = jnp.zeros((hp, w), F32)
        rows = [jnp.where(head_mask, jnp.broadcast_to(q_ref[t:t + 1, :], (hp, w)), zero)
                for t in range(n_tok)]
        qrow_ref[...] = jnp.concatenate(rows, axis=0).astype(BF16)
        m_ref[...] = jnp.full_like(m_ref, NEG_INF)
        l_ref[...] = jnp.zeros_like(l_ref)
        acc_ref[...] = jnp.zeros_like(acc_ref)
        carry_ref[...] = jnp.zeros_like(carry_ref)

    def update(pages, valid=None):
        scores, base = [], carry_ref[:, 0:1]
        for s, lf, _ in pages:
            local = _cumsum_lanes(lf, tri)
            c = local + base
            base = base + local[:, page - 1:]
            s = s - jnp.concatenate([c] * n_tok, axis=0)
            scores.append(s if valid is None else jnp.where(valid, s, NEG_INF))
        carry_ref[...] = jnp.broadcast_to(base, carry_ref.shape)
        m_old = m_ref[...]
        m_new = functools.reduce(jnp.maximum, [jnp.max(s, axis=1, keepdims=True) for s in scores], m_old)
        alpha = jnp.exp(m_old - m_new)
        probs = [jnp.exp(s - m_new) for s in scores]
        l_ref[...] = alpha * l_ref[...] + sum(jnp.sum(p, axis=1, keepdims=True) for p in probs)
        acc_ref[...] = alpha * acc_ref[...] + sum(pv(p.astype(BF16)) for p, (_, _, pv) in zip(probs, pages))
        m_ref[...] = m_new

    def cached(i):
        kt = kt_refs[i][...].reshape(w, page).astype(BF16)
        vt = vt_refs[i][...].reshape(w, page).astype(BF16)
        return _dot(qrow_ref[...], kt), lf_refs[i][...], lambda p: _dot_nt(p, vt)

    update([cached(i) for i in range(n)])

    @pl.when(g == pl.num_programs(1) - 1)
    def _():
        pad = jnp.zeros((page - kn_ref.shape[0], w), F32)
        kn = jnp.concatenate([kn_ref[...], pad], axis=0).astype(BF16)
        vn = jnp.concatenate([vn_ref[...], pad], axis=0).astype(BF16)
        key = lax.broadcasted_iota(jnp.int32, (n_tok * hp, page), 1)
        tok = lax.broadcasted_iota(jnp.int32, (n_tok * hp, page), 0) // hp
        update([(_dot_nt(qrow_ref[...], kn), lfn_ref[...], lambda p: _dot(p, vn))], key <= tok)
        out = acc_ref[...] / l_ref[...]
        for t in range(n_tok):
            picked = jnp.where(head_mask, out[t * hp:(t + 1) * hp, :], 0.0)
            o_ref[t:t + 1, :] = jnp.sum(picked, axis=0, keepdims=True)


def _fox_sample(q, k_new, v_new, lft_new, cache_kt, cache_vt, cache_lft, page_table, *, layer, pages_per_step):
    ns, n_tok, w = q.shape
    n_pages = page_table.shape[1]
    _, _, heads, head_dim, page = cache_kt.shape
    hp = cache_lft.shape[1]
    n = pages_per_step
    pad_rows = lambda a: jnp.pad(a, ((0, 0), (0, SUBLANES - n_tok), (0, 0)))

    def kv_page(i):
        return pl.BlockSpec((None, None, heads, head_dim, page),
                            lambda s, g, pt: (layer, pt[s, g * n + i], 0, 0, 0))

    def lf_page(i):
        return pl.BlockSpec((None, hp, page), lambda s, g, pt: (pt[s, g * n + i], 0, 0))

    per_seq = lambda shape: pl.BlockSpec((None,) + shape, lambda s, g, pt: (s, 0, 0))
    in_specs = [per_seq((n_tok, w))] + [kv_page(i) for i in range(n)] * 2 + [lf_page(i) for i in range(n)]
    in_specs += [per_seq((SUBLANES, w)), per_seq((SUBLANES, w)), per_seq((hp, page))]
    rows = n_tok * hp
    return pl.pallas_call(
        functools.partial(_fox_sample_body, pages_per_step=n, head_dim=head_dim),
        grid_spec=pltpu.PrefetchScalarGridSpec(
            num_scalar_prefetch=1,
            grid=(ns, n_pages // n),
            in_specs=in_specs,
            out_specs=per_seq((n_tok, w)),
            scratch_shapes=[pltpu.VMEM((rows, w), BF16), pltpu.VMEM((rows, 1), F32),
                            pltpu.VMEM((rows, 1), F32), pltpu.VMEM((rows, w), F32),
                            pltpu.VMEM((hp, page), F32)],
        ),
        out_shape=jax.ShapeDtypeStruct((ns, n_tok, w), F32),
        compiler_params=_params("parallel", "arbitrary"),
        name="fox_sample",
    )(page_table, q, *([cache_kt] * n), *([cache_vt] * n), *([cache_lft] * n),
      pad_rows(k_new), pad_rows(v_new), lft_new)


def _softmax_pv(s, vb):
    m = jnp.max(s, axis=-1, keepdims=True)
    p = jnp.exp(s - m)
    return _dot(p.astype(BF16), vb) / jnp.sum(p, axis=-1, keepdims=True)


def _cross_ffn_prompt_body(*refs, heads, head_dim, ff_chunk, final):
    if final:
        (x_ref, mix_ref, qc_ref, mk_ref, mv_ref, wom_ref, woc_ref, g_ref, wg_ref, wu_ref, wd_ref, gf_ref,
         o_ref, t_ref) = refs
    else:
        (x_ref, mix_ref, qc_ref, mk_ref, mv_ref, wom_ref, woc_ref, g_ref, wg_ref, wu_ref, wd_ref,
         o_ref, t_ref) = refs
    tm, cw = qc_ref.shape
    q = qc_ref[...] * head_dim ** -0.5
    head_of_lane = lax.broadcasted_iota(jnp.int32, (1, cw), 1) // head_dim
    zero = jnp.zeros_like(q)
    q4 = jnp.concatenate([jnp.where(head_of_lane == h, q, zero) for h in range(heads)], axis=0)
    s = _dot_nt(q4.astype(BF16), mk_ref[0].astype(BF16))
    o4 = _softmax_pv(s, mv_ref[0].astype(BF16))
    cross = zero
    for h in range(heads):
        cross = cross + jnp.where(head_of_lane == h, o4[h * tm:(h + 1) * tm, :], zero)
    x = (x_ref[...] + _dot(mix_ref[...].astype(BF16), wom_ref[...])
         + _dot(cross.astype(BF16), woc_ref[...]))
    y = _half_ffn(x, g_ref, wg_ref, wu_ref, wd_ref, t_ref, ff_chunk)
    if final:
        y = _rms(y, gf_ref[...])
    o_ref[...] = y


def _cross_ffn_prompt(x, mix, qc, mk, mv, wom, woc, g, wg, wu, wd, idx, g_final=None, *, heads, head_dim, tm):
    t, d = x.shape
    nb, n_mem, cw = mk.shape
    d_ff = wg.shape[2]
    per_b = t // nb // tm
    final = g_final is not None
    row = lambda n: pl.BlockSpec((tm, n), lambda i: (i, 0))
    mem = pl.BlockSpec((1, n_mem, cw), lambda i: (i // per_b, 0, 0))
    pick = lambda r, c: pl.BlockSpec((None, r, c), lambda i: (idx, 0, 0), pipeline_mode=pl.Buffered(1))
    in_specs = [row(d), row(mix.shape[1]), row(cw), mem, mem, _resident(wom.shape), _resident(woc.shape),
                _resident((1, d)), pick(d, d_ff), pick(d, d_ff), pick(d_ff, d)]
    args = [x, mix, qc, mk, mv, wom, woc, g.reshape(1, d), wg, wu, wd]
    if final:
        in_specs.append(_resident((1, d)))
        args.append(g_final.reshape(1, d))
    return pl.pallas_call(
        functools.partial(_cross_ffn_prompt_body, heads=heads, head_dim=head_dim, ff_chunk=MXU_DIM, final=final),
        grid=(t // tm,),
        in_specs=in_specs,
        out_specs=row(d),
        out_shape=jax.ShapeDtypeStruct((t, d), F32),
        scratch_shapes=[pltpu.VMEM((tm, d_ff), BF16)],
        compiler_params=_params("parallel"),
        name="cross_ffn_prompt",
    )(*args)


def _cross_out_sample_body(x_ref, mix_ref, qc_ref, mkt_ref, mvt_ref, wom_ref, woc_ref, o_ref, cross_ref,
                           *, n_tok):
    n_seq, _, head_dim, n_mem = mkt_ref.shape
    cw = qc_ref.shape[1]
    scale = head_dim ** -0.5
    head_of_lane = lax.broadcasted_iota(jnp.int32, (SUBLANES, cw), 1) // head_dim
    head_mask = head_of_lane == lax.broadcasted_iota(jnp.int32, (SUBLANES, cw), 0)
    zero = jnp.zeros((SUBLANES, cw), F32)
    for i in range(n_seq):
        rows = [jnp.where(head_mask, jnp.broadcast_to(qc_ref[pl.ds(i * n_tok + t, 1), :] * scale,
                                                     (SUBLANES, cw)), zero) for t in range(n_tok)]
        q = jnp.concatenate(rows, axis=0).astype(BF16)
        s = _dot(q, mkt_ref[i].reshape(cw, n_mem).astype(BF16))
        p = jnp.exp(s - jnp.max(s, axis=-1, keepdims=True))
        o = _dot_nt(p.astype(BF16), mvt_ref[i].reshape(cw, n_mem).astype(BF16)) / jnp.sum(p, axis=-1, keepdims=True)
        for t in range(n_tok):
            picked = jnp.where(head_mask, o[t * SUBLANES:(t + 1) * SUBLANES, :], zero)
            cross_ref[pl.ds(i * n_tok + t, 1), :] = jnp.sum(picked, axis=0, keepdims=True)
    o_ref[...] = (x_ref[...] + _dot(mix_ref[...].astype(BF16), wom_ref[...])
                  + _dot(cross_ref[...].astype(BF16), woc_ref[...]))


def _cross_out_sample(x, mix, qc, mkt, mvt, wom, woc, *, layer, n_tok, seqs_per_step):
    t, d = x.shape
    _, ns, heads, head_dim, n_mem = mkt.shape
    cw = heads * head_dim
    tm = seqs_per_step * n_tok
    row = lambda n: pl.BlockSpec((tm, n), lambda i: (i, 0))
    mem = pl.BlockSpec((None, seqs_per_step, heads, head_dim, n_mem), lambda i: (layer, i, 0, 0, 0))
    return pl.pallas_call(
        functools.partial(_cross_out_sample_body, n_tok=n_tok),
        grid=(ns // seqs_per_step,),
        in_specs=[row(d), row(mix.shape[1]), row(cw), mem, mem, _resident(wom.shape),
                  _resident(woc.shape)],
        out_specs=row(d),
        out_shape=jax.ShapeDtypeStruct((t, d), F32),
        scratch_shapes=[pltpu.VMEM((tm, cw), F32)],
        compiler_params=_params("parallel"),
        name="cross_out_sample",
    )(x, mix, qc, mkt, mvt, wom, woc)


def kernel(x_prompt, x_sample, mem_prompt, state_s5_re, state_s5_im, cache_fox_k, cache_fox_v, cache_fox_logf, cache_mem_k, cache_mem_v, page_table, ffn_norm, ffn_w_gate, ffn_w_up, ffn_w_down, norm_mix, norm_mem, w_mem_kv, w_in_s5, s5_a_re, s5_a_im, s5_log_dt, s5_b_re, s5_b_im, s5_c_re, s5_c_im, s5_d, s5_w_glu, w_in_fox, fox_b_f, w_out, norm_final):
    nb, seq, d = x_prompt.shape
    ns, n_tok, _ = x_sample.shape
    depth = ffn_norm.shape[0]
    n_mem = mem_prompt.shape[1]
    cross_heads, head_dim = cache_mem_k.shape[3], cache_mem_k.shape[4]
    cross_width = cross_heads * head_dim
    mixer_width = w_out.shape[1] - cross_width
    fox_heads = cache_fox_k.shape[3]
    n_phys, page = cache_fox_k.shape[1], cache_fox_k.shape[2]
    groups, n_state = state_s5_re.shape[2], state_s5_re.shape[3]
    hp = 2 * SUBLANES

    tm_p = 512
    tm_s = ns * n_tok
    xp = x_prompt.reshape(nb * seq, d)
    xs = x_sample.reshape(tm_s, d)
    mem = mem_prompt.reshape(nb * n_mem, d)

    s5_re_p, s5_im_p, s5_re_s, s5_im_s = [], [], [], []
    fk_p, fv_p, fl_p, fk_s, fv_s, fl_s = [], [], [], [], [], []
    mk_list, mv_list = [], []
    d_ff = ffn_w_gate.shape[-1]
    wg_all = _to_bf16(ffn_w_gate.reshape(2 * depth, d, d_ff), rows=tm_p)
    wu_all = _to_bf16(ffn_w_up.reshape(2 * depth, d, d_ff), rows=tm_p)
    wd_all = _to_bf16(ffn_w_down.reshape(2 * depth, d_ff, d), rows=d_ff // 2)
    for i in range(depth):
        j = i // N_MIXERS
        ffn_w = lambda half: (ffn_norm[i, half], wg_all, wu_all, wd_all, 2 * i + half)
        xp = _ffn(xp, *ffn_w(0), tm=tm_p)
        xs = _ffn(xs, *ffn_w(0), tm=tm_s)

        mk_p, mv_p = _norm_proj(mem, norm_mem[i],
                                [w_mem_kv[i][:, :cross_width], w_mem_kv[i][:, cross_width:]], tm=nb * n_mem)
        mk_list.append(mk_p.reshape(nb, n_mem, cross_heads, head_dim))
        mv_list.append(mv_p.reshape(nb, n_mem, cross_heads, head_dim))

        if i % N_MIXERS == 0:
            w_in = [w_in_s5[j][:, :mixer_width], w_in_s5[j][:, mixer_width:]]
            u_p, qc_p = _norm_proj(xp, norm_mix[i], w_in, tm=tm_p)
            u_s, qc_s = _norm_proj(xs, norm_mix[i], w_in, tm=tm_s)
            params = _s5_params(s5_a_re[j], s5_a_im[j], s5_log_dt[j], s5_b_re[j], s5_b_im[j],
                                s5_c_re[j], s5_c_im[j], n_pow=tm_p // SUBLANES)
            mix_p, hr_p, hi_p = _s5_prompt(u_p.reshape(nb, seq, mixer_width), params, s5_d[j],
                                           s5_w_glu[j], rows=tm_p)
            mix_p = mix_p.reshape(nb * seq, mixer_width)
            u_t = u_s.reshape(ns, n_tok, mixer_width).transpose(1, 0, 2)
            mix_t, hr_s, hi_s = _s5_sample(u_t, state_s5_re[j].reshape(ns, groups * n_state),
                                           state_s5_im[j].reshape(ns, groups * n_state),
                                           params, s5_d[j], s5_w_glu[j])
            mix_s = mix_t.transpose(1, 0, 2).reshape(tm_s, mixer_width)
            s5_re_p.append(hr_p.reshape(nb, groups, n_state))
            s5_im_p.append(hi_p.reshape(nb, groups, n_state))
            s5_re_s.append(hr_s.reshape(ns, groups, n_state))
            s5_im_s.append(hi_s.reshape(ns, groups, n_state))
        else:
            fox = dict(heads=fox_heads, head_dim=head_dim, cross_width=cross_width)
            kt_p, vt_p, qc_p, lf_p, qa_p, ka_p, vta_p, stats_p = _fox_proj_prompt(
                xp, norm_mix[i], w_in_fox[j], fox_b_f[j], tm=tm_p, tiles_per_seq=seq // tm_p, **fox)
            q_s, k_s, v_s, qc_s, lf_s, lft_s = _fox_proj_sample(
                xs, norm_mix[i], w_in_fox[j], fox_b_f[j], tm=tm_s, **fox)

            seq3 = lambda a: a.reshape(nb, seq, a.shape[-1])
            mix_p = _fox_prompt(seq3(qa_p), seq3(ka_p), vta_p, stats_p, heads=fox_heads, head_dim=head_dim,
                                tq=min(seq, 4 * tm_p), tk=tm_p)
            mix_p = mix_p.reshape(nb * seq, mixer_width)

            tok3 = lambda a: a.reshape(ns, n_tok, mixer_width)
            lft_new = jnp.pad(lft_s.reshape(hp, ns, n_tok).transpose(1, 0, 2),
                              ((0, 0), (0, 0), (0, page - n_tok)))
            cache_lft = jnp.pad(cache_fox_logf[j].transpose(0, 2, 1), ((0, 0), (0, hp - fox_heads), (0, 0)))
            to_stored = lambda a: a.transpose(0, 1, 3, 4, 2)
            mix_s = _fox_sample(tok3(q_s), tok3(k_s), tok3(v_s), lft_new, to_stored(cache_fox_k),
                                to_stored(cache_fox_v), cache_lft, page_table, layer=j,
                                pages_per_step=page_table.shape[1])
            mix_s = mix_s.reshape(tm_s, mixer_width)

            from_stored = lambda a: a.reshape(nb, fox_heads, head_dim, seq).transpose(0, 3, 1, 2)
            fk_p.append(from_stored(kt_p))
            fv_p.append(from_stored(vt_p))
            fl_p.append(lf_p[:, :fox_heads].reshape(nb, seq, fox_heads))
            fk_s.append(k_s.reshape(ns, n_tok, fox_heads, head_dim))
            fv_s.append(v_s.reshape(ns, n_tok, fox_heads, head_dim))
            fl_s.append(lf_s[:, :fox_heads].reshape(ns, n_tok, fox_heads))

        wom = w_out[i][:mixer_width].astype(BF16)
        woc = w_out[i][mixer_width:].astype(BF16)
        g_final = norm_final if i == depth - 1 else None
        xp = _cross_ffn_prompt(xp, mix_p, qc_p, mk_p.reshape(nb, n_mem, cross_width),
                               mv_p.reshape(nb, n_mem, cross_width), wom, woc, *ffn_w(1), g_final,
                               heads=cross_heads, head_dim=head_dim, tm=tm_p)
        xs = _cross_out_sample(xs, mix_s, qc_s, cache_mem_k.transpose(0, 1, 3, 4, 2),
                               cache_mem_v.transpose(0, 1, 3, 4, 2), wom, woc,
                               layer=i, n_tok=n_tok, seqs_per_step=8)
        xs = _ffn(xs, *ffn_w(1), g_final, tm=tm_s)

    return (xp.reshape(nb, seq, d), xs.reshape(ns, n_tok, d),
            jnp.stack(s5_re_p), jnp.stack(s5_im_p), jnp.stack(s5_re_s), jnp.stack(s5_im_s),
            jnp.stack(fk_p), jnp.stack(fv_p), jnp.stack(fl_p),
            jnp.stack(fk_s), jnp.stack(fv_s), jnp.stack(fl_s),
            jnp.stack(mk_list), jnp.stack(mv_list))
```

```python
import functools

import jax
import jax.numpy as jnp
import numpy as np
from jax import lax
from jax.experimental import pallas as pl
from jax.experimental.pallas import tpu as pltpu

F32 = jnp.float32
BF16 = jnp.bfloat16

RMS_EPS = 1e-6
NEG_INF = -1e30
N_MIXERS = 2

LANES = 128
SUBLANES = 8
MXU_DIM = 256
VMEM_LIMIT_BYTES = 56 * 1024 * 1024

NT_DIMS = (((1,), (1,)), ((), ()))


def _params(*sem):
    return pltpu.CompilerParams(dimension_semantics=sem, vmem_limit_bytes=VMEM_LIMIT_BYTES)


def _resident(shape):
    nd = len(shape)
    return pl.BlockSpec(shape, lambda *_: (0,) * nd, pipeline_mode=pl.Buffered(1))


def _rms(x, g):
    return x * lax.rsqrt(jnp.mean(x * x, axis=-1, keepdims=True) + RMS_EPS) * g


def _dot(a, b):
    return jnp.dot(a, b, preferred_element_type=F32)


def _dot_nt(a, b):
    return lax.dot_general(a, b, NT_DIMS, preferred_element_type=F32)


def _log_sigmoid(x):
    return jnp.minimum(x, 0.0) - jnp.log1p(jnp.exp(-jnp.abs(x)))


def _split3(x):
    hi = x.astype(BF16)
    r1 = x - hi.astype(F32)
    mid = r1.astype(BF16)
    lo = (r1 - mid.astype(F32)).astype(BF16)
    return hi, mid, lo


def _cumsum_lanes(x, tri):
    hi, mid, lo = _split3(x)
    return _dot(hi, tri) + _dot(mid, tri) + _dot(lo, tri)


def _upper_tri():
    r = lax.broadcasted_iota(jnp.int32, (LANES, LANES), 0)
    c = lax.broadcasted_iota(jnp.int32, (LANES, LANES), 1)
    return jnp.where(r <= c, 1.0, 0.0).astype(BF16)


def _half_ffn(x, g_ref, wg_ref, wu_ref, wd_ref, t_ref, ff_chunk):
    h = _rms(x, g_ref[...]).astype(BF16)
    for lo in range(0, wg_ref.shape[1], ff_chunk):
        a = _dot(h, wg_ref[:, lo:lo + ff_chunk])
        b = _dot(h, wu_ref[:, lo:lo + ff_chunk])
        t_ref[:, lo:lo + ff_chunk] = (jax.nn.silu(a) * b).astype(BF16)
    return x + 0.5 * _dot(t_ref[...], wd_ref[...])


def _ffn_body(*refs, ff_chunk, final):
    if final:
        x_ref, g_ref, wg_ref, wu_ref, wd_ref, gf_ref, o_ref, t_ref = refs
    else:
        x_ref, g_ref, wg_ref, wu_ref, wd_ref, o_ref, t_ref = refs
    y = _half_ffn(x_ref[...], g_ref, wg_ref, wu_ref, wd_ref, t_ref, ff_chunk)
    if final:
        y = _rms(y, gf_ref[...])
    o_ref[...] = y


def _ffn_proj_body(x_ref, g_ref, wg_ref, wu_ref, wd_ref, gm_ref, *refs, ff_chunk, n):
    w_refs, o_ref, p_refs, t_ref = refs[:n], refs[n], refs[n + 1:2 * n + 1], refs[2 * n + 1]
    y = _half_ffn(x_ref[...], g_ref, wg_ref, wu_ref, wd_ref, t_ref, ff_chunk)
    o_ref[...] = y
    h = _rms(y, gm_ref[...]).astype(BF16)
    for w_ref, p_ref in zip(w_refs, p_refs):
        p_ref[...] = _dot(h, w_ref[...])


def _ffn_proj(x, g, wg, wu, wd, idx, g_mix, ws, *, tm):
    t, d = x.shape
    d_ff = wg.shape[2]
    n = len(ws)
    row = lambda c: pl.BlockSpec((tm, c), lambda i: (i, 0))
    pick = lambda r, c: pl.BlockSpec((None, r, c), lambda i: (idx, 0, 0), pipeline_mode=pl.Buffered(1))
    return pl.pallas_call(
        functools.partial(_ffn_proj_body, ff_chunk=MXU_DIM, n=n),
        grid=(t // tm,),
        in_specs=[row(d), _resident((1, d)), pick(d, d_ff), pick(d, d_ff), pick(d_ff, d), _resident((1, d))]
        + [_resident(w.shape) for w in ws],
        out_specs=[row(d)] + [row(w.shape[1]) for w in ws],
        out_shape=[jax.ShapeDtypeStruct((t, d), F32)] + [jax.ShapeDtypeStruct((t, w.shape[1]), F32) for w in ws],
        scratch_shapes=[pltpu.VMEM((tm, d_ff), BF16)],
        compiler_params=_params("parallel"),
        name="ffn_proj",
    )(x, g.reshape(1, d), wg, wu, wd, g_mix.reshape(1, d), *[w.astype(BF16) for w in ws])


def _cast_body(x_ref, o_ref):
    o_ref[...] = x_ref[...].astype(o_ref.dtype)


def _to_bf16(w, *, rows):
    n, r, c = w.shape
    spec = pl.BlockSpec((None, rows, c), lambda i, j: (i, j, 0))
    return pl.pallas_call(
        _cast_body, grid=(n, r // rows), in_specs=[spec], out_specs=spec,
        out_shape=jax.ShapeDtypeStruct(w.shape, BF16),
        compiler_params=_params("parallel", "parallel"), name="to_bf16",
    )(w)


def _ffn(x, g, wg, wu, wd, idx, g_final=None, *, tm):
    t, d = x.shape
    d_ff = wg.shape[2]
    final = g_final is not None
    row = pl.BlockSpec((tm, d), lambda i: (i, 0))
    pick = lambda r, c: pl.BlockSpec((None, r, c), lambda i: (idx, 0, 0), pipeline_mode=pl.Buffered(1))
    in_specs = [row, _resident((1, d)), pick(d, d_ff), pick(d, d_ff), pick(d_ff, d)]
    args = [x, g.reshape(1, d), wg, wu, wd]
    if final:
        in_specs.append(_resident((1, d)))
        args.append(g_final.reshape(1, d))
    return pl.pallas_call(
        functools.partial(_ffn_body, ff_chunk=MXU_DIM, final=final),
        grid=(t // tm,),
        in_specs=in_specs,
        out_specs=row,
        out_shape=jax.ShapeDtypeStruct((t, d), F32),
        scratch_shapes=[pltpu.VMEM((tm, d_ff), BF16)],
        compiler_params=_params("parallel"),
        name="ffn_final" if final else "ffn",
    )(*args)


def _norm_proj_body(x_ref, g_ref, *refs, n):
    h = _rms(x_ref[...], g_ref[...]).astype(BF16)
    for w_ref, o_ref in zip(refs[:n], refs[n:]):
        o_ref[...] = _dot(h, w_ref[...])


def _norm_proj(x, g, ws, *, tm):
    t, d = x.shape
    n = len(ws)
    in_specs = [pl.BlockSpec((tm, d), lambda i: (i, 0)), _resident((1, d))]
    in_specs += [_resident(w.shape) for w in ws]
    return pl.pallas_call(
        functools.partial(_norm_proj_body, n=n),
        grid=(t // tm,),
        in_specs=in_specs,
        out_specs=[pl.BlockSpec((tm, w.shape[1]), lambda i: (i, 0)) for w in ws],
        out_shape=[jax.ShapeDtypeStruct((t, w.shape[1]), F32) for w in ws],
        compiler_params=_params("parallel"),
        name="norm_proj",
    )(x, g.reshape(1, d), *[w.astype(BF16) for w in ws])


def _fox_split(w_in, b_f, heads, head_dim):
    w = heads * head_dim
    wq, wk, wv = w_in[:, :w], w_in[:, w:2 * w], w_in[:, 2 * w:3 * w]
    wf = jnp.pad(w_in[:, 3 * w:3 * w + heads], ((0, 0), (0, LANES - heads)))
    wc = w_in[:, 3 * w + heads:]
    bf = jnp.pad(b_f, (0, LANES - heads)).reshape(1, LANES)
    return wq, wk, wv, wf, wc, bf


def _fox_proj_sample_body(x_ref, g_ref, wq_ref, wk_ref, wv_ref, wc_ref, wf_ref, wft_ref, bf_ref, bft_ref,
                          q_ref, k_ref, v_ref, qc_ref, lf_ref, lft_ref, *, scale):
    h = _rms(x_ref[...], g_ref[...]).astype(BF16)
    q_ref[...] = _dot(h, wq_ref[...]) * scale
    k_ref[...] = _dot(h, wk_ref[...])
    v_ref[...] = _dot(h, wv_ref[...])
    qc_ref[...] = _dot(h, wc_ref[...])
    lf_ref[...] = _log_sigmoid(_dot(h, wf_ref[...]) + bf_ref[...])
    lft_ref[...] = _log_sigmoid(_dot_nt(wft_ref[...], h) + bft_ref[...])


def _fox_proj_sample(x, g, w_in, b_f, *, heads, head_dim, cross_width, tm):
    t, d = x.shape
    w = heads * head_dim
    hp = 2 * SUBLANES
    wq, wk, wv, wf, wc, bf = _fox_split(w_in, b_f, heads, head_dim)
    wft = wf[:, :hp].T
    bft = bf[0, :hp].reshape(hp, 1)
    row = lambda n: pl.BlockSpec((tm, n), lambda i: (i, 0))
    outs = [w, w, w, cross_width, LANES]
    return pl.pallas_call(
        functools.partial(_fox_proj_sample_body, scale=head_dim ** -0.5),
        grid=(t // tm,),
        in_specs=[row(d), _resident((1, d)), _resident((d, w)), _resident((d, w)),
                  _resident((d, w)), _resident((d, cross_width)), _resident((d, LANES)),
                  _resident((hp, d)), _resident((1, LANES)), _resident((hp, 1))],
        out_specs=[row(n) for n in outs] + [pl.BlockSpec((hp, tm), lambda i: (0, i))],
        out_shape=[jax.ShapeDtypeStruct((t, n), F32) for n in outs]
        + [jax.ShapeDtypeStruct((hp, t), F32)],
        compiler_params=_params("parallel"),
        name="fox_proj_sample",
    )(x, g.reshape(1, d), wq.astype(BF16), wk.astype(BF16), wv.astype(BF16), wc.astype(BF16),
      wf.astype(BF16), wft.astype(BF16), bf, bft)


N_BIAS = 3


NORM_SLACK = 1.02


def _fox_proj_prompt_body(x_ref, g_ref, wqa_ref, wkt_ref, wka_ref, wvt_ref, wc_ref, wf_ref,
                          bf_ref, place_ref, oneq_ref, headsum_ref,
                          kt_ref, vt_ref, qc_ref, lf_ref, qa_ref, ka_ref, vta_ref, stats_ref, carry_ref,
                          *, scale, tiles_per_seq, head_dim):
    tile = pl.program_id(0)

    @pl.when(tile % tiles_per_seq == 0)
    def _():
        carry_ref[...] = jnp.zeros_like(carry_ref)

    def max_norm(x):
        sq = _dot((x * x).astype(BF16), headsum_ref[...])
        return jnp.sqrt(jnp.max(sq, axis=0, keepdims=True)) * NORM_SLACK

    h = _rms(x_ref[...], g_ref[...]).astype(BF16)
    q_aug = _dot(h, wqa_ref[...]) * scale
    qa_ref[...] = (q_aug + oneq_ref[...]).astype(BF16)
    stats_ref[0, pl.ds(tile, 1), :] = max_norm(q_aug)
    kt = _dot_nt(wkt_ref[...], h)
    kt_ref[...] = kt
    head_lane = lax.broadcasted_iota(jnp.int32, (1, LANES), 1)
    k_norm = jnp.zeros((1, LANES), F32)
    for hd in range(kt.shape[0] // head_dim):
        rows = kt[hd * head_dim:(hd + 1) * head_dim, :]
        sq = jnp.max(jnp.sum(rows * rows, axis=0, keepdims=True), axis=1, keepdims=True)
        k_norm = jnp.where(head_lane == hd, jnp.sqrt(sq) * NORM_SLACK, k_norm)
    stats_ref[1, pl.ds(tile, 1), :] = k_norm
    vt = _dot_nt(wvt_ref[...], h)
    vt_ref[...] = vt
    tm = vt.shape[1]
    tail = jnp.where(lax.broadcasted_iota(jnp.int32, (LANES - head_dim, tm), 0) == 0, 1.0, 0.0)
    blocks = []
    for r0 in range(0, vt.shape[0], head_dim):
        blocks += [vt[r0:r0 + head_dim, :], tail]
    vta_ref[...] = jnp.concatenate(blocks, axis=0).astype(BF16)
    qc_ref[...] = _dot(h, wc_ref[...])
    lf = _log_sigmoid(_dot(h, wf_ref[...]) + bf_ref[...])
    lf_ref[...] = lf
    r = lax.broadcasted_iota(jnp.int32, (tm, tm), 0)
    c = lax.broadcasted_iota(jnp.int32, (tm, tm), 1)
    low = jnp.where(c <= r, 1.0, 0.0).astype(BF16)
    cum = carry_ref[0:1, :] + sum(_dot(low, piece) for piece in _split3(lf))
    carry_ref[...] = jnp.broadcast_to(cum[tm - 1:, :], carry_ref.shape)
    bias = sum(_dot(piece, place_ref[j]) for j, piece in enumerate(_split3(-cum)))
    ka_ref[...] = (_dot(h, wka_ref[...]) + bias).astype(BF16)
    stats_ref[2, pl.ds(tile, 1), :] = cum[0:1, :]
    stats_ref[3, pl.ds(tile, 1), :] = cum[tm - 1:, :]


def _fox_proj_prompt(x, g, w_in, b_f, *, heads, head_dim, cross_width, tm, tiles_per_seq):
    t, d = x.shape
    w = heads * head_dim
    wa = heads * LANES
    wq, wk, wv, wf, wc, bf = _fox_split(w_in, b_f, heads, head_dim)
    aug = lambda a: jnp.pad(a.reshape(d, heads, head_dim), ((0, 0), (0, 0), (0, LANES - head_dim))).reshape(d, wa)
    head = np.arange(heads)
    place = np.zeros((N_BIAS, LANES, wa), np.float32)
    oneq = np.zeros((1, wa), np.float32)
    for j in range(N_BIAS):
        place[j, head, head * LANES + head_dim + j] = 1.0
        oneq[0, head * LANES + head_dim + j] = 1.0
    place = jnp.asarray(place)
    headsum = jnp.asarray(np.arange(wa)[:, None] // LANES == np.arange(LANES)[None, :], BF16)
    row = lambda n: pl.BlockSpec((tm, n), lambda i: (i, 0))
    col = lambda n: pl.BlockSpec((None, n, tm), lambda i: (i // tiles_per_seq, 0, i % tiles_per_seq))
    seq_len = tm * tiles_per_seq
    n_tiles = t // tm
    outs = [(cross_width, F32), (LANES, F32), (wa, BF16), (wa, BF16)]
    stats_shape = (4, n_tiles, LANES)
    return pl.pallas_call(
        functools.partial(_fox_proj_prompt_body, scale=head_dim ** -0.5, tiles_per_seq=tiles_per_seq,
                          head_dim=head_dim),
        grid=(n_tiles,),
        in_specs=[row(d), _resident((1, d)), _resident((d, wa)), _resident((w, d)), _resident((d, wa)),
                  _resident((w, d)), _resident((d, cross_width)),
                  _resident((d, LANES)), _resident((1, LANES)), _resident(place.shape),
                  _resident((1, wa)), _resident(headsum.shape)],
        out_specs=[col(w), col(w)] + [row(n) for n, _ in outs]
        + [pl.BlockSpec((wa, tm), lambda i: (0, i)), pl.BlockSpec(stats_shape, lambda i: (0, 0, 0))],
        out_shape=[jax.ShapeDtypeStruct((t // seq_len, w, seq_len), F32)] * 2
        + [jax.ShapeDtypeStruct((t, n), dt) for n, dt in outs]
        + [jax.ShapeDtypeStruct((wa, t), BF16), jax.ShapeDtypeStruct(stats_shape, F32)],
        scratch_shapes=[pltpu.VMEM((SUBLANES, LANES), F32)],
        compiler_params=_params("arbitrary"),
        name="fox_proj_prompt",
    )(x, g.reshape(1, d), aug(wq).astype(BF16), wk.T.astype(BF16), aug(wk).astype(BF16),
      wv.T.astype(BF16), wc.astype(BF16), wf.astype(BF16), bf, place.astype(BF16), oneq, headsum)


def _s5_discretise(a_re, a_im, log_dt):
    dt = jnp.exp(log_dt)
    mag = jnp.exp(dt * a_re)
    ab_re = mag * jnp.cos(dt * a_im)
    ab_im = mag * jnp.sin(dt * a_im)
    den = a_re * a_re + a_im * a_im
    nr = ab_re - 1.0
    ni = ab_im
    return ab_re, ab_im, (nr * a_re + ni * a_im) / den, (ni * a_re - nr * a_im) / den


def _s5_param_body(are_ref, aim_ref, ldt_ref, arex_ref, aimx_ref, ldtx_ref, bre_ref, bim_ref,
                   bbr_ref, bbi_ref, pwr_ref, pwi_ref):
    _, _, zr, zi = _s5_discretise(arex_ref[...], aimx_ref[...], ldtx_ref[...])
    b_re = bre_ref[...]
    b_im = bim_ref[...]
    bbr_ref[...] = zr * b_re - zi * b_im
    bbi_ref[...] = zr * b_im + zi * b_re
    ab_re, ab_im, _, _ = _s5_discretise(are_ref[...], aim_ref[...], ldt_ref[...])
    pr, pi = ab_re, ab_im
    for r in range(pwr_ref.shape[0]):
        pwr_ref[r] = pr
        pwi_ref[r] = pi
        pr, pi = pr * ab_re - pi * ab_im, pr * ab_im + pi * ab_re


def _s5_params(a_re, a_im, log_dt, b_re, b_im, c_re, c_im, *, n_pow):
    g, p, c = b_re.shape
    rows = g * c
    rep = lambda a: jnp.repeat(a, c, axis=0)
    bt = lambda b: b.transpose(0, 2, 1).reshape(rows, p)
    whole = lambda shape: pl.BlockSpec(shape, lambda: (0,) * len(shape))
    log_dt = log_dt.reshape(g, 1)
    bbr, bbi, pwr, pwi = pl.pallas_call(
        _s5_param_body,
        in_specs=[whole((g, p)), whole((g, p)), whole((g, 1)), whole((rows, p)), whole((rows, p)),
                  whole((rows, 1)), whole((rows, p)), whole((rows, p))],
        out_specs=[whole((rows, p)), whole((rows, p)), whole((n_pow, g, p)), whole((n_pow, g, p))],
        out_shape=[jax.ShapeDtypeStruct((rows, p), F32)] * 2 + [jax.ShapeDtypeStruct((n_pow, g, p), F32)] * 2,
        name="s5_params",
    )(a_re, a_im, log_dt, rep(a_re), rep(a_im), rep(log_dt), bt(b_re), bt(b_im))
    width = g * p
    pw_re, pw_im = pwr.reshape(n_pow, width), pwi.reshape(n_pow, width)
    gpt = MXU_DIM // c
    n_tiles = g // gpt
    eye = jnp.eye(gpt, dtype=F32)

    def in_tiles(bb):
        blocks = bb.reshape(n_tiles, gpt, c, p)
        return jnp.einsum("tgcp,gh->tgchp", blocks, eye).reshape(n_tiles, gpt * c, gpt * p)

    def out_tiles(cc):
        blocks = cc.reshape(n_tiles, gpt, c, p)
        return jnp.einsum("tgcp,gh->tgphc", blocks, eye).reshape(n_tiles, gpt * p, gpt * c)

    w_in = jnp.concatenate([in_tiles(bbr), in_tiles(bbi)], axis=-1).astype(BF16)
    return (pw_re[0:1], pw_im[0:1], pw_re, pw_im, w_in,
            out_tiles(c_re).astype(BF16), out_tiles(c_im).astype(BF16))


def _s5_in_proj(u, wb_ref, bre_ref, bim_ref):
    ub = u.astype(BF16)
    n_tiles, ch, two_w = wb_ref.shape
    w = two_w // 2
    for t in range(n_tiles):
        bu = _dot(ub[:, t * ch:(t + 1) * ch], wb_ref[t])
        bre_ref[:, t * w:(t + 1) * w] = bu[:, :w]
        bim_ref[:, t * w:(t + 1) * w] = bu[:, w:]


def _s5_out(u, hre_ref, him_ref, wcr_ref, wci_ref, d_ref, wglu_ref):
    n_tiles, w, _ = wcr_ref.shape
    ys = []
    for t in range(n_tiles):
        hr = hre_ref[:, t * w:(t + 1) * w].astype(BF16)
        hi = him_ref[:, t * w:(t + 1) * w].astype(BF16)
        ys.append(_dot(hr, wcr_ref[t]) - _dot(hi, wci_ref[t]))
    y = jax.nn.gelu(jnp.concatenate(ys, axis=-1) + d_ref[...] * u)
    z = _dot(y.astype(BF16), wglu_ref[...])
    half = z.shape[-1] // 2
    return z[:, :half] * jax.nn.sigmoid(z[:, half:])


SEGMENT_PITCH_PAD = SUBLANES


def _s5_prompt_body(u_ref, wb_ref, wcr_ref, wci_ref, d_ref, wglu_ref, ab8_ref, aseg_ref,
                    o_ref, hre_ref, him_ref, up_ref, slab_ref, bre_ref, bim_ref, cin_ref, carry_ref,
                    *, lane_group):
    ci = pl.program_id(1)
    rows, width = bre_ref.shape
    ch = up_ref.shape[1]
    seg = rows // SUBLANES
    pitch = slab_ref.shape[1] // SUBLANES
    n_slab = ch // LANES
    lanes = [slice(j * LANES, (j + 1) * LANES) for j in range(n_slab)]
    groups = [slice(g * lane_group, (g + 1) * lane_group) for g in range(width // lane_group)]
    tile = lambda k: pl.ds(pl.multiple_of(k * SUBLANES, SUBLANES), SUBLANES)

    @pl.when(ci == 0)
    def _():
        carry_ref[...] = jnp.zeros_like(carry_ref)

    for s in range(SUBLANES):
        for j, ls in enumerate(lanes):
            slab_ref[j, pitch * s:pitch * s + seg, :] = u_ref[0, seg * s:seg * (s + 1), ls]

    def gather(k, carry):
        for j, ls in enumerate(lanes):
            up_ref[tile(k), ls] = slab_ref[j, pl.ds(k, SUBLANES, stride=pitch), :]
        return carry

    lax.fori_loop(0, seg, gather, 0)
    u = up_ref[...]
    _s5_in_proj(u, wb_ref, bre_ref, bim_ref)

    for ls in groups:
        ar = ab8_ref[0, :, ls]
        ai = ab8_ref[1, :, ls]

        def step(k, carry, ls=ls, ar=ar, ai=ai):
            hr, hi = carry
            hr, hi = ar * hr - ai * hi + bre_ref[tile(k), ls], ar * hi + ai * hr + bim_ref[tile(k), ls]
            bre_ref[tile(k), ls] = hr
            bim_ref[tile(k), ls] = hi
            return hr, hi

        zero = jnp.zeros((SUBLANES, lane_group), F32)
        lax.fori_loop(0, seg, step, (zero, zero))

    er, ei = carry_ref[0, 0:1, :], carry_ref[1, 0:1, :]
    sr, si = aseg_ref[0], aseg_ref[1]
    for s in range(SUBLANES):
        cin_ref[0, s:s + 1, :] = er
        cin_ref[1, s:s + 1, :] = ei
        end = rows - SUBLANES + s
        er, ei = (bre_ref[end:end + 1, :] + sr * er - si * ei, bim_ref[end:end + 1, :] + sr * ei + si * er)
    carry_ref[0, 0:1, :] = er
    carry_ref[1, 0:1, :] = ei

    for ls in groups:
        ar = ab8_ref[0, :, ls]
        ai = ab8_ref[1, :, ls]

        def fix(k, carry, ls=ls, ar=ar, ai=ai):
            gr, gi = carry
            gr, gi = ar * gr - ai * gi, ar * gi + ai * gr
            bre_ref[tile(k), ls] = bre_ref[tile(k), ls] + gr
            bim_ref[tile(k), ls] = bim_ref[tile(k), ls] + gi
            return gr, gi

        lax.fori_loop(0, seg, fix, (cin_ref[0, :, ls], cin_ref[1, :, ls]))

    up_ref[...] = _s5_out(u, bre_ref, bim_ref, wcr_ref, wci_ref, d_ref, wglu_ref)

    def scatter(k, carry):
        for j, ls in enumerate(lanes):
            slab_ref[j, pl.ds(k, SUBLANES, stride=pitch), :] = up_ref[tile(k), ls]
        return carry

    lax.fori_loop(0, seg, scatter, 0)
    for s in range(SUBLANES):
        for j, ls in enumerate(lanes):
            o_ref[0, seg * s:seg * (s + 1), ls] = slab_ref[j, pitch * s:pitch * s + seg, :]

    @pl.when(ci == pl.num_programs(1) - 1)
    def _():
        hre_ref[0] = er
        him_ref[0] = ei


def _s5_prompt(u, params, d_skip, w_glu, *, rows):
    ab_re, ab_im, pw_re, pw_im, w_in, wc_re, wc_im = params
    nb, length, ch = u.shape
    width = ab_re.shape[1]
    seg = rows // SUBLANES
    ab8 = jnp.stack([jnp.broadcast_to(ab_re, (SUBLANES, width)), jnp.broadcast_to(ab_im, (SUBLANES, width))])
    a_seg = jnp.stack([pw_re[seg - 1:seg], pw_im[seg - 1:seg]])
    state = jax.ShapeDtypeStruct((nb, 1, width), F32)
    state_spec = pl.BlockSpec((1, 1, width), lambda b, c: (b, 0, 0))
    tok_spec = pl.BlockSpec((1, rows, ch), lambda b, c: (b, c, 0))
    slab_rows = SUBLANES * (seg + SEGMENT_PITCH_PAD)
    return pl.pallas_call(
        functools.partial(_s5_prompt_body, lane_group=4 * LANES),
        grid=(nb, length // rows),
        in_specs=[tok_spec, _resident(w_in.shape), _resident(wc_re.shape), _resident(wc_im.shape),
                  _resident((1, ch)), _resident(w_glu.shape), _resident(ab8.shape), _resident(a_seg.shape)],
        out_specs=[tok_spec, state_spec, state_spec],
        out_shape=[jax.ShapeDtypeStruct((nb, length, ch), F32), state, state],
        scratch_shapes=[pltpu.VMEM((rows, ch), F32), pltpu.VMEM((ch // LANES, slab_rows, LANES), F32),
                        pltpu.VMEM((rows, width), F32), pltpu.VMEM((rows, width), F32),
                        pltpu.VMEM((2, SUBLANES, width), F32), pltpu.VMEM((2, SUBLANES, width), F32)],
        compiler_params=_params("parallel", "arbitrary"),
        name="s5_prompt",
    )(u, w_in, wc_re, wc_im, d_skip.reshape(1, ch), w_glu.astype(BF16), ab8, a_seg)


def _s5_sample_body(u_ref, h0r_ref, h0i_ref, wb_ref, wcr_ref, wci_ref, d_ref, wglu_ref, ab_ref,
                    o_ref, hre_ref, him_ref, bre_ref, bim_ref):
    hre_ref[...] = h0r_ref[...]
    him_ref[...] = h0i_ref[...]
    ar = ab_ref[0:1, :]
    ai = ab_ref[1:2, :]
    for t in range(u_ref.shape[0]):
        u = u_ref[t]
        _s5_in_proj(u, wb_ref, bre_ref, bim_ref)
        hr = hre_ref[...]
        hi = him_ref[...]
        hre_ref[...] = ar * hr - ai * hi + bre_ref[...]
        him_ref[...] = ar * hi + ai * hr + bim_ref[...]
        o_ref[t] = _s5_out(u, hre_ref, him_ref, wcr_ref, wci_ref, d_ref, wglu_ref)


def _s5_sample(u, h0_re, h0_im, params, d_skip, w_glu):
    ab_re, ab_im, _, _, w_in, wc_re, wc_im = params
    nt, nb, ch = u.shape
    width = ab_re.shape[1]
    ab = jnp.concatenate([ab_re, ab_im], axis=0)
    whole = lambda shape: pl.BlockSpec(shape, lambda: (0,) * len(shape))
    state = jax.ShapeDtypeStruct((nb, width), F32)
    return pl.pallas_call(
        _s5_sample_body,
        in_specs=[whole(u.shape), whole((nb, width)), whole((nb, width)), whole(w_in.shape),
                  whole(wc_re.shape), whole(wc_im.shape), whole((1, ch)), whole(w_glu.shape),
                  whole(ab.shape)],
        out_specs=[whole(u.shape), whole((nb, width)), whole((nb, width))],
        out_shape=[jax.ShapeDtypeStruct(u.shape, F32), state, state],
        scratch_shapes=[pltpu.VMEM((nb, width), F32), pltpu.VMEM((nb, width), F32)],
        compiler_params=pltpu.CompilerParams(vmem_limit_bytes=VMEM_LIMIT_BYTES),
        name="s5_sample",
    )(u, h0_re, h0_im, w_in, wc_re, wc_im, d_skip.reshape(1, ch), w_glu.astype(BF16), ab)


HEADS_PER_STEP = 2


UNDERFLOW_MARGIN = 110.0


def _first_live_block(st_ref, base, n_seq_blk, first_chunk, n_chunk, pair):
    kn = st_ref[1, pl.ds(base, n_seq_blk), :]
    c_last = st_ref[3, pl.ds(base, n_seq_blk), :]
    rows = [kn[0:1]]
    for j in range(1, n_seq_blk):
        rows.append(jnp.maximum(rows[-1], kn[j:j + 1]))
    kn_run = jnp.concatenate(rows, axis=0)
    worst = None
    for c in range(n_chunk):
        r = base + first_chunk + c
        term = st_ref[0, pl.ds(r, 1), :] * (kn_run + st_ref[1, pl.ds(r, 1), :]) + st_ref[2, pl.ds(r, 1), :]
        worst = term if worst is None else jnp.maximum(worst, term)
    blk = lax.broadcasted_iota(jnp.int32, worst.shape, 0)
    lane = lax.broadcasted_iota(jnp.int32, (1, LANES), 1)
    dead = (worst - c_last < -UNDERFLOW_MARGIN) & (blk < first_chunk)
    count = jnp.sum(jnp.where(dead, 1, 0), axis=0, keepdims=True)
    return jnp.min(jnp.where(lane // HEADS_PER_STEP == pair, count, n_seq_blk))


def _fox_prompt_body(q_ref, k_ref, vt_ref, st_ref, o_ref, m_ref, acc_ref, *, head_dim, tk):
    qi = pl.program_id(2)
    tq = q_ref.shape[1]
    n_chunk = tq // tk
    n_seq_blk = k_ref.shape[1] // tk
    first_live = _first_live_block(st_ref, pl.program_id(0) * n_seq_blk, n_seq_blk, qi * n_chunk, n_chunk,
                                   pl.program_id(1))
    m_ref[...] = jnp.full_like(m_ref, NEG_INF)
    acc_ref[...] = jnp.zeros_like(acc_ref)

    def block(ks, first_chunk, masked_chunk):
        chains = [(h, slice(h * LANES, (h + 1) * LANES), c, slice(c * tk, (c + 1) * tk))
                  for h in range(HEADS_PER_STEP) for c in range(first_chunk, n_chunk)]
        scores = []
        for _, hl, c, qs in chains:
            s = _dot_nt(k_ref[0, pl.ds(ks, tk), hl], q_ref[0, qs, hl])
            if c == masked_chunk:
                key = lax.broadcasted_iota(jnp.int32, s.shape, 0)
                qry = lax.broadcasted_iota(jnp.int32, s.shape, 1)
                s = jnp.where(key <= qry, s, NEG_INF)
            scores.append(s)
        probs, alphas = [], []
        for s, (h, _, _, qs) in zip(scores, chains):
            m_old = m_ref[h, :, qs]
            m_new = jnp.maximum(m_old, jnp.max(s, axis=0, keepdims=True))
            alphas.append(jnp.exp(m_old - m_new))
            probs.append(jnp.exp(s - m_new).astype(BF16))
            m_ref[h, :, qs] = m_new
        for p, alpha, (h, hl, _, qs) in zip(probs, alphas, chains):
            acc_ref[h, :, qs] = alpha * acc_ref[h, :, qs] + _dot(vt_ref[hl, pl.ds(ks, tk)], p)

    def body(kj, carry):
        block(pl.multiple_of(kj * tk, tk), 0, None)
        return carry

    lax.fori_loop(first_live, qi * n_chunk, body, 0)
    for c in range(n_chunk):
        block(pl.multiple_of((qi * n_chunk + c) * tk, tk), c, c)
    outs = [acc_ref[h, :head_dim, :] / acc_ref[h, head_dim:head_dim + 1, :] for h in range(HEADS_PER_STEP)]
    o_ref[0] = jnp.concatenate(outs, axis=0).T


def _fox_prompt(qa, ka, vta, stats, *, heads, head_dim, tq, tk):
    nb, length, _ = qa.shape
    wide = HEADS_PER_STEP * LANES
    return pl.pallas_call(
        functools.partial(_fox_prompt_body, head_dim=head_dim, tk=tk),
        grid=(nb, heads // HEADS_PER_STEP, length // tq),
        in_specs=[pl.BlockSpec((1, tq, wide), lambda b, hp, i: (b, i, hp)),
                  pl.BlockSpec((1, length, wide), lambda b, hp, i: (b, 0, hp)),
                  pl.BlockSpec((wide, length), lambda b, hp, i: (hp, b)),
                  pl.BlockSpec(stats.shape, lambda b, hp, i: (0, 0, 0))],
        out_specs=pl.BlockSpec((1, tq, HEADS_PER_STEP * head_dim), lambda b, hp, i: (b, i, hp)),
        out_shape=jax.ShapeDtypeStruct((nb, length, heads * head_dim), F32),
        scratch_shapes=[pltpu.VMEM((HEADS_PER_STEP, 1, tq), F32),
                        pltpu.VMEM((HEADS_PER_STEP, LANES, tq), F32)],
        compiler_params=_params("parallel", "parallel", "arbitrary"),
        name="fox_prompt",
    )(qa, ka, vta, stats)


def _fox_sample_body(pt_ref, q_ref, *refs, pages_per_step, head_dim):
    n = pages_per_step
    kt_refs, vt_refs, lf_refs = refs[:n], refs[n:2 * n], refs[2 * n:3 * n]
    kn_ref, vn_ref, lfn_ref, o_ref, qrow_ref, m_ref, l_ref, acc_ref, carry_ref = refs[3 * n:]
    del pt_ref
    g = pl.program_id(1)
    n_tok, w = q_ref.shape
    hp, page = lf_refs[0].shape
    head_of_lane = lax.broadcasted_iota(jnp.int32, (hp, w), 1) // head_dim
    head_mask = head_of_lane == lax.broadcasted_iota(jnp.int32, (hp, w), 0)
    tri = _upper_tri()

    @pl.when(g == 0)
    def _():
        zero = jnp.zeros((hp, w), F32)
        rows = [jnp.where(head_mask, jnp.broadcast_to(q_ref[t:t + 1, :], (hp, w)), zero)
                for t in range(n_tok)]
        qrow_ref[...] = jnp.concatenate(rows, axis=0).astype(BF16)
        m_ref[...] = jnp.full_like(m_ref, NEG_INF)
        l_ref[...] = jnp.zeros_like(l_ref)
        acc_ref[...] = jnp.zeros_like(acc_ref)
        carry_ref[...] = jnp.zeros_like(carry_ref)

    def update(pages, valid=None):
        scores, base = [], carry_ref[:, 0:1]
        for s, lf, _ in pages:
            local = _cumsum_lanes(lf, tri)
            c = local + base
            base = base + local[:, page - 1:]
            s = s - jnp.concatenate([c] * n_tok, axis=0)
            scores.append(s if valid is None else jnp.where(valid, s, NEG_INF))
        carry_ref[...] = jnp.broadcast_to(base, carry_ref.shape)
        m_old = m_ref[...]
        m_new = functools.reduce(jnp.maximum, [jnp.max(s, axis=1, keepdims=True) for s in scores], m_old)
        alpha = jnp.exp(m_old - m_new)
        probs = [jnp.exp(s - m_new) for s in scores]
        l_ref[...] = alpha * l_ref[...] + sum(jnp.sum(p, axis=1, keepdims=True) for p in probs)
        acc_ref[...] = alpha * acc_ref[...] + sum(pv(p.astype(BF16)) for p, (_, _, pv) in zip(probs, pages))
        m_ref[...] = m_new

    def cached(i):
        kt = kt_refs[i][...].reshape(w, page).astype(BF16)
        vt = vt_refs[i][...].reshape(w, page).astype(BF16)
        return _dot(qrow_ref[...], kt), lf_refs[i][...], lambda p: _dot_nt(p, vt)

    update([cached(i) for i in range(n)])

    @pl.when(g == pl.num_programs(1) - 1)
    def _():
        pad = jnp.zeros((page - kn_ref.shape[0], w), F32)
        kn = jnp.concatenate([kn_ref[...], pad], axis=0).astype(BF16)
        vn = jnp.concatenate([vn_ref[...], pad], axis=0).astype(BF16)
        key = lax.broadcasted_iota(jnp.int32, (n_tok * hp, page), 1)
        tok = lax.broadcasted_iota(jnp.int32, (n_tok * hp, page), 0) // hp
        update([(_dot_nt(qrow_ref[...], kn), lfn_ref[...], lambda p: _dot(p, vn))], key <= tok)
        out = acc_ref[...] / l_ref[...]
        for t in range(n_tok):
            picked = jnp.where(head_mask, out[t * hp:(t + 1) * hp, :], 0.0)
            o_ref[t:t + 1, :] = jnp.sum(picked, axis=0, keepdims=True)


def _fox_sample(q, k_new, v_new, lft_new, cache_kt, cache_vt, cache_lft, page_table, *, layer, pages_per_step):
    ns, n_tok, w = q.shape
    n_pages = page_table.shape[1]
    _, _, heads, head_dim, page = cache_kt.shape
    hp = cache_lft.shape[1]
    n = pages_per_step
    pad_rows = lambda a: jnp.pad(a, ((0, 0), (0, SUBLANES - n_tok), (0, 0)))

    def kv_page(i):
        return pl.BlockSpec((None, None, heads, head_dim, page),
                            lambda s, g, pt: (layer, pt[s, g * n + i], 0, 0, 0))

    def lf_page(i):
        return pl.BlockSpec((None, hp, page), lambda s, g, pt: (pt[s, g * n + i], 0, 0))

    per_seq = lambda shape: pl.BlockSpec((None,) + shape, lambda s, g, pt: (s, 0, 0))
    in_specs = [per_seq((n_tok, w))] + [kv_page(i) for i in range(n)] * 2 + [lf_page(i) for i in range(n)]
    in_specs += [per_seq((SUBLANES, w)), per_seq((SUBLANES, w)), per_seq((hp, page))]
    rows = n_tok * hp
    return pl.pallas_call(
        functools.partial(_fox_sample_body, pages_per_step=n, head_dim=head_dim),
        grid_spec=pltpu.PrefetchScalarGridSpec(
            num_scalar_prefetch=1,
            grid=(ns, n_pages // n),
            in_specs=in_specs,
            out_specs=per_seq((n_tok, w)),
            scratch_shapes=[pltpu.VMEM((rows, w), BF16), pltpu.VMEM((rows, 1), F32),
                            pltpu.VMEM((rows, 1), F32), pltpu.VMEM((rows, w), F32),
                            pltpu.VMEM((hp, page), F32)],
        ),
        out_shape=jax.ShapeDtypeStruct((ns, n_tok, w), F32),
        compiler_params=_params("parallel", "arbitrary"),
        name="fox_sample",
    )(page_table, q, *([cache_kt] * n), *([cache_vt] * n), *([cache_lft] * n),
      pad_rows(k_new), pad_rows(v_new), lft_new)


def _softmax_pv(s, vb):
    m = jnp.max(s, axis=-1, keepdims=True)
    p = jnp.exp(s - m)
    return _dot(p.astype(BF16), vb) / jnp.sum(p, axis=-1, keepdims=True)


def _cross_ffn_prompt_body(*refs, heads, head_dim, ff_chunk, final):
    if final:
        (x_ref, mix_ref, qc_ref, mk_ref, mv_ref, wom_ref, woc_ref, g_ref, wg_ref, wu_ref, wd_ref, gf_ref,
         o_ref, t_ref) = refs
    else:
        (x_ref, mix_ref, qc_ref, mk_ref, mv_ref, wom_ref, woc_ref, g_ref, wg_ref, wu_ref, wd_ref,
         o_ref, t_ref) = refs
    tm, cw = qc_ref.shape
    q = qc_ref[...] * head_dim ** -0.5
    head_of_lane = lax.broadcasted_iota(jnp.int32, (1, cw), 1) // head_dim
    zero = jnp.zeros_like(q)
    q4 = jnp.concatenate([jnp.where(head_of_lane == h, q, zero) for h in range(heads)], axis=0)
    s = _dot_nt(q4.astype(BF16), mk_ref[0].astype(BF16))
    o4 = _softmax_pv(s, mv_ref[0].astype(BF16))
    cross = zero
    for h in range(heads):
        cross = cross + jnp.where(head_of_lane == h, o4[h * tm:(h + 1) * tm, :], zero)
    x = (x_ref[...] + _dot(mix_ref[...].astype(BF16), wom_ref[...])
         + _dot(cross.astype(BF16), woc_ref[...]))
    y = _half_ffn(x, g_ref, wg_ref, wu_ref, wd_ref, t_ref, ff_chunk)
    if final:
        y = _rms(y, gf_ref[...])
    o_ref[...] = y


def _cross_ffn_prompt(x, mix, qc, mk, mv, wom, woc, g, wg, wu, wd, idx, g_final=None, *, heads, head_dim, tm):
    t, d = x.shape
    nb, n_mem, cw = mk.shape
    d_ff = wg.shape[2]
    per_b = t // nb // tm
    final = g_final is not None
    row = lambda n: pl.BlockSpec((tm, n), lambda i: (i, 0))
    mem = pl.BlockSpec((1, n_mem, cw), lambda i: (i // per_b, 0, 0))
    pick = lambda r, c: pl.BlockSpec((None, r, c), lambda i: (idx, 0, 0), pipeline_mode=pl.Buffered(1))
    in_specs = [row(d), row(mix.shape[1]), row(cw), mem, mem, _resident(wom.shape), _resident(woc.shape),
                _resident((1, d)), pick(d, d_ff), pick(d, d_ff), pick(d_ff, d)]
    args = [x, mix, qc, mk, mv, wom, woc, g.reshape(1, d), wg, wu, wd]
    if final:
        in_specs.append(_resident((1, d)))
        args.append(g_final.reshape(1, d))
    return pl.pallas_call(
        functools.partial(_cross_ffn_prompt_body, heads=heads, head_dim=head_dim, ff_chunk=MXU_DIM, final=final),
        grid=(t // tm,),
        in_specs=in_specs,
        out_specs=row(d),
        out_shape=jax.ShapeDtypeStruct((t, d), F32),
        scratch_shapes=[pltpu.VMEM((tm, d_ff), BF16)],
        compiler_params=_params("parallel"),
        name="cross_ffn_prompt",
    )(*args)


def _cross_out_sample_body(x_ref, mix_ref, qc_ref, mkt_ref, mvt_ref, wom_ref, woc_ref, o_ref, cross_ref,
                           *, n_tok):
    n_seq, _, head_dim, n_mem = mkt_ref.shape
    cw = qc_ref.shape[1]
    scale = head_dim ** -0.5
    head_of_lane = lax.broadcasted_iota(jnp.int32, (SUBLANES, cw), 1) // head_dim
    head_mask = head_of_lane == lax.broadcasted_iota(jnp.int32, (SUBLANES, cw), 0)
    zero = jnp.zeros((SUBLANES, cw), F32)
    for i in range(n_seq):
        rows = [jnp.where(head_mask, jnp.broadcast_to(qc_ref[pl.ds(i * n_tok + t, 1), :] * scale,
                                                     (SUBLANES, cw)), zero) for t in range(n_tok)]
        q = jnp.concatenate(rows, axis=0).astype(BF16)
        s = _dot(q, mkt_ref[i].reshape(cw, n_mem).astype(BF16))
        p = jnp.exp(s - jnp.max(s, axis=-1, keepdims=True))
        o = _dot_nt(p.astype(BF16), mvt_ref[i].reshape(cw, n_mem).astype(BF16)) / jnp.sum(p, axis=-1, keepdims=True)
        for t in range(n_tok):
            picked = jnp.where(head_mask, o[t * SUBLANES:(t + 1) * SUBLANES, :], zero)
            cross_ref[pl.ds(i * n_tok + t, 1), :] = jnp.sum(picked, axis=0, keepdims=True)
    o_ref[...] = (x_ref[...] + _dot(mix_ref[...].astype(BF16), wom_ref[...])
                  + _dot(cross_ref[...].astype(BF16), woc_ref[...]))


def _cross_out_sample(x, mix, qc, mkt, mvt, wom, woc, *, layer, n_tok, seqs_per_step):
    t, d = x.shape
    _, ns, heads, head_dim, n_mem = mkt.shape
    cw = heads * head_dim
    tm = seqs_per_step * n_tok
    row = lambda n: pl.BlockSpec((tm, n), lambda i: (i, 0))
    mem = pl.BlockSpec((None, seqs_per_step, heads, head_dim, n_mem), lambda i: (layer, i, 0, 0, 0))
    return pl.pallas_call(
        functools.partial(_cross_out_sample_body, n_tok=n_tok),
        grid=(ns // seqs_per_step,),
        in_specs=[row(d), row(mix.shape[1]), row(cw), mem, mem, _resident(wom.shape),
                  _resident(woc.shape)],
        out_specs=row(d),
        out_shape=jax.ShapeDtypeStruct((t, d), F32),
        scratch_shapes=[pltpu.VMEM((tm, cw), F32)],
        compiler_params=_params("parallel"),
        name="cross_out_sample",
    )(x, mix, qc, mkt, mvt, wom, woc)


def kernel(x_prompt, x_sample, mem_prompt, state_s5_re, state_s5_im, cache_fox_k, cache_fox_v, cache_fox_logf, cache_mem_k, cache_mem_v, page_table, ffn_norm, ffn_w_gate, ffn_w_up, ffn_w_down, norm_mix, norm_mem, w_mem_kv, w_in_s5, s5_a_re, s5_a_im, s5_log_dt, s5_b_re, s5_b_im, s5_c_re, s5_c_im, s5_d, s5_w_glu, w_in_fox, fox_b_f, w_out, norm_final):
    nb, seq, d = x_prompt.shape
    ns, n_tok, _ = x_sample.shape
    depth = ffn_norm.shape[0]
    n_mem = mem_prompt.shape[1]
    cross_heads, head_dim = cache_mem_k.shape[3], cache_mem_k.shape[4]
    cross_width = cross_heads * head_dim
    mixer_width = w_out.shape[1] - cross_width
    fox_heads = cache_fox_k.shape[3]
    n_phys, page = cache_fox_k.shape[1], cache_fox_k.shape[2]
    groups, n_state = state_s5_re.shape[2], state_s5_re.shape[3]
    hp = 2 * SUBLANES

    tm_p = 512
    tm_s = ns * n_tok
    xp = x_prompt.reshape(nb * seq, d)
    xs = x_sample.reshape(tm_s, d)
    mem = mem_prompt.reshape(nb * n_mem, d)

    s5_re_p, s5_im_p, s5_re_s, s5_im_s = [], [], [], []
    fk_p, fv_p, fl_p, fk_s, fv_s, fl_s = [], [], [], [], [], []
    mk_list, mv_list = [], []
    d_ff = ffn_w_gate.shape[-1]
    wg_all = _to_bf16(ffn_w_gate.reshape(2 * depth, d, d_ff), rows=tm_p)
    wu_all = _to_bf16(ffn_w_up.reshape(2 * depth, d, d_ff), rows=tm_p)
    wd_all = _to_bf16(ffn_w_down.reshape(2 * depth, d_ff, d), rows=d_ff // 2)
    for i in range(depth):
        j = i // N_MIXERS
        ffn_w = lambda half: (ffn_norm[i, half], wg_all, wu_all, wd_all, 2 * i + half)
        if i % N_MIXERS == 0:
            w_in = [w_in_s5[j][:, :mixer_width], w_in_s5[j][:, mixer_width:]]
            xp, u_p, qc_p = _ffn_proj(xp, *ffn_w(0), norm_mix[i], w_in, tm=tm_p)
        else:
            xp = _ffn(xp, *ffn_w(0), tm=tm_p)
        xs = _ffn(xs, *ffn_w(0), tm=tm_s)

        mk_p, mv_p = _norm_proj(mem, norm_mem[i],
                                [w_mem_kv[i][:, :cross_width], w_mem_kv[i][:, cross_width:]], tm=nb * n_mem)
        mk_list.append(mk_p.reshape(nb, n_mem, cross_heads, head_dim))
        mv_list.append(mv_p.reshape(nb, n_mem, cross_heads, head_dim))

        if i % N_MIXERS == 0:
            u_s, qc_s = _norm_proj(xs, norm_mix[i], w_in, tm=tm_s)
            params = _s5_params(s5_a_re[j], s5_a_im[j], s5_log_dt[j], s5_b_re[j], s5_b_im[j],
                                s5_c_re[j], s5_c_im[j], n_pow=tm_p // SUBLANES)
            mix_p, hr_p, hi_p = _s5_prompt(u_p.reshape(nb, seq, mixer_width), params, s5_d[j],
                                           s5_w_glu[j], rows=tm_p)
            mix_p = mix_p.reshape(nb * seq, mixer_width)
            u_t = u_s.reshape(ns, n_tok, mixer_width).transpose(1, 0, 2)
            mix_t, hr_s, hi_s = _s5_sample(u_t, state_s5_re[j].reshape(ns, groups * n_state),
                                           state_s5_im[j].reshape(ns, groups * n_state),
                                           params, s5_d[j], s5_w_glu[j])
            mix_s = mix_t.transpose(1, 0, 2).reshape(tm_s, mixer_width)
            s5_re_p.append(hr_p.reshape(nb, groups, n_state))
            s5_im_p.append(hi_p.reshape(nb, groups, n_state))
            s5_re_s.append(hr_s.reshape(ns, groups, n_state))
            s5_im_s.append(hi_s.reshape(ns, groups, n_state))
        else:
            fox = dict(heads=fox_heads, head_dim=head_dim, cross_width=cross_width)
            kt_p, vt_p, qc_p, lf_p, qa_p, ka_p, vta_p, stats_p = _fox_proj_prompt(
                xp, norm_mix[i], w_in_fox[j], fox_b_f[j], tm=tm_p, tiles_per_seq=seq // tm_p, **fox)
            q_s, k_s, v_s, qc_s, lf_s, lft_s = _fox_proj_sample(
                xs, norm_mix[i], w_in_fox[j], fox_b_f[j], tm=tm_s, **fox)

            seq3 = lambda a: a.reshape(nb, seq, a.shape[-1])
            mix_p = _fox_prompt(seq3(qa_p), seq3(ka_p), vta_p, stats_p, heads=fox_heads, head_dim=head_dim,
                                tq=min(seq, 4 * tm_p), tk=tm_p)
            mix_p = mix_p.reshape(nb * seq, mixer_width)

            tok3 = lambda a: a.reshape(ns, n_tok, mixer_width)
            lft_new = jnp.pad(lft_s.reshape(hp, ns, n_tok).transpose(1, 0, 2),
                              ((0, 0), (0, 0), (0, page - n_tok)))
            cache_lft = jnp.pad(cache_fox_logf[j].transpose(0, 2, 1), ((0, 0), (0, hp - fox_heads), (0, 0)))
            to_stored = lambda a: a.transpose(0, 1, 3, 4, 2)
            mix_s = _fox_sample(tok3(q_s), tok3(k_s), tok3(v_s), lft_new, to_stored(cache_fox_k),
                                to_stored(cache_fox_v), cache_lft, page_table, layer=j,
                                pages_per_step=page_table.shape[1])
            mix_s = mix_s.reshape(tm_s, mixer_width)

            from_stored = lambda a: a.reshape(nb, fox_heads, head_dim, seq).transpose(0, 3, 1, 2)
            fk_p.append(from_stored(kt_p))
            fv_p.append(from_stored(vt_p))
            fl_p.append(lf_p[:, :fox_heads].reshape(nb, seq, fox_heads))
            fk_s.append(k_s.reshape(ns, n_tok, fox_heads, head_dim))
            fv_s.append(v_s.reshape(ns, n_tok, fox_heads, head_dim))
            fl_s.append(lf_s[:, :fox_heads].reshape(ns, n_tok, fox_heads))

        wom = w_out[i][:mixer_width].astype(BF16)
        woc = w_out[i][mixer_width:].astype(BF16)
        g_final = norm_final if i == depth - 1 else None
        xp = _cross_ffn_prompt(xp, mix_p, qc_p, mk_p.reshape(nb, n_mem, cross_width),
                               mv_p.reshape(nb, n_mem, cross_width), wom, woc, *ffn_w(1), g_final,
                               heads=cross_heads, head_dim=head_dim, tm=tm_p)
        xs = _cross_out_sample(xs, mix_s, qc_s, cache_mem_k.transpose(0, 1, 3, 4, 2),
                               cache_mem_v.transpose(0, 1, 3, 4, 2), wom, woc,
                               layer=i, n_tok=n_tok, seqs_per_step=8)
        xs = _ffn(xs, *ffn_w(1), g_final, tm=tm_s)

    return (xp.reshape(nb, seq, d), xs.reshape(ns, n_tok, d),
            jnp.stack(s5_re_p), jnp.stack(s5_im_p), jnp.stack(s5_re_s), jnp.stack(s5_im_s),
            jnp.stack(fk_p), jnp.stack(fv_p), jnp.stack(fl_p),
            jnp.stack(fk_s), jnp.stack(fv_s), jnp.stack(fl_s),
            jnp.stack(mk_list), jnp.stack(mv_list))
```

```python
import functools

import jax
import jax.numpy as jnp
import numpy as np
from jax import lax
from jax.experimental import pallas as pl
from jax.experimental.pallas import tpu as pltpu

F32 = jnp.float32
BF16 = jnp.bfloat16

RMS_EPS = 1e-6
NEG_INF = -1e30
N_MIXERS = 2

LANES = 128
SUBLANES = 8
MXU_DIM = 256
VMEM_LIMIT_BYTES = 56 * 1024 * 1024

NT_DIMS = (((1,), (1,)), ((), ()))


def _params(*sem):
    return pltpu.CompilerParams(dimension_semantics=sem, vmem_limit_bytes=VMEM_LIMIT_BYTES)


def _resident(shape):
    nd = len(shape)
    return pl.BlockSpec(shape, lambda *_: (0,) * nd, pipeline_mode=pl.Buffered(1))


def _rms(x, g):
    return x * lax.rsqrt(jnp.mean(x * x, axis=-1, keepdims=True) + RMS_EPS) * g


def _dot(a, b):
    return jnp.dot(a, b, preferred_element_type=F32)


def _dot_nt(a, b):
    return lax.dot_general(a, b, NT_DIMS, preferred_element_type=F32)


def _log_sigmoid(x):
    return jnp.minimum(x, 0.0) - jnp.log1p(jnp.exp(-jnp.abs(x)))


def _split3(x):
    hi = x.astype(BF16)
    r1 = x - hi.astype(F32)
    mid = r1.astype(BF16)
    lo = (r1 - mid.astype(F32)).astype(BF16)
    return hi, mid, lo


def _cumsum_lanes(x, tri):
    hi, mid, lo = _split3(x)
    return _dot(hi, tri) + _dot(mid, tri) + _dot(lo, tri)


def _upper_tri():
    r = lax.broadcasted_iota(jnp.int32, (LANES, LANES), 0)
    c = lax.broadcasted_iota(jnp.int32, (LANES, LANES), 1)
    return jnp.where(r <= c, 1.0, 0.0).astype(BF16)


def _half_ffn(x, g_ref, wg_ref, wu_ref, wd_ref, t_ref, ff_chunk):
    h = _rms(x, g_ref[...]).astype(BF16)
    for lo in range(0, wg_ref.shape[1], ff_chunk):
        a = _dot(h, wg_ref[:, lo:lo + ff_chunk])
        b = _dot(h, wu_ref[:, lo:lo + ff_chunk])
        t_ref[:, lo:lo + ff_chunk] = (jax.nn.silu(a) * b).astype(BF16)
    return x + 0.5 * _dot(t_ref[...], wd_ref[...])


def _ffn_body(*refs, ff_chunk, final):
    if final:
        x_ref, g_ref, wg_ref, wu_ref, wd_ref, gf_ref, o_ref, t_ref = refs
    else:
        x_ref, g_ref, wg_ref, wu_ref, wd_ref, o_ref, t_ref = refs
    y = _half_ffn(x_ref[...], g_ref, wg_ref, wu_ref, wd_ref, t_ref, ff_chunk)
    if final:
        y = _rms(y, gf_ref[...])
    o_ref[...] = y


def _ffn_proj_body(x_ref, g_ref, wg_ref, wu_ref, wd_ref, gm_ref, *refs, ff_chunk, n):
    w_refs, o_ref, p_refs, t_ref = refs[:n], refs[n], refs[n + 1:2 * n + 1], refs[2 * n + 1]
    y = _half_ffn(x_ref[...], g_ref, wg_ref, wu_ref, wd_ref, t_ref, ff_chunk)
    o_ref[...] = y
    h = _rms(y, gm_ref[...]).astype(BF16)
    for w_ref, p_ref in zip(w_refs, p_refs):
        p_ref[...] = _dot(h, w_ref[...])


def _ffn_proj(x, g, wg, wu, wd, idx, g_mix, ws, *, tm):
    t, d = x.shape
    d_ff = wg.shape[2]
    n = len(ws)
    row = lambda c: pl.BlockSpec((tm, c), lambda i: (i, 0))
    pick = lambda r, c: pl.BlockSpec((None, r, c), lambda i: (idx, 0, 0), pipeline_mode=pl.Buffered(1))
    return pl.pallas_call(
        functools.partial(_ffn_proj_body, ff_chunk=MXU_DIM, n=n),
        grid=(t // tm,),
        in_specs=[row(d), _resident((1, d)), pick(d, d_ff), pick(d, d_ff), pick(d_ff, d), _resident((1, d))]
        + [_resident(w.shape) for w in ws],
        out_specs=[row(d)] + [row(w.shape[1]) for w in ws],
        out_shape=[jax.ShapeDtypeStruct((t, d), F32)] + [jax.ShapeDtypeStruct((t, w.shape[1]), F32) for w in ws],
        scratch_shapes=[pltpu.VMEM((tm, d_ff), BF16)],
        compiler_params=_params("parallel"),
        name="ffn_proj",
    )(x, g.reshape(1, d), wg, wu, wd, g_mix.reshape(1, d), *[w.astype(BF16) for w in ws])


def _cast_body(x_ref, o_ref):
    o_ref[...] = x_ref[...].astype(o_ref.dtype)


def _to_bf16(w, *, rows):
    n, r, c = w.shape
    spec = pl.BlockSpec((None, rows, c), lambda i, j: (i, j, 0))
    return pl.pallas_call(
        _cast_body, grid=(n, r // rows), in_specs=[spec], out_specs=spec,
        out_shape=jax.ShapeDtypeStruct(w.shape, BF16),
        compiler_params=_params("parallel", "parallel"), name="to_bf16",
    )(w)


def _ffn(x, g, wg, wu, wd, idx, g_final=None, *, tm):
    t, d = x.shape
    d_ff = wg.shape[2]
    final = g_final is not None
    row = pl.BlockSpec((tm, d), lambda i: (i, 0))
    pick = lambda r, c: pl.BlockSpec((None, r, c), lambda i: (idx, 0, 0), pipeline_mode=pl.Buffered(1))
    in_specs = [row, _resident((1, d)), pick(d, d_ff), pick(d, d_ff), pick(d_ff, d)]
    args = [x, g.reshape(1, d), wg, wu, wd]
    if final:
        in_specs.append(_resident((1, d)))
        args.append(g_final.reshape(1, d))
    return pl.pallas_call(
        functools.partial(_ffn_body, ff_chunk=MXU_DIM, final=final),
        grid=(t // tm,),
        in_specs=in_specs,
        out_specs=row,
        out_shape=jax.ShapeDtypeStruct((t, d), F32),
        scratch_shapes=[pltpu.VMEM((tm, d_ff), BF16)],
        compiler_params=_params("parallel"),
        name="ffn_final" if final else "ffn",
    )(*args)


def _norm_proj_body(x_ref, g_ref, *refs, n):
    h = _rms(x_ref[...], g_ref[...]).astype(BF16)
    for w_ref, o_ref in zip(refs[:n], refs[n:]):
        o_ref[...] = _dot(h, w_ref[...])


def _norm_proj(x, g, ws, *, tm):
    t, d = x.shape
    n = len(ws)
    in_specs = [pl.BlockSpec((tm, d), lambda i: (i, 0)), _resident((1, d))]
    in_specs += [_resident(w.shape) for w in ws]
    return pl.pallas_call(
        functools.partial(_norm_proj_body, n=n),
        grid=(t // tm,),
        in_specs=in_specs,
        out_specs=[pl.BlockSpec((tm, w.shape[1]), lambda i: (i, 0)) for w in ws],
        out_shape=[jax.ShapeDtypeStruct((t, w.shape[1]), F32) for w in ws],
        compiler_params=_params("parallel"),
        name="norm_proj",
    )(x, g.reshape(1, d), *[w.astype(BF16) for w in ws])


def _fox_split(w_in, b_f, heads, head_dim):
    w = heads * head_dim
    wq, wk, wv = w_in[:, :w], w_in[:, w:2 * w], w_in[:, 2 * w:3 * w]
    wf = jnp.pad(w_in[:, 3 * w:3 * w + heads], ((0, 0), (0, LANES - heads)))
    wc = w_in[:, 3 * w + heads:]
    bf = jnp.pad(b_f, (0, LANES - heads)).reshape(1, LANES)
    return wq, wk, wv, wf, wc, bf


def _fox_proj_sample_body(x_ref, g_ref, wq_ref, wk_ref, wv_ref, wc_ref, wf_ref, wft_ref, bf_ref, bft_ref,
                          q_ref, k_ref, v_ref, qc_ref, lf_ref, lft_ref, *, scale):
    h = _rms(x_ref[...], g_ref[...]).astype(BF16)
    q_ref[...] = _dot(h, wq_ref[...]) * scale
    k_ref[...] = _dot(h, wk_ref[...])
    v_ref[...] = _dot(h, wv_ref[...])
    qc_ref[...] = _dot(h, wc_ref[...])
    lf_ref[...] = _log_sigmoid(_dot(h, wf_ref[...]) + bf_ref[...])
    lft_ref[...] = _log_sigmoid(_dot_nt(wft_ref[...], h) + bft_ref[...])


def _fox_proj_sample(x, g, w_in, b_f, *, heads, head_dim, cross_width, tm):
    t, d = x.shape
    w = heads * head_dim
    hp = 2 * SUBLANES
    wq, wk, wv, wf, wc, bf = _fox_split(w_in, b_f, heads, head_dim)
    wft = wf[:, :hp].T
    bft = bf[0, :hp].reshape(hp, 1)
    row = lambda n: pl.BlockSpec((tm, n), lambda i: (i, 0))
    outs = [w, w, w, cross_width, LANES]
    return pl.pallas_call(
        functools.partial(_fox_proj_sample_body, scale=head_dim ** -0.5),
        grid=(t // tm,),
        in_specs=[row(d), _resident((1, d)), _resident((d, w)), _resident((d, w)),
                  _resident((d, w)), _resident((d, cross_width)), _resident((d, LANES)),
                  _resident((hp, d)), _resident((1, LANES)), _resident((hp, 1))],
        out_specs=[row(n) for n in outs] + [pl.BlockSpec((hp, tm), lambda i: (0, i))],
        out_shape=[jax.ShapeDtypeStruct((t, n), F32) for n in outs]
        + [jax.ShapeDtypeStruct((hp, t), F32)],
        compiler_params=_params("parallel"),
        name="fox_proj_sample",
    )(x, g.reshape(1, d), wq.astype(BF16), wk.astype(BF16), wv.astype(BF16), wc.astype(BF16),
      wf.astype(BF16), wft.astype(BF16), bf, bft)


N_BIAS = 3


NORM_SLACK = 1.02


def _fox_proj_prompt_body(x_ref, g_ref, wqa_ref, wkt_ref, wka_ref, wvt_ref, wc_ref, wf_ref,
                          bf_ref, place_ref, oneq_ref, headsum_ref,
                          kt_ref, vt_ref, qc_ref, lf_ref, qa_ref, ka_ref, vta_ref, stats_ref, carry_ref,
                          *, scale, tiles_per_seq, head_dim):
    tile = pl.program_id(0)

    @pl.when(tile % tiles_per_seq == 0)
    def _():
        carry_ref[...] = jnp.zeros_like(carry_ref)

    def max_norm(x):
        sq = _dot((x * x).astype(BF16), headsum_ref[...])
        return jnp.sqrt(jnp.max(sq, axis=0, keepdims=True)) * NORM_SLACK

    h = _rms(x_ref[...], g_ref[...]).astype(BF16)
    q_aug = _dot(h, wqa_ref[...]) * scale
    qa_ref[...] = (q_aug + oneq_ref[...]).astype(BF16)
    stats_ref[0, pl.ds(tile, 1), :] = max_norm(q_aug)
    kt = _dot_nt(wkt_ref[...], h)
    kt_ref[...] = kt
    head_lane = lax.broadcasted_iota(jnp.int32, (1, LANES), 1)
    k_norm = jnp.zeros((1, LANES), F32)
    for hd in range(kt.shape[0] // head_dim):
        rows = kt[hd * head_dim:(hd + 1) * head_dim, :]
        sq = jnp.max(jnp.sum(rows * rows, axis=0, keepdims=True), axis=1, keepdims=True)
        k_norm = jnp.where(head_lane == hd, jnp.sqrt(sq) * NORM_SLACK, k_norm)
    stats_ref[1, pl.ds(tile, 1), :] = k_norm
    vt = _dot_nt(wvt_ref[...], h)
    vt_ref[...] = vt
    tm = vt.shape[1]
    tail = jnp.where(lax.broadcasted_iota(jnp.int32, (LANES - head_dim, tm), 0) == 0, 1.0, 0.0)
    blocks = []
    for r0 in range(0, vt.shape[0], head_dim):
        blocks += [vt[r0:r0 + head_dim, :], tail]
    vta_ref[...] = jnp.concatenate(blocks, axis=0).astype(BF16)
    qc_ref[...] = _dot(h, wc_ref[...])
    lf = _log_sigmoid(_dot(h, wf_ref[...]) + bf_ref[...])
    lf_ref[...] = lf
    r = lax.broadcasted_iota(jnp.int32, (tm, tm), 0)
    c = lax.broadcasted_iota(jnp.int32, (tm, tm), 1)
    low = jnp.where(c <= r, 1.0, 0.0).astype(BF16)
    cum = carry_ref[0:1, :] + sum(_dot(low, piece) for piece in _split3(lf))
    carry_ref[...] = jnp.broadcast_to(cum[tm - 1:, :], carry_ref.shape)
    bias = sum(_dot(piece, place_ref[j]) for j, piece in enumerate(_split3(-cum)))
    ka_ref[...] = (_dot(h, wka_ref[...]) + bias).astype(BF16)
    stats_ref[2, pl.ds(tile, 1), :] = cum[0:1, :]
    stats_ref[3, pl.ds(tile, 1), :] = cum[tm - 1:, :]


def _fox_proj_prompt(x, g, w_in, b_f, *, heads, head_dim, cross_width, tm, tiles_per_seq):
    t, d = x.shape
    w = heads * head_dim
    wa = heads * LANES
    wq, wk, wv, wf, wc, bf = _fox_split(w_in, b_f, heads, head_dim)
    aug = lambda a: jnp.pad(a.reshape(d, heads, head_dim), ((0, 0), (0, 0), (0, LANES - head_dim))).reshape(d, wa)
    head = np.arange(heads)
    place = np.zeros((N_BIAS, LANES, wa), np.float32)
    oneq = np.zeros((1, wa), np.float32)
    for j in range(N_BIAS):
        place[j, head, head * LANES + head_dim + j] = 1.0
        oneq[0, head * LANES + head_dim + j] = 1.0
    place = jnp.asarray(place)
    headsum = jnp.asarray(np.arange(wa)[:, None] // LANES == np.arange(LANES)[None, :], BF16)
    row = lambda n: pl.BlockSpec((tm, n), lambda i: (i, 0))
    col = lambda n: pl.BlockSpec((None, n, tm), lambda i: (i // tiles_per_seq, 0, i % tiles_per_seq))
    seq_len = tm * tiles_per_seq
    n_tiles = t // tm
    outs = [(cross_width, F32), (LANES, F32), (wa, BF16), (wa, BF16)]
    stats_shape = (4, n_tiles, LANES)
    return pl.pallas_call(
        functools.partial(_fox_proj_prompt_body, scale=head_dim ** -0.5, tiles_per_seq=tiles_per_seq,
                          head_dim=head_dim),
        grid=(n_tiles,),
        in_specs=[row(d), _resident((1, d)), _resident((d, wa)), _resident((w, d)), _resident((d, wa)),
                  _resident((w, d)), _resident((d, cross_width)),
                  _resident((d, LANES)), _resident((1, LANES)), _resident(place.shape),
                  _resident((1, wa)), _resident(headsum.shape)],
        out_specs=[col(w), col(w)] + [row(n) for n, _ in outs]
        + [pl.BlockSpec((wa, tm), lambda i: (0, i)), pl.BlockSpec(stats_shape, lambda i: (0, 0, 0))],
        out_shape=[jax.ShapeDtypeStruct((t // seq_len, w, seq_len), F32)] * 2
        + [jax.ShapeDtypeStruct((t, n), dt) for n, dt in outs]
        + [jax.ShapeDtypeStruct((wa, t), BF16), jax.ShapeDtypeStruct(stats_shape, F32)],
        scratch_shapes=[pltpu.VMEM((SUBLANES, LANES), F32)],
        compiler_params=_params("arbitrary"),
        name="fox_proj_prompt",
    )(x, g.reshape(1, d), aug(wq).astype(BF16), wk.T.astype(BF16), aug(wk).astype(BF16),
      wv.T.astype(BF16), wc.astype(BF16), wf.astype(BF16), bf, place.astype(BF16), oneq, headsum)


def _s5_discretise(a_re, a_im, log_dt):
    dt = jnp.exp(log_dt)
    mag = jnp.exp(dt * a_re)
    ab_re = mag * jnp.cos(dt * a_im)
    ab_im = mag * jnp.sin(dt * a_im)
    den = a_re * a_re + a_im * a_im
    nr = ab_re - 1.0
    ni = ab_im
    return ab_re, ab_im, (nr * a_re + ni * a_im) / den, (ni * a_re - nr * a_im) / den


def _s5_param_body(are_ref, aim_ref, ldt_ref, arex_ref, aimx_ref, ldtx_ref, bre_ref, bim_ref,
                   bbr_ref, bbi_ref, pwr_ref, pwi_ref):
    _, _, zr, zi = _s5_discretise(arex_ref[...], aimx_ref[...], ldtx_ref[...])
    b_re = bre_ref[...]
    b_im = bim_ref[...]
    bbr_ref[...] = zr * b_re - zi * b_im
    bbi_ref[...] = zr * b_im + zi * b_re
    ab_re, ab_im, _, _ = _s5_discretise(are_ref[...], aim_ref[...], ldt_ref[...])
    pr, pi = ab_re, ab_im
    for r in range(pwr_ref.shape[0]):
        pwr_ref[r] = pr
        pwi_ref[r] = pi
        pr, pi = pr * ab_re - pi * ab_im, pr * ab_im + pi * ab_re


def _s5_params(a_re, a_im, log_dt, b_re, b_im, c_re, c_im, *, n_pow):
    g, p, c = b_re.shape
    rows = g * c
    rep = lambda a: jnp.repeat(a, c, axis=0)
    bt = lambda b: b.transpose(0, 2, 1).reshape(rows, p)
    whole = lambda shape: pl.BlockSpec(shape, lambda: (0,) * len(shape))
    log_dt = log_dt.reshape(g, 1)
    bbr, bbi, pwr, pwi = pl.pallas_call(
        _s5_param_body,
        in_specs=[whole((g, p)), whole((g, p)), whole((g, 1)), whole((rows, p)), whole((rows, p)),
                  whole((rows, 1)), whole((rows, p)), whole((rows, p))],
        out_specs=[whole((rows, p)), whole((rows, p)), whole((n_pow, g, p)), whole((n_pow, g, p))],
        out_shape=[jax.ShapeDtypeStruct((rows, p), F32)] * 2 + [jax.ShapeDtypeStruct((n_pow, g, p), F32)] * 2,
        name="s5_params",
    )(a_re, a_im, log_dt, rep(a_re), rep(a_im), rep(log_dt), bt(b_re), bt(b_im))
    width = g * p
    pw_re, pw_im = pwr.reshape(n_pow, width), pwi.reshape(n_pow, width)
    gpt = MXU_DIM // c
    n_tiles = g // gpt
    eye = jnp.eye(gpt, dtype=F32)

    def in_tiles(bb):
        blocks = bb.reshape(n_tiles, gpt, c, p)
        return jnp.einsum("tgcp,gh->tgchp", blocks, eye).reshape(n_tiles, gpt * c, gpt * p)

    def out_tiles(cc):
        blocks = cc.reshape(n_tiles, gpt, c, p)
        return jnp.einsum("tgcp,gh->tgphc", blocks, eye).reshape(n_tiles, gpt * p, gpt * c)

    w_in = jnp.concatenate([in_tiles(bbr), in_tiles(bbi)], axis=-1).astype(BF16)
    return (pw_re[0:1], pw_im[0:1], pw_re, pw_im, w_in,
            out_tiles(c_re).astype(BF16), out_tiles(c_im).astype(BF16))


def _s5_in_proj(u, wb_ref, bre_ref, bim_ref):
    ub = u.astype(BF16)
    n_tiles, ch, two_w = wb_ref.shape
    w = two_w // 2
    for t in range(n_tiles):
        bu = _dot(ub[:, t * ch:(t + 1) * ch], wb_ref[t])
        bre_ref[:, t * w:(t + 1) * w] = bu[:, :w]
        bim_ref[:, t * w:(t + 1) * w] = bu[:, w:]


def _s5_out(u, hre_ref, him_ref, wcr_ref, wci_ref, d_ref, wglu_ref):
    n_tiles, w, _ = wcr_ref.shape
    ys = []
    for t in range(n_tiles):
        hr = hre_ref[:, t * w:(t + 1) * w].astype(BF16)
        hi = him_ref[:, t * w:(t + 1) * w].astype(BF16)
        ys.append(_dot(hr, wcr_ref[t]) - _dot(hi, wci_ref[t]))
    y = jax.nn.gelu(jnp.concatenate(ys, axis=-1) + d_ref[...] * u)
    z = _dot(y.astype(BF16), wglu_ref[...])
    half = z.shape[-1] // 2
    return z[:, :half] * jax.nn.sigmoid(z[:, half:])


SEGMENT_PITCH_PAD = SUBLANES


def _s5_prompt_body(u_ref, wb_ref, wcr_ref, wci_ref, d_ref, wglu_ref, ab8_ref, aseg_ref,
                    o_ref, hre_ref, him_ref, up_ref, slab_ref, bre_ref, bim_ref, cin_ref, carry_ref,
                    *, lane_group):
    ci = pl.program_id(1)
    rows, width = bre_ref.shape
    ch = up_ref.shape[1]
    seg = rows // SUBLANES
    pitch = slab_ref.shape[1] // SUBLANES
    n_slab = ch // LANES
    lanes = [slice(j * LANES, (j + 1) * LANES) for j in range(n_slab)]
    groups = [slice(g * lane_group, (g + 1) * lane_group) for g in range(width // lane_group)]
    tile = lambda k: pl.ds(pl.multiple_of(k * SUBLANES, SUBLANES), SUBLANES)

    @pl.when(ci == 0)
    def _():
        carry_ref[...] = jnp.zeros_like(carry_ref)

    for s in range(SUBLANES):
        for j, ls in enumerate(lanes):
            slab_ref[j, pitch * s:pitch * s + seg, :] = u_ref[0, seg * s:seg * (s + 1), ls]

    def gather(k, carry):
        for j, ls in enumerate(lanes):
            up_ref[tile(k), ls] = slab_ref[j, pl.ds(k, SUBLANES, stride=pitch), :]
        return carry

    lax.fori_loop(0, seg, gather, 0)
    u = up_ref[...]
    _s5_in_proj(u, wb_ref, bre_ref, bim_ref)

    for ls in groups:
        ar = ab8_ref[0, :, ls]
        ai = ab8_ref[1, :, ls]

        def step(k, carry, ls=ls, ar=ar, ai=ai):
            hr, hi = carry
            hr, hi = ar * hr - ai * hi + bre_ref[tile(k), ls], ar * hi + ai * hr + bim_ref[tile(k), ls]
            bre_ref[tile(k), ls] = hr
            bim_ref[tile(k), ls] = hi
            return hr, hi

        zero = jnp.zeros((SUBLANES, lane_group), F32)
        lax.fori_loop(0, seg, step, (zero, zero))

    er, ei = carry_ref[0, 0:1, :], carry_ref[1, 0:1, :]
    sr, si = aseg_ref[0], aseg_ref[1]
    for s in range(SUBLANES):
        cin_ref[0, s:s + 1, :] = er
        cin_ref[1, s:s + 1, :] = ei
        end = rows - SUBLANES + s
        er, ei = (bre_ref[end:end + 1, :] + sr * er - si * ei, bim_ref[end:end + 1, :] + sr * ei + si * er)
    carry_ref[0, 0:1, :] = er
    carry_ref[1, 0:1, :] = ei

    for ls in groups:
        ar = ab8_ref[0, :, ls]
        ai = ab8_ref[1, :, ls]

        def fix(k, carry, ls=ls, ar=ar, ai=ai):
            gr, gi = carry
            gr, gi = ar * gr - ai * gi, ar * gi + ai * gr
            bre_ref[tile(k), ls] = bre_ref[tile(k), ls] + gr
            bim_ref[tile(k), ls] = bim_ref[tile(k), ls] + gi
            return gr, gi

        lax.fori_loop(0, seg, fix, (cin_ref[0, :, ls], cin_ref[1, :, ls]))

    up_ref[...] = _s5_out(u, bre_ref, bim_ref, wcr_ref, wci_ref, d_ref, wglu_ref)

    def scatter(k, carry):
        for j, ls in enumerate(lanes):
            slab_ref[j, pl.ds(k, SUBLANES, stride=pitch), :] = up_ref[tile(k), ls]
        return carry

    lax.fori_loop(0, seg, scatter, 0)
    for s in range(SUBLANES):
        for j, ls in enumerate(lanes):
            o_ref[0, seg * s:seg * (s + 1), ls] = slab_ref[j, pitch * s:pitch * s + seg, :]

    @pl.when(ci == pl.num_programs(1) - 1)
    def _():
        hre_ref[0] = er
        him_ref[0] = ei


def _s5_prompt(u, params, d_skip, w_glu, *, rows):
    ab_re, ab_im, pw_re, pw_im, w_in, wc_re, wc_im = params
    nb, length, ch = u.shape
    width = ab_re.shape[1]
    seg = rows // SUBLANES
    ab8 = jnp.stack([jnp.broadcast_to(ab_re, (SUBLANES, width)), jnp.broadcast_to(ab_im, (SUBLANES, width))])
    a_seg = jnp.stack([pw_re[seg - 1:seg], pw_im[seg - 1:seg]])
    state = jax.ShapeDtypeStruct((nb, 1, width), F32)
    state_spec = pl.BlockSpec((1, 1, width), lambda b, c: (b, 0, 0))
    tok_spec = pl.BlockSpec((1, rows, ch), lambda b, c: (b, c, 0))
    slab_rows = SUBLANES * (seg + SEGMENT_PITCH_PAD)
    return pl.pallas_call(
        functools.partial(_s5_prompt_body, lane_group=4 * LANES),
        grid=(nb, length // rows),
        in_specs=[tok_spec, _resident(w_in.shape), _resident(wc_re.shape), _resident(wc_im.shape),
                  _resident((1, ch)), _resident(w_glu.shape), _resident(ab8.shape), _resident(a_seg.shape)],
        out_specs=[tok_spec, state_spec, state_spec],
        out_shape=[jax.ShapeDtypeStruct((nb, length, ch), F32), state, state],
        scratch_shapes=[pltpu.VMEM((rows, ch), F32), pltpu.VMEM((ch // LANES, slab_rows, LANES), F32),
                        pltpu.VMEM((rows, width), F32), pltpu.VMEM((rows, width), F32),
                        pltpu.VMEM((2, SUBLANES, width), F32), pltpu.VMEM((2, SUBLANES, width), F32)],
        compiler_params=_params("parallel", "arbitrary"),
        name="s5_prompt",
    )(u, w_in, wc_re, wc_im, d_skip.reshape(1, ch), w_glu.astype(BF16), ab8, a_seg)


def _s5_sample_body(u_ref, h0r_ref, h0i_ref, wb_ref, wcr_ref, wci_ref, d_ref, wglu_ref, ab_ref,
                    o_ref, hre_ref, him_ref, bre_ref, bim_ref):
    hre_ref[...] = h0r_ref[...]
    him_ref[...] = h0i_ref[...]
    ar = ab_ref[0:1, :]
    ai = ab_ref[1:2, :]
    for t in range(u_ref.shape[0]):
        u = u_ref[t]
        _s5_in_proj(u, wb_ref, bre_ref, bim_ref)
        hr = hre_ref[...]
        hi = him_ref[...]
        hre_ref[...] = ar * hr - ai * hi + bre_ref[...]
        him_ref[...] = ar * hi + ai * hr + bim_ref[...]
        o_ref[t] = _s5_out(u, hre_ref, him_ref, wcr_ref, wci_ref, d_ref, wglu_ref)


def _s5_sample(u, h0_re, h0_im, params, d_skip, w_glu):
    ab_re, ab_im, _, _, w_in, wc_re, wc_im = params
    nt, nb, ch = u.shape
    width = ab_re.shape[1]
    ab = jnp.concatenate([ab_re, ab_im], axis=0)
    whole = lambda shape: pl.BlockSpec(shape, lambda: (0,) * len(shape))
    state = jax.ShapeDtypeStruct((nb, width), F32)
    return pl.pallas_call(
        _s5_sample_body,
        in_specs=[whole(u.shape), whole((nb, width)), whole((nb, width)), whole(w_in.shape),
                  whole(wc_re.shape), whole(wc_im.shape), whole((1, ch)), whole(w_glu.shape),
                  whole(ab.shape)],
        out_specs=[whole(u.shape), whole((nb, width)), whole((nb, width))],
        out_shape=[jax.ShapeDtypeStruct(u.shape, F32), state, state],
        scratch_shapes=[pltpu.VMEM((nb, width), F32), pltpu.VMEM((nb, width), F32)],
        compiler_params=pltpu.CompilerParams(vmem_limit_bytes=VMEM_LIMIT_BYTES),
        name="s5_sample",
    )(u, h0_re, h0_im, w_in, wc_re, wc_im, d_skip.reshape(1, ch), w_glu.astype(BF16), ab)


HEADS_PER_STEP = 2


UNDERFLOW_MARGIN = 110.0


def _first_live_block(st_ref, base, n_seq_blk, first_chunk, n_chunk, pair):
    kn = st_ref[1, pl.ds(base, n_seq_blk), :]
    c_last = st_ref[3, pl.ds(base, n_seq_blk), :]
    rows = [kn[0:1]]
    for j in range(1, n_seq_blk):
        rows.append(jnp.maximum(rows[-1], kn[j:j + 1]))
    kn_run = jnp.concatenate(rows, axis=0)
    worst = None
    for c in range(n_chunk):
        r = base + first_chunk + c
        term = st_ref[0, pl.ds(r, 1), :] * (kn_run + st_ref[1, pl.ds(r, 1), :]) + st_ref[2, pl.ds(r, 1), :]
        worst = term if worst is None else jnp.maximum(worst, term)
    blk = lax.broadcasted_iota(jnp.int32, worst.shape, 0)
    lane = lax.broadcasted_iota(jnp.int32, (1, LANES), 1)
    dead = (worst - c_last < -UNDERFLOW_MARGIN) & (blk < first_chunk)
    count = jnp.sum(jnp.where(dead, 1, 0), axis=0, keepdims=True)
    return jnp.min(jnp.where(lane // HEADS_PER_STEP == pair, count, n_seq_blk))


def _fox_prompt_body(q_ref, k_ref, vt_ref, st_ref, o_ref, m_ref, acc_ref, *, head_dim, tk):
    qi = pl.program_id(2)
    tq = q_ref.shape[1]
    n_chunk = tq // tk
    n_seq_blk = k_ref.shape[1] // tk
    first_live = _first_live_block(st_ref, pl.program_id(0) * n_seq_blk, n_seq_blk, qi * n_chunk, n_chunk,
                                   pl.program_id(1))
    m_ref[...] = jnp.full_like(m_ref, NEG_INF)
    acc_ref[...] = jnp.zeros_like(acc_ref)

    def block(ks, first_chunk, masked_chunk):
        chains = [(h, slice(h * LANES, (h + 1) * LANES), c, slice(c * tk, (c + 1) * tk))
                  for h in range(HEADS_PER_STEP) for c in range(first_chunk, n_chunk)]
        scores = []
        for _, hl, c, qs in chains:
            s = _dot_nt(k_ref[0, pl.ds(ks, tk), hl], q_ref[0, qs, hl])
            if c == masked_chunk:
                key = lax.broadcasted_iota(jnp.int32, s.shape, 0)
                qry = lax.broadcasted_iota(jnp.int32, s.shape, 1)
                s = jnp.where(key <= qry, s, NEG_INF)
            scores.append(s)
        probs, alphas = [], []
        for s, (h, _, _, qs) in zip(scores, chains):
            m_old = m_ref[h, :, qs]
            m_new = jnp.maximum(m_old, jnp.max(s, axis=0, keepdims=True))
            alphas.append(jnp.exp(m_old - m_new))
            probs.append(jnp.exp(s - m_new).astype(BF16))
            m_ref[h, :, qs] = m_new
        for p, alpha, (h, hl, _, qs) in zip(probs, alphas, chains):
            acc_ref[h, :, qs] = alpha * acc_ref[h, :, qs] + _dot(vt_ref[hl, pl.ds(ks, tk)], p)

    def body(kj, carry):
        block(pl.multiple_of(kj * tk, tk), 0, None)
        return carry

    lax.fori_loop(first_live, qi * n_chunk, body, 0)
    for c in range(n_chunk):
        block(pl.multiple_of((qi * n_chunk + c) * tk, tk), c, c)
    outs = [acc_ref[h, :head_dim, :] / acc_ref[h, head_dim:head_dim + 1, :] for h in range(HEADS_PER_STEP)]
    o_ref[0] = jnp.concatenate(outs, axis=0).T


def _fox_prompt(qa, ka, vta, stats, *, heads, head_dim, tq, tk):
    nb, length, _ = qa.shape
    wide = HEADS_PER_STEP * LANES
    return pl.pallas_call(
        functools.partial(_fox_prompt_body, head_dim=head_dim, tk=tk),
        grid=(nb, heads // HEADS_PER_STEP, length // tq),
        in_specs=[pl.BlockSpec((1, tq, wide), lambda b, hp, i: (b, i, hp)),
                  pl.BlockSpec((1, length, wide), lambda b, hp, i: (b, 0, hp)),
                  pl.BlockSpec((wide, length), lambda b, hp, i: (hp, b)),
                  pl.BlockSpec(stats.shape, lambda b, hp, i: (0, 0, 0))],
        out_specs=pl.BlockSpec((1, tq, HEADS_PER_STEP * head_dim), lambda b, hp, i: (b, i, hp)),
        out_shape=jax.ShapeDtypeStruct((nb, length, heads * head_dim), F32),
        scratch_shapes=[pltpu.VMEM((HEADS_PER_STEP, 1, tq), F32),
                        pltpu.VMEM((HEADS_PER_STEP, LANES, tq), F32)],
        compiler_params=_params("parallel", "parallel", "arbitrary"),
        name="fox_prompt",
    )(qa, ka, vta, stats)


def _fox_sample_body(pt_ref, q_ref, *refs, pages_per_step, head_dim):
    n = pages_per_step
    kt_refs, vt_refs, lf_refs = refs[:n], refs[n:2 * n], refs[2 * n:3 * n]
    kn_ref, vn_ref, lfn_ref, o_ref, qrow_ref, m_ref, l_ref, acc_ref, carry_ref = refs[3 * n:]
    del pt_ref
    g = pl.program_id(1)
    n_tok, w = q_ref.shape
    hp, page = lf_refs[0].shape
    head_of_lane = lax.broadcasted_iota(jnp.int32, (hp, w), 1) // head_dim
    head_mask = head_of_lane == lax.broadcasted_iota(jnp.int32, (hp, w), 0)
    tri = _upper_tri()

    @pl.when(g == 0)
    def _():
        zero = jnp.zeros((hp, w), F32)
        rows = [jnp.where(head_mask, jnp.broadcast_to(q_ref[t:t + 1, :], (hp, w)), zero)
                for t in range(n_tok)]
        qrow_ref[...] = jnp.concatenate(rows, axis=0).astype(BF16)
        m_ref[...] = jnp.full_like(m_ref, NEG_INF)
        l_ref[...] = jnp.zeros_like(l_ref)
        acc_ref[...] = jnp.zeros_like(acc_ref)
        carry_ref[...] = jnp.zeros_like(carry_ref)

    def update(pages, valid=None):
        scores, base = [], carry_ref[:, 0:1]
        for s, lf, _ in pages:
            local = _cumsum_lanes(lf, tri)
            c = local + base
            base = base + local[:, page - 1:]
            s = s - jnp.concatenate([c] * n_tok, axis=0)
            scores.append(s if valid is None else jnp.where(valid, s, NEG_INF))
        carry_ref[...] = jnp.broadcast_to(base, carry_ref.shape)
        m_old = m_ref[...]
        m_new = functools.reduce(jnp.maximum, [jnp.max(s, axis=1, keepdims=True) for s in scores], m_old)
        alpha = jnp.exp(m_old - m_new)
        probs = [jnp.exp(s - m_new) for s in scores]
        l_ref[...] = alpha * l_ref[...] + sum(jnp.sum(p, axis=1, keepdims=True) for p in probs)
        acc_ref[...] = alpha * acc_ref[...] + sum(pv(p.astype(BF16)) for p, (_, _, pv) in zip(probs, pages))
        m_ref[...] = m_new

    def cached(i):
        kt = kt_refs[i][...].reshape(w, page).astype(BF16)
        vt = vt_refs[i][...].reshape(w, page).astype(BF16)
        return _dot(qrow_ref[...], kt), lf_refs[i][...], lambda p: _dot_nt(p, vt)

    update([cached(i) for i in range(n)])

    @pl.when(g == pl.num_programs(1) - 1)
    def _():
        pad = jnp.zeros((page - kn_ref.shape[0], w), F32)
        kn = jnp.concatenate([kn_ref[...], pad], axis=0).astype(BF16)
        vn = jnp.concatenate([vn_ref[...], pad], axis=0).astype(BF16)
        key = lax.broadcasted_iota(jnp.int32, (n_tok * hp, page), 1)
        tok = lax.broadcasted_iota(jnp.int32, (n_tok * hp, page), 0) // hp
        update([(_dot_nt(qrow_ref[...], kn), lfn_ref[...], lambda p: _dot(p, vn))], key <= tok)
        out = acc_ref[...] / l_ref[...]
        for t in range(n_tok):
            picked = jnp.where(head_mask, out[t * hp:(t + 1) * hp, :], 0.0)
            o_ref[t:t + 1, :] = jnp.sum(picked, axis=0, keepdims=True)


def _fox_sample(q, k_new, v_new, lft_new, cache_kt, cache_vt, cache_lft, page_table, *, layer, pages_per_step):
    ns, n_tok, w = q.shape
    n_pages = page_table.shape[1]
    _, _, heads, head_dim, page = cache_kt.shape
    hp = cache_lft.shape[1]
    n = pages_per_step
    pad_rows = lambda a: jnp.pad(a, ((0, 0), (0, SUBLANES - n_tok), (0, 0)))

    def kv_page(i):
        return pl.BlockSpec((None, None, heads, head_dim, page),
                            lambda s, g, pt: (layer, pt[s, g * n + i], 0, 0, 0))

    def lf_page(i):
        return pl.BlockSpec((None, hp, page), lambda s, g, pt: (pt[s, g * n + i], 0, 0))

    per_seq = lambda shape: pl.BlockSpec((None,) + shape, lambda s, g, pt: (s, 0, 0))
    in_specs = [per_seq((n_tok, w))] + [kv_page(i) for i in range(n)] * 2 + [lf_page(i) for i in range(n)]
    in_specs += [per_seq((SUBLANES, w)), per_seq((SUBLANES, w)), per_seq((hp, page))]
    rows = n_tok * hp
    return pl.pallas_call(
        functools.partial(_fox_sample_body, pages_per_step=n, head_dim=head_dim),
        grid_spec=pltpu.PrefetchScalarGridSpec(
            num_scalar_prefetch=1,
            grid=(ns, n_pages // n),
            in_specs=in_specs,
            out_specs=per_seq((n_tok, w)),
            scratch_shapes=[pltpu.VMEM((rows, w), BF16), pltpu.VMEM((rows, 1), F32),
                            pltpu.VMEM((rows, 1), F32), pltpu.VMEM((rows, w), F32),
                            pltpu.VMEM((hp, page), F32)],
        ),
        out_shape=jax.ShapeDtypeStruct((ns, n_tok, w), F32),
        compiler_params=_params("parallel", "arbitrary"),
        name="fox_sample",
    )(page_table, q, *([cache_kt] * n), *([cache_vt] * n), *([cache_lft] * n),
      pad_rows(k_new), pad_rows(v_new), lft_new)


def _softmax_pv(s, vb):
    m = jnp.max(s, axis=-1, keepdims=True)
    p = jnp.exp(s - m)
    return _dot(p.astype(BF16), vb) / jnp.sum(p, axis=-1, keepdims=True)


def _cross_ffn_prompt_body(*refs, heads, head_dim, ff_chunk, final):
    if final:
        (x_ref, mix_ref, qc_ref, mk_ref, mv_ref, wom_ref, woc_ref, g_ref, wg_ref, wu_ref, wd_ref, gf_ref,
         o_ref, t_ref) = refs
    else:
        (x_ref, mix_ref, qc_ref, mk_ref, mv_ref, wom_ref, woc_ref, g_ref, wg_ref, wu_ref, wd_ref,
         o_ref, t_ref) = refs
    tm, cw = qc_ref.shape
    q = qc_ref[...] * head_dim ** -0.5
    head_of_lane = lax.broadcasted_iota(jnp.int32, (1, cw), 1) // head_dim
    zero = jnp.zeros_like(q)
    q4 = jnp.concatenate([jnp.where(head_of_lane == h, q, zero) for h in range(heads)], axis=0)
    s = _dot_nt(q4.astype(BF16), mk_ref[0].astype(BF16))
    o4 = _softmax_pv(s, mv_ref[0].astype(BF16))
    cross = zero
    for h in range(heads):
        cross = cross + jnp.where(head_of_lane == h, o4[h * tm:(h + 1) * tm, :], zero)
    x = (x_ref[...] + _dot(mix_ref[...].astype(BF16), wom_ref[...])
         + _dot(cross.astype(BF16), woc_ref[...]))
    y = _half_ffn(x, g_ref, wg_ref, wu_ref, wd_ref, t_ref, ff_chunk)
    if final:
        y = _rms(y, gf_ref[...])
    o_ref[...] = y


def _cross_ffn_prompt(x, mix, qc, mk, mv, wom, woc, g, wg, wu, wd, idx, g_final=None, *, heads, head_dim, tm):
    t, d = x.shape
    nb, n_mem, cw = mk.shape
    d_ff = wg.shape[2]
    per_b = t // nb // tm
    final = g_final is not None
    row = lambda n: pl.BlockSpec((tm, n), lambda i: (i, 0))
    mem = pl.BlockSpec((1, n_mem, cw), lambda i: (i // per_b, 0, 0))
    pick = lambda r, c: pl.BlockSpec((None, r, c), lambda i: (idx, 0, 0), pipeline_mode=pl.Buffered(1))
    in_specs = [row(d), row(mix.shape[1]), row(cw), mem, mem, _resident(wom.shape), _resident(woc.shape),
                _resident((1, d)), pick(d, d_ff), pick(d, d_ff), pick(d_ff, d)]
    args = [x, mix, qc, mk, mv, wom, woc, g.reshape(1, d), wg, wu, wd]
    if final:
        in_specs.append(_resident((1, d)))
        args.append(g_final.reshape(1, d))
    return pl.pallas_call(
        functools.partial(_cross_ffn_prompt_body, heads=heads, head_dim=head_dim, ff_chunk=MXU_DIM, final=final),
        grid=(t // tm,),
        in_specs=in_specs,
        out_specs=row(d),
        out_shape=jax.ShapeDtypeStruct((t, d), F32),
        scratch_shapes=[pltpu.VMEM((tm, d_ff), BF16)],
        compiler_params=_params("parallel"),
        name="cross_ffn_prompt",
    )(*args)


def _cross_out_sample_body(x_ref, mix_ref, qc_ref, mkt_ref, mvt_ref, wom_ref, woc_ref, o_ref, cross_ref,
                           *, n_tok):
    n_seq, _, head_dim, n_mem = mkt_ref.shape
    cw = qc_ref.shape[1]
    scale = head_dim ** -0.5
    head_of_lane = lax.broadcasted_iota(jnp.int32, (SUBLANES, cw), 1) // head_dim
    head_mask = head_of_lane == lax.broadcasted_iota(jnp.int32, (SUBLANES, cw), 0)
    zero = jnp.zeros((SUBLANES, cw), F32)
    for i in range(n_seq):
        rows = [jnp.where(head_mask, jnp.broadcast_to(qc_ref[pl.ds(i * n_tok + t, 1), :] * scale,
                                                     (SUBLANES, cw)), zero) for t in range(n_tok)]
        q = jnp.concatenate(rows, axis=0).astype(BF16)
        s = _dot(q, mkt_ref[i].reshape(cw, n_mem).astype(BF16))
        p = jnp.exp(s - jnp.max(s, axis=-1, keepdims=True))
        o = _dot_nt(p.astype(BF16), mvt_ref[i].reshape(cw, n_mem).astype(BF16)) / jnp.sum(p, axis=-1, keepdims=True)
        for t in range(n_tok):
            picked = jnp.where(head_mask, o[t * SUBLANES:(t + 1) * SUBLANES, :], zero)
            cross_ref[pl.ds(i * n_tok + t, 1), :] = jnp.sum(picked, axis=0, keepdims=True)
    o_ref[...] = (x_ref[...] + _dot(mix_ref[...].astype(BF16), wom_ref[...])
                  + _dot(cross_ref[...].astype(BF16), woc_ref[...]))


def _cross_out_sample(x, mix, qc, mkt, mvt, wom, woc, *, layer, n_tok, seqs_per_step):
    t, d = x.shape
    _, ns, heads, head_dim, n_mem = mkt.shape
    cw = heads * head_dim
    tm = seqs_per_step * n_tok
    row = lambda n: pl.BlockSpec((tm, n), lambda i: (i, 0))
    mem = pl.BlockSpec((None, seqs_per_step, heads, head_dim, n_mem), lambda i: (layer, i, 0, 0, 0))
    return pl.pallas_call(
        functools.partial(_cross_out_sample_body, n_tok=n_tok),
        grid=(ns // seqs_per_step,),
        in_specs=[row(d), row(mix.shape[1]), row(cw), mem, mem, _resident(wom.shape),
                  _resident(woc.shape)],
        out_specs=row(d),
        out_shape=jax.ShapeDtypeStruct((t, d), F32),
        scratch_shapes=[pltpu.VMEM((tm, cw), F32)],
        compiler_params=_params("parallel"),
        name="cross_out_sample",
    )(x, mix, qc, mkt, mvt, wom, woc)


def kernel(x_prompt, x_sample, mem_prompt, state_s5_re, state_s5_im, cache_fox_k, cache_fox_v, cache_fox_logf, cache_mem_k, cache_mem_v, page_table, ffn_norm, ffn_w_gate, ffn_w_up, ffn_w_down, norm_mix, norm_mem, w_mem_kv, w_in_s5, s5_a_re, s5_a_im, s5_log_dt, s5_b_re, s5_b_im, s5_c_re, s5_c_im, s5_d, s5_w_glu, w_in_fox, fox_b_f, w_out, norm_final):
    nb, seq, d = x_prompt.shape
    ns, n_tok, _ = x_sample.shape
    depth = ffn_norm.shape[0]
    n_mem = mem_prompt.shape[1]
    cross_heads, head_dim = cache_mem_k.shape[3], cache_mem_k.shape[4]
    cross_width = cross_heads * head_dim
    mixer_width = w_out.shape[1] - cross_width
    fox_heads = cache_fox_k.shape[3]
    n_phys, page = cache_fox_k.shape[1], cache_fox_k.shape[2]
    groups, n_state = state_s5_re.shape[2], state_s5_re.shape[3]
    hp = 2 * SUBLANES

    tm_p = 512
    tm_s = ns * n_tok
    xp = x_prompt.reshape(nb * seq, d)
    xs = x_sample.reshape(tm_s, d)
    mem = mem_prompt.reshape(nb * n_mem, d)

    s5_re_p, s5_im_p, s5_re_s, s5_im_s = [], [], [], []
    fk_p, fv_p, fl_p, fk_s, fv_s, fl_s = [], [], [], [], [], []
    mk_list, mv_list = [], []
    d_ff = ffn_w_gate.shape[-1]
    wg_all = _to_bf16(ffn_w_gate.reshape(2 * depth, d, d_ff), rows=tm_p)
    wu_all = _to_bf16(ffn_w_up.reshape(2 * depth, d, d_ff), rows=tm_p)
    wd_all = _to_bf16(ffn_w_down.reshape(2 * depth, d_ff, d), rows=d_ff // 2)
    for i in range(depth):
        j = i // N_MIXERS
        ffn_w = lambda half: (ffn_norm[i, half], wg_all, wu_all, wd_all, 2 * i + half)
        if i % N_MIXERS == 0:
            w_in = [w_in_s5[j][:, :mixer_width], w_in_s5[j][:, mixer_width:]]
            xp, u_p, qc_p = _ffn_proj(xp, *ffn_w(0), norm_mix[i], w_in, tm=tm_p)
            xs, u_s, qc_s = _ffn_proj(xs, *ffn_w(0), norm_mix[i], w_in, tm=tm_s)
        else:
            xp = _ffn(xp, *ffn_w(0), tm=tm_p)
            xs = _ffn(xs, *ffn_w(0), tm=tm_s)

        mk_p, mv_p = _norm_proj(mem, norm_mem[i],
                                [w_mem_kv[i][:, :cross_width], w_mem_kv[i][:, cross_width:]], tm=nb * n_mem)
        mk_list.append(mk_p.reshape(nb, n_mem, cross_heads, head_dim))
        mv_list.append(mv_p.reshape(nb, n_mem, cross_heads, head_dim))

        if i % N_MIXERS == 0:
            params = _s5_params(s5_a_re[j], s5_a_im[j], s5_log_dt[j], s5_b_re[j], s5_b_im[j],
                                s5_c_re[j], s5_c_im[j], n_pow=tm_p // SUBLANES)
            mix_p, hr_p, hi_p = _s5_prompt(u_p.reshape(nb, seq, mixer_width), params, s5_d[j],
                                           s5_w_glu[j], rows=tm_p)
            mix_p = mix_p.reshape(nb * seq, mixer_width)
            u_t = u_s.reshape(ns, n_tok, mixer_width).transpose(1, 0, 2)
            mix_t, hr_s, hi_s = _s5_sample(u_t, state_s5_re[j].reshape(ns, groups * n_state),
                                           state_s5_im[j].reshape(ns, groups * n_state),
                                           params, s5_d[j], s5_w_glu[j])
            mix_s = mix_t.transpose(1, 0, 2).reshape(tm_s, mixer_width)
            s5_re_p.append(hr_p.reshape(nb, groups, n_state))
            s5_im_p.append(hi_p.reshape(nb, groups, n_state))
            s5_re_s.append(hr_s.reshape(ns, groups, n_state))
            s5_im_s.append(hi_s.reshape(ns, groups, n_state))
        else:
            fox = dict(heads=fox_heads, head_dim=head_dim, cross_width=cross_width)
            kt_p, vt_p, qc_p, lf_p, qa_p, ka_p, vta_p, stats_p = _fox_proj_prompt(
                xp, norm_mix[i], w_in_fox[j], fox_b_f[j], tm=tm_p, tiles_per_seq=seq // tm_p, **fox)
            q_s, k_s, v_s, qc_s, lf_s, lft_s = _fox_proj_sample(
                xs, norm_mix[i], w_in_fox[j], fox_b_f[j], tm=tm_s, **fox)

            seq3 = lambda a: a.reshape(nb, seq, a.shape[-1])
            mix_p = _fox_prompt(seq3(qa_p), seq3(ka_p), vta_p, stats_p, heads=fox_heads, head_dim=head_dim,
                                tq=min(seq, 4 * tm_p), tk=tm_p)
            mix_p = mix_p.reshape(nb * seq, mixer_width)

            tok3 = lambda a: a.reshape(ns, n_tok, mixer_width)
            lft_new = jnp.pad(lft_s.reshape(hp, ns, n_tok).transpose(1, 0, 2),
                              ((0, 0), (0, 0), (0, page - n_tok)))
            cache_lft = jnp.pad(cache_fox_logf[j].transpose(0, 2, 1), ((0, 0), (0, hp - fox_heads), (0, 0)))
            to_stored = lambda a: a.transpose(0, 1, 3, 4, 2)
            mix_s = _fox_sample(tok3(q_s), tok3(k_s), tok3(v_s), lft_new, to_stored(cache_fox_k),
                                to_stored(cache_fox_v), cache_lft, page_table, layer=j,
                                pages_per_step=page_table.shape[1])
            mix_s = mix_s.reshape(tm_s, mixer_width)

            from_stored = lambda a: a.reshape(nb, fox_heads, head_dim, seq).transpose(0, 3, 1, 2)
            fk_p.append(from_stored(kt_p))
            fv_p.append(from_stored(vt_p))
            fl_p.append(lf_p[:, :fox_heads].reshape(nb, seq, fox_heads))
            fk_s.append(k_s.reshape(ns, n_tok, fox_heads, head_dim))
            fv_s.append(v_s.reshape(ns, n_tok, fox_heads, head_dim))
            fl_s.append(lf_s[:, :fox_heads].reshape(ns, n_tok, fox_heads))

        wom = w_out[i][:mixer_width].astype(BF16)
        woc = w_out[i][mixer_width:].astype(BF16)
        g_final = norm_final if i == depth - 1 else None
        xp = _cross_ffn_prompt(xp, mix_p, qc_p, mk_p.reshape(nb, n_mem, cross_width),
                               mv_p.reshape(nb, n_mem, cross_width), wom, woc, *ffn_w(1), g_final,
                               heads=cross_heads, head_dim=head_dim, tm=tm_p)
        xs = _cross_out_sample(xs, mix_s, qc_s, cache_mem_k.transpose(0, 1, 3, 4, 2),
                               cache_mem_v.transpose(0, 1, 3, 4, 2), wom, woc,
                               layer=i, n_tok=n_tok, seqs_per_step=16)
        xs = _ffn(xs, *ffn_w(1), g_final, tm=tm_s)

    return (xp.reshape(nb, seq, d), xs.reshape(ns, n_tok, d),
            jnp.stack(s5_re_p), jnp.stack(s5_im_p), jnp.stack(s5_re_s), jnp.stack(s5_im_s),
            jnp.stack(fk_p), jnp.stack(fv_p), jnp.stack(fl_p),
            jnp.stack(fk_s), jnp.stack(fv_s), jnp.stack(fl_s),
            jnp.stack(mk_list), jnp.stack(mv_list))
```

```python
import functools

import jax
import jax.numpy as jnp
import numpy as np
from jax import lax
from jax.experimental import pallas as pl
from jax.experimental.pallas import tpu as pltpu

F32 = jnp.float32
BF16 = jnp.bfloat16

RMS_EPS = 1e-6
NEG_INF = -1e30
N_MIXERS = 2

LANES = 128
SUBLANES = 8
MXU_DIM = 256
VMEM_LIMIT_BYTES = 56 * 1024 * 1024

NT_DIMS = (((1,), (1,)), ((), ()))


def _params(*sem):
    return pltpu.CompilerParams(dimension_semantics=sem, vmem_limit_bytes=VMEM_LIMIT_BYTES)


def _resident(shape):
    nd = len(shape)
    return pl.BlockSpec(shape, lambda *_: (0,) * nd, pipeline_mode=pl.Buffered(1))


def _rms(x, g):
    return x * lax.rsqrt(jnp.mean(x * x, axis=-1, keepdims=True) + RMS_EPS) * g


def _dot(a, b):
    return jnp.dot(a, b, preferred_element_type=F32)


def _dot_nt(a, b):
    return lax.dot_general(a, b, NT_DIMS, preferred_element_type=F32)


def _log_sigmoid(x):
    return jnp.minimum(x, 0.0) - jnp.log1p(jnp.exp(-jnp.abs(x)))


def _split3(x):
    hi = x.astype(BF16)
    r1 = x - hi.astype(F32)
    mid = r1.astype(BF16)
    lo = (r1 - mid.astype(F32)).astype(BF16)
    return hi, mid, lo


def _cumsum_lanes(x, tri):
    hi, mid, lo = _split3(x)
    return _dot(hi, tri) + _dot(mid, tri) + _dot(lo, tri)


def _upper_tri():
    r = lax.broadcasted_iota(jnp.int32, (LANES, LANES), 0)
    c = lax.broadcasted_iota(jnp.int32, (LANES, LANES), 1)
    return jnp.where(r <= c, 1.0, 0.0).astype(BF16)


def _half_ffn(x, g_ref, wg_ref, wu_ref, wd_ref, t_ref, ff_chunk):
    h = _rms(x, g_ref[...]).astype(BF16)
    for lo in range(0, wg_ref.shape[1], ff_chunk):
        a = _dot(h, wg_ref[:, lo:lo + ff_chunk])
        b = _dot(h, wu_ref[:, lo:lo + ff_chunk])
        t_ref[:, lo:lo + ff_chunk] = (jax.nn.silu(a) * b).astype(BF16)
    return x + 0.5 * _dot(t_ref[...], wd_ref[...])


def _ffn_body(*refs, ff_chunk, final):
    if final:
        x_ref, g_ref, wg_ref, wu_ref, wd_ref, gf_ref, o_ref, t_ref = refs
    else:
        x_ref, g_ref, wg_ref, wu_ref, wd_ref, o_ref, t_ref = refs
    y = _half_ffn(x_ref[...], g_ref, wg_ref, wu_ref, wd_ref, t_ref, ff_chunk)
    if final:
        y = _rms(y, gf_ref[...])
    o_ref[...] = y


def _ffn_proj_body(x_ref, g_ref, wg_ref, wu_ref, wd_ref, gm_ref, *refs, ff_chunk, n):
    w_refs, o_ref, p_refs, t_ref = refs[:n], refs[n], refs[n + 1:2 * n + 1], refs[2 * n + 1]
    y = _half_ffn(x_ref[...], g_ref, wg_ref, wu_ref, wd_ref, t_ref, ff_chunk)
    o_ref[...] = y
    h = _rms(y, gm_ref[...]).astype(BF16)
    for w_ref, p_ref in zip(w_refs, p_refs):
        p_ref[...] = _dot(h, w_ref[...])


def _ffn_proj(x, g, wg, wu, wd, idx, g_mix, ws, *, tm):
    t, d = x.shape
    d_ff = wg.shape[2]
    n = len(ws)
    row = lambda c: pl.BlockSpec((tm, c), lambda i: (i, 0))
    pick = lambda r, c: pl.BlockSpec((None, r, c), lambda i: (idx, 0, 0), pipeline_mode=pl.Buffered(1))
    return pl.pallas_call(
        functools.partial(_ffn_proj_body, ff_chunk=MXU_DIM, n=n),
        grid=(t // tm,),
        in_specs=[row(d), _resident((1, d)), pick(d, d_ff), pick(d, d_ff), pick(d_ff, d), _resident((1, d))]
        + [_resident(w.shape) for w in ws],
        out_specs=[row(d)] + [row(w.shape[1]) for w in ws],
        out_shape=[jax.ShapeDtypeStruct((t, d), F32)] + [jax.ShapeDtypeStruct((t, w.shape[1]), F32) for w in ws],
        scratch_shapes=[pltpu.VMEM((tm, d_ff), BF16)],
        compiler_params=_params("parallel"),
        name="ffn_proj",
    )(x, g.reshape(1, d), wg, wu, wd, g_mix.reshape(1, d), *[w.astype(BF16) for w in ws])


def _cast_body(x_ref, o_ref):
    o_ref[...] = x_ref[...].astype(o_ref.dtype)


def _to_bf16(w, *, rows):
    n, r, c = w.shape
    spec = pl.BlockSpec((None, rows, c), lambda i, j: (i, j, 0))
    return pl.pallas_call(
        _cast_body, grid=(n, r // rows), in_specs=[spec], out_specs=spec,
        out_shape=jax.ShapeDtypeStruct(w.shape, BF16),
        compiler_params=_params("parallel", "parallel"), name="to_bf16",
    )(w)


def _ffn(x, g, wg, wu, wd, idx, g_final=None, *, tm):
    t, d = x.shape
    d_ff = wg.shape[2]
    final = g_final is not None
    row = pl.BlockSpec((tm, d), lambda i: (i, 0))
    pick = lambda r, c: pl.BlockSpec((None, r, c), lambda i: (idx, 0, 0), pipeline_mode=pl.Buffered(1))
    in_specs = [row, _resident((1, d)), pick(d, d_ff), pick(d, d_ff), pick(d_ff, d)]
    args = [x, g.reshape(1, d), wg, wu, wd]
    if final:
        in_specs.append(_resident((1, d)))
        args.append(g_final.reshape(1, d))
    return pl.pallas_call(
        functools.partial(_ffn_body, ff_chunk=MXU_DIM, final=final),
        grid=(t // tm,),
        in_specs=in_specs,
        out_specs=row,
        out_shape=jax.ShapeDtypeStruct((t, d), F32),
        scratch_shapes=[pltpu.VMEM((tm, d_ff), BF16)],
        compiler_params=_params("parallel"),
        name="ffn_final" if final else "ffn",
    )(*args)


def _norm_proj_body(x_ref, g_ref, *refs, n):
    h = _rms(x_ref[...], g_ref[...]).astype(BF16)
    for w_ref, o_ref in zip(refs[:n], refs[n:]):
        o_ref[...] = _dot(h, w_ref[...])


def _norm_proj(x, g, ws, *, tm):
    t, d = x.shape
    n = len(ws)
    in_specs = [pl.BlockSpec((tm, d), lambda i: (i, 0)), _resident((1, d))]
    in_specs += [_resident(w.shape) for w in ws]
    return pl.pallas_call(
        functools.partial(_norm_proj_body, n=n),
        grid=(t // tm,),
        in_specs=in_specs,
        out_specs=[pl.BlockSpec((tm, w.shape[1]), lambda i: (i, 0)) for w in ws],
        out_shape=[jax.ShapeDtypeStruct((t, w.shape[1]), F32) for w in ws],
        compiler_params=_params("parallel"),
        name="norm_proj",
    )(x, g.reshape(1, d), *[w.astype(BF16) for w in ws])


def _fox_split(w_in, b_f, heads, head_dim):
    w = heads * head_dim
    wq, wk, wv = w_in[:, :w], w_in[:, w:2 * w], w_in[:, 2 * w:3 * w]
    wf = jnp.pad(w_in[:, 3 * w:3 * w + heads], ((0, 0), (0, LANES - heads)))
    wc = w_in[:, 3 * w + heads:]
    bf = jnp.pad(b_f, (0, LANES - heads)).reshape(1, LANES)
    return wq, wk, wv, wf, wc, bf


def _fox_proj_sample_body(x_ref, g_ref, wq_ref, wk_ref, wv_ref, wc_ref, wf_ref, wft_ref, bf_ref, bft_ref,
                          q_ref, k_ref, v_ref, qc_ref, lf_ref, lft_ref, *, scale):
    h = _rms(x_ref[...], g_ref[...]).astype(BF16)
    q_ref[...] = _dot(h, wq_ref[...]) * scale
    k_ref[...] = _dot(h, wk_ref[...])
    v_ref[...] = _dot(h, wv_ref[...])
    qc_ref[...] = _dot(h, wc_ref[...])
    lf_ref[...] = _log_sigmoid(_dot(h, wf_ref[...]) + bf_ref[...])
    lft_ref[...] = _log_sigmoid(_dot_nt(wft_ref[...], h) + bft_ref[...])


def _fox_proj_sample(x, g, w_in, b_f, *, heads, head_dim, cross_width, tm):
    t, d = x.shape
    w = heads * head_dim
    hp = 2 * SUBLANES
    wq, wk, wv, wf, wc, bf = _fox_split(w_in, b_f, heads, head_dim)
    wft = wf[:, :hp].T
    bft = bf[0, :hp].reshape(hp, 1)
    row = lambda n: pl.BlockSpec((tm, n), lambda i: (i, 0))
    outs = [w, w, w, cross_width, LANES]
    return pl.pallas_call(
        functools.partial(_fox_proj_sample_body, scale=head_dim ** -0.5),
        grid=(t // tm,),
        in_specs=[row(d), _resident((1, d)), _resident((d, w)), _resident((d, w)),
                  _resident((d, w)), _resident((d, cross_width)), _resident((d, LANES)),
                  _resident((hp, d)), _resident((1, LANES)), _resident((hp, 1))],
        out_specs=[row(n) for n in outs] + [pl.BlockSpec((hp, tm), lambda i: (0, i))],
        out_shape=[jax.ShapeDtypeStruct((t, n), F32) for n in outs]
        + [jax.ShapeDtypeStruct((hp, t), F32)],
        compiler_params=_params("parallel"),
        name="fox_proj_sample",
    )(x, g.reshape(1, d), wq.astype(BF16), wk.astype(BF16), wv.astype(BF16), wc.astype(BF16),
      wf.astype(BF16), wft.astype(BF16), bf, bft)


N_BIAS = 3


NORM_SLACK = 1.02


def _fox_proj_prompt_body(x_ref, g_ref, wqa_ref, wkt_ref, wka_ref, wvt_ref, wc_ref, wf_ref,
                          bf_ref, place_ref, oneq_ref, headsum_ref,
                          kt_ref, vt_ref, qc_ref, lf_ref, qa_ref, ka_ref, vta_ref, stats_ref, carry_ref,
                          *, scale, tiles_per_seq, head_dim):
    tile = pl.program_id(0)

    @pl.when(tile % tiles_per_seq == 0)
    def _():
        carry_ref[...] = jnp.zeros_like(carry_ref)

    def max_norm(x):
        sq = _dot((x * x).astype(BF16), headsum_ref[...])
        return jnp.sqrt(jnp.max(sq, axis=0, keepdims=True)) * NORM_SLACK

    h = _rms(x_ref[...], g_ref[...]).astype(BF16)
    q_aug = _dot(h, wqa_ref[...]) * scale
    qa_ref[...] = (q_aug + oneq_ref[...]).astype(BF16)
    stats_ref[0, pl.ds(tile, 1), :] = max_norm(q_aug)
    kt = _dot_nt(wkt_ref[...], h)
    kt_ref[...] = kt
    head_lane = lax.broadcasted_iota(jnp.int32, (1, LANES), 1)
    k_norm = jnp.zeros((1, LANES), F32)
    for hd in range(kt.shape[0] // head_dim):
        rows = kt[hd * head_dim:(hd + 1) * head_dim, :]
        sq = jnp.max(jnp.sum(rows * rows, axis=0, keepdims=True), axis=1, keepdims=True)
        k_norm = jnp.where(head_lane == hd, jnp.sqrt(sq) * NORM_SLACK, k_norm)
    stats_ref[1, pl.ds(tile, 1), :] = k_norm
    vt = _dot_nt(wvt_ref[...], h)
    vt_ref[...] = vt
    tm = vt.shape[1]
    tail = jnp.where(lax.broadcasted_iota(jnp.int32, (LANES - head_dim, tm), 0) == 0, 1.0, 0.0)
    blocks = []
    for r0 in range(0, vt.shape[0], head_dim):
        blocks += [vt[r0:r0 + head_dim, :], tail]
    vta_ref[...] = jnp.concatenate(blocks, axis=0).astype(BF16)
    qc_ref[...] = _dot(h, wc_ref[...])
    lf = _log_sigmoid(_dot(h, wf_ref[...]) + bf_ref[...])
    lf_ref[...] = lf
    r = lax.broadcasted_iota(jnp.int32, (tm, tm), 0)
    c = lax.broadcasted_iota(jnp.int32, (tm, tm), 1)
    low = jnp.where(c <= r, 1.0, 0.0).astype(BF16)
    cum = carry_ref[0:1, :] + sum(_dot(low, piece) for piece in _split3(lf))
    carry_ref[...] = jnp.broadcast_to(cum[tm - 1:, :], carry_ref.shape)
    bias = sum(_dot(piece, place_ref[j]) for j, piece in enumerate(_split3(-cum)))
    ka_ref[...] = (_dot(h, wka_ref[...]) + bias).astype(BF16)
    stats_ref[2, pl.ds(tile, 1), :] = cum[0:1, :]
    stats_ref[3, pl.ds(tile, 1), :] = cum[tm - 1:, :]


def _fox_proj_prompt(x, g, w_in, b_f, *, heads, head_dim, cross_width, tm, tiles_per_seq):
    t, d = x.shape
    w = heads * head_dim
    wa = heads * LANES
    wq, wk, wv, wf, wc, bf = _fox_split(w_in, b_f, heads, head_dim)
    aug = lambda a: jnp.pad(a.reshape(d, heads, head_dim), ((0, 0), (0, 0), (0, LANES - head_dim))).reshape(d, wa)
    head = np.arange(heads)
    place = np.zeros((N_BIAS, LANES, wa), np.float32)
    oneq = np.zeros((1, wa), np.float32)
    for j in range(N_BIAS):
        place[j, head, head * LANES + head_dim + j] = 1.0
        oneq[0, head * LANES + head_dim + j] = 1.0
    place = jnp.asarray(place)
    headsum = jnp.asarray(np.arange(wa)[:, None] // LANES == np.arange(LANES)[None, :], BF16)
    row = lambda n: pl.BlockSpec((tm, n), lambda i: (i, 0))
    col = lambda n: pl.BlockSpec((None, n, tm), lambda i: (i // tiles_per_seq, 0, i % tiles_per_seq))
    seq_len = tm * tiles_per_seq
    n_tiles = t // tm
    outs = [(cross_width, F32), (LANES, F32), (wa, BF16), (wa, BF16)]
    stats_shape = (4, n_tiles, LANES)
    return pl.pallas_call(
        functools.partial(_fox_proj_prompt_body, scale=head_dim ** -0.5, tiles_per_seq=tiles_per_seq,
                          head_dim=head_dim),
        grid=(n_tiles,),
        in_specs=[row(d), _resident((1, d)), _resident((d, wa)), _resident((w, d)), _resident((d, wa)),
                  _resident((w, d)), _resident((d, cross_width)),
                  _resident((d, LANES)), _resident((1, LANES)), _resident(place.shape),
                  _resident((1, wa)), _resident(headsum.shape)],
        out_specs=[col(w), col(w)] + [row(n) for n, _ in outs]
        + [pl.BlockSpec((wa, tm), lambda i: (0, i)), pl.BlockSpec(stats_shape, lambda i: (0, 0, 0))],
        out_shape=[jax.ShapeDtypeStruct((t // seq_len, w, seq_len), F32)] * 2
        + [jax.ShapeDtypeStruct((t, n), dt) for n, dt in outs]
        + [jax.ShapeDtypeStruct((wa, t), BF16), jax.ShapeDtypeStruct(stats_shape, F32)],
        scratch_shapes=[pltpu.VMEM((SUBLANES, LANES), F32)],
        compiler_params=_params("arbitrary"),
        name="fox_proj_prompt",
    )(x, g.reshape(1, d), aug(wq).astype(BF16), wk.T.astype(BF16), aug(wk).astype(BF16),
      wv.T.astype(BF16), wc.astype(BF16), wf.astype(BF16), bf, place.astype(BF16), oneq, headsum)


def _s5_discretise(a_re, a_im, log_dt):
    dt = jnp.exp(log_dt)
    mag = jnp.exp(dt * a_re)
    ab_re = mag * jnp.cos(dt * a_im)
    ab_im = mag * jnp.sin(dt * a_im)
    den = a_re * a_re + a_im * a_im
    nr = ab_re - 1.0
    ni = ab_im
    return ab_re, ab_im, (nr * a_re + ni * a_im) / den, (ni * a_re - nr * a_im) / den


def _s5_param_body(are_ref, aim_ref, ldt_ref, arex_ref, aimx_ref, ldtx_ref, bre_ref, bim_ref,
                   bbr_ref, bbi_ref, pwr_ref, pwi_ref):
    _, _, zr, zi = _s5_discretise(arex_ref[...], aimx_ref[...], ldtx_ref[...])
    b_re = bre_ref[...]
    b_im = bim_ref[...]
    bbr_ref[...] = zr * b_re - zi * b_im
    bbi_ref[...] = zr * b_im + zi * b_re
    ab_re, ab_im, _, _ = _s5_discretise(are_ref[...], aim_ref[...], ldt_ref[...])
    pr, pi = ab_re, ab_im
    for r in range(pwr_ref.shape[0]):
        pwr_ref[r] = pr
        pwi_ref[r] = pi
        pr, pi = pr * ab_re - pi * ab_im, pr * ab_im + pi * ab_re


def _s5_params(a_re, a_im, log_dt, b_re, b_im, c_re, c_im, *, n_pow):
    g, p, c = b_re.shape
    rows = g * c
    rep = lambda a: jnp.repeat(a, c, axis=0)
    bt = lambda b: b.transpose(0, 2, 1).reshape(rows, p)
    whole = lambda shape: pl.BlockSpec(shape, lambda: (0,) * len(shape))
    log_dt = log_dt.reshape(g, 1)
    bbr, bbi, pwr, pwi = pl.pallas_call(
        _s5_param_body,
        in_specs=[whole((g, p)), whole((g, p)), whole((g, 1)), whole((rows, p)), whole((rows, p)),
                  whole((rows, 1)), whole((rows, p)), whole((rows, p))],
        out_specs=[whole((rows, p)), whole((rows, p)), whole((n_pow, g, p)), whole((n_pow, g, p))],
        out_shape=[jax.ShapeDtypeStruct((rows, p), F32)] * 2 + [jax.ShapeDtypeStruct((n_pow, g, p), F32)] * 2,
        name="s5_params",
    )(a_re, a_im, log_dt, rep(a_re), rep(a_im), rep(log_dt), bt(b_re), bt(b_im))
    width = g * p
    pw_re, pw_im = pwr.reshape(n_pow, width), pwi.reshape(n_pow, width)
    gpt = MXU_DIM // c
    n_tiles = g // gpt
    eye = jnp.eye(gpt, dtype=F32)

    def in_tiles(bb):
        blocks = bb.reshape(n_tiles, gpt, c, p)
        return jnp.einsum("tgcp,gh->tgchp", blocks, eye).reshape(n_tiles, gpt * c, gpt * p)

    def out_tiles(cc):
        blocks = cc.reshape(n_tiles, gpt, c, p)
        return jnp.einsum("tgcp,gh->tgphc", blocks, eye).reshape(n_tiles, gpt * p, gpt * c)

    w_in = jnp.concatenate([in_tiles(bbr), in_tiles(bbi)], axis=-1).astype(BF16)
    return (pw_re[0:1], pw_im[0:1], pw_re, pw_im, w_in,
            out_tiles(c_re).astype(BF16), out_tiles(c_im).astype(BF16))


def _s5_in_proj(u, wb_ref, bre_ref, bim_ref):
    ub = u.astype(BF16)
    n_tiles, ch, two_w = wb_ref.shape
    w = two_w // 2
    for t in range(n_tiles):
        bu = _dot(ub[:, t * ch:(t + 1) * ch], wb_ref[t])
        bre_ref[:, t * w:(t + 1) * w] = bu[:, :w]
        bim_ref[:, t * w:(t + 1) * w] = bu[:, w:]


def _s5_out(u, hre_ref, him_ref, wcr_ref, wci_ref, d_ref, wglu_ref):
    n_tiles, w, _ = wcr_ref.shape
    ys = []
    for t in range(n_tiles):
        hr = hre_ref[:, t * w:(t + 1) * w].astype(BF16)
        hi = him_ref[:, t * w:(t + 1) * w].astype(BF16)
        ys.append(_dot(hr, wcr_ref[t]) - _dot(hi, wci_ref[t]))
    y = jax.nn.gelu(jnp.concatenate(ys, axis=-1) + d_ref[...] * u)
    z = _dot(y.astype(BF16), wglu_ref[...])
    half = z.shape[-1] // 2
    return z[:, :half] * jax.nn.sigmoid(z[:, half:])


SEGMENT_PITCH_PAD = SUBLANES


def _s5_prompt_body(u_ref, wb_ref, wcr_ref, wci_ref, d_ref, wglu_ref, ab8_ref, aseg_ref,
                    o_ref, hre_ref, him_ref, up_ref, slab_ref, bre_ref, bim_ref, cin_ref, carry_ref,
                    *, lane_group):
    ci = pl.program_id(1)
    rows, width = bre_ref.shape
    ch = up_ref.shape[1]
    seg = rows // SUBLANES
    pitch = slab_ref.shape[1] // SUBLANES
    n_slab = ch // LANES
    lanes = [slice(j * LANES, (j + 1) * LANES) for j in range(n_slab)]
    groups = [slice(g * lane_group, (g + 1) * lane_group) for g in range(width // lane_group)]
    tile = lambda k: pl.ds(pl.multiple_of(k * SUBLANES, SUBLANES), SUBLANES)

    @pl.when(ci == 0)
    def _():
        carry_ref[...] = jnp.zeros_like(carry_ref)

    for s in range(SUBLANES):
        for j, ls in enumerate(lanes):
            slab_ref[j, pitch * s:pitch * s + seg, :] = u_ref[0, seg * s:seg * (s + 1), ls]

    def gather(k, carry):
        for j, ls in enumerate(lanes):
            up_ref[tile(k), ls] = slab_ref[j, pl.ds(k, SUBLANES, stride=pitch), :]
        return carry

    lax.fori_loop(0, seg, gather, 0)
    u = up_ref[...]
    _s5_in_proj(u, wb_ref, bre_ref, bim_ref)

    for ls in groups:
        ar = ab8_ref[0, :, ls]
        ai = ab8_ref[1, :, ls]

        def step(k, carry, ls=ls, ar=ar, ai=ai):
            hr, hi = carry
            hr, hi = ar * hr - ai * hi + bre_ref[tile(k), ls], ar * hi + ai * hr + bim_ref[tile(k), ls]
            bre_ref[tile(k), ls] = hr
            bim_ref[tile(k), ls] = hi
            return hr, hi

        zero = jnp.zeros((SUBLANES, lane_group), F32)
        lax.fori_loop(0, seg, step, (zero, zero))

    er, ei = carry_ref[0, 0:1, :], carry_ref[1, 0:1, :]
    sr, si = aseg_ref[0], aseg_ref[1]
    for s in range(SUBLANES):
        cin_ref[0, s:s + 1, :] = er
        cin_ref[1, s:s + 1, :] = ei
        end = rows - SUBLANES + s
        er, ei = (bre_ref[end:end + 1, :] + sr * er - si * ei, bim_ref[end:end + 1, :] + sr * ei + si * er)
    carry_ref[0, 0:1, :] = er
    carry_ref[1, 0:1, :] = ei

    for ls in groups:
        ar = ab8_ref[0, :, ls]
        ai = ab8_ref[1, :, ls]

        def fix(k, carry, ls=ls, ar=ar, ai=ai):
            gr, gi = carry
            gr, gi = ar * gr - ai * gi, ar * gi + ai * gr
            bre_ref[tile(k), ls] = bre_ref[tile(k), ls] + gr
            bim_ref[tile(k), ls] = bim_ref[tile(k), ls] + gi
            return gr, gi

        lax.fori_loop(0, seg, fix, (cin_ref[0, :, ls], cin_ref[1, :, ls]))

    up_ref[...] = _s5_out(u, bre_ref, bim_ref, wcr_ref, wci_ref, d_ref, wglu_ref)

    def scatter(k, carry):
        for j, ls in enumerate(lanes):
            slab_ref[j, pl.ds(k, SUBLANES, stride=pitch), :] = up_ref[tile(k), ls]
        return carry

    lax.fori_loop(0, seg, scatter, 0)
    for s in range(SUBLANES):
        for j, ls in enumerate(lanes):
            o_ref[0, seg * s:seg * (s + 1), ls] = slab_ref[j, pitch * s:pitch * s + seg, :]

    @pl.when(ci == pl.num_programs(1) - 1)
    def _():
        hre_ref[0] = er
        him_ref[0] = ei


def _s5_prompt(u, params, d_skip, w_glu, *, rows):
    ab_re, ab_im, pw_re, pw_im, w_in, wc_re, wc_im = params
    nb, length, ch = u.shape
    width = ab_re.shape[1]
    seg = rows // SUBLANES
    ab8 = jnp.stack([jnp.broadcast_to(ab_re, (SUBLANES, width)), jnp.broadcast_to(ab_im, (SUBLANES, width))])
    a_seg = jnp.stack([pw_re[seg - 1:seg], pw_im[seg - 1:seg]])
    state = jax.ShapeDtypeStruct((nb, 1, width), F32)
    state_spec = pl.BlockSpec((1, 1, width), lambda b, c: (b, 0, 0))
    tok_spec = pl.BlockSpec((1, rows, ch), lambda b, c: (b, c, 0))
    slab_rows = SUBLANES * (seg + SEGMENT_PITCH_PAD)
    return pl.pallas_call(
        functools.partial(_s5_prompt_body, lane_group=8 * LANES),
        grid=(nb, length // rows),
        in_specs=[tok_spec, _resident(w_in.shape), _resident(wc_re.shape), _resident(wc_im.shape),
                  _resident((1, ch)), _resident(w_glu.shape), _resident(ab8.shape), _resident(a_seg.shape)],
        out_specs=[tok_spec, state_spec, state_spec],
        out_shape=[jax.ShapeDtypeStruct((nb, length, ch), F32), state, state],
        scratch_shapes=[pltpu.VMEM((rows, ch), F32), pltpu.VMEM((ch // LANES, slab_rows, LANES), F32),
                        pltpu.VMEM((rows, width), F32), pltpu.VMEM((rows, width), F32),
                        pltpu.VMEM((2, SUBLANES, width), F32), pltpu.VMEM((2, SUBLANES, width), F32)],
        compiler_params=_params("parallel", "arbitrary"),
        name="s5_prompt",
    )(u, w_in, wc_re, wc_im, d_skip.reshape(1, ch), w_glu.astype(BF16), ab8, a_seg)


def _s5_sample_body(u_ref, h0r_ref, h0i_ref, wb_ref, wcr_ref, wci_ref, d_ref, wglu_ref, ab_ref,
                    o_ref, hre_ref, him_ref, bre_ref, bim_ref):
    hre_ref[...] = h0r_ref[...]
    him_ref[...] = h0i_ref[...]
    ar = ab_ref[0:1, :]
    ai = ab_ref[1:2, :]
    for t in range(u_ref.shape[0]):
        u = u_ref[t]
        _s5_in_proj(u, wb_ref, bre_ref, bim_ref)
        hr = hre_ref[...]
        hi = him_ref[...]
        hre_ref[...] = ar * hr - ai * hi + bre_ref[...]
        him_ref[...] = ar * hi + ai * hr + bim_ref[...]
        o_ref[t] = _s5_out(u, hre_ref, him_ref, wcr_ref, wci_ref, d_ref, wglu_ref)


def _s5_sample(u, h0_re, h0_im, params, d_skip, w_glu):
    ab_re, ab_im, _, _, w_in, wc_re, wc_im = params
    nt, nb, ch = u.shape
    width = ab_re.shape[1]
    ab = jnp.concatenate([ab_re, ab_im], axis=0)
    whole = lambda shape: pl.BlockSpec(shape, lambda: (0,) * len(shape))
    state = jax.ShapeDtypeStruct((nb, width), F32)
    return pl.pallas_call(
        _s5_sample_body,
        in_specs=[whole(u.shape), whole((nb, width)), whole((nb, width)), whole(w_in.shape),
                  whole(wc_re.shape), whole(wc_im.shape), whole((1, ch)), whole(w_glu.shape),
                  whole(ab.shape)],
        out_specs=[whole(u.shape), whole((nb, width)), whole((nb, width))],
        out_shape=[jax.ShapeDtypeStruct(u.shape, F32), state, state],
        scratch_shapes=[pltpu.VMEM((nb, width), F32), pltpu.VMEM((nb, width), F32)],
        compiler_params=pltpu.CompilerParams(vmem_limit_bytes=VMEM_LIMIT_BYTES),
        name="s5_sample",
    )(u, h0_re, h0_im, w_in, wc_re, wc_im, d_skip.reshape(1, ch), w_glu.astype(BF16), ab)


HEADS_PER_STEP = 2


UNDERFLOW_MARGIN = 110.0


def _first_live_block(st_ref, base, n_seq_blk, first_chunk, n_chunk, pair):
    kn = st_ref[1, pl.ds(base, n_seq_blk), :]
    c_last = st_ref[3, pl.ds(base, n_seq_blk), :]
    rows = [kn[0:1]]
    for j in range(1, n_seq_blk):
        rows.append(jnp.maximum(rows[-1], kn[j:j + 1]))
    kn_run = jnp.concatenate(rows, axis=0)
    worst = None
    for c in range(n_chunk):
        r = base + first_chunk + c
        term = st_ref[0, pl.ds(r, 1), :] * (kn_run + st_ref[1, pl.ds(r, 1), :]) + st_ref[2, pl.ds(r, 1), :]
        worst = term if worst is None else jnp.maximum(worst, term)
    blk = lax.broadcasted_iota(jnp.int32, worst.shape, 0)
    lane = lax.broadcasted_iota(jnp.int32, (1, LANES), 1)
    dead = (worst - c_last < -UNDERFLOW_MARGIN) & (blk < first_chunk)
    count = jnp.sum(jnp.where(dead, 1, 0), axis=0, keepdims=True)
    return jnp.min(jnp.where(lane // HEADS_PER_STEP == pair, count, n_seq_blk))


def _fox_prompt_body(q_ref, k_ref, vt_ref, st_ref, o_ref, m_ref, acc_ref, *, head_dim, tk):
    qi = pl.program_id(2)
    tq = q_ref.shape[1]
    n_chunk = tq // tk
    n_seq_blk = k_ref.shape[1] // tk
    first_live = _first_live_block(st_ref, pl.program_id(0) * n_seq_blk, n_seq_blk, qi * n_chunk, n_chunk,
                                   pl.program_id(1))
    m_ref[...] = jnp.full_like(m_ref, NEG_INF)
    acc_ref[...] = jnp.zeros_like(acc_ref)

    def block(ks, first_chunk, masked_chunk):
        chains = [(h, slice(h * LANES, (h + 1) * LANES), c, slice(c * tk, (c + 1) * tk))
                  for h in range(HEADS_PER_STEP) for c in range(first_chunk, n_chunk)]
        scores = []
        for _, hl, c, qs in chains:
            s = _dot_nt(k_ref[0, pl.ds(ks, tk), hl], q_ref[0, qs, hl])
            if c == masked_chunk:
                key = lax.broadcasted_iota(jnp.int32, s.shape, 0)
                qry = lax.broadcasted_iota(jnp.int32, s.shape, 1)
                s = jnp.where(key <= qry, s, NEG_INF)
            scores.append(s)
        probs, alphas = [], []
        for s, (h, _, _, qs) in zip(scores, chains):
            m_old = m_ref[h, :, qs]
            m_new = jnp.maximum(m_old, jnp.max(s, axis=0, keepdims=True))
            alphas.append(jnp.exp(m_old - m_new))
            probs.append(jnp.exp(s - m_new).astype(BF16))
            m_ref[h, :, qs] = m_new
        for p, alpha, (h, hl, _, qs) in zip(probs, alphas, chains):
            acc_ref[h, :, qs] = alpha * acc_ref[h, :, qs] + _dot(vt_ref[hl, pl.ds(ks, tk)], p)

    def body(kj, carry):
        block(pl.multiple_of(kj * tk, tk), 0, None)
        return carry

    lax.fori_loop(first_live, qi * n_chunk, body, 0)
    for c in range(n_chunk):
        block(pl.multiple_of((qi * n_chunk + c) * tk, tk), c, c)
    outs = [acc_ref[h, :head_dim, :] / acc_ref[h, head_dim:head_dim + 1, :] for h in range(HEADS_PER_STEP)]
    o_ref[0] = jnp.concatenate(outs, axis=0).T


def _fox_prompt(qa, ka, vta, stats, *, heads, head_dim, tq, tk):
    nb, length, _ = qa.shape
    wide = HEADS_PER_STEP * LANES
    return pl.pallas_call(
        functools.partial(_fox_prompt_body, head_dim=head_dim, tk=tk),
        grid=(nb, heads // HEADS_PER_STEP, length // tq),
        in_specs=[pl.BlockSpec((1, tq, wide), lambda b, hp, i: (b, i, hp)),
                  pl.BlockSpec((1, length, wide), lambda b, hp, i: (b, 0, hp)),
                  pl.BlockSpec((wide, length), lambda b, hp, i: (hp, b)),
                  pl.BlockSpec(stats.shape, lambda b, hp, i: (0, 0, 0))],
        out_specs=pl.BlockSpec((1, tq, HEADS_PER_STEP * head_dim), lambda b, hp, i: (b, i, hp)),
        out_shape=jax.ShapeDtypeStruct((nb, length, heads * head_dim), F32),
        scratch_shapes=[pltpu.VMEM((HEADS_PER_STEP, 1, tq), F32),
                        pltpu.VMEM((HEADS_PER_STEP, LANES, tq), F32)],
        compiler_params=_params("parallel", "parallel", "arbitrary"),
        name="fox_prompt",
    )(qa, ka, vta, stats)


def _fox_sample_body(pt_ref, q_ref, *refs, pages_per_step, head_dim):
    n = pages_per_step
    kt_refs, vt_refs, lf_refs = refs[:n], refs[n:2 * n], refs[2 * n:3 * n]
    kn_ref, vn_ref, lfn_ref, o_ref, qrow_ref, m_ref, l_ref, acc_ref, carry_ref = refs[3 * n:]
    del pt_ref
    g = pl.program_id(1)
    n_tok, w = q_ref.shape
    hp, page = lf_refs[0].shape
    head_of_lane = lax.broadcasted_iota(jnp.int32, (hp, w), 1) // head_dim
    head_mask = head_of_lane == lax.broadcasted_iota(jnp.int32, (hp, w), 0)
    tri = _upper_tri()

    @pl.when(g == 0)
    def _():
        zero = jnp.zeros((hp, w), F32)
        rows = [jnp.where(head_mask, jnp.broadcast_to(q_ref[t:t + 1, :], (hp, w)), zero)
                for t in range(n_tok)]
        qrow_ref[...] = jnp.concatenate(rows, axis=0).astype(BF16)
        m_ref[...] = jnp.full_like(m_ref, NEG_INF)
        l_ref[...] = jnp.zeros_like(l_ref)
        acc_ref[...] = jnp.zeros_like(acc_ref)
        carry_ref[...] = jnp.zeros_like(carry_ref)

    def update(pages, valid=None):
        scores, base = [], carry_ref[:, 0:1]
        for s, lf, _ in pages:
            local = _cumsum_lanes(lf, tri)
            c = local + base
            base = base + local[:, page - 1:]
            s = s - jnp.concatenate([c] * n_tok, axis=0)
            scores.append(s if valid is None else jnp.where(valid, s, NEG_INF))
        carry_ref[...] = jnp.broadcast_to(base, carry_ref.shape)
        m_old = m_ref[...]
        m_new = functools.reduce(jnp.maximum, [jnp.max(s, axis=1, keepdims=True) for s in scores], m_old)
        alpha = jnp.exp(m_old - m_new)
        probs = [jnp.exp(s - m_new) for s in scores]
        l_ref[...] = alpha * l_ref[...] + sum(jnp.sum(p, axis=1, keepdims=True) for p in probs)
        acc_ref[...] = alpha * acc_ref[...] + sum(pv(p.astype(BF16)) for p, (_, _, pv) in zip(probs, pages))
        m_ref[...] = m_new

    def cached(i):
        kt = kt_refs[i][...].reshape(w, page).astype(BF16)
        vt = vt_refs[i][...].reshape(w, page).astype(BF16)
        return _dot(qrow_ref[...], kt), lf_refs[i][...], lambda p: _dot_nt(p, vt)

    update([cached(i) for i in range(n)])

    @pl.when(g == pl.num_programs(1) - 1)
    def _():
        pad = jnp.zeros((page - kn_ref.shape[0], w), F32)
        kn = jnp.concatenate([kn_ref[...], pad], axis=0).astype(BF16)
        vn = jnp.concatenate([vn_ref[...], pad], axis=0).astype(BF16)
        key = lax.broadcasted_iota(jnp.int32, (n_tok * hp, page), 1)
        tok = lax.broadcasted_iota(jnp.int32, (n_tok * hp, page), 0) // hp
        update([(_dot_nt(qrow_ref[...], kn), lfn_ref[...], lambda p: _dot(p, vn))], key <= tok)
        out = acc_ref[...] / l_ref[...]
        for t in range(n_tok):
            picked = jnp.where(head_mask, out[t * hp:(t + 1) * hp, :], 0.0)
            o_ref[t:t + 1, :] = jnp.sum(picked, axis=0, keepdims=True)


def _fox_sample(q, k_new, v_new, lft_new, cache_kt, cache_vt, cache_lft, page_table, *, layer, pages_per_step):
    ns, n_tok, w = q.shape
    n_pages = page_table.shape[1]
    _, _, heads, head_dim, page = cache_kt.shape
    hp = cache_lft.shape[1]
    n = pages_per_step
    pad_rows = lambda a: jnp.pad(a, ((0, 0), (0, SUBLANES - n_tok), (0, 0)))

    def kv_page(i):
        return pl.BlockSpec((None, None, heads, head_dim, page),
                            lambda s, g, pt: (layer, pt[s, g * n + i], 0, 0, 0))

    def lf_page(i):
        return pl.BlockSpec((None, hp, page), lambda s, g, pt: (pt[s, g * n + i], 0, 0))

    per_seq = lambda shape: pl.BlockSpec((None,) + shape, lambda s, g, pt: (s, 0, 0))
    in_specs = [per_seq((n_tok, w))] + [kv_page(i) for i in range(n)] * 2 + [lf_page(i) for i in range(n)]
    in_specs += [per_seq((SUBLANES, w)), per_seq((SUBLANES, w)), per_seq((hp, page))]
    rows = n_tok * hp
    return pl.pallas_call(
        functools.partial(_fox_sample_body, pages_per_step=n, head_dim=head_dim),
        grid_spec=pltpu.PrefetchScalarGridSpec(
            num_scalar_prefetch=1,
            grid=(ns, n_pages // n),
            in_specs=in_specs,
            out_specs=per_seq((n_tok, w)),
            scratch_shapes=[pltpu.VMEM((rows, w), BF16), pltpu.VMEM((rows, 1), F32),
                            pltpu.VMEM((rows, 1), F32), pltpu.VMEM((rows, w), F32),
                            pltpu.VMEM((hp, page), F32)],
        ),
        out_shape=jax.ShapeDtypeStruct((ns, n_tok, w), F32),
        compiler_params=_params("parallel", "arbitrary"),
        name="fox_sample",
    )(page_table, q, *([cache_kt] * n), *([cache_vt] * n), *([cache_lft] * n),
      pad_rows(k_new), pad_rows(v_new), lft_new)


def _softmax_pv(s, vb):
    m = jnp.max(s, axis=-1, keepdims=True)
    p = jnp.exp(s - m)
    return _dot(p.astype(BF16), vb) / jnp.sum(p, axis=-1, keepdims=True)


def _cross_ffn_prompt_body(*refs, heads, head_dim, ff_chunk, final):
    if final:
        (x_ref, mix_ref, qc_ref, mk_ref, mv_ref, wom_ref, woc_ref, g_ref, wg_ref, wu_ref, wd_ref, gf_ref,
         o_ref, t_ref) = refs
    else:
        (x_ref, mix_ref, qc_ref, mk_ref, mv_ref, wom_ref, woc_ref, g_ref, wg_ref, wu_ref, wd_ref,
         o_ref, t_ref) = refs
    tm, cw = qc_ref.shape
    q = qc_ref[...] * head_dim ** -0.5
    head_of_lane = lax.broadcasted_iota(jnp.int32, (1, cw), 1) // head_dim
    zero = jnp.zeros_like(q)
    q4 = jnp.concatenate([jnp.where(head_of_lane == h, q, zero) for h in range(heads)], axis=0)
    s = _dot_nt(q4.astype(BF16), mk_ref[0].astype(BF16))
    o4 = _softmax_pv(s, mv_ref[0].astype(BF16))
    cross = zero
    for h in range(heads):
        cross = cross + jnp.where(head_of_lane == h, o4[h * tm:(h + 1) * tm, :], zero)
    x = (x_ref[...] + _dot(mix_ref[...].astype(BF16), wom_ref[...])
         + _dot(cross.astype(BF16), woc_ref[...]))
    y = _half_ffn(x, g_ref, wg_ref, wu_ref, wd_ref, t_ref, ff_chunk)
    if final:
        y = _rms(y, gf_ref[...])
    o_ref[...] = y


def _cross_ffn_prompt(x, mix, qc, mk, mv, wom, woc, g, wg, wu, wd, idx, g_final=None, *, heads, head_dim, tm):
    t, d = x.shape
    nb, n_mem, cw = mk.shape
    d_ff = wg.shape[2]
    per_b = t // nb // tm
    final = g_final is not None
    row = lambda n: pl.BlockSpec((tm, n), lambda i: (i, 0))
    mem = pl.BlockSpec((1, n_mem, cw), lambda i: (i // per_b, 0, 0))
    pick = lambda r, c: pl.BlockSpec((None, r, c), lambda i: (idx, 0, 0), pipeline_mode=pl.Buffered(1))
    in_specs = [row(d), row(mix.shape[1]), row(cw), mem, mem, _resident(wom.shape), _resident(woc.shape),
                _resident((1, d)), pick(d, d_ff), pick(d, d_ff), pick(d_ff, d)]
    args = [x, mix, qc, mk, mv, wom, woc, g.reshape(1, d), wg, wu, wd]
    if final:
        in_specs.append(_resident((1, d)))
        args.append(g_final.reshape(1, d))
    return pl.pallas_call(
        functools.partial(_cross_ffn_prompt_body, heads=heads, head_dim=head_dim, ff_chunk=MXU_DIM, final=final),
        grid=(t // tm,),
        in_specs=in_specs,
        out_specs=row(d),
        out_shape=jax.ShapeDtypeStruct((t, d), F32),
        scratch_shapes=[pltpu.VMEM((tm, d_ff), BF16)],
        compiler_params=_params("parallel"),
        name="cross_ffn_prompt",
    )(*args)


def _cross_out_sample_body(x_ref, mix_ref, qc_ref, mkt_ref, mvt_ref, wom_ref, woc_ref, o_ref, cross_ref,
                           *, n_tok):
    n_seq, _, head_dim, n_mem = mkt_ref.shape
    cw = qc_ref.shape[1]
    scale = head_dim ** -0.5
    head_of_lane = lax.broadcasted_iota(jnp.int32, (SUBLANES, cw), 1) // head_dim
    head_mask = head_of_lane == lax.broadcasted_iota(jnp.int32, (SUBLANES, cw), 0)
    zero = jnp.zeros((SUBLANES, cw), F32)
    for i in range(n_seq):
        rows = [jnp.where(head_mask, jnp.broadcast_to(qc_ref[pl.ds(i * n_tok + t, 1), :] * scale,
                                                     (SUBLANES, cw)), zero) for t in range(n_tok)]
        q = jnp.concatenate(rows, axis=0).astype(BF16)
        s = _dot(q, mkt_ref[i].reshape(cw, n_mem).astype(BF16))
        p = jnp.exp(s - jnp.max(s, axis=-1, keepdims=True))
        o = _dot_nt(p.astype(BF16), mvt_ref[i].reshape(cw, n_mem).astype(BF16)) / jnp.sum(p, axis=-1, keepdims=True)
        for t in range(n_tok):
            picked = jnp.where(head_mask, o[t * SUBLANES:(t + 1) * SUBLANES, :], zero)
            cross_ref[pl.ds(i * n_tok + t, 1), :] = jnp.sum(picked, axis=0, keepdims=True)
    o_ref[...] = (x_ref[...] + _dot(mix_ref[...].astype(BF16), wom_ref[...])
                  + _dot(cross_ref[...].astype(BF16), woc_ref[...]))


def _cross_out_sample(x, mix, qc, mkt, mvt, wom, woc, *, layer, n_tok, seqs_per_step):
    t, d = x.shape
    _, ns, heads, head_dim, n_mem = mkt.shape
    cw = heads * head_dim
    tm = seqs_per_step * n_tok
    row = lambda n: pl.BlockSpec((tm, n), lambda i: (i, 0))
    mem = pl.BlockSpec((None, seqs_per_step, heads, head_dim, n_mem), lambda i: (layer, i, 0, 0, 0))
    return pl.pallas_call(
        functools.partial(_cross_out_sample_body, n_tok=n_tok),
        grid=(ns // seqs_per_step,),
        in_specs=[row(d), row(mix.shape[1]), row(cw), mem, mem, _resident(wom.shape),
                  _resident(woc.shape)],
        out_specs=row(d),
        out_shape=jax.ShapeDtypeStruct((t, d), F32),
        scratch_shapes=[pltpu.VMEM((tm, cw), F32)],
        compiler_params=_params("parallel"),
        name="cross_out_sample",
    )(x, mix, qc, mkt, mvt, wom, woc)


def kernel(x_prompt, x_sample, mem_prompt, state_s5_re, state_s5_im, cache_fox_k, cache_fox_v, cache_fox_logf, cache_mem_k, cache_mem_v, page_table, ffn_norm, ffn_w_gate, ffn_w_up, ffn_w_down, norm_mix, norm_mem, w_mem_kv, w_in_s5, s5_a_re, s5_a_im, s5_log_dt, s5_b_re, s5_b_im, s5_c_re, s5_c_im, s5_d, s5_w_glu, w_in_fox, fox_b_f, w_out, norm_final):
    nb, seq, d = x_prompt.shape
    ns, n_tok, _ = x_sample.shape
    depth = ffn_norm.shape[0]
    n_mem = mem_prompt.shape[1]
    cross_heads, head_dim = cache_mem_k.shape[3], cache_mem_k.shape[4]
    cross_width = cross_heads * head_dim
    mixer_width = w_out.shape[1] - cross_width
    fox_heads = cache_fox_k.shape[3]
    n_phys, page = cache_fox_k.shape[1], cache_fox_k.shape[2]
    groups, n_state = state_s5_re.shape[2], state_s5_re.shape[3]
    hp = 2 * SUBLANES

    tm_p = 512
    tm_s = ns * n_tok
    xp = x_prompt.reshape(nb * seq, d)
    xs = x_sample.reshape(tm_s, d)
    mem = mem_prompt.reshape(nb * n_mem, d)

    s5_re_p, s5_im_p, s5_re_s, s5_im_s = [], [], [], []
    fk_p, fv_p, fl_p, fk_s, fv_s, fl_s = [], [], [], [], [], []
    mk_list, mv_list = [], []
    d_ff = ffn_w_gate.shape[-1]
    wg_all = _to_bf16(ffn_w_gate.reshape(2 * depth, d, d_ff), rows=tm_p)
    wu_all = _to_bf16(ffn_w_up.reshape(2 * depth, d, d_ff), rows=tm_p)
    wd_all = _to_bf16(ffn_w_down.reshape(2 * depth, d_ff, d), rows=d_ff // 2)
    for i in range(depth):
        j = i // N_MIXERS
        ffn_w = lambda half: (ffn_norm[i, half], wg_all, wu_all, wd_all, 2 * i + half)
        if i % N_MIXERS == 0:
            w_in = [w_in_s5[j][:, :mixer_width], w_in_s5[j][:, mixer_width:]]
            xp, u_p, qc_p = _ffn_proj(xp, *ffn_w(0), norm_mix[i], w_in, tm=tm_p)
            xs, u_s, qc_s = _ffn_proj(xs, *ffn_w(0), norm_mix[i], w_in, tm=tm_s)
        else:
            xp = _ffn(xp, *ffn_w(0), tm=tm_p)
            xs = _ffn(xs, *ffn_w(0), tm=tm_s)

        mk_p, mv_p = _norm_proj(mem, norm_mem[i],
                                [w_mem_kv[i][:, :cross_width], w_mem_kv[i][:, cross_width:]], tm=nb * n_mem)
        mk_list.append(mk_p.reshape(nb, n_mem, cross_heads, head_dim))
        mv_list.append(mv_p.reshape(nb, n_mem, cross_heads, head_dim))

        if i % N_MIXERS == 0:
            params = _s5_params(s5_a_re[j], s5_a_im[j], s5_log_dt[j], s5_b_re[j], s5_b_im[j],
                                s5_c_re[j], s5_c_im[j], n_pow=tm_p // SUBLANES)
            mix_p, hr_p, hi_p = _s5_prompt(u_p.reshape(nb, seq, mixer_width), params, s5_d[j],
                                           s5_w_glu[j], rows=tm_p)
            mix_p = mix_p.reshape(nb * seq, mixer_width)
            u_t = u_s.reshape(ns, n_tok, mixer_width).transpose(1, 0, 2)
            mix_t, hr_s, hi_s = _s5_sample(u_t, state_s5_re[j].reshape(ns, groups * n_state),
                                           state_s5_im[j].reshape(ns, groups * n_state),
                                           params, s5_d[j], s5_w_glu[j])
            mix_s = mix_t.transpose(1, 0, 2).reshape(tm_s, mixer_width)
            s5_re_p.append(hr_p.reshape(nb, groups, n_state))
            s5_im_p.append(hi_p.reshape(nb, groups, n_state))
            s5_re_s.append(hr_s.reshape(ns, groups, n_state))
            s5_im_s.append(hi_s.reshape(ns, groups, n_state))
        else:
            fox = dict(heads=fox_heads, head_dim=head_dim, cross_width=cross_width)
            kt_p, vt_p, qc_p, lf_p, qa_p, ka_p, vta_p, stats_p = _fox_proj_prompt(
                xp, norm_mix[i], w_in_fox[j], fox_b_f[j], tm=tm_p, tiles_per_seq=seq // tm_p, **fox)
            q_s, k_s, v_s, qc_s, lf_s, lft_s = _fox_proj_sample(
                xs, norm_mix[i], w_in_fox[j], fox_b_f[j], tm=tm_s, **fox)

            seq3 = lambda a: a.reshape(nb, seq, a.shape[-1])
            mix_p = _fox_prompt(seq3(qa_p), seq3(ka_p), vta_p, stats_p, heads=fox_heads, head_dim=head_dim,
                                tq=min(seq, 4 * tm_p), tk=tm_p)
            mix_p = mix_p.reshape(nb * seq, mixer_width)

            tok3 = lambda a: a.reshape(ns, n_tok, mixer_width)
            lft_new = jnp.pad(lft_s.reshape(hp, ns, n_tok).transpose(1, 0, 2),
                              ((0, 0), (0, 0), (0, page - n_tok)))
            cache_lft = jnp.pad(cache_fox_logf[j].transpose(0, 2, 1), ((0, 0), (0, hp - fox_heads), (0, 0)))
            to_stored = lambda a: a.transpose(0, 1, 3, 4, 2)
            mix_s = _fox_sample(tok3(q_s), tok3(k_s), tok3(v_s), lft_new, to_stored(cache_fox_k),
                                to_stored(cache_fox_v), cache_lft, page_table, layer=j,
                                pages_per_step=page_table.shape[1])
            mix_s = mix_s.reshape(tm_s, mixer_width)

            from_stored = lambda a: a.reshape(nb, fox_heads, head_dim, seq).transpose(0, 3, 1, 2)
            fk_p.append(from_stored(kt_p))
            fv_p.append(from_stored(vt_p))
            fl_p.append(lf_p[:, :fox_heads].reshape(nb, seq, fox_heads))
            fk_s.append(k_s.reshape(ns, n_tok, fox_heads, head_dim))
            fv_s.append(v_s.reshape(ns, n_tok, fox_heads, head_dim))
            fl_s.append(lf_s[:, :fox_heads].reshape(ns, n_tok, fox_heads))

        wom = w_out[i][:mixer_width].astype(BF16)
        woc = w_out[i][mixer_width:].astype(BF16)
        g_final = norm_final if i == depth - 1 else None
        xp = _cross_ffn_prompt(xp, mix_p, qc_p, mk_p.reshape(nb, n_mem, cross_width),
                               mv_p.reshape(nb, n_mem, cross_width), wom, woc, *ffn_w(1), g_final,
                               heads=cross_heads, head_dim=head_dim, tm=tm_p)
        xs = _cross_out_sample(xs, mix_s, qc_s, cache_mem_k.transpose(0, 1, 3, 4, 2),
                               cache_mem_v.transpose(0, 1, 3, 4, 2), wom, woc,
                               layer=i, n_tok=n_tok, seqs_per_step=16)
        xs = _ffn(xs, *ffn_w(1), g_final, tm=tm_s)

    return (xp.reshape(nb, seq, d), xs.reshape(ns, n_tok, d),
            jnp.stack(s5_re_p), jnp.stack(s5_im_p), jnp.stack(s5_re_s), jnp.stack(s5_im_s),
            jnp.stack(fk_p), jnp.stack(fv_p), jnp.stack(fl_p),
            jnp.stack(fk_s), jnp.stack(fv_s), jnp.stack(fl_s),
            jnp.stack(mk_list), jnp.stack(mv_list))
```
